```python
import jax, jax.numpy as jnp
from jax import lax
import numpy as np

D_MODEL = 1024
BATCH = 16
SEQ = 4096
DEPTH = 1

PLE_DIM = 256
CHUNK = 128
A_WIDTH = 1024
A_GROUPS = 8
A_GROUP_DIM = A_WIDTH // A_GROUPS
B_WIDTH = 1024
B_CONV = 31
FFN_DIM = 2816
FFN_CONV = 3
IN_COLS = 2 * A_WIDTH + 2 * B_WIDTH + 2 * D_MODEL
EPS_RMS = 1e-6
EPS_LN = 1e-5

kernel_name = "hybrid_gmlp_conformer_gated_block"


def rmsnorm(x, g):
    xf = x.astype(jnp.float32)
    y = xf * lax.rsqrt(jnp.mean(xf * xf, axis=-1, keepdims=True) + EPS_RMS)
    return y.astype(x.dtype) * g


def layernorm(x, g, b):
    xf = x.astype(jnp.float32)
    mu = jnp.mean(xf, axis=-1, keepdims=True)
    var = jnp.mean(jnp.square(xf - mu), axis=-1, keepdims=True)
    y = (xf - mu) * lax.rsqrt(var + EPS_LN)
    return y.astype(x.dtype) * g + b


def causal_dwconv(x, w, b):
    k, c = w.shape
    y = lax.conv_general_dilated(
        x, w[:, None, :], window_strides=(1,), padding=((k - 1, 0),),
        dimension_numbers=("NWC", "WIO", "NWC"), feature_group_count=c)
    return y + b


def _fwd_setup_inputs(seed: int = 0) -> dict:
    key = jax.random.key(seed)
    ks = jax.random.split(key, 32)
    f32 = jnp.float32

    def nrm(k, shape, scale):
        return jax.random.normal(k, shape, f32) * scale

    def gain(k, shape):
        return 1.0 + 0.01 * jax.random.normal(k, shape, f32)

    L = DEPTH
    return {
        "x": jax.random.normal(ks[0], (BATCH, SEQ, D_MODEL), f32),
        "p": jax.random.normal(ks[1], (DEPTH, BATCH, SEQ, PLE_DIM), f32),
        "g_mix": gain(ks[2], (L, D_MODEL)),
        "w_in": nrm(ks[3], (L, D_MODEL, IN_COLS), D_MODEL ** -0.5),
        "ln_v_g": gain(ks[4], (L, A_WIDTH)),
        "ln_v_b": nrm(ks[5], (L, A_WIDTH), 0.01),
        "w_s": nrm(ks[6], (L, A_GROUPS, CHUNK, CHUNK), CHUNK ** -0.5),
        "b_s": 1.0 + 0.1 * jax.random.normal(ks[7], (L, A_GROUPS, CHUNK), f32),
        "w_a_out": nrm(ks[8], (L, A_WIDTH, D_MODEL), A_WIDTH ** -0.5),
        "conv_b_w": nrm(ks[9], (L, B_CONV, B_WIDTH), B_CONV ** -0.5),
        "conv_b_b": nrm(ks[10], (L, B_WIDTH), 0.01),
        "ln_b_g": gain(ks[11], (L, B_WIDTH)),
        "ln_b_b": nrm(ks[12], (L, B_WIDTH), 0.01),
        "w_b_out": nrm(ks[13], (L, B_WIDTH, D_MODEL), B_WIDTH ** -0.5),
        "w_o": nrm(ks[14], (L, D_MODEL, D_MODEL), D_MODEL ** -0.5),
        "g_ffn": gain(ks[15], (L, D_MODEL)),
        "w_up": nrm(ks[16], (L, D_MODEL, 2 * FFN_DIM), D_MODEL ** -0.5),
        "ffn_conv_w": nrm(ks[17], (L, FFN_CONV, 2 * FFN_DIM), FFN_CONV ** -0.5),
        "ffn_conv_b": nrm(ks[18], (L, 2 * FFN_DIM), 0.01),
        "w_down": nrm(ks[19], (L, FFN_DIM, D_MODEL), FFN_DIM ** -0.5),
        "g_pg": gain(ks[20], (L, D_MODEL)),
        "w_pg": nrm(ks[21], (L, D_MODEL, D_MODEL), D_MODEL ** -0.5),
        "w_ple": nrm(ks[22], (L, PLE_DIM, D_MODEL), PLE_DIM ** -0.5),
        "g_ple": gain(ks[23], (L, D_MODEL)),
        "g_final": gain(ks[24], (D_MODEL,)),
    }


def _fwd_reference(x, p, g_mix, w_in, ln_v_g, ln_v_b, w_s, b_s, w_a_out,
              conv_b_w, conv_b_b, ln_b_g, ln_b_b, w_b_out, w_o,
              g_ffn, w_up, ffn_conv_w, ffn_conv_b, w_down,
              g_pg, w_pg, w_ple, g_ple, g_final):
    bsz, seq = x.shape[0], x.shape[1]
    n_chunks = seq // CHUNK
    tril = jnp.tril(jnp.ones((CHUNK, CHUNK), dtype=bool))
    splits = [A_WIDTH, 2 * A_WIDTH, 2 * A_WIDTH + B_WIDTH,
              2 * A_WIDTH + 2 * B_WIDTH, 2 * A_WIDTH + 2 * B_WIDTH + D_MODEL]

    for i in range(DEPTH):
        h = rmsnorm(x, g_mix[i])
        z = h @ w_in[i]
        u, v, a_b, gl_b, gate_a, gate_b = jnp.split(z, splits, axis=-1)

        u = jax.nn.gelu(u)
        v = layernorm(jax.nn.gelu(v), ln_v_g[i], ln_v_b[i])
        vr = v.reshape(bsz, n_chunks, CHUNK, A_GROUPS, A_GROUP_DIM)
        ws = jnp.where(tril[None], w_s[i], jnp.zeros_like(w_s[i]))
        mix = jnp.einsum("gts,bnsgc->bntgc", ws, vr) + b_s[i].T[:, :, None]
        y_a = (u * mix.reshape(bsz, seq, A_WIDTH)) @ w_a_out[i]

        glu = a_b * jax.nn.sigmoid(gl_b)
        c = causal_dwconv(glu, conv_b_w[i], conv_b_b[i])
        c = jax.nn.silu(layernorm(c, ln_b_g[i], ln_b_b[i]))
        y_b = c @ w_b_out[i]

        merged = jax.nn.sigmoid(gate_a) * y_a + jax.nn.sigmoid(gate_b) * y_b
        x = x + merged @ w_o[i]

        h = rmsnorm(x, g_ffn[i])
        up = causal_dwconv(h @ w_up[i], ffn_conv_w[i], ffn_conv_b[i])
        gate, val = jnp.split(up, 2, axis=-1)
        x = x + (jax.nn.gelu(gate) * val) @ w_down[i]

        pe = rmsnorm(p[i] @ w_ple[i], g_ple[i])
        pg = jax.nn.sigmoid(rmsnorm(x, g_pg[i]) @ w_pg[i])
        x = x + pe * pg

    return rmsnorm(x, g_final)


import jax as _jax
import jax.numpy as _jnp

TWIN_FORMAT = 'train_step'
FWD_PARAMS = ['x', 'p', 'g_mix', 'w_in', 'ln_v_g', 'ln_v_b', 'w_s', 'b_s', 'w_a_out', 'conv_b_w', 'conv_b_b', 'ln_b_g', 'ln_b_b', 'w_b_out', 'w_o', 'g_ffn', 'w_up', 'ffn_conv_w', 'ffn_conv_b', 'w_down', 'g_pg', 'w_pg', 'w_ple', 'g_ple', 'g_final']
TWIN_WEIGHTS = ['g_mix', 'w_in', 'ln_v_g', 'ln_v_b', 'w_s', 'b_s', 'w_a_out', 'conv_b_w', 'conv_b_b', 'ln_b_g', 'ln_b_b', 'w_b_out', 'w_o', 'g_ffn', 'w_up', 'ffn_conv_w', 'ffn_conv_b', 'w_down', 'g_pg', 'w_pg', 'w_ple', 'g_ple', 'g_final']
TWIN_DIFF_INPUT = 'x'
TWIN_INPUTS = ['x', 'p', 'g_mix', 'w_in', 'ln_v_g', 'ln_v_b', 'w_s', 'b_s', 'w_a_out', 'conv_b_w', 'conv_b_b', 'ln_b_g', 'ln_b_b', 'w_b_out', 'w_o', 'g_ffn', 'w_up', 'ffn_conv_w', 'ffn_conv_b', 'w_down', 'g_pg', 'w_pg', 'w_ple', 'g_ple', 'g_final', 'loss_target', 'm_g_mix', 'm_w_in', 'm_ln_v_g', 'm_ln_v_b', 'm_w_s', 'm_b_s', 'm_w_a_out', 'm_conv_b_w', 'm_conv_b_b', 'm_ln_b_g', 'm_ln_b_b', 'm_w_b_out', 'm_w_o', 'm_g_ffn', 'm_w_up', 'm_ffn_conv_w', 'm_ffn_conv_b', 'm_w_down', 'm_g_pg', 'm_w_pg', 'm_w_ple', 'm_g_ple', 'm_g_final', 'v_g_mix', 'v_w_in', 'v_ln_v_g', 'v_ln_v_b', 'v_w_s', 'v_b_s', 'v_w_a_out', 'v_conv_b_w', 'v_conv_b_b', 'v_ln_b_g', 'v_ln_b_b', 'v_w_b_out', 'v_w_o', 'v_g_ffn', 'v_w_up', 'v_ffn_conv_w', 'v_ffn_conv_b', 'v_w_down', 'v_g_pg', 'v_w_pg', 'v_w_ple', 'v_g_ple', 'v_g_final']
TWIN_OUTPUTS = ['loss', 'grad_x', 'grad_g_mix', 'grad_w_in', 'grad_ln_v_g', 'grad_ln_v_b', 'grad_w_s', 'grad_b_s', 'grad_w_a_out', 'grad_conv_b_w', 'grad_conv_b_b', 'grad_ln_b_g', 'grad_ln_b_b', 'grad_w_b_out', 'grad_w_o', 'grad_g_ffn', 'grad_w_up', 'grad_ffn_conv_w', 'grad_ffn_conv_b', 'grad_w_down', 'grad_g_pg', 'grad_w_pg', 'grad_w_ple', 'grad_g_ple', 'grad_g_final', 'delta_g_mix', 'delta_w_in', 'delta_ln_v_g', 'delta_ln_v_b', 'delta_w_s', 'delta_b_s', 'delta_w_a_out', 'delta_conv_b_w', 'delta_conv_b_b', 'delta_ln_b_g', 'delta_ln_b_b', 'delta_w_b_out', 'delta_w_o', 'delta_g_ffn', 'delta_w_up', 'delta_ffn_conv_w', 'delta_ffn_conv_b', 'delta_w_down', 'delta_g_pg', 'delta_w_pg', 'delta_w_ple', 'delta_g_ple', 'delta_g_final', 'new_m_g_mix', 'new_m_w_in', 'new_m_ln_v_g', 'new_m_ln_v_b', 'new_m_w_s', 'new_m_b_s', 'new_m_w_a_out', 'new_m_conv_b_w', 'new_m_conv_b_b', 'new_m_ln_b_g', 'new_m_ln_b_b', 'new_m_w_b_out', 'new_m_w_o', 'new_m_g_ffn', 'new_m_w_up', 'new_m_ffn_conv_w', 'new_m_ffn_conv_b', 'new_m_w_down', 'new_m_g_pg', 'new_m_w_pg', 'new_m_w_ple', 'new_m_g_ple', 'new_m_g_final', 'new_v_g_mix', 'new_v_w_in', 'new_v_ln_v_g', 'new_v_ln_v_b', 'new_v_w_s', 'new_v_b_s', 'new_v_w_a_out', 'new_v_conv_b_w', 'new_v_conv_b_b', 'new_v_ln_b_g', 'new_v_ln_b_b', 'new_v_w_b_out', 'new_v_w_o', 'new_v_g_ffn', 'new_v_w_up', 'new_v_ffn_conv_w', 'new_v_ffn_conv_b', 'new_v_w_down', 'new_v_g_pg', 'new_v_w_pg', 'new_v_w_ple', 'new_v_g_ple', 'new_v_g_final']
TWIN_LEAF_KINDS = {'loss': 'loss', 'grad_x': 'grad_x', 'grad_g_mix': 'grad_w', 'grad_w_in': 'grad_w', 'grad_ln_v_g': 'grad_w', 'grad_ln_v_b': 'grad_w', 'grad_w_s': 'grad_w', 'grad_b_s': 'grad_w', 'grad_w_a_out': 'grad_w', 'grad_conv_b_w': 'grad_w', 'grad_conv_b_b': 'grad_w', 'grad_ln_b_g': 'grad_w', 'grad_ln_b_b': 'grad_w', 'grad_w_b_out': 'grad_w', 'grad_w_o': 'grad_w', 'grad_g_ffn': 'grad_w', 'grad_w_up': 'grad_w', 'grad_ffn_conv_w': 'grad_w', 'grad_ffn_conv_b': 'grad_w', 'grad_w_down': 'grad_w', 'grad_g_pg': 'grad_w', 'grad_w_pg': 'grad_w', 'grad_w_ple': 'grad_w', 'grad_g_ple': 'grad_w', 'grad_g_final': 'grad_w', 'delta_g_mix': 'delta_w', 'delta_w_in': 'delta_w', 'delta_ln_v_g': 'delta_w', 'delta_ln_v_b': 'delta_w', 'delta_w_s': 'delta_w', 'delta_b_s': 'delta_w', 'delta_w_a_out': 'delta_w', 'delta_conv_b_w': 'delta_w', 'delta_conv_b_b': 'delta_w', 'delta_ln_b_g': 'delta_w', 'delta_ln_b_b': 'delta_w', 'delta_w_b_out': 'delta_w', 'delta_w_o': 'delta_w', 'delta_g_ffn': 'delta_w', 'delta_w_up': 'delta_w', 'delta_ffn_conv_w': 'delta_w', 'delta_ffn_conv_b': 'delta_w', 'delta_w_down': 'delta_w', 'delta_g_pg': 'delta_w', 'delta_w_pg': 'delta_w', 'delta_w_ple': 'delta_w', 'delta_g_ple': 'delta_w', 'delta_g_final': 'delta_w', 'new_m_g_mix': 'new_m', 'new_m_w_in': 'new_m', 'new_m_ln_v_g': 'new_m', 'new_m_ln_v_b': 'new_m', 'new_m_w_s': 'new_m', 'new_m_b_s': 'new_m', 'new_m_w_a_out': 'new_m', 'new_m_conv_b_w': 'new_m', 'new_m_conv_b_b': 'new_m', 'new_m_ln_b_g': 'new_m', 'new_m_ln_b_b': 'new_m', 'new_m_w_b_out': 'new_m', 'new_m_w_o': 'new_m', 'new_m_g_ffn': 'new_m', 'new_m_w_up': 'new_m', 'new_m_ffn_conv_w': 'new_m', 'new_m_ffn_conv_b': 'new_m', 'new_m_w_down': 'new_m', 'new_m_g_pg': 'new_m', 'new_m_w_pg': 'new_m', 'new_m_w_ple': 'new_m', 'new_m_g_ple': 'new_m', 'new_m_g_final': 'new_m', 'new_v_g_mix': 'new_v', 'new_v_w_in': 'new_v', 'new_v_ln_v_g': 'new_v', 'new_v_ln_v_b': 'new_v', 'new_v_w_s': 'new_v', 'new_v_b_s': 'new_v', 'new_v_w_a_out': 'new_v', 'new_v_conv_b_w': 'new_v', 'new_v_conv_b_b': 'new_v', 'new_v_ln_b_g': 'new_v', 'new_v_ln_b_b': 'new_v', 'new_v_w_b_out': 'new_v', 'new_v_w_o': 'new_v', 'new_v_g_ffn': 'new_v', 'new_v_w_up': 'new_v', 'new_v_ffn_conv_w': 'new_v', 'new_v_ffn_conv_b': 'new_v', 'new_v_w_down': 'new_v', 'new_v_g_pg': 'new_v', 'new_v_w_pg': 'new_v', 'new_v_w_ple': 'new_v', 'new_v_g_ple': 'new_v', 'new_v_g_final': 'new_v'}


def _forward(args):
    return _fwd_reference(*[args[k] for k in FWD_PARAMS])


def _output_shape():
    out = _jax.eval_shape(lambda: _forward(_fwd_setup_inputs(0)))
    return out.shape, out.dtype

N_MICROBATCH = 1
ADAM_LR = 0.001
ADAM_B1 = 0.9
ADAM_B2 = 0.999
ADAM_EPS = 1e-08
ADAM_WD = 0.01
ADAM_STEP = 10
PER_EXAMPLE_BATCH_AXIS = {'x': 0, 'p': 1, 'loss_target': 0}
SHARED_INPUTS = []
_WEIGHT_DTYPES = {'g_mix': _jnp.float32, 'w_in': _jnp.float32, 'ln_v_g': _jnp.float32, 'ln_v_b': _jnp.float32, 'w_s': _jnp.float32, 'b_s': _jnp.float32, 'w_a_out': _jnp.float32, 'conv_b_w': _jnp.float32, 'conv_b_b': _jnp.float32, 'ln_b_g': _jnp.float32, 'ln_b_b': _jnp.float32, 'w_b_out': _jnp.float32, 'w_o': _jnp.float32, 'g_ffn': _jnp.float32, 'w_up': _jnp.float32, 'ffn_conv_w': _jnp.float32, 'ffn_conv_b': _jnp.float32, 'w_down': _jnp.float32, 'g_pg': _jnp.float32, 'w_pg': _jnp.float32, 'w_ple': _jnp.float32, 'g_ple': _jnp.float32, 'g_final': _jnp.float32}
MOMENT_SCALE = {'g_mix': 1.642308e-01, 'w_in': 6.415303e-02, 'ln_v_g': 5.852998e-02, 'ln_v_b': 5.990694e-02, 'w_s': 5.806487e-02, 'b_s': 8.276414e-02, 'w_a_out': 9.999813e-02, 'conv_b_w': 7.860073e-02, 'conv_b_b': 1.670490e-01, 'ln_b_g': 9.944234e-02, 'ln_b_b': 9.346665e-02, 'w_b_out': 7.627089e-02, 'w_o': 1.249669e-01, 'g_ffn': 1.772896e-01, 'w_up': 7.154718e-02, 'ffn_conv_w': 7.296085e-02, 'ffn_conv_b': 7.412196e-02, 'w_down': 1.165959e-01, 'g_pg': 3.729614e-02, 'w_pg': 3.742383e-02, 'w_ple': 9.621130e-02, 'g_ple': 1.261670e-01, 'g_final': 6.395715e+01}


def _to_microbatches(a, axis):
    t = _jnp.moveaxis(a, axis, 0)
    t = t.reshape((N_MICROBATCH, t.shape[0] // N_MICROBATCH) + t.shape[1:])
    return _jnp.moveaxis(t, 1, axis + 1)


def setup_inputs(seed: int = 0) -> dict:
    inp = _fwd_setup_inputs(seed)
    key = _jax.random.fold_in(_jax.random.key(seed), 7919)
    shape, _ = _output_shape()
    out = dict(inp)
    out["loss_target"] = _jax.random.normal(_jax.random.fold_in(key, 0), shape, _jnp.float32)
    for i, name in enumerate(TWIN_WEIGHTS):
        w = inp[name].astype(_jnp.float32)
        if MOMENT_SCALE is None:
            s = _jnp.sqrt(_jnp.mean(_jnp.square(w)) + 1e-30)
        else:
            s = MOMENT_SCALE[name]
        km, kv = _jax.random.split(_jax.random.fold_in(key, i + 1))
        out[name] = w
        out["m_" + name] = s * _jax.random.normal(km, w.shape, _jnp.float32)
        out["v_" + name] = (s * s) * _jax.random.uniform(kv, w.shape, _jnp.float32, 0.5, 1.5)
    if N_MICROBATCH > 1:
        for name, axis in PER_EXAMPLE_BATCH_AXIS.items():
            out[name] = _to_microbatches(out[name], axis)
    return {'x': out['x'], 'p': out['p'], 'g_mix': out['g_mix'], 'w_in': out['w_in'], 'ln_v_g': out['ln_v_g'], 'ln_v_b': out['ln_v_b'], 'w_s': out['w_s'], 'b_s': out['b_s'], 'w_a_out': out['w_a_out'], 'conv_b_w': out['conv_b_w'], 'conv_b_b': out['conv_b_b'], 'ln_b_g': out['ln_b_g'], 'ln_b_b': out['ln_b_b'], 'w_b_out': out['w_b_out'], 'w_o': out['w_o'], 'g_ffn': out['g_ffn'], 'w_up': out['w_up'], 'ffn_conv_w': out['ffn_conv_w'], 'ffn_conv_b': out['ffn_conv_b'], 'w_down': out['w_down'], 'g_pg': out['g_pg'], 'w_pg': out['w_pg'], 'w_ple': out['w_ple'], 'g_ple': out['g_ple'], 'g_final': out['g_final'], 'loss_target': out['loss_target'], 'm_g_mix': out['m_g_mix'], 'm_w_in': out['m_w_in'], 'm_ln_v_g': out['m_ln_v_g'], 'm_ln_v_b': out['m_ln_v_b'], 'm_w_s': out['m_w_s'], 'm_b_s': out['m_b_s'], 'm_w_a_out': out['m_w_a_out'], 'm_conv_b_w': out['m_conv_b_w'], 'm_conv_b_b': out['m_conv_b_b'], 'm_ln_b_g': out['m_ln_b_g'], 'm_ln_b_b': out['m_ln_b_b'], 'm_w_b_out': out['m_w_b_out'], 'm_w_o': out['m_w_o'], 'm_g_ffn': out['m_g_ffn'], 'm_w_up': out['m_w_up'], 'm_ffn_conv_w': out['m_ffn_conv_w'], 'm_ffn_conv_b': out['m_ffn_conv_b'], 'm_w_down': out['m_w_down'], 'm_g_pg': out['m_g_pg'], 'm_w_pg': out['m_w_pg'], 'm_w_ple': out['m_w_ple'], 'm_g_ple': out['m_g_ple'], 'm_g_final': out['m_g_final'], 'v_g_mix': out['v_g_mix'], 'v_w_in': out['v_w_in'], 'v_ln_v_g': out['v_ln_v_g'], 'v_ln_v_b': out['v_ln_v_b'], 'v_w_s': out['v_w_s'], 'v_b_s': out['v_b_s'], 'v_w_a_out': out['v_w_a_out'], 'v_conv_b_w': out['v_conv_b_w'], 'v_conv_b_b': out['v_conv_b_b'], 'v_ln_b_g': out['v_ln_b_g'], 'v_ln_b_b': out['v_ln_b_b'], 'v_w_b_out': out['v_w_b_out'], 'v_w_o': out['v_w_o'], 'v_g_ffn': out['v_g_ffn'], 'v_w_up': out['v_w_up'], 'v_ffn_conv_w': out['v_ffn_conv_w'], 'v_ffn_conv_b': out['v_ffn_conv_b'], 'v_w_down': out['v_w_down'], 'v_g_pg': out['v_g_pg'], 'v_w_pg': out['v_w_pg'], 'v_w_ple': out['v_w_ple'], 'v_g_ple': out['v_g_ple'], 'v_g_final': out['v_g_final']}


def _loss(weights, diff, rest, loss_target):
    with _jax.named_scope("forward"):
        args = {**rest, TWIN_DIFF_INPUT: diff, **{k: w.astype(_WEIGHT_DTYPES[k]) for k, w in weights.items()}}
        y = _forward(args)
    with _jax.named_scope("loss_head"):
        err = _jnp.square(y.astype(_jnp.float32) - loss_target)
        return 0.5 * _jnp.sum(_jnp.mean(err, axis=-1)) if err.ndim else 0.5 * err


def _adamw(w, g, m, v):
    m = ADAM_B1 * m + (1.0 - ADAM_B1) * g
    v = ADAM_B2 * v + (1.0 - ADAM_B2) * _jnp.square(g)
    m_hat = m / (1.0 - ADAM_B1 ** ADAM_STEP)
    v_hat = v / (1.0 - ADAM_B2 ** ADAM_STEP)
    delta = -ADAM_LR * (m_hat / (_jnp.sqrt(v_hat) + ADAM_EPS) + ADAM_WD * w)
    return delta, m, v


def reference(x, p, g_mix, w_in, ln_v_g, ln_v_b, w_s, b_s, w_a_out, conv_b_w, conv_b_b, ln_b_g, ln_b_b, w_b_out, w_o, g_ffn, w_up, ffn_conv_w, ffn_conv_b, w_down, g_pg, w_pg, w_ple, g_ple, g_final, loss_target, m_g_mix, m_w_in, m_ln_v_g, m_ln_v_b, m_w_s, m_b_s, m_w_a_out, m_conv_b_w, m_conv_b_b, m_ln_b_g, m_ln_b_b, m_w_b_out, m_w_o, m_g_ffn, m_w_up, m_ffn_conv_w, m_ffn_conv_b, m_w_down, m_g_pg, m_w_pg, m_w_ple, m_g_ple, m_g_final, v_g_mix, v_w_in, v_ln_v_g, v_ln_v_b, v_w_s, v_b_s, v_w_a_out, v_conv_b_w, v_conv_b_b, v_ln_b_g, v_ln_b_b, v_w_b_out, v_w_o, v_g_ffn, v_w_up, v_ffn_conv_w, v_ffn_conv_b, v_w_down, v_g_pg, v_w_pg, v_w_ple, v_g_ple, v_g_final):
    given = dict(x=x, p=p, g_mix=g_mix, w_in=w_in, ln_v_g=ln_v_g, ln_v_b=ln_v_b, w_s=w_s, b_s=b_s, w_a_out=w_a_out, conv_b_w=conv_b_w, conv_b_b=conv_b_b, ln_b_g=ln_b_g, ln_b_b=ln_b_b, w_b_out=w_b_out, w_o=w_o, g_ffn=g_ffn, w_up=w_up, ffn_conv_w=ffn_conv_w, ffn_conv_b=ffn_conv_b, w_down=w_down, g_pg=g_pg, w_pg=w_pg, w_ple=w_ple, g_ple=g_ple, g_final=g_final, loss_target=loss_target, m_g_mix=m_g_mix, m_w_in=m_w_in, m_ln_v_g=m_ln_v_g, m_ln_v_b=m_ln_v_b, m_w_s=m_w_s, m_b_s=m_b_s, m_w_a_out=m_w_a_out, m_conv_b_w=m_conv_b_w, m_conv_b_b=m_conv_b_b, m_ln_b_g=m_ln_b_g, m_ln_b_b=m_ln_b_b, m_w_b_out=m_w_b_out, m_w_o=m_w_o, m_g_ffn=m_g_ffn, m_w_up=m_w_up, m_ffn_conv_w=m_ffn_conv_w, m_ffn_conv_b=m_ffn_conv_b, m_w_down=m_w_down, m_g_pg=m_g_pg, m_w_pg=m_w_pg, m_w_ple=m_w_ple, m_g_ple=m_g_ple, m_g_final=m_g_final, v_g_mix=v_g_mix, v_w_in=v_w_in, v_ln_v_g=v_ln_v_g, v_ln_v_b=v_ln_v_b, v_w_s=v_w_s, v_b_s=v_b_s, v_w_a_out=v_w_a_out, v_conv_b_w=v_conv_b_w, v_conv_b_b=v_conv_b_b, v_ln_b_g=v_ln_b_g, v_ln_b_b=v_ln_b_b, v_w_b_out=v_w_b_out, v_w_o=v_w_o, v_g_ffn=v_g_ffn, v_w_up=v_w_up, v_ffn_conv_w=v_ffn_conv_w, v_ffn_conv_b=v_ffn_conv_b, v_w_down=v_w_down, v_g_pg=v_g_pg, v_w_pg=v_w_pg, v_w_ple=v_w_ple, v_g_ple=v_g_ple, v_g_final=v_g_final)
    weights = {n: given[n] for n in TWIN_WEIGHTS}
    shared = {n: given[n] for n in SHARED_INPUTS}
    per_example = {n: given[n] for n in ['x', 'p']}
    grad_fn = _jax.value_and_grad(_loss, argnums=(0, 1))

    def one_microbatch(ex, loss_target):
        ex = dict(ex)
        diff = ex.pop(TWIN_DIFF_INPUT)
        return grad_fn(weights, diff, {**shared, **ex}, loss_target)

    if N_MICROBATCH == 1:
        loss, (grad_w, grad_x) = one_microbatch(per_example, given["loss_target"])
    else:
        def body(carry, xs):
            loss_sum, grad_sum = carry
            l_k, (gw_k, gx_k) = one_microbatch(xs[0], xs[1])
            with _jax.named_scope("update"):
                return (loss_sum + l_k, _jax.tree.map(_jnp.add, grad_sum, gw_k)), gx_k

        init = (_jnp.zeros((), _jnp.float32), _jax.tree.map(_jnp.zeros_like, weights))
        (loss, grad_w), grad_x = _jax.lax.scan(body, init, (per_example, given["loss_target"]))
    with _jax.named_scope("update"):
        delta_w, new_m, new_v = {}, {}, {}
        for n in TWIN_WEIGHTS:
            delta_w[n], new_m[n], new_v[n] = _adamw(weights[n], grad_w[n], given["m_" + n], given["v_" + n])
    return (loss, grad_x, *[grad_w[n] for n in TWIN_WEIGHTS], *[delta_w[n] for n in TWIN_WEIGHTS],
            *[new_m[n] for n in TWIN_WEIGHTS], *[new_v[n] for n in TWIN_WEIGHTS])
```

```python
import functools
import math

import jax
import jax.numpy as jnp
from jax import lax
from jax.experimental import pallas as pl
from jax.experimental.pallas import tpu as pltpu

F32 = jnp.float32
BF16 = jnp.bfloat16

N_DEV = 8
EPS_RMS = 1e-6
EPS_LN = 1e-5
CHUNK = 128
GROUPS = 8
CONV_B = 31
CONV_F = 3
HALO_B = 32
HALO_F = 8

ADAM_LR = 0.001
ADAM_B1 = 0.9
ADAM_B2 = 0.999
ADAM_EPS = 1e-08
ADAM_WD = 0.01
ADAM_STEP = 10

VMEM_LIMIT = 56 * 1024 * 1024
TM_MM = 1024
TT_MM = 512
TM_EW = 256

_NN = (((1,), (0,)), ((), ()))
_NT = (((1,), (1,)), ((), ()))
_TN = (((0,), (0,)), ((), ()))
MESH = pl.DeviceIdType.MESH
HBM_SPEC = pl.BlockSpec(memory_space=pltpu.HBM)


def _params(n_axes):
    return pltpu.CompilerParams(dimension_semantics=("arbitrary",) * n_axes, vmem_limit_bytes=VMEM_LIMIT)


def _gelu(x):
    k = math.sqrt(2.0 / math.pi)
    return 0.5 * x * (1.0 + jnp.tanh(k * (x + 0.044715 * (x * x * x))))


def _gelu_and_grad(x):
    k = math.sqrt(2.0 / math.pi)
    x2 = x * x
    t = jnp.tanh(k * (x + 0.044715 * (x2 * x)))
    g = 0.5 * x * (1.0 + t)
    dg = 0.5 * (1.0 + t) + 0.5 * x * (1.0 - t * t) * (k * (1.0 + 3.0 * 0.044715 * x2))
    return g, dg


def _sigmoid(x):
    return 1.0 / (1.0 + jnp.exp(-x))


def _rowsum(x):
    return jnp.sum(x, axis=0, keepdims=True)


def _mean(x):
    return jnp.mean(x, axis=-1, keepdims=True)


def _matmul(name, a, b, *, dims, grid, a_spec, b_spec, o_spec, acc_shape, out_shape, res=None, res_spec=None):
    nk = grid[2]
    has_res = res is not None

    def body(*refs):
        if has_res:
            a_ref, b_ref, r_ref, o_ref = refs[:4]
        else:
            a_ref, b_ref, o_ref = refs[:3]
            r_ref = None
        part = lax.dot_general(a_ref[...].astype(BF16), b_ref[...].astype(BF16), dims, preferred_element_type=F32)

        def finish(acc):
            if has_res:
                acc = acc + r_ref[...]
            o_ref[...] = acc.astype(o_ref.dtype)

        if nk == 1:
            finish(part)
        else:
            acc_ref = refs[-1]
            k = pl.program_id(2)

            @pl.when(k == 0)
            def _():
                acc_ref[...] = part

            @pl.when(k > 0)
            def _():
                acc_ref[...] += part

            @pl.when(k == nk - 1)
            def _():
                finish(acc_ref[...])

    in_specs = [a_spec, b_spec] + ([res_spec] if has_res else [])
    args = (a, b) + ((res,) if has_res else ())
    scratch = [pltpu.VMEM(acc_shape, F32)] if nk > 1 else []
    return pl.pallas_call(body, grid=grid, in_specs=in_specs, out_specs=o_spec, out_shape=out_shape,
                          scratch_shapes=scratch, name=name, compiler_params=_params(3))(*args)


def _mm_rows(name, a, w, *, dims, out_dtype, res=None):
    t, k = a.shape
    n = w.shape[1] if dims == _NN else w.shape[0]
    tm = min(TM_MM, t)
    row = pl.BlockSpec((tm, n), lambda i, j, kk: (i, 0))
    return _matmul(name, a, w, dims=dims, grid=(t // tm, 1, 1),
                   a_spec=pl.BlockSpec((tm, k), lambda i, j, kk: (i, 0)),
                   b_spec=pl.BlockSpec(w.shape, lambda i, j, kk: (0, 0)),
                   o_spec=row, acc_shape=(tm, n), out_shape=jax.ShapeDtypeStruct((t, n), out_dtype),
                   res=res, res_spec=row if res is not None else None)


def _mm_wgrad(name, a, b, *, out_dtype):
    t, m = a.shape
    n = b.shape[1]
    tt = min(TT_MM, t)
    return _matmul(name, a, b, dims=_TN, grid=(1, 1, t // tt),
                   a_spec=pl.BlockSpec((tt, m), lambda i, j, kk: (kk, 0)),
                   b_spec=pl.BlockSpec((tt, n), lambda i, j, kk: (kk, 0)),
                   o_spec=pl.BlockSpec((m, n), lambda i, j, kk: (0, 0)),
                   acc_shape=(m, n), out_shape=jax.ShapeDtypeStruct((m, n), out_dtype))


def _vec_spec(d):
    return pl.BlockSpec((1, d), lambda i: (0, 0))


def _rms_fwd(name, x, g):
    t, d = x.shape
    tm = min(TM_EW * 2, t)

    def body(x_ref, g_ref, h_ref):
        xv = x_ref[...]
        rstd = lax.rsqrt(_mean(xv * xv) + EPS_RMS)
        h_ref[...] = ((xv * rstd) * g_ref[...]).astype(BF16)

    row = pl.BlockSpec((tm, d), lambda i: (i, 0))
    return pl.pallas_call(body, grid=(t // tm,), in_specs=[row, _vec_spec(d)], out_specs=row,
                          out_shape=jax.ShapeDtypeStruct((t, d), BF16), name=name, compiler_params=_params(1))(x, g)


def _rms_bwd(name, x, g, dh, dres, *, want_bf16):
    t, d = x.shape
    tm = min(TM_EW * 2, t)
    n = t // tm

    def body(x_ref, g_ref, dh_ref, dres_ref, dx_ref, *rest):
        dg_ref = rest[-1]
        i = pl.program_id(0)
        xv = x_ref[...]
        rstd = lax.rsqrt(_mean(xv * xv) + EPS_RMS)
        nrm = xv * rstd
        dhv = dh_ref[...].astype(F32)
        dn = dhv * g_ref[...]
        dx = dres_ref[...] + rstd * (dn - nrm * _mean(dn * nrm))
        dx_ref[...] = dx
        if want_bf16:
            rest[0][...] = dx.astype(BF16)
        part = _rowsum(dhv * nrm)

        @pl.when(i == 0)
        def _():
            dg_ref[...] = part

        @pl.when(i > 0)
        def _():
            dg_ref[...] += part

    row = pl.BlockSpec((tm, d), lambda i: (i, 0))
    out_specs = [row] + ([row] if want_bf16 else []) + [_vec_spec(d)]
    out_shape = ([jax.ShapeDtypeStruct((t, d), F32)] + ([jax.ShapeDtypeStruct((t, d), BF16)] if want_bf16 else [])
                 + [jax.ShapeDtypeStruct((1, d), F32)])
    return pl.pallas_call(body, grid=(n,), in_specs=[row, _vec_spec(d), row, row], out_specs=out_specs,
                          out_shape=out_shape, name=name, compiler_params=_params(1))(x, g, dh, dres)


def _branch_fwd(z, ln_v_g, ln_v_b, ws_m, bias_full, conv_w, conv_b, ln_b_g, ln_b_b, seq):
    t = z.shape[0]
    w = 1024
    tm = min(TM_EW, seq)
    tiles_per_seq = seq // tm
    n_chunks = tm // CHUNK

    def body(z_ref, lvg_ref, lvb_ref, ws_ref, bias_ref, cw_ref, cb_ref, lbg_ref, lbb_ref,
             pa_ref, cs_ref, c_ref, hist_ref, buf_ref, mix_ref):
        i = pl.program_id(0)
        u = z_ref[:, 0:w].astype(F32)
        v = z_ref[:, w:2 * w].astype(F32)
        ug = _gelu(u)
        vg = _gelu(v)
        dv = vg - _mean(vg)
        vhat = dv * lax.rsqrt(_mean(dv * dv) + EPS_LN)
        vn = (vhat * lvg_ref[...] + lvb_ref[...]).astype(BF16)
        for ci in range(n_chunks):
            rows = slice(ci * CHUNK, (ci + 1) * CHUNK)
            for g in range(GROUPS):
                cols = slice(g * CHUNK, (g + 1) * CHUNK)
                mix_ref[rows, cols] = lax.dot_general(ws_ref[g], vn[rows, cols], _NN, preferred_element_type=F32)
            mix_ref[rows, :] += bias_ref[...]
        pa_ref[...] = (ug * mix_ref[...]).astype(BF16)

        a = z_ref[:, 2 * w:3 * w].astype(F32)
        gl = z_ref[:, 3 * w:4 * w].astype(F32)
        glu = a * _sigmoid(gl)

        @pl.when(i % tiles_per_seq == 0)
        def _():
            hist_ref[...] = jnp.zeros_like(hist_ref)

        buf_ref[0:HALO_B, :] = hist_ref[...]
        buf_ref[HALO_B:, :] = glu
        hist_ref[...] = glu[tm - HALO_B:, :]
        c = jnp.zeros((tm, w), F32) + cb_ref[...]
        for k in range(CONV_B):
            off = HALO_B - (CONV_B - 1) + k
            c = c + cw_ref[k:k + 1, :] * buf_ref[off:off + tm, :]
        c_ref[...] = c
        dc = c - _mean(c)
        chat = dc * lax.rsqrt(_mean(dc * dc) + EPS_LN)
        cn = chat * lbg_ref[...] + lbb_ref[...]
        cs_ref[...] = (cn * _sigmoid(cn)).astype(BF16)

    row = pl.BlockSpec((tm, w), lambda i: (i, 0))
    in_specs = [pl.BlockSpec((tm, 4 * w), lambda i: (i, 0)), _vec_spec(w), _vec_spec(w),
                pl.BlockSpec((GROUPS, CHUNK, CHUNK), lambda i: (0, 0, 0)), pl.BlockSpec((CHUNK, w), lambda i: (0, 0)),
                pl.BlockSpec((CONV_B, w), lambda i: (0, 0)), _vec_spec(w), _vec_spec(w), _vec_spec(w)]
    return pl.pallas_call(
        body, grid=(t // tm,), in_specs=in_specs, out_specs=[row, row, row],
        out_shape=[jax.ShapeDtypeStruct((t, w), BF16), jax.ShapeDtypeStruct((t, w), BF16), jax.ShapeDtypeStruct((t, w), F32)],
        scratch_shapes=[pltpu.VMEM((HALO_B, w), F32), pltpu.VMEM((HALO_B + tm, w), F32), pltpu.VMEM((tm, w), F32)],
        name="branch_fwd", compiler_params=_params(1))(z, ln_v_g, ln_v_b, ws_m, bias_full, conv_w, conv_b, ln_b_g, ln_b_b)


def _merge_fwd(z, ya, yb):
    t, w = ya.shape
    tm = min(TM_EW * 2, t)

    def body(ga_ref, gb_ref, ya_ref, yb_ref, o_ref):
        sa = _sigmoid(ga_ref[...].astype(F32))
        sb = _sigmoid(gb_ref[...].astype(F32))
        o_ref[...] = (sa * ya_ref[...].astype(F32) + sb * yb_ref[...].astype(F32)).astype(BF16)

    row = pl.BlockSpec((tm, w), lambda i: (i, 0))
    in_specs = [pl.BlockSpec((tm, w), lambda i: (i, 4)), pl.BlockSpec((tm, w), lambda i: (i, 5)), row, row]
    return pl.pallas_call(body, grid=(t // tm,), in_specs=in_specs, out_specs=row,
                          out_shape=jax.ShapeDtypeStruct((t, w), BF16), name="merge_fwd",
                          compiler_params=_params(1))(z, z, ya, yb)


def _merge_bwd(z, ya, yb, dmerged):
    t, w = ya.shape
    tm = min(TM_EW * 2, t)

    def body(ga_ref, gb_ref, ya_ref, yb_ref, dm_ref, dya_ref, dyb_ref, dg_ref):
        sa = _sigmoid(ga_ref[...].astype(F32))
        sb = _sigmoid(gb_ref[...].astype(F32))
        dm = dm_ref[...].astype(F32)
        dya_ref[...] = (dm * sa).astype(BF16)
        dyb_ref[...] = (dm * sb).astype(BF16)
        dg_ref[:, 0:w] = (dm * ya_ref[...].astype(F32) * sa * (1.0 - sa)).astype(BF16)
        dg_ref[:, w:2 * w] = (dm * yb_ref[...].astype(F32) * sb * (1.0 - sb)).astype(BF16)

    row = pl.BlockSpec((tm, w), lambda i: (i, 0))
    in_specs = [pl.BlockSpec((tm, w), lambda i: (i, 4)), pl.BlockSpec((tm, w), lambda i: (i, 5)), row, row, row]
    return pl.pallas_call(
        body, grid=(t // tm,), in_specs=in_specs, out_specs=[row, row, pl.BlockSpec((tm, 2 * w), lambda i: (i, 0))],
        out_shape=[jax.ShapeDtypeStruct((t, w), BF16), jax.ShapeDtypeStruct((t, w), BF16),
                   jax.ShapeDtypeStruct((t, 2 * w), BF16)],
        name="merge_bwd", compiler_params=_params(1))(z, z, ya, yb, dmerged)


def _branch_bwd(z, c_saved, dpa, dcs, dgates, ln_v_g, ln_v_b, ws_m, ws_mt, bias_full, conv_w, ln_b_g, ln_b_b, seq):
    t = z.shape[0]
    w = 1024
    tm = min(TM_EW, seq)
    tiles_per_seq = seq // tm
    n_tiles = t // tm
    n_chunks = tm // CHUNK
    halo_blocks = tm // HALO_B

    def body(z_ref, zh_ref, c_ref, dpa_ref, dcs_ref, dgt_ref, lvg_ref, lvb_ref, ws_ref, wst_ref, bias_ref, cw_ref,
             lbg_ref, lbb_ref,
             dz_ref, dlvg_ref, dlvb_ref, dws_ref, dbs_ref, dcw_ref, dcb_ref, dlbg_ref, dlbb_ref,
             carry_ref, gbuf_ref, dbuf_ref, mix_ref, dvn_ref, dbs_acc_ref):
        i = pl.program_id(0)
        r = n_tiles - 1 - i

        @pl.when(i == 0)
        def _():
            for ref in (dlvg_ref, dlvb_ref, dws_ref, dbs_acc_ref, dcw_ref, dcb_ref, dlbg_ref, dlbb_ref):
                ref[...] = jnp.zeros_like(ref)

        u = z_ref[:, 0:w].astype(F32)
        v = z_ref[:, w:2 * w].astype(F32)
        ug, dug = _gelu_and_grad(u)
        vg, dvg = _gelu_and_grad(v)
        dv0 = vg - _mean(vg)
        rstd_v = lax.rsqrt(_mean(dv0 * dv0) + EPS_LN)
        vhat = dv0 * rstd_v
        vn = (vhat * lvg_ref[...] + lvb_ref[...]).astype(BF16)
        dpa = dpa_ref[...].astype(F32)
        dmix = dpa * ug
        dmix_b = dmix.astype(BF16)
        for ci in range(n_chunks):
            rows = slice(ci * CHUNK, (ci + 1) * CHUNK)
            for g in range(GROUPS):
                cols = slice(g * CHUNK, (g + 1) * CHUNK)
                mix_ref[rows, cols] = lax.dot_general(ws_ref[g], vn[rows, cols], _NN, preferred_element_type=F32)
                dvn_ref[rows, cols] = lax.dot_general(wst_ref[g], dmix_b[rows, cols], _NN, preferred_element_type=F32)
                dws_ref[g] += lax.dot_general(dmix_b[rows, cols], vn[rows, cols], _NT, preferred_element_type=F32)
            mix_ref[rows, :] += bias_ref[...]
            dbs_acc_ref[...] += dmix[rows, :]
        dz_ref[:, 0:w] = (dpa * mix_ref[...] * dug).astype(BF16)
        dvn = dvn_ref[...]
        dlvg_ref[...] += _rowsum(dvn * vhat)
        dlvb_ref[...] += _rowsum(dvn)
        dvh = dvn * lvg_ref[...]
        dvg_in = rstd_v * (dvh - _mean(dvh) - vhat * _mean(dvh * vhat))
        dz_ref[:, w:2 * w] = (dvg_in * dvg).astype(BF16)

        c = c_ref[...]
        dc0 = c - _mean(c)
        rstd_c = lax.rsqrt(_mean(dc0 * dc0) + EPS_LN)
        chat = dc0 * rstd_c
        cn = chat * lbg_ref[...] + lbb_ref[...]
        sg = _sigmoid(cn)
        dcn = dcs_ref[...].astype(F32) * (sg * (1.0 + cn * (1.0 - sg)))
        dlbg_ref[...] += _rowsum(dcn * chat)
        dlbb_ref[...] += _rowsum(dcn)
        dch = dcn * lbg_ref[...]
        dc = rstd_c * (dch - _mean(dch) - chat * _mean(dch * chat))
        dcb_ref[...] += _rowsum(dc)

        a = z_ref[:, 2 * w:3 * w].astype(F32)
        gl = z_ref[:, 3 * w:4 * w].astype(F32)
        sgl = _sigmoid(gl)
        ah = zh_ref[:, 0:w].astype(F32)
        glh = zh_ref[:, w:2 * w].astype(F32)
        first = (r % tiles_per_seq) == 0
        gbuf_ref[0:HALO_B, :] = jnp.where(first, 0.0, ah * _sigmoid(glh))
        gbuf_ref[HALO_B:, :] = a * sgl

        @pl.when(r % tiles_per_seq == tiles_per_seq - 1)
        def _():
            carry_ref[...] = jnp.zeros_like(carry_ref)

        dbuf_ref[0:tm, :] = dc
        dbuf_ref[tm:, :] = carry_ref[...]
        carry_ref[...] = dc[0:HALO_B, :]
        dglu = jnp.zeros((tm, w), F32)
        for k in range(CONV_B):
            off = HALO_B - (CONV_B - 1) + k
            dcw_ref[k:k + 1, :] += _rowsum(dc * gbuf_ref[off:off + tm, :])
            dglu = dglu + cw_ref[k:k + 1, :] * dbuf_ref[CONV_B - 1 - k:CONV_B - 1 - k + tm, :]
        dz_ref[:, 2 * w:3 * w] = (dglu * sgl).astype(BF16)
        dz_ref[:, 3 * w:4 * w] = (dglu * a * sgl * (1.0 - sgl)).astype(BF16)
        dz_ref[:, 4 * w:6 * w] = dgt_ref[...]

        @pl.when(i == n_tiles - 1)
        def _():
            tri = lax.broadcasted_iota(jnp.int32, (CHUNK, CHUNK), 0) >= lax.broadcasted_iota(jnp.int32, (CHUNK, CHUNK), 1)
            lane = lax.broadcasted_iota(jnp.int32, (CHUNK, CHUNK), 1)
            dbs = jnp.zeros((CHUNK, CHUNK), F32)
            for g in range(GROUPS):
                dws_ref[g] = jnp.where(tri, dws_ref[g], 0.0)
                group_sum = jnp.sum(dbs_acc_ref[:, g * CHUNK:(g + 1) * CHUNK], axis=1, keepdims=True)
                dbs = jnp.where(lane == g, group_sum, dbs)
            dbs_ref[...] = dbs

    def rev(i):
        return n_tiles - 1 - i

    def halo_map(i):
        return (jnp.maximum(rev(i) * halo_blocks - 1, 0), 1)

    row = pl.BlockSpec((tm, w), lambda i: (rev(i), 0))
    full = lambda shape: pl.BlockSpec(shape, lambda i: (0,) * len(shape))
    in_specs = [pl.BlockSpec((tm, 4 * w), lambda i: (rev(i), 0)), pl.BlockSpec((HALO_B, 2 * w), halo_map),
                row, row, row, pl.BlockSpec((tm, 2 * w), lambda i: (rev(i), 0)),
                _vec_spec(w), _vec_spec(w), full((GROUPS, CHUNK, CHUNK)), full((GROUPS, CHUNK, CHUNK)), full((CHUNK, w)),
                full((CONV_B, w)), _vec_spec(w), _vec_spec(w)]
    out_specs = [pl.BlockSpec((tm, 6 * w), lambda i: (rev(i), 0)), _vec_spec(w), _vec_spec(w), full((GROUPS, CHUNK, CHUNK)),
                 full((CHUNK, CHUNK)), full((CONV_B, w)), _vec_spec(w), _vec_spec(w), _vec_spec(w)]
    vec = jax.ShapeDtypeStruct((1, w), F32)
    out_shape = [jax.ShapeDtypeStruct((t, 6 * w), BF16), vec, vec, jax.ShapeDtypeStruct((GROUPS, CHUNK, CHUNK), F32),
                 jax.ShapeDtypeStruct((CHUNK, CHUNK), F32), jax.ShapeDtypeStruct((CONV_B, w), F32), vec, vec, vec]
    scratch = [pltpu.VMEM((HALO_B, w), F32), pltpu.VMEM((HALO_B + tm, w), F32), pltpu.VMEM((tm + HALO_B, w), F32),
               pltpu.VMEM((tm, w), F32), pltpu.VMEM((tm, w), F32), pltpu.VMEM((CHUNK, w), F32)]
    return pl.pallas_call(body, grid=(n_tiles,), in_specs=in_specs, out_specs=out_specs, out_shape=out_shape,
                          scratch_shapes=scratch, name="branch_bwd", compiler_params=_params(1))(
        z, z, c_saved, dpa, dcs, dgates, ln_v_g, ln_v_b, ws_m, ws_mt, bias_full, conv_w, ln_b_g, ln_b_b)


def _ffn_mid_fwd(up0, conv_w, conv_b, seq):
    nb, t, f = up0.shape
    half = nb // 2
    tm = min(TM_EW, seq)
    tiles_per_seq = seq // tm

    def body(up_ref, w_ref, b_ref, act_ref, hist_ref, buf_ref):
        i = pl.program_id(0)

        @pl.when(i % tiles_per_seq == 0)
        def _():
            hist_ref[...] = jnp.zeros_like(hist_ref)

        def conv(j):
            x = up_ref[j].astype(F32)
            buf_ref[0:HALO_F, :] = hist_ref[j]
            buf_ref[HALO_F:, :] = x
            hist_ref[j] = x[tm - HALO_F:, :]
            y = jnp.zeros((tm, f), F32) + b_ref[j]
            for k in range(CONV_F):
                off = HALO_F - (CONV_F - 1) + k
                y = y + w_ref[j, k:k + 1, :] * buf_ref[off:off + tm, :]
            return y

        for j in range(half):
            act_ref[j] = (_gelu(conv(j)) * conv(j + half)).astype(BF16)

    return pl.pallas_call(
        body, grid=(t // tm,),
        in_specs=[pl.BlockSpec((nb, tm, f), lambda i: (0, i, 0)), pl.BlockSpec((nb, CONV_F, f), lambda i: (0, 0, 0)),
                  pl.BlockSpec((nb, 1, f), lambda i: (0, 0, 0))],
        out_specs=pl.BlockSpec((half, tm, f), lambda i: (0, i, 0)),
        out_shape=jax.ShapeDtypeStruct((half, t, f), BF16),
        scratch_shapes=[pltpu.VMEM((nb, HALO_F, f), F32), pltpu.VMEM((HALO_F + tm, f), F32)],
        name="ffn_mid_fwd", compiler_params=_params(1))(up0, conv_w, conv_b)


def _ffn_mid_bwd(up0, dact, conv_w, conv_b, seq):
    nb, t, f = up0.shape
    half = nb // 2
    tm = min(TM_EW, seq)
    tiles_per_seq = seq // tm
    n_tiles = t // tm
    halo_blocks = tm // HALO_F

    def body(up_ref, uph_ref, da_ref, w_ref, b_ref, dup_ref, dw_ref, db_ref, carry_ref, xbuf_ref, dbuf_ref):
        i = pl.program_id(0)
        r = n_tiles - 1 - i
        first = (r % tiles_per_seq) == 0

        @pl.when(i == 0)
        def _():
            dw_ref[...] = jnp.zeros_like(dw_ref)
            db_ref[...] = jnp.zeros_like(db_ref)

        @pl.when(r % tiles_per_seq == tiles_per_seq - 1)
        def _():
            carry_ref[...] = jnp.zeros_like(carry_ref)

        def conv(j):
            xbuf_ref[0:HALO_F, :] = jnp.where(first, 0.0, uph_ref[j].astype(F32))
            xbuf_ref[HALO_F:, :] = up_ref[j].astype(F32)
            y = jnp.zeros((tm, f), F32) + b_ref[j]
            for k in range(CONV_F):
                off = HALO_F - (CONV_F - 1) + k
                y = y + w_ref[j, k:k + 1, :] * xbuf_ref[off:off + tm, :]
            return y

        def back(j, d):
            db_ref[j] += _rowsum(d)
            dbuf_ref[0:tm, :] = d
            dbuf_ref[tm:, :] = carry_ref[j]
            carry_ref[j] = d[0:HALO_F, :]
            dx = jnp.zeros((tm, f), F32)
            for k in range(CONV_F):
                off = HALO_F - (CONV_F - 1) + k
                dw_ref[j, k:k + 1, :] += _rowsum(d * xbuf_ref[off:off + tm, :])
                dx = dx + w_ref[j, k:k + 1, :] * dbuf_ref[CONV_F - 1 - k:CONV_F - 1 - k + tm, :]
            dup_ref[j] = dx.astype(BF16)

        for j in range(half):
            val = conv(j + half)
            gate = conv(j)
            gg, dgg = _gelu_and_grad(gate)
            da = da_ref[j].astype(F32)
            back(j, da * val * dgg)
            conv(j + half)
            back(j + half, da * gg)

    def rev(i):
        return n_tiles - 1 - i

    return pl.pallas_call(
        body, grid=(n_tiles,),
        in_specs=[pl.BlockSpec((nb, tm, f), lambda i: (0, rev(i), 0)),
                  pl.BlockSpec((nb, HALO_F, f), lambda i: (0, jnp.maximum(rev(i) * halo_blocks - 1, 0), 0)),
                  pl.BlockSpec((half, tm, f), lambda i: (0, rev(i), 0)),
                  pl.BlockSpec((nb, CONV_F, f), lambda i: (0, 0, 0)), pl.BlockSpec((nb, 1, f), lambda i: (0, 0, 0))],
        out_specs=[pl.BlockSpec((nb, tm, f), lambda i: (0, rev(i), 0)), pl.BlockSpec((nb, CONV_F, f), lambda i: (0, 0, 0)),
                   pl.BlockSpec((nb, 1, f), lambda i: (0, 0, 0))],
        out_shape=[jax.ShapeDtypeStruct((nb, t, f), BF16), jax.ShapeDtypeStruct((nb, CONV_F, f), F32),
                   jax.ShapeDtypeStruct((nb, 1, f), F32)],
        scratch_shapes=[pltpu.VMEM((nb, HALO_F, f), F32), pltpu.VMEM((HALO_F + tm, f), F32), pltpu.VMEM((tm + HALO_F, f), F32)],
        name="ffn_mid_bwd", compiler_params=_params(1))(up0, up0, dact, conv_w, conv_b)


def _head(x2, q, r, target, g_ple, g_final):
    t, d = x2.shape
    tm = min(TM_EW * 2, t)

    def body(x2_ref, q_ref, r_ref, tg_ref, gple_ref, gfin_ref, dx3_ref, dq_ref, dr_ref, loss_ref, dgfin_ref, dgple_ref):
        i = pl.program_id(0)
        pg = _sigmoid(q_ref[...])
        rv = r_ref[...]
        rstd_r = lax.rsqrt(_mean(rv * rv) + EPS_RMS)
        nr = rv * rstd_r
        pe = nr * gple_ref[...]
        x3 = x2_ref[...] + pe * pg
        rstd3 = lax.rsqrt(_mean(x3 * x3) + EPS_RMS)
        n3 = x3 * rstd3
        err = n3 * gfin_ref[...] - tg_ref[...]
        loss_part = jnp.sum(_rowsum(err * err), axis=1, keepdims=True) * (0.5 / d)
        dy = err * (1.0 / d)
        dn3 = dy * gfin_ref[...]
        dx3 = rstd3 * (dn3 - n3 * _mean(dn3 * n3))
        dx3_ref[...] = dx3
        dq_ref[...] = (dx3 * pe * pg * (1.0 - pg)).astype(BF16)
        dpe = dx3 * pg
        dnr = dpe * gple_ref[...]
        dr_ref[...] = (rstd_r * (dnr - nr * _mean(dnr * nr))).astype(BF16)
        dgfin = _rowsum(dy * n3)
        dgple = _rowsum(dpe * nr)

        @pl.when(i == 0)
        def _():
            loss_ref[...] = jnp.zeros_like(loss_ref) + loss_part
            dgfin_ref[...] = dgfin
            dgple_ref[...] = dgple

        @pl.when(i > 0)
        def _():
            loss_ref[...] += loss_part
            dgfin_ref[...] += dgfin
            dgple_ref[...] += dgple

    row = pl.BlockSpec((tm, d), lambda i: (i, 0))
    vec = jax.ShapeDtypeStruct((1, d), F32)
    return pl.pallas_call(
        body, grid=(t // tm,), in_specs=[row, row, row, row, _vec_spec(d), _vec_spec(d)],
        out_specs=[row, row, row, _vec_spec(d), _vec_spec(d), _vec_spec(d)],
        out_shape=[jax.ShapeDtypeStruct((t, d), F32), jax.ShapeDtypeStruct((t, d), BF16), jax.ShapeDtypeStruct((t, d), BF16),
                   vec, vec, vec],
        name="head", compiler_params=_params(1))(x2, q, r, target, g_ple, g_final)


def _all_gather(xs):
    n = len(xs)

    def body(*refs):
        x_refs, o_refs = refs[:n], refs[n:2 * n]
        send_sems, recv_sems, local_sems = refs[2 * n:]
        x, y, c = lax.axis_index("x"), lax.axis_index("y"), lax.axis_index("c")
        me, sibling = (x, y, c), (x, y, 1 - c)
        chips = [(1 - x, y), (x, 1 - y), (1 - x, 1 - y)]

        def slot(pos):
            return 4 * pos[0] + 2 * pos[1] + pos[2]

        def copy(a, k, block, to, src=None):
            dst = o_refs[a].at[slot(block)]
            return pltpu.make_async_remote_copy(
                src_ref=dst if src is None else src, dst_ref=dst, send_sem=send_sems.at[a * 7 + k],
                recv_sem=recv_sems.at[a * 7 + k], device_id=to, device_id_type=MESH)

        mine = [pltpu.make_async_copy(x_refs[a], o_refs[a].at[slot(me)], local_sems.at[a]) for a in range(n)]
        for cp in mine:
            cp.start()
        first = []
        for a in range(n):
            first.append(copy(a, 0, me, sibling, src=x_refs[a]))
            first += [copy(a, 1 + j, me, (*chip, c), src=x_refs[a]) for j, chip in enumerate(chips)]
        for cp in first:
            cp.start()
        passed = []
        for j, chip in enumerate(chips):
            for a in range(n):
                copy(a, 1 + j, (*chip, c), me).wait_recv()
                cp = copy(a, 4 + j, (*chip, c), sibling)
                cp.start()
                passed.append(cp)
        for a in range(n):
            copy(a, 0, sibling, me).wait_recv()
        for j, chip in enumerate(chips):
            for a in range(n):
                copy(a, 4 + j, (*chip, 1 - c), me).wait_recv()
        for cp in first + passed:
            cp.wait_send()
        for cp in mine:
            cp.wait()

    return pl.pallas_call(
        body, in_specs=[HBM_SPEC] * n, out_specs=[HBM_SPEC] * n,
        out_shape=[jax.ShapeDtypeStruct((N_DEV,) + v.shape, v.dtype) for v in xs],
        scratch_shapes=[pltpu.SemaphoreType.DMA((7 * n,)), pltpu.SemaphoreType.DMA((7 * n,)), pltpu.SemaphoreType.DMA((n,))],
        name="all_gather_weights")(*xs)


def _grad_exchange(scatter, bcast):
    ns, n = len(scatter), len(scatter) + len(bcast)
    xs = list(scatter) + list(bcast)

    def body(*refs):
        x_refs, o_refs = refs[:n], refs[n:2 * n]
        send_sems, recv_sems, local_sems = refs[2 * n:]
        x, y, c = lax.axis_index("x"), lax.axis_index("y"), lax.axis_index("c")
        me = 4 * x + 2 * y + c

        def src(a, to_slot):
            return x_refs[a].at[to_slot] if a < ns else x_refs[a]

        mine = [pltpu.make_async_copy(src(a, me), o_refs[a].at[me], local_sems.at[a]) for a in range(n)]
        for cp in mine:
            cp.start()
        sends, recvs = [], []
        for m in range(1, N_DEV):
            mx, my, mc = (m >> 2) & 1, (m >> 1) & 1, m & 1
            px, py, pc = (1 - x if mx else x), (1 - y if my else y), (1 - c if mc else c)
            peer = 4 * px + 2 * py + pc
            for a in range(n):
                k = a * 7 + m - 1
                sends.append(pltpu.make_async_remote_copy(
                    src_ref=src(a, peer), dst_ref=o_refs[a].at[me], send_sem=send_sems.at[k], recv_sem=recv_sems.at[k],
                    device_id=(px, py, pc), device_id_type=MESH))
                recvs.append(pltpu.make_async_remote_copy(
                    src_ref=src(a, peer), dst_ref=o_refs[a].at[peer], send_sem=send_sems.at[k], recv_sem=recv_sems.at[k],
                    device_id=(px, py, pc), device_id_type=MESH))
        for cp in sends:
            cp.start()
        for cp in recvs:
            cp.wait_recv()
        for cp in sends:
            cp.wait_send()
        for cp in mine:
            cp.wait()

    out_shape = [jax.ShapeDtypeStruct(v.shape if a < ns else (N_DEV,) + v.shape, v.dtype) for a, v in enumerate(xs)]
    return pl.pallas_call(
        body, in_specs=[HBM_SPEC] * n, out_specs=[HBM_SPEC] * n, out_shape=out_shape,
        scratch_shapes=[pltpu.SemaphoreType.DMA((7 * n,)), pltpu.SemaphoreType.DMA((7 * n,)), pltpu.SemaphoreType.DMA((n,))],
        name="grad_exchange")(*xs)


def _adamw(name, parts, w, m, v, rows_per_step):
    r, c = w.shape
    tr = r if r <= rows_per_step else (rows_per_step if r % rows_per_step == 0 else r // 2)
    c1 = 1.0 - ADAM_B1 ** ADAM_STEP
    c2 = 1.0 - ADAM_B2 ** ADAM_STEP

    def body(p_ref, w_ref, m_ref, v_ref, g_ref, d_ref, mo_ref, vo_ref):
        g = p_ref[0].astype(F32)
        for s in range(1, N_DEV):
            g = g + p_ref[s].astype(F32)
        m_new = ADAM_B1 * m_ref[...] + (1.0 - ADAM_B1) * g
        v_new = ADAM_B2 * v_ref[...] + (1.0 - ADAM_B2) * (g * g)
        g_ref[...] = g
        mo_ref[...] = m_new
        vo_ref[...] = v_new
        d_ref[...] = -ADAM_LR * ((m_new / c1) / (jnp.sqrt(v_new / c2) + ADAM_EPS) + ADAM_WD * w_ref[...])

    row = pl.BlockSpec((tr, c), lambda i: (i, 0))
    out = jax.ShapeDtypeStruct((r, c), F32)
    return pl.pallas_call(body, grid=(r // tr,), in_specs=[pl.BlockSpec((N_DEV, tr, c), lambda i: (0, i, 0)), row, row, row],
                          out_specs=[row, row, row, row], out_shape=[out, out, out, out], name=name,
                          compiler_params=_params(1))(parts, w, m, v)


_VEC_NAMES = ("g_mix", "ln_v_g", "ln_v_b", "b_s", "conv_b_b", "ln_b_g", "ln_b_b", "g_ffn", "g_pg", "g_ple", "g_final")


def _pack_replicated(d, loss=None):
    def rows8(vec):
        return jnp.pad(vec.reshape(1, 1024), ((0, 7), (0, 0)))

    parts = [rows8(d[k]) for k in _VEC_NAMES]
    fb = jnp.pad(d["ffn_conv_b"].reshape(-1), (0, 6144 - 5632)).reshape(6, 1024)
    parts.append(jnp.pad(fb, ((0, 2), (0, 0))))
    parts.append(d["w_s"].reshape(128, 1024))
    lo = jnp.zeros((8, 1024), F32) if loss is None else jnp.pad(loss.reshape(1, 1), ((0, 7), (0, 1023)))
    parts.append(lo)
    return jnp.concatenate(parts, axis=0)


def _unpack_replicated(pk, shapes):
    out = {}
    for i, k in enumerate(_VEC_NAMES):
        out[k] = pk[8 * i, :].reshape(shapes[k])
    base = 8 * len(_VEC_NAMES)
    out["ffn_conv_b"] = pk[base:base + 6, :].reshape(-1)[:5632].reshape(shapes["ffn_conv_b"])
    out["w_s"] = pk[base + 8:base + 136, :].reshape(shapes["w_s"])
    out["loss"] = pk[base + 136, 0]
    return out


def _pack_sharded_small(conv_b_w, ffn_conv_w):
    lead = conv_b_w.shape[:-2]
    pad0 = [(0, 0)] * len(lead)
    a = jnp.pad(conv_b_w, pad0 + [(0, 1), (0, 0)])
    b = jnp.pad(ffn_conv_w.reshape(lead + (CONV_F * 704,)), pad0 + [(0, 24 * 128 - CONV_F * 704)]).reshape(lead + (24, 128))
    return jnp.concatenate([a, b], axis=-2)


def _unpack_sharded_small(pk):
    lead = pk.shape[:-2]
    conv_b_w = pk[..., 0:CONV_B, :]
    ffn = pk[..., 32:56, :].reshape(lead + (24 * 128,))[..., :CONV_F * 704].reshape(lead + (CONV_F, 704))
    return conv_b_w, ffn


_WEIGHTS = ("g_mix", "w_in", "ln_v_g", "ln_v_b", "w_s", "b_s", "w_a_out", "conv_b_w", "conv_b_b", "ln_b_g", "ln_b_b",
            "w_b_out", "w_o", "g_ffn", "w_up", "ffn_conv_w", "ffn_conv_b", "w_down", "g_pg", "w_pg", "w_ple", "g_ple",
            "g_final")
_BIG = ("w_in", "w_a_out", "w_b_out", "w_o", "w_up", "w_down", "w_pg", "w_ple")


def kernel(x, p, g_mix, w_in, ln_v_g, ln_v_b, w_s, b_s, w_a_out, conv_b_w, conv_b_b, ln_b_g, ln_b_b, w_b_out, w_o, g_ffn, w_up, ffn_conv_w, ffn_conv_b, w_down, g_pg, w_pg, w_ple, g_ple, g_final, loss_target, m_g_mix, m_w_in, m_ln_v_g, m_ln_v_b, m_w_s, m_b_s, m_w_a_out, m_conv_b_w, m_conv_b_b, m_ln_b_g, m_ln_b_b, m_w_b_out, m_w_o, m_g_ffn, m_w_up, m_ffn_conv_w, m_ffn_conv_b, m_w_down, m_g_pg, m_w_pg, m_w_ple, m_g_ple, m_g_final, v_g_mix, v_w_in, v_ln_v_g, v_ln_v_b, v_w_s, v_b_s, v_w_a_out, v_conv_b_w, v_conv_b_b, v_ln_b_g, v_ln_b_b, v_w_b_out, v_w_o, v_g_ffn, v_w_up, v_ffn_conv_w, v_ffn_conv_b, v_w_down, v_g_pg, v_w_pg, v_w_ple, v_g_ple, v_g_final):
    local = dict(locals())
    wts = {k: local[k] for k in _WEIGHTS}
    mom = {k: local["m_" + k] for k in _WEIGHTS}
    var = {k: local["v_" + k] for k in _WEIGHTS}
    shapes = {k: wts[k].shape for k in _WEIGHTS}

    bsz, seq, d = x.shape
    t = bsz * seq
    x0 = x.reshape(t, d)
    p0 = p.reshape(t, p.shape[-1])
    target = loss_target.reshape(t, d)
    tm = min(TM_MM, t)
    tt = min(TT_MM, t)
    n_row = t // tm
    n_tok = t // tt

    def sq(a):
        return a.reshape(a.shape[1:])

    shard_bf16 = [sq(wts[k]).astype(BF16) for k in _BIG]
    small_shard = _pack_sharded_small(sq(conv_b_w), sq(ffn_conv_w))
    gathered = _all_gather(shard_bf16 + [small_shard])
    w_in3, wa3, wb3, wo3, w_up3, wd3, wpg3, wple3 = gathered[:8]
    conv_b_w8, ffn_conv_w8 = _unpack_sharded_small(gathered[8])
    conv_w_full = conv_b_w8.transpose(1, 0, 2).reshape(CONV_B, N_DEV * conv_b_w8.shape[-1])
    n_in = w_in3.shape[2]
    f_blk = w_up3.shape[2]
    w_a = wa3.reshape(-1, d)
    w_b = wb3.reshape(-1, d)
    w_om = wo3.reshape(-1, d)
    w_pgm = wpg3.reshape(-1, d)
    w_d4 = wd3.reshape(N_DEV // 2, f_blk, d)
    w_plem = wple3.transpose(1, 0, 2).reshape(wple3.shape[1], d)

    ws_m = jnp.where(jnp.tril(jnp.ones((CHUNK, CHUNK), bool))[None], sq(w_s), 0.0).astype(BF16)
    ws_mt = jnp.swapaxes(ws_m, 1, 2)
    bias_full = jnp.broadcast_to(sq(b_s).T[:, :, None], (CHUNK, GROUPS, CHUNK)).reshape(CHUNK, GROUPS * CHUNK)
    ffn_b8 = ffn_conv_b.reshape(N_DEV, 1, f_blk)

    h1 = _rms_fwd("rms_mix", x0, g_mix)
    z = _matmul("mm_in", h1, w_in3, dims=_NN, grid=(N_DEV, n_row, 1),
                a_spec=pl.BlockSpec((tm, d), lambda j, i, k: (i, 0)),
                b_spec=pl.BlockSpec((None, d, n_in), lambda j, i, k: (j, 0, 0)),
                o_spec=pl.BlockSpec((tm, n_in), lambda j, i, k: (i, j)), acc_shape=(tm, n_in),
                out_shape=jax.ShapeDtypeStruct((t, N_DEV * n_in), BF16))
    pa, cs, c_saved = _branch_fwd(z, ln_v_g, ln_v_b, ws_m, bias_full, conv_w_full, conv_b_b, ln_b_g, ln_b_b, seq)
    ya = _mm_rows("mm_a_out", pa, w_a, dims=_NN, out_dtype=BF16)
    yb = _mm_rows("mm_b_out", cs, w_b, dims=_NN, out_dtype=BF16)
    merged = _merge_fwd(z, ya, yb)
    x1 = _mm_rows("mm_o", merged, w_om, dims=_NN, out_dtype=F32, res=x0)
    h2 = _rms_fwd("rms_ffn", x1, g_ffn)
    up0 = _matmul("mm_up", h2, w_up3, dims=_NN, grid=(N_DEV, n_row, 1),
                  a_spec=pl.BlockSpec((tm, d), lambda j, i, k: (i, 0)),
                  b_spec=pl.BlockSpec((None, d, f_blk), lambda j, i, k: (j, 0, 0)),
                  o_spec=pl.BlockSpec((None, tm, f_blk), lambda j, i, k: (j, i, 0)), acc_shape=(tm, f_blk),
                  out_shape=jax.ShapeDtypeStruct((N_DEV, t, f_blk), BF16))
    act = _ffn_mid_fwd(up0, ffn_conv_w8, ffn_b8, seq)
    row_spec = pl.BlockSpec((tm, d), lambda i, j, k: (i, 0))
    x2 = _matmul("mm_down", act, w_d4, dims=_NN, grid=(n_row, 1, N_DEV // 2),
                 a_spec=pl.BlockSpec((None, tm, f_blk), lambda i, j, k: (k, i, 0)),
                 b_spec=pl.BlockSpec((None, f_blk, d), lambda i, j, k: (k, 0, 0)),
                 o_spec=row_spec, acc_shape=(tm, d), out_shape=jax.ShapeDtypeStruct((t, d), F32),
                 res=x1, res_spec=row_spec)
    hq = _rms_fwd("rms_pg", x2, g_pg)
    q = _mm_rows("mm_pg", hq, w_pgm, dims=_NN, out_dtype=F32)
    r = _mm_rows("mm_ple", p0, w_plem, dims=_NN, out_dtype=F32)

    dx3, dq, dr, loss_v, dg_final, dg_ple = _head(x2, q, r, target, g_ple, g_final.reshape(1, d))

    gw = {}
    gw["w_pg"] = _mm_wgrad("wg_pg", hq, dq, out_dtype=BF16).reshape(wpg3.shape)
    dw_ple = _mm_wgrad("wg_ple", p0, dr, out_dtype=BF16)
    gw["w_ple"] = dw_ple.reshape(dw_ple.shape[0], N_DEV, -1).transpose(1, 0, 2)
    dhq = _mm_rows("mm_pg_t", dq, w_pgm, dims=_NT, out_dtype=F32)
    dx2, dx2b, dg_pg = _rms_bwd("rms_pg_bwd", x2, g_pg, dhq, dx3, want_bf16=True)

    dact = _matmul("mm_down_t", dx2b, w_d4, dims=_NT, grid=(N_DEV // 2, n_row, 1),
                   a_spec=pl.BlockSpec((tm, d), lambda j, i, k: (i, 0)),
                   b_spec=pl.BlockSpec((None, f_blk, d), lambda j, i, k: (j, 0, 0)),
                   o_spec=pl.BlockSpec((None, tm, f_blk), lambda j, i, k: (j, i, 0)), acc_shape=(tm, f_blk),
                   out_shape=jax.ShapeDtypeStruct((N_DEV // 2, t, f_blk), BF16))
    gw["w_down"] = _matmul("wg_down", act, dx2b, dims=_TN, grid=(N_DEV // 2, 1, n_tok),
                           a_spec=pl.BlockSpec((None, tt, f_blk), lambda j, i, k: (j, k, 0)),
                           b_spec=pl.BlockSpec((tt, d), lambda j, i, k: (k, 0)),
                           o_spec=pl.BlockSpec((None, f_blk, d), lambda j, i, k: (j, 0, 0)), acc_shape=(f_blk, d),
                           out_shape=jax.ShapeDtypeStruct((N_DEV // 2, f_blk, d), BF16)).reshape(wd3.shape)
    d_up0, dffn_w8, dffn_b8 = _ffn_mid_bwd(up0, dact, ffn_conv_w8, ffn_b8, seq)
    gw["w_up"] = _matmul("wg_up", h2, d_up0, dims=_TN, grid=(N_DEV, 1, n_tok),
                         a_spec=pl.BlockSpec((tt, d), lambda j, i, k: (k, 0)),
                         b_spec=pl.BlockSpec((None, tt, f_blk), lambda j, i, k: (j, k, 0)),
                         o_spec=pl.BlockSpec((None, d, f_blk), lambda j, i, k: (j, 0, 0)), acc_shape=(d, f_blk),
                         out_shape=jax.ShapeDtypeStruct((N_DEV, d, f_blk), BF16))
    dh2 = _matmul("mm_up_t", d_up0, w_up3, dims=_NT, grid=(n_row, 1, N_DEV),
                  a_spec=pl.BlockSpec((None, tm, f_blk), lambda i, j, k: (k, i, 0)),
                  b_spec=pl.BlockSpec((None, d, f_blk), lambda i, j, k: (k, 0, 0)),
                  o_spec=row_spec, acc_shape=(tm, d), out_shape=jax.ShapeDtypeStruct((t, d), F32))
    dx1, dx1b, dg_ffn = _rms_bwd("rms_ffn_bwd", x1, g_ffn, dh2, dx2, want_bf16=True)

    dmerged = _mm_rows("mm_o_t", dx1b, w_om, dims=_NT, out_dtype=BF16)
    gw["w_o"] = _mm_wgrad("wg_o", merged, dx1b, out_dtype=BF16).reshape(wo3.shape)
    dya, dyb, dgates = _merge_bwd(z, ya, yb, dmerged)
    dpa = _mm_rows("mm_a_out_t", dya, w_a, dims=_NT, out_dtype=BF16)
    dcs = _mm_rows("mm_b_out_t", dyb, w_b, dims=_NT, out_dtype=BF16)
    gw["w_a_out"] = _mm_wgrad("wg_a_out", pa, dya, out_dtype=BF16).reshape(wa3.shape)
    gw["w_b_out"] = _mm_wgrad("wg_b_out", cs, dyb, out_dtype=BF16).reshape(wb3.shape)
    dz, dlvg, dlvb, dws, dbs_full, dconv_w, dconv_b, dlbg, dlbb = _branch_bwd(
        z, c_saved, dpa, dcs, dgates, ln_v_g, ln_v_b, ws_m, ws_mt, bias_full, conv_w_full, ln_b_g, ln_b_b, seq)
    gw["w_in"] = _matmul("wg_in", h1, dz, dims=_TN, grid=(N_DEV, 1, n_tok),
                         a_spec=pl.BlockSpec((tt, d), lambda j, i, k: (k, 0)),
                         b_spec=pl.BlockSpec((tt, n_in), lambda j, i, k: (k, j)),
                         o_spec=pl.BlockSpec((None, d, n_in), lambda j, i, k: (j, 0, 0)), acc_shape=(d, n_in),
                         out_shape=jax.ShapeDtypeStruct((N_DEV, d, n_in), BF16))
    dh1 = _matmul("mm_in_t", dz, w_in3, dims=_NT, grid=(n_row, 1, N_DEV),
                  a_spec=pl.BlockSpec((tm, n_in), lambda i, j, k: (i, k)),
                  b_spec=pl.BlockSpec((None, d, n_in), lambda i, j, k: (k, 0, 0)),
                  o_spec=row_spec, acc_shape=(tm, d), out_shape=jax.ShapeDtypeStruct((t, d), F32))
    grad_x, dg_mix = _rms_bwd("rms_mix_bwd", x0, g_mix, dh1, dx1, want_bf16=False)

    db_s = dbs_full[:, :GROUPS].T
    rep_partial = _pack_replicated(
        dict(g_mix=dg_mix, ln_v_g=dlvg, ln_v_b=dlvb, b_s=db_s, conv_b_b=dconv_b, ln_b_g=dlbg, ln_b_b=dlbb, g_ffn=dg_ffn,
             g_pg=dg_pg, g_ple=dg_ple, g_final=dg_final, ffn_conv_b=dffn_b8, w_s=dws), loss=loss_v[0, 0])
    dconv_w8 = dconv_w.reshape(CONV_B, N_DEV, -1).transpose(1, 0, 2)
    small_partial = _pack_sharded_small(dconv_w8, dffn_w8)
    received = _grad_exchange([gw[k] for k in _BIG] + [small_partial], [rep_partial])

    grads, deltas, new_m, new_v = {}, {}, {}, {}

    def two_d(a):
        a = sq(a)
        return a.reshape(-1, a.shape[-1])

    for idx, k in enumerate(_BIG):
        parts = received[idx]
        parts = parts.reshape(N_DEV, -1, parts.shape[-1])
        outs = _adamw("adamw_" + k, parts, two_d(wts[k]), two_d(mom[k]), two_d(var[k]), 128)
        grads[k], deltas[k], new_m[k], new_v[k] = [o.reshape(shapes[k]) for o in outs]

    small = [_pack_sharded_small(sq(a[0]), sq(a[1])) for a in
             ((wts["conv_b_w"], wts["ffn_conv_w"]), (mom["conv_b_w"], mom["ffn_conv_w"]), (var["conv_b_w"], var["ffn_conv_w"]))]
    outs = _adamw("adamw_conv", received[8], small[0], small[1], small[2], 56)
    for name, o in zip(("g", "d", "m", "v"), outs):
        cw, fw = _unpack_sharded_small(o)
        tgt = {"g": grads, "d": deltas, "m": new_m, "v": new_v}[name]
        tgt["conv_b_w"] = cw.reshape(shapes["conv_b_w"])
        tgt["ffn_conv_w"] = fw.reshape(shapes["ffn_conv_w"])

    rep = [_pack_replicated(s) for s in (wts, mom, var)]
    outs = _adamw("adamw_replicated", received[9], rep[0], rep[1], rep[2], 240)
    loss = None
    for name, o in zip(("g", "d", "m", "v"), outs):
        un = _unpack_replicated(o, shapes)
        tgt = {"g": grads, "d": deltas, "m": new_m, "v": new_v}[name]
        if name == "g":
            loss = un["loss"]
        for k in _VEC_NAMES + ("ffn_conv_b", "w_s"):
            tgt[k] = un[k]

    return (loss, grad_x.reshape(x.shape), *[grads[k] for k in _WEIGHTS], *[deltas[k] for k in _WEIGHTS],
            *[new_m[k] for k in _WEIGHTS], *[new_v[k] for k in _WEIGHTS])
```

```python
import math

import jax
import jax.numpy as jnp
from jax import lax
from jax.experimental import pallas as pl
from jax.experimental.pallas import tpu as pltpu

F32 = jnp.float32
BF16 = jnp.bfloat16

N_DEV = 8
EPS_RMS = 1e-6
EPS_LN = 1e-5
CHUNK = 128
GROUPS = 8
CONV_B = 31
CONV_F = 3
HALO_B = 32
HALO_F = 8
ROWS_F = 16

ADAM_LR = 0.001
ADAM_B1 = 0.9
ADAM_B2 = 0.999
ADAM_EPS = 1e-08
ADAM_WD = 0.01
ADAM_STEP = 10

VMEM_LIMIT = 56 * 1024 * 1024
TM_MM = 1024
TT_MM = 512
TM_EW = 256

_NN = (((1,), (0,)), ((), ()))
_NT = (((1,), (1,)), ((), ()))
_TN = (((0,), (0,)), ((), ()))
MESH = pl.DeviceIdType.MESH
HBM_SPEC = pl.BlockSpec(memory_space=pltpu.HBM)


def _params(n_axes):
    return pltpu.CompilerParams(dimension_semantics=("arbitrary",) * n_axes, vmem_limit_bytes=VMEM_LIMIT)


def _gelu(x):
    k = math.sqrt(2.0 / math.pi)
    return 0.5 * x * (1.0 + jnp.tanh(k * (x + 0.044715 * (x * x * x))))


def _gelu_and_grad(x):
    k = math.sqrt(2.0 / math.pi)
    x2 = x * x
    t = jnp.tanh(k * (x + 0.044715 * (x2 * x)))
    g = 0.5 * x * (1.0 + t)
    dg = 0.5 * (1.0 + t) + 0.5 * x * (1.0 - t * t) * (k * (1.0 + 3.0 * 0.044715 * x2))
    return g, dg


def _sigmoid(x):
    return 1.0 / (1.0 + jnp.exp(-x))


def _rowsum(x):
    return jnp.sum(x, axis=0, keepdims=True)


def _mean(x):
    return jnp.mean(x, axis=-1, keepdims=True)


def _exchange_io(exch):
    n = len(exch)
    out_shape = [jax.ShapeDtypeStruct(v.shape if kind == "scatter" else (N_DEV,) + v.shape, v.dtype) for kind, v in exch]
    scratch = [pltpu.SemaphoreType.DMA((7 * n,)), pltpu.SemaphoreType.DMA((7 * n,)), pltpu.SemaphoreType.DMA((n,))] if n else []
    return [HBM_SPEC] * n, [HBM_SPEC] * n, out_shape, scratch


def _exchange_step(kinds, x_refs, o_refs, send_sems, recv_sems, local_sems):
    n = len(kinds)
    x, y, c = lax.axis_index("x"), lax.axis_index("y"), lax.axis_index("c")
    me = 4 * x + 2 * y + c

    def src(a, to_slot):
        return x_refs[a].at[to_slot] if kinds[a] == "scatter" else x_refs[a]

    mine = [pltpu.make_async_copy(src(a, me), o_refs[a].at[me], local_sems.at[a]) for a in range(n)]
    sends, recvs = [], []
    for m in range(1, N_DEV):
        mx, my, mc = (m >> 2) & 1, (m >> 1) & 1, m & 1
        px, py, pc = (1 - x if mx else x), (1 - y if my else y), (1 - c if mc else c)
        peer = 4 * px + 2 * py + pc
        for a in range(n):
            k = a * 7 + m - 1
            sends.append(pltpu.make_async_remote_copy(
                src_ref=src(a, peer), dst_ref=o_refs[a].at[me], send_sem=send_sems.at[k], recv_sem=recv_sems.at[k],
                device_id=(px, py, pc), device_id_type=MESH))
            recvs.append(pltpu.make_async_remote_copy(
                src_ref=src(a, peer), dst_ref=o_refs[a].at[peer], send_sem=send_sems.at[k], recv_sem=recv_sems.at[k],
                device_id=(px, py, pc), device_id_type=MESH))

    def start():
        for cp in mine + sends:
            cp.start()

    def finish():
        for cp in recvs:
            cp.wait_recv()
        for cp in sends:
            cp.wait_send()
        for cp in mine:
            cp.wait()

    return start, finish


def _exchange(name, exch):
    n = len(exch)
    kinds = [k for k, _ in exch]
    in_specs, out_specs, out_shape, scratch = _exchange_io(exch)

    def body(*refs):
        start, finish = _exchange_step(kinds, refs[:n], refs[n:2 * n], *refs[2 * n:])
        start()
        finish()

    return pl.pallas_call(body, in_specs=in_specs, out_specs=out_specs, out_shape=out_shape, scratch_shapes=scratch,
                          name=name)(*[v for _, v in exch])


def _matmul(name, a, b, *, dims, grid, a_spec, b_spec, o_spec, acc_shape, out_shape, res=None, res_spec=None, exch=()):
    nk = grid[2]
    has_res = res is not None
    n_in = 3 if has_res else 2
    n_ex = len(exch)
    kinds = [k for k, _ in exch]
    ex_in, ex_out, ex_shape, ex_scratch = _exchange_io(exch)

    def body(*refs):
        if has_res:
            a_ref, b_ref, r_ref = refs[:3]
        else:
            a_ref, b_ref = refs[:2]
            r_ref = None
        o_ref = refs[n_in + n_ex]
        if n_ex:
            pid = [pl.program_id(ax) for ax in range(3)]
            ex_start, ex_finish = _exchange_step(kinds, refs[n_in:n_in + n_ex], refs[n_in + n_ex + 1:n_in + 2 * n_ex + 1],
                                                 *refs[len(refs) - 3:])
            pl.when((pid[0] == 0) & (pid[1] == 0) & (pid[2] == 0))(ex_start)
        part = lax.dot_general(a_ref[...].astype(BF16), b_ref[...].astype(BF16), dims, preferred_element_type=F32)

        def finish(acc):
            if has_res:
                acc = acc + r_ref[...]
            o_ref[...] = acc.astype(o_ref.dtype)

        if nk == 1:
            finish(part)
        else:
            acc_ref = refs[n_in + 2 * n_ex + 1]
            k = pl.program_id(2)

            @pl.when(k == 0)
            def _():
                acc_ref[...] = part

            @pl.when(k > 0)
            def _():
                acc_ref[...] += part

            @pl.when(k == nk - 1)
            def _():
                finish(acc_ref[...])
        if n_ex:
            pl.when((pid[0] == grid[0] - 1) & (pid[1] == grid[1] - 1) & (pid[2] == grid[2] - 1))(ex_finish)

    in_specs = [a_spec, b_spec] + ([res_spec] if has_res else []) + ex_in
    args = (a, b) + ((res,) if has_res else ()) + tuple(v for _, v in exch)
    scratch = ([pltpu.VMEM(acc_shape, F32)] if nk > 1 else []) + ex_scratch
    outs = pl.pallas_call(body, grid=grid, in_specs=in_specs, out_specs=[o_spec] + ex_out, out_shape=[out_shape] + ex_shape,
                          scratch_shapes=scratch, name=name, compiler_params=_params(3))(*args)
    return (outs[0], outs[1:]) if n_ex else outs[0]


def _mm_rows(name, a, w, *, dims, out_dtype, res=None):
    t, k = a.shape
    n = w.shape[1] if dims == _NN else w.shape[0]
    tm = min(TM_MM, t)
    row = pl.BlockSpec((tm, n), lambda i, j, kk: (i, 0))
    return _matmul(name, a, w, dims=dims, grid=(t // tm, 1, 1),
                   a_spec=pl.BlockSpec((tm, k), lambda i, j, kk: (i, 0)),
                   b_spec=pl.BlockSpec(w.shape, lambda i, j, kk: (0, 0)),
                   o_spec=row, acc_shape=(tm, n), out_shape=jax.ShapeDtypeStruct((t, n), out_dtype),
                   res=res, res_spec=row if res is not None else None)


def _mm_wgrad(name, a, b, *, out_dtype):
    t, m = a.shape
    n = b.shape[1]
    tt = min(TT_MM, t)
    return _matmul(name, a, b, dims=_TN, grid=(1, 1, t // tt),
                   a_spec=pl.BlockSpec((tt, m), lambda i, j, kk: (kk, 0)),
                   b_spec=pl.BlockSpec((tt, n), lambda i, j, kk: (kk, 0)),
                   o_spec=pl.BlockSpec((m, n), lambda i, j, kk: (0, 0)),
                   acc_shape=(m, n), out_shape=jax.ShapeDtypeStruct((m, n), out_dtype))


def _vec_spec(d):
    return pl.BlockSpec((1, d), lambda i: (0, 0))


def _row_call(name, body, *, n_steps, in_specs, out_specs, out_shape, scratch, args, exch=()):
    n_in, n_out, n_scr, n_ex = len(in_specs), len(out_specs), len(scratch), len(exch)
    kinds = [k for k, _ in exch]
    ex_in, ex_out, ex_shape, ex_scratch = _exchange_io(exch)

    def wrapped(*refs):
        ins, rest = refs[:n_in], refs[n_in:]
        x_refs, rest = rest[:n_ex], rest[n_ex:]
        outs, rest = rest[:n_out], rest[n_out:]
        o_refs, rest = rest[:n_ex], rest[n_ex:]
        scr, sems = rest[:n_scr], rest[n_scr:]
        if n_ex:
            ex_start, ex_finish = _exchange_step(kinds, x_refs, o_refs, *sems)
            pl.when(pl.program_id(0) == 0)(ex_start)
        body(*ins, *outs, *scr)
        if n_ex:
            pl.when(pl.program_id(0) == n_steps - 1)(ex_finish)

    res = pl.pallas_call(wrapped, grid=(n_steps,), in_specs=list(in_specs) + ex_in, out_specs=list(out_specs) + ex_out,
                         out_shape=list(out_shape) + ex_shape, scratch_shapes=list(scratch) + ex_scratch, name=name,
                         compiler_params=_params(1))(*args, *[v for _, v in exch])
    return res[:n_out], res[n_out:]


def _rms_fwd(name, x, g):
    t, d = x.shape
    tm = min(TM_EW * 2, t)

    def body(x_ref, g_ref, h_ref):
        xv = x_ref[...]
        rstd = lax.rsqrt(_mean(xv * xv) + EPS_RMS)
        h_ref[...] = ((xv * rstd) * g_ref[...]).astype(BF16)

    row = pl.BlockSpec((tm, d), lambda i: (i, 0))
    return pl.pallas_call(body, grid=(t // tm,), in_specs=[row, _vec_spec(d)], out_specs=row,
                          out_shape=jax.ShapeDtypeStruct((t, d), BF16), name=name, compiler_params=_params(1))(x, g)


def _rms_bwd(name, x, g, dh, dres, *, want_bf16):
    t, d = x.shape
    tm = min(TM_EW * 2, t)
    n = t // tm

    def body(x_ref, g_ref, dh_ref, dres_ref, dx_ref, *rest):
        dg_ref = rest[-1]
        i = pl.program_id(0)
        xv = x_ref[...]
        rstd = lax.rsqrt(_mean(xv * xv) + EPS_RMS)
        nrm = xv * rstd
        dhv = dh_ref[...].astype(F32)
        dn = dhv * g_ref[...]
        dx = dres_ref[...] + rstd * (dn - nrm * _mean(dn * nrm))
        dx_ref[...] = dx
        if want_bf16:
            rest[0][...] = dx.astype(BF16)
        part = _rowsum(dhv * nrm)

        @pl.when(i == 0)
        def _():
            dg_ref[...] = part

        @pl.when(i > 0)
        def _():
            dg_ref[...] += part

    row = pl.BlockSpec((tm, d), lambda i: (i, 0))
    out_specs = [row] + ([row] if want_bf16 else []) + [_vec_spec(d)]
    out_shape = ([jax.ShapeDtypeStruct((t, d), F32)] + ([jax.ShapeDtypeStruct((t, d), BF16)] if want_bf16 else [])
                 + [jax.ShapeDtypeStruct((1, d), F32)])
    return pl.pallas_call(body, grid=(n,), in_specs=[row, _vec_spec(d), row, row], out_specs=out_specs,
                          out_shape=out_shape, name=name, compiler_params=_params(1))(x, g, dh, dres)


def _branch_fwd(z, ln_v_g, ln_v_b, ws_m, bias_full, conv_w, conv_b, ln_b_g, ln_b_b, seq, exch=()):
    t = z.shape[0]
    w = 1024
    tm = min(TM_EW, seq)
    tiles_per_seq = seq // tm
    n_chunks = tm // CHUNK

    def body(z_ref, lvg_ref, lvb_ref, ws_ref, bias_ref, cw_ref, cb_ref, lbg_ref, lbb_ref,
             pa_ref, cs_ref, c_ref, hist_ref, buf_ref, mix_ref):
        i = pl.program_id(0)
        u = z_ref[:, 0:w].astype(F32)
        v = z_ref[:, w:2 * w].astype(F32)
        ug = _gelu(u)
        vg = _gelu(v)
        dv = vg - _mean(vg)
        vhat = dv * lax.rsqrt(_mean(dv * dv) + EPS_LN)
        vn = (vhat * lvg_ref[...] + lvb_ref[...]).astype(BF16)
        for ci in range(n_chunks):
            rows = slice(ci * CHUNK, (ci + 1) * CHUNK)
            for g in range(GROUPS):
                cols = slice(g * CHUNK, (g + 1) * CHUNK)
                mix_ref[rows, cols] = lax.dot_general(ws_ref[g], vn[rows, cols], _NN, preferred_element_type=F32)
            mix_ref[rows, :] += bias_ref[...]
        pa_ref[...] = (ug * mix_ref[...]).astype(BF16)

        a = z_ref[:, 2 * w:3 * w].astype(F32)
        gl = z_ref[:, 3 * w:4 * w].astype(F32)
        glu = a * _sigmoid(gl)

        @pl.when(i % tiles_per_seq == 0)
        def _():
            hist_ref[...] = jnp.zeros_like(hist_ref)

        buf_ref[0:HALO_B, :] = hist_ref[...]
        buf_ref[HALO_B:, :] = glu
        hist_ref[...] = glu[tm - HALO_B:, :]
        c = jnp.zeros((tm, w), F32) + cb_ref[...]
        for k in range(CONV_B):
            off = HALO_B - (CONV_B - 1) + k
            c = c + cw_ref[k:k + 1, :] * buf_ref[off:off + tm, :]
        c_ref[...] = c
        dc = c - _mean(c)
        chat = dc * lax.rsqrt(_mean(dc * dc) + EPS_LN)
        cn = chat * lbg_ref[...] + lbb_ref[...]
        cs_ref[...] = (cn * _sigmoid(cn)).astype(BF16)

    row = pl.BlockSpec((tm, w), lambda i: (i, 0))
    in_specs = [pl.BlockSpec((tm, 4 * w), lambda i: (i, 0)), _vec_spec(w), _vec_spec(w),
                pl.BlockSpec((GROUPS, CHUNK, CHUNK), lambda i: (0, 0, 0)), pl.BlockSpec((CHUNK, w), lambda i: (0, 0)),
                pl.BlockSpec((CONV_B, w), lambda i: (0, 0)), _vec_spec(w), _vec_spec(w), _vec_spec(w)]
    return _row_call(
        "branch_fwd", body, n_steps=t // tm, in_specs=in_specs, out_specs=[row, row, row],
        out_shape=[jax.ShapeDtypeStruct((t, w), BF16), jax.ShapeDtypeStruct((t, w), BF16), jax.ShapeDtypeStruct((t, w), F32)],
        scratch=[pltpu.VMEM((HALO_B, w), F32), pltpu.VMEM((HALO_B + tm, w), F32), pltpu.VMEM((tm, w), F32)],
        args=(z, ln_v_g, ln_v_b, ws_m, bias_full, conv_w, conv_b, ln_b_g, ln_b_b), exch=exch)


def _merge_fwd(z, ya, yb):
    t, w = ya.shape
    tm = min(TM_EW * 2, t)

    def body(ga_ref, gb_ref, ya_ref, yb_ref, o_ref):
        sa = _sigmoid(ga_ref[...].astype(F32))
        sb = _sigmoid(gb_ref[...].astype(F32))
        o_ref[...] = (sa * ya_ref[...].astype(F32) + sb * yb_ref[...].astype(F32)).astype(BF16)

    row = pl.BlockSpec((tm, w), lambda i: (i, 0))
    in_specs = [pl.BlockSpec((tm, w), lambda i: (i, 4)), pl.BlockSpec((tm, w), lambda i: (i, 5)), row, row]
    return pl.pallas_call(body, grid=(t // tm,), in_specs=in_specs, out_specs=row,
                          out_shape=jax.ShapeDtypeStruct((t, w), BF16), name="merge_fwd",
                          compiler_params=_params(1))(z, z, ya, yb)


def _merge_bwd(z, ya, yb, dmerged):
    t, w = ya.shape
    tm = min(TM_EW * 2, t)

    def body(ga_ref, gb_ref, ya_ref, yb_ref, dm_ref, dya_ref, dyb_ref, dg_ref):
        sa = _sigmoid(ga_ref[...].astype(F32))
        sb = _sigmoid(gb_ref[...].astype(F32))
        dm = dm_ref[...].astype(F32)
        dya_ref[...] = (dm * sa).astype(BF16)
        dyb_ref[...] = (dm * sb).astype(BF16)
        dg_ref[:, 0:w] = (dm * ya_ref[...].astype(F32) * sa * (1.0 - sa)).astype(BF16)
        dg_ref[:, w:2 * w] = (dm * yb_ref[...].astype(F32) * sb * (1.0 - sb)).astype(BF16)

    row = pl.BlockSpec((tm, w), lambda i: (i, 0))
    in_specs = [pl.BlockSpec((tm, w), lambda i: (i, 4)), pl.BlockSpec((tm, w), lambda i: (i, 5)), row, row, row]
    return pl.pallas_call(
        body, grid=(t // tm,), in_specs=in_specs, out_specs=[row, row, pl.BlockSpec((tm, 2 * w), lambda i: (i, 0))],
        out_shape=[jax.ShapeDtypeStruct((t, w), BF16), jax.ShapeDtypeStruct((t, w), BF16),
                   jax.ShapeDtypeStruct((t, 2 * w), BF16)],
        name="merge_bwd", compiler_params=_params(1))(z, z, ya, yb, dmerged)


def _branch_bwd(z, c_saved, dpa, dcs, dgates, ln_v_g, ln_v_b, ws_m, ws_mt, bias_full, conv_w, ln_b_g, ln_b_b, seq, exch=()):
    t = z.shape[0]
    w = 1024
    tm = min(TM_EW, seq)
    tiles_per_seq = seq // tm
    n_tiles = t // tm
    n_chunks = tm // CHUNK
    halo_blocks = tm // HALO_B

    def body(z_ref, zh_ref, c_ref, dpa_ref, dcs_ref, dgt_ref, lvg_ref, lvb_ref, ws_ref, wst_ref, bias_ref, cw_ref,
             lbg_ref, lbb_ref,
             dz_ref, dlvg_ref, dlvb_ref, dws_ref, dbs_ref, dcw_ref, dcb_ref, dlbg_ref, dlbb_ref,
             carry_ref, gbuf_ref, dbuf_ref, mix_ref, dvn_ref, dbs_acc_ref):
        i = pl.program_id(0)
        r = n_tiles - 1 - i

        @pl.when(i == 0)
        def _():
            for ref in (dlvg_ref, dlvb_ref, dws_ref, dbs_acc_ref, dcw_ref, dcb_ref, dlbg_ref, dlbb_ref):
                ref[...] = jnp.zeros_like(ref)

        u = z_ref[:, 0:w].astype(F32)
        v = z_ref[:, w:2 * w].astype(F32)
        ug, dug = _gelu_and_grad(u)
        vg, dvg = _gelu_and_grad(v)
        dv0 = vg - _mean(vg)
        rstd_v = lax.rsqrt(_mean(dv0 * dv0) + EPS_LN)
        vhat = dv0 * rstd_v
        vn = (vhat * lvg_ref[...] + lvb_ref[...]).astype(BF16)
        dpa = dpa_ref[...].astype(F32)
        dmix = dpa * ug
        dmix_b = dmix.astype(BF16)
        for ci in range(n_chunks):
            rows = slice(ci * CHUNK, (ci + 1) * CHUNK)
            for g in range(GROUPS):
                cols = slice(g * CHUNK, (g + 1) * CHUNK)
                mix_ref[rows, cols] = lax.dot_general(ws_ref[g], vn[rows, cols], _NN, preferred_element_type=F32)
                dvn_ref[rows, cols] = lax.dot_general(wst_ref[g], dmix_b[rows, cols], _NN, preferred_element_type=F32)
                dws_ref[g] += lax.dot_general(dmix_b[rows, cols], vn[rows, cols], _NT, preferred_element_type=F32)
            mix_ref[rows, :] += bias_ref[...]
            dbs_acc_ref[...] += dmix[rows, :]
        dz_ref[:, 0:w] = (dpa * mix_ref[...] * dug).astype(BF16)
        dvn = dvn_ref[...]
        dlvg_ref[...] += _rowsum(dvn * vhat)
        dlvb_ref[...] += _rowsum(dvn)
        dvh = dvn * lvg_ref[...]
        dvg_in = rstd_v * (dvh - _mean(dvh) - vhat * _mean(dvh * vhat))
        dz_ref[:, w:2 * w] = (dvg_in * dvg).astype(BF16)

        c = c_ref[...]
        dc0 = c - _mean(c)
        rstd_c = lax.rsqrt(_mean(dc0 * dc0) + EPS_LN)
        chat = dc0 * rstd_c
        cn = chat * lbg_ref[...] + lbb_ref[...]
        sg = _sigmoid(cn)
        dcn = dcs_ref[...].astype(F32) * (sg * (1.0 + cn * (1.0 - sg)))
        dlbg_ref[...] += _rowsum(dcn * chat)
        dlbb_ref[...] += _rowsum(dcn)
        dch = dcn * lbg_ref[...]
        dc = rstd_c * (dch - _mean(dch) - chat * _mean(dch * chat))
        dcb_ref[...] += _rowsum(dc)

        a = z_ref[:, 2 * w:3 * w].astype(F32)
        gl = z_ref[:, 3 * w:4 * w].astype(F32)
        sgl = _sigmoid(gl)
        ah = zh_ref[:, 0:w].astype(F32)
        glh = zh_ref[:, w:2 * w].astype(F32)
        first = (r % tiles_per_seq) == 0
        gbuf_ref[0:HALO_B, :] = jnp.where(first, 0.0, ah * _sigmoid(glh))
        gbuf_ref[HALO_B:, :] = a * sgl

        @pl.when(r % tiles_per_seq == tiles_per_seq - 1)
        def _():
            carry_ref[...] = jnp.zeros_like(carry_ref)

        dbuf_ref[0:tm, :] = dc
        dbuf_ref[tm:, :] = carry_ref[...]
        carry_ref[...] = dc[0:HALO_B, :]
        dglu = jnp.zeros((tm, w), F32)
        for k in range(CONV_B):
            off = HALO_B - (CONV_B - 1) + k
            dcw_ref[k:k + 1, :] += _rowsum(dc * gbuf_ref[off:off + tm, :])
            dglu = dglu + cw_ref[k:k + 1, :] * dbuf_ref[CONV_B - 1 - k:CONV_B - 1 - k + tm, :]
        dz_ref[:, 2 * w:3 * w] = (dglu * sgl).astype(BF16)
        dz_ref[:, 3 * w:4 * w] = (dglu * a * sgl * (1.0 - sgl)).astype(BF16)
        dz_ref[:, 4 * w:6 * w] = dgt_ref[...]

        @pl.when(i == n_tiles - 1)
        def _():
            tri = lax.broadcasted_iota(jnp.int32, (CHUNK, CHUNK), 0) >= lax.broadcasted_iota(jnp.int32, (CHUNK, CHUNK), 1)
            lane = lax.broadcasted_iota(jnp.int32, (CHUNK, CHUNK), 1)
            dbs = jnp.zeros((CHUNK, CHUNK), F32)
            for g in range(GROUPS):
                dws_ref[g] = jnp.where(tri, dws_ref[g], 0.0)
                group_sum = jnp.sum(dbs_acc_ref[:, g * CHUNK:(g + 1) * CHUNK], axis=1, keepdims=True)
                dbs = jnp.where(lane == g, group_sum, dbs)
            dbs_ref[...] = dbs

    def rev(i):
        return n_tiles - 1 - i

    def halo_map(i):
        return (jnp.maximum(rev(i) * halo_blocks - 1, 0), 1)

    row = pl.BlockSpec((tm, w), lambda i: (rev(i), 0))
    full = lambda shape: pl.BlockSpec(shape, lambda i: (0,) * len(shape))
    in_specs = [pl.BlockSpec((tm, 4 * w), lambda i: (rev(i), 0)), pl.BlockSpec((HALO_B, 2 * w), halo_map),
                row, row, row, pl.BlockSpec((tm, 2 * w), lambda i: (rev(i), 0)),
                _vec_spec(w), _vec_spec(w), full((GROUPS, CHUNK, CHUNK)), full((GROUPS, CHUNK, CHUNK)), full((CHUNK, w)),
                full((CONV_B, w)), _vec_spec(w), _vec_spec(w)]
    out_specs = [pl.BlockSpec((tm, 6 * w), lambda i: (rev(i), 0)), _vec_spec(w), _vec_spec(w), full((GROUPS, CHUNK, CHUNK)),
                 full((CHUNK, CHUNK)), full((CONV_B, w)), _vec_spec(w), _vec_spec(w), _vec_spec(w)]
    vec = jax.ShapeDtypeStruct((1, w), F32)
    out_shape = [jax.ShapeDtypeStruct((t, 6 * w), BF16), vec, vec, jax.ShapeDtypeStruct((GROUPS, CHUNK, CHUNK), F32),
                 jax.ShapeDtypeStruct((CHUNK, CHUNK), F32), jax.ShapeDtypeStruct((CONV_B, w), F32), vec, vec, vec]
    scratch = [pltpu.VMEM((HALO_B, w), F32), pltpu.VMEM((HALO_B + tm, w), F32), pltpu.VMEM((tm + HALO_B, w), F32),
               pltpu.VMEM((tm, w), F32), pltpu.VMEM((tm, w), F32), pltpu.VMEM((CHUNK, w), F32)]
    return _row_call("branch_bwd", body, n_steps=n_tiles, in_specs=in_specs, out_specs=out_specs, out_shape=out_shape,
                     scratch=scratch, exch=exch,
                     args=(z, z, c_saved, dpa, dcs, dgates, ln_v_g, ln_v_b, ws_m, ws_mt, bias_full, conv_w, ln_b_g, ln_b_b))


def _conv3_window(prev8, x):
    win = jnp.concatenate([prev8, x], axis=0)
    n = x.shape[0]
    return [win[HALO_F - 2:HALO_F - 2 + n], win[HALO_F - 1:HALO_F - 1 + n], x]


def _ffn_mid_fwd(up0, conv_w, conv_b, seq):
    nb, t, f = up0.shape
    half = nb // 2
    tm = min(TM_EW, seq)
    tiles_per_seq = seq // tm
    n_strips = tm // ROWS_F

    def body(up_ref, w_ref, b_ref, act_ref, hist_ref):
        i = pl.program_id(0)

        @pl.when(i % tiles_per_seq == 0)
        def _():
            hist_ref[...] = jnp.zeros_like(hist_ref)

        for j in range(half):
            jv = j + half
            wg = [w_ref[j, k:k + 1, :] for k in range(CONV_F)]
            wv = [w_ref[jv, k:k + 1, :] for k in range(CONV_F)]
            bg, bv = b_ref[j], b_ref[jv]

            def strip(c, carry):
                s = pl.multiple_of(c * ROWS_F, ROWS_F)
                xg = up_ref[j, pl.ds(s, ROWS_F), :].astype(F32)
                xv = up_ref[jv, pl.ds(s, ROWS_F), :].astype(F32)
                sg = _conv3_window(carry[0], xg)
                sv = _conv3_window(carry[1], xv)
                gate = bg + wg[0] * sg[0] + wg[1] * sg[1] + wg[2] * sg[2]
                val = bv + wv[0] * sv[0] + wv[1] * sv[1] + wv[2] * sv[2]
                act_ref[j, pl.ds(s, ROWS_F), :] = (_gelu(gate) * val).astype(BF16)
                return xg[ROWS_F - HALO_F:], xv[ROWS_F - HALO_F:]

            last = lax.fori_loop(0, n_strips, strip, (hist_ref[j], hist_ref[jv]))
            hist_ref[j] = last[0]
            hist_ref[jv] = last[1]

    return pl.pallas_call(
        body, grid=(t // tm,),
        in_specs=[pl.BlockSpec((nb, tm, f), lambda i: (0, i, 0)), pl.BlockSpec((nb, CONV_F, f), lambda i: (0, 0, 0)),
                  pl.BlockSpec((nb, 1, f), lambda i: (0, 0, 0))],
        out_specs=pl.BlockSpec((half, tm, f), lambda i: (0, i, 0)),
        out_shape=jax.ShapeDtypeStruct((half, t, f), BF16),
        scratch_shapes=[pltpu.VMEM((nb, HALO_F, f), F32)],
        name="ffn_mid_fwd", compiler_params=_params(1))(up0, conv_w, conv_b)


def _ffn_mid_bwd(up0, dact, conv_w, conv_b, seq, exch=()):
    nb, t, f = up0.shape
    half = nb // 2
    tm = min(TM_EW, seq)
    tiles_per_seq = seq // tm
    n_tiles = t // tm
    halo_blocks = tm // HALO_F
    n_strips = tm // ROWS_F

    def body(up_ref, uph_ref, da_ref, w_ref, b_ref, dup_ref, dw_ref, db_ref, carry_ref, dwacc_ref, dbacc_ref):
        i = pl.program_id(0)
        r = n_tiles - 1 - i
        first = (r % tiles_per_seq) == 0

        @pl.when(i == 0)
        def _():
            dwacc_ref[...] = jnp.zeros_like(dwacc_ref)
            dbacc_ref[...] = jnp.zeros_like(dbacc_ref)

        @pl.when(r % tiles_per_seq == tiles_per_seq - 1)
        def _():
            carry_ref[...] = jnp.zeros_like(carry_ref)

        for j in range(half):
            jv = j + half
            wg = [w_ref[j, k:k + 1, :] for k in range(CONV_F)]
            wv = [w_ref[jv, k:k + 1, :] for k in range(CONV_F)]
            bg, bv = b_ref[j], b_ref[jv]

            def process(s, prev_g, prev_v, next_dg, next_dv):
                rows = pl.ds(s, ROWS_F)
                sg = _conv3_window(prev_g, up_ref[j, rows, :].astype(F32))
                sv = _conv3_window(prev_v, up_ref[jv, rows, :].astype(F32))
                gate = bg + wg[0] * sg[0] + wg[1] * sg[1] + wg[2] * sg[2]
                val = bv + wv[0] * sv[0] + wv[1] * sv[1] + wv[2] * sv[2]
                gg, dgg = _gelu_and_grad(gate)
                da = da_ref[j, rows, :].astype(F32)
                d_gate = da * val * dgg
                d_val = da * gg
                for blk, d, sx, nxt, wk in ((j, d_gate, sg, next_dg, wg), (jv, d_val, sv, next_dv, wv)):
                    dbacc_ref[blk] += d
                    for k in range(CONV_F):
                        dwacc_ref[blk, k] += d * sx[k]
                    dwin = jnp.concatenate([d, nxt], axis=0)
                    dx = wk[2] * d + wk[1] * dwin[1:1 + ROWS_F] + wk[0] * dwin[2:2 + ROWS_F]
                    dup_ref[blk, rows, :] = dx.astype(BF16)
                return d_gate[0:HALO_F], d_val[0:HALO_F]

            def strip(ci, carry):
                s = pl.multiple_of((n_strips - 1 - ci) * ROWS_F, ROWS_F)
                before = pl.ds(s - ROWS_F, ROWS_F)
                prev_g = up_ref[j, before, :].astype(F32)[ROWS_F - HALO_F:]
                prev_v = up_ref[jv, before, :].astype(F32)[ROWS_F - HALO_F:]
                return process(s, prev_g, prev_v, carry[0], carry[1])

            carry = lax.fori_loop(0, n_strips - 1, strip, (carry_ref[j], carry_ref[jv]))
            halo_g = jnp.where(first, 0.0, uph_ref[j].astype(F32))
            halo_v = jnp.where(first, 0.0, uph_ref[jv].astype(F32))
            carry = process(0, halo_g, halo_v, carry[0], carry[1])
            carry_ref[j] = carry[0]
            carry_ref[jv] = carry[1]

        @pl.when(i == n_tiles - 1)
        def _():
            for blk in range(nb):
                db_ref[blk] = _rowsum(dbacc_ref[blk])
                for k in range(CONV_F):
                    dw_ref[blk, k:k + 1, :] = _rowsum(dwacc_ref[blk, k])

    def rev(i):
        return n_tiles - 1 - i

    return _row_call(
        "ffn_mid_bwd", body, n_steps=n_tiles,
        in_specs=[pl.BlockSpec((nb, tm, f), lambda i: (0, rev(i), 0)),
                  pl.BlockSpec((nb, HALO_F, f), lambda i: (0, jnp.maximum(rev(i) * halo_blocks - 1, 0), 0)),
                  pl.BlockSpec((half, tm, f), lambda i: (0, rev(i), 0)),
                  pl.BlockSpec((nb, CONV_F, f), lambda i: (0, 0, 0)), pl.BlockSpec((nb, 1, f), lambda i: (0, 0, 0))],
        out_specs=[pl.BlockSpec((nb, tm, f), lambda i: (0, rev(i), 0)), pl.BlockSpec((nb, CONV_F, f), lambda i: (0, 0, 0)),
                   pl.BlockSpec((nb, 1, f), lambda i: (0, 0, 0))],
        out_shape=[jax.ShapeDtypeStruct((nb, t, f), BF16), jax.ShapeDtypeStruct((nb, CONV_F, f), F32),
                   jax.ShapeDtypeStruct((nb, 1, f), F32)],
        scratch=[pltpu.VMEM((nb, HALO_F, f), F32), pltpu.VMEM((nb, CONV_F, ROWS_F, f), F32), pltpu.VMEM((nb, ROWS_F, f), F32)],
        args=(up0, up0, dact, conv_w, conv_b), exch=exch)


def _head(x2, q, r, target, g_ple, g_final):
    t, d = x2.shape
    tm = min(TM_EW * 2, t)

    def body(x2_ref, q_ref, r_ref, tg_ref, gple_ref, gfin_ref, dx3_ref, dq_ref, dr_ref, loss_ref, dgfin_ref, dgple_ref):
        i = pl.program_id(0)
        pg = _sigmoid(q_ref[...])
        rv = r_ref[...]
        rstd_r = lax.rsqrt(_mean(rv * rv) + EPS_RMS)
        nr = rv * rstd_r
        pe = nr * gple_ref[...]
        x3 = x2_ref[...] + pe * pg
        rstd3 = lax.rsqrt(_mean(x3 * x3) + EPS_RMS)
        n3 = x3 * rstd3
        err = n3 * gfin_ref[...] - tg_ref[...]
        loss_part = jnp.sum(_rowsum(err * err), axis=1, keepdims=True) * (0.5 / d)
        dy = err * (1.0 / d)
        dn3 = dy * gfin_ref[...]
        dx3 = rstd3 * (dn3 - n3 * _mean(dn3 * n3))
        dx3_ref[...] = dx3
        dq_ref[...] = (dx3 * pe * pg * (1.0 - pg)).astype(BF16)
        dpe = dx3 * pg
        dnr = dpe * gple_ref[...]
        dr_ref[...] = (rstd_r * (dnr - nr * _mean(dnr * nr))).astype(BF16)
        dgfin = _rowsum(dy * n3)
        dgple = _rowsum(dpe * nr)

        @pl.when(i == 0)
        def _():
            loss_ref[...] = jnp.zeros_like(loss_ref) + loss_part
            dgfin_ref[...] = dgfin
            dgple_ref[...] = dgple

        @pl.when(i > 0)
        def _():
            loss_ref[...] += loss_part
            dgfin_ref[...] += dgfin
            dgple_ref[...] += dgple

    row = pl.BlockSpec((tm, d), lambda i: (i, 0))
    vec = jax.ShapeDtypeStruct((1, d), F32)
    return pl.pallas_call(
        body, grid=(t // tm,), in_specs=[row, row, row, row, _vec_spec(d), _vec_spec(d)],
        out_specs=[row, row, row, _vec_spec(d), _vec_spec(d), _vec_spec(d)],
        out_shape=[jax.ShapeDtypeStruct((t, d), F32), jax.ShapeDtypeStruct((t, d), BF16), jax.ShapeDtypeStruct((t, d), BF16),
                   vec, vec, vec],
        name="head", compiler_params=_params(1))(x2, q, r, target, g_ple, g_final)


def _all_gather(xs):
    n = len(xs)

    def body(*refs):
        x_refs, o_refs = refs[:n], refs[n:2 * n]
        send_sems, recv_sems, local_sems = refs[2 * n:]
        x, y, c = lax.axis_index("x"), lax.axis_index("y"), lax.axis_index("c")
        me, sibling = (x, y, c), (x, y, 1 - c)
        chips = [(1 - x, y), (x, 1 - y), (1 - x, 1 - y)]

        def slot(pos):
            return 4 * pos[0] + 2 * pos[1] + pos[2]

        def copy(a, k, block, to, src=None):
            dst = o_refs[a].at[slot(block)]
            return pltpu.make_async_remote_copy(
                src_ref=dst if src is None else src, dst_ref=dst, send_sem=send_sems.at[a * 7 + k],
                recv_sem=recv_sems.at[a * 7 + k], device_id=to, device_id_type=MESH)

        mine = [pltpu.make_async_copy(x_refs[a], o_refs[a].at[slot(me)], local_sems.at[a]) for a in range(n)]
        for cp in mine:
            cp.start()
        first = []
        for a in range(n):
            first.append(copy(a, 0, me, sibling, src=x_refs[a]))
            first += [copy(a, 1 + j, me, (*chip, c), src=x_refs[a]) for j, chip in enumerate(chips)]
        for cp in first:
            cp.start()
        passed = []
        for j, chip in enumerate(chips):
            for a in range(n):
                copy(a, 1 + j, (*chip, c), me).wait_recv()
                cp = copy(a, 4 + j, (*chip, c), sibling)
                cp.start()
                passed.append(cp)
        for a in range(n):
            copy(a, 0, sibling, me).wait_recv()
        for j, chip in enumerate(chips):
            for a in range(n):
                copy(a, 4 + j, (*chip, 1 - c), me).wait_recv()
        for cp in first + passed:
            cp.wait_send()
        for cp in mine:
            cp.wait()

    return pl.pallas_call(
        body, in_specs=[HBM_SPEC] * n, out_specs=[HBM_SPEC] * n,
        out_shape=[jax.ShapeDtypeStruct((N_DEV,) + v.shape, v.dtype) for v in xs],
        scratch_shapes=[pltpu.SemaphoreType.DMA((7 * n,)), pltpu.SemaphoreType.DMA((7 * n,)), pltpu.SemaphoreType.DMA((n,))],
        name="all_gather_weights")(*xs)


def _adamw(name, parts, w, m, v, rows_per_step):
    r, c = w.shape
    tr = r if r <= rows_per_step else (rows_per_step if r % rows_per_step == 0 else r // 2)
    c1 = 1.0 - ADAM_B1 ** ADAM_STEP
    c2 = 1.0 - ADAM_B2 ** ADAM_STEP

    def body(p_ref, w_ref, m_ref, v_ref, g_ref, d_ref, mo_ref, vo_ref):
        g = p_ref[0].astype(F32)
        for s in range(1, N_DEV):
            g = g + p_ref[s].astype(F32)
        m_new = ADAM_B1 * m_ref[...] + (1.0 - ADAM_B1) * g
        v_new = ADAM_B2 * v_ref[...] + (1.0 - ADAM_B2) * (g * g)
        g_ref[...] = g
        mo_ref[...] = m_new
        vo_ref[...] = v_new
        d_ref[...] = -ADAM_LR * ((m_new / c1) / (jnp.sqrt(v_new / c2) + ADAM_EPS) + ADAM_WD * w_ref[...])

    row = pl.BlockSpec((tr, c), lambda i: (i, 0))
    out = jax.ShapeDtypeStruct((r, c), F32)
    return pl.pallas_call(body, grid=(r // tr,), in_specs=[pl.BlockSpec((N_DEV, tr, c), lambda i: (0, i, 0)), row, row, row],
                          out_specs=[row, row, row, row], out_shape=[out, out, out, out], name=name,
                          compiler_params=_params(1))(parts, w, m, v)


_VEC_NAMES = ("ln_v_g", "ln_v_b", "b_s", "conv_b_b", "ln_b_g", "ln_b_b", "g_ffn", "g_pg", "g_ple", "g_final")
_LOSS_ROW = len(_VEC_NAMES)


def _pack_replicated(d, loss=None):
    head = jnp.zeros((16, 1024), F32)
    for i, k in enumerate(_VEC_NAMES):
        head = head.at[i].set(d[k].reshape(1024))
    if loss is not None:
        head = head.at[_LOSS_ROW, 0].set(loss)
    fb = jnp.pad(d["ffn_conv_b"].reshape(-1), (0, 6144 - 5632)).reshape(6, 1024)
    return jnp.concatenate([head, jnp.pad(fb, ((0, 2), (0, 0))), d["w_s"].reshape(128, 1024)], axis=0)


def _unpack_replicated(pk, shapes):
    out = {k: pk[i, :].reshape(shapes[k]) for i, k in enumerate(_VEC_NAMES)}
    out["loss"] = pk[_LOSS_ROW, 0]
    out["ffn_conv_b"] = pk[16:22, :].reshape(-1)[:5632].reshape(shapes["ffn_conv_b"])
    out["w_s"] = pk[24:152, :].reshape(shapes["w_s"])
    return out


def _rows8(vec):
    return jnp.pad(vec.reshape(1, 1024), ((0, 7), (0, 0)))


def _pack_sharded_small(conv_b_w, ffn_conv_w):
    lead = conv_b_w.shape[:-2]
    pad0 = [(0, 0)] * len(lead)
    a = jnp.pad(conv_b_w, pad0 + [(0, 1), (0, 0)])
    b = jnp.pad(ffn_conv_w.reshape(lead + (CONV_F * 704,)), pad0 + [(0, 24 * 128 - CONV_F * 704)]).reshape(lead + (24, 128))
    return jnp.concatenate([a, b], axis=-2)


def _unpack_sharded_small(pk):
    lead = pk.shape[:-2]
    conv_b_w = pk[..., 0:CONV_B, :]
    ffn = pk[..., 32:56, :].reshape(lead + (24 * 128,))[..., :CONV_F * 704].reshape(lead + (CONV_F, 704))
    return conv_b_w, ffn


_WEIGHTS = ("g_mix", "w_in", "ln_v_g", "ln_v_b", "w_s", "b_s", "w_a_out", "conv_b_w", "conv_b_b", "ln_b_g", "ln_b_b",
            "w_b_out", "w_o", "g_ffn", "w_up", "ffn_conv_w", "ffn_conv_b", "w_down", "g_pg", "w_pg", "w_ple", "g_ple",
            "g_final")
_BIG = ("w_in", "w_a_out", "w_b_out", "w_o", "w_up", "w_down", "w_pg", "w_ple")


def kernel(x, p, g_mix, w_in, ln_v_g, ln_v_b, w_s, b_s, w_a_out, conv_b_w, conv_b_b, ln_b_g, ln_b_b, w_b_out, w_o, g_ffn, w_up, ffn_conv_w, ffn_conv_b, w_down, g_pg, w_pg, w_ple, g_ple, g_final, loss_target, m_g_mix, m_w_in, m_ln_v_g, m_ln_v_b, m_w_s, m_b_s, m_w_a_out, m_conv_b_w, m_conv_b_b, m_ln_b_g, m_ln_b_b, m_w_b_out, m_w_o, m_g_ffn, m_w_up, m_ffn_conv_w, m_ffn_conv_b, m_w_down, m_g_pg, m_w_pg, m_w_ple, m_g_ple, m_g_final, v_g_mix, v_w_in, v_ln_v_g, v_ln_v_b, v_w_s, v_b_s, v_w_a_out, v_conv_b_w, v_conv_b_b, v_ln_b_g, v_ln_b_b, v_w_b_out, v_w_o, v_g_ffn, v_w_up, v_ffn_conv_w, v_ffn_conv_b, v_w_down, v_g_pg, v_w_pg, v_w_ple, v_g_ple, v_g_final):
    local = dict(locals())
    wts = {k: local[k] for k in _WEIGHTS}
    mom = {k: local["m_" + k] for k in _WEIGHTS}
    var = {k: local["v_" + k] for k in _WEIGHTS}
    shapes = {k: wts[k].shape for k in _WEIGHTS}

    bsz, seq, d = x.shape
    t = bsz * seq
    x0 = x.reshape(t, d)
    p0 = p.reshape(t, p.shape[-1])
    target = loss_target.reshape(t, d)
    tm = min(TM_MM, t)
    tt = min(TT_MM, t)
    n_row = t // tm
    n_tok = t // tt

    def sq(a):
        return a.reshape(a.shape[1:])

    shard = {k: sq(wts[k]).astype(BF16) for k in _BIG}
    w_in3, small8 = _all_gather([shard["w_in"], _pack_sharded_small(sq(conv_b_w), sq(ffn_conv_w))])
    conv_b_w8, ffn_conv_w8 = _unpack_sharded_small(small8)
    conv_w_full = conv_b_w8.transpose(1, 0, 2).reshape(CONV_B, N_DEV * conv_b_w8.shape[-1])
    n_in = w_in3.shape[2]
    f_blk = shard["w_up"].shape[1]

    ws_m = jnp.where(jnp.tril(jnp.ones((CHUNK, CHUNK), bool))[None], sq(w_s), 0.0).astype(BF16)
    ws_mt = jnp.swapaxes(ws_m, 1, 2)
    bias_full = jnp.broadcast_to(sq(b_s).T[:, :, None], (CHUNK, GROUPS, CHUNK)).reshape(CHUNK, GROUPS * CHUNK)
    ffn_b8 = ffn_conv_b.reshape(N_DEV, 1, f_blk)

    h1 = _rms_fwd("rms_mix", x0, g_mix)
    z, (wa3, wb3, wo3) = _matmul(
        "mm_in", h1, w_in3, dims=_NN, grid=(N_DEV, n_row, 1),
        a_spec=pl.BlockSpec((tm, d), lambda j, i, k: (i, 0)),
        b_spec=pl.BlockSpec((None, d, n_in), lambda j, i, k: (j, 0, 0)),
        o_spec=pl.BlockSpec((tm, n_in), lambda j, i, k: (i, j)), acc_shape=(tm, n_in),
        out_shape=jax.ShapeDtypeStruct((t, N_DEV * n_in), BF16),
        exch=[("gather", shard[k]) for k in ("w_a_out", "w_b_out", "w_o")])
    w_a = wa3.reshape(-1, d)
    w_b = wb3.reshape(-1, d)
    w_om = wo3.reshape(-1, d)
    (pa, cs, c_saved), (w_up3, wd3, wpg3, wple3) = _branch_fwd(
        z, ln_v_g, ln_v_b, ws_m, bias_full, conv_w_full, conv_b_b, ln_b_g, ln_b_b, seq,
        exch=[("gather", shard[k]) for k in ("w_up", "w_down", "w_pg", "w_ple")])
    w_pgm = wpg3.reshape(-1, d)
    w_d4 = wd3.reshape(N_DEV // 2, f_blk, d)
    w_plem = wple3.transpose(1, 0, 2).reshape(wple3.shape[1], d)
    ya = _mm_rows("mm_a_out", pa, w_a, dims=_NN, out_dtype=BF16)
    yb = _mm_rows("mm_b_out", cs, w_b, dims=_NN, out_dtype=BF16)
    merged = _merge_fwd(z, ya, yb)
    x1 = _mm_rows("mm_o", merged, w_om, dims=_NN, out_dtype=F32, res=x0)
    h2 = _rms_fwd("rms_ffn", x1, g_ffn)
    up0 = _matmul("mm_up", h2, w_up3, dims=_NN, grid=(N_DEV, n_row, 1),
                  a_spec=pl.BlockSpec((tm, d), lambda j, i, k: (i, 0)),
                  b_spec=pl.BlockSpec((None, d, f_blk), lambda j, i, k: (j, 0, 0)),
                  o_spec=pl.BlockSpec((None, tm, f_blk), lambda j, i, k: (j, i, 0)), acc_shape=(tm, f_blk),
                  out_shape=jax.ShapeDtypeStruct((N_DEV, t, f_blk), BF16))
    act = _ffn_mid_fwd(up0, ffn_conv_w8, ffn_b8, seq)
    row_spec = pl.BlockSpec((tm, d), lambda i, j, k: (i, 0))
    x2 = _matmul("mm_down", act, w_d4, dims=_NN, grid=(n_row, 1, N_DEV // 2),
                 a_spec=pl.BlockSpec((None, tm, f_blk), lambda i, j, k: (k, i, 0)),
                 b_spec=pl.BlockSpec((None, f_blk, d), lambda i, j, k: (k, 0, 0)),
                 o_spec=row_spec, acc_shape=(tm, d), out_shape=jax.ShapeDtypeStruct((t, d), F32),
                 res=x1, res_spec=row_spec)
    hq = _rms_fwd("rms_pg", x2, g_pg)
    q = _mm_rows("mm_pg", hq, w_pgm, dims=_NN, out_dtype=F32)
    r = _mm_rows("mm_ple", p0, w_plem, dims=_NN, out_dtype=F32)

    dx3, dq, dr, loss_v, dg_final, dg_ple = _head(x2, q, r, target, g_ple, g_final.reshape(1, d))

    recv = {}
    gw_pg = _mm_wgrad("wg_pg", hq, dq, out_dtype=BF16).reshape(wpg3.shape)
    dw_ple = _mm_wgrad("wg_ple", p0, dr, out_dtype=BF16)
    gw_ple = dw_ple.reshape(dw_ple.shape[0], N_DEV, -1).transpose(1, 0, 2)
    dhq = _mm_rows("mm_pg_t", dq, w_pgm, dims=_NT, out_dtype=F32)
    dx2, dx2b, dg_pg = _rms_bwd("rms_pg_bwd", x2, g_pg, dhq, dx3, want_bf16=True)

    dact, (recv["w_pg"], recv["w_ple"]) = _matmul(
        "mm_down_t", dx2b, w_d4, dims=_NT, grid=(N_DEV // 2, n_row, 1),
        a_spec=pl.BlockSpec((tm, d), lambda j, i, k: (i, 0)),
        b_spec=pl.BlockSpec((None, f_blk, d), lambda j, i, k: (j, 0, 0)),
        o_spec=pl.BlockSpec((None, tm, f_blk), lambda j, i, k: (j, i, 0)), acc_shape=(tm, f_blk),
        out_shape=jax.ShapeDtypeStruct((N_DEV // 2, t, f_blk), BF16),
        exch=[("scatter", gw_pg), ("scatter", gw_ple)])
    gw_down = _matmul("wg_down", act, dx2b, dims=_TN, grid=(N_DEV // 2, 1, n_tok),
                      a_spec=pl.BlockSpec((None, tt, f_blk), lambda j, i, k: (j, k, 0)),
                      b_spec=pl.BlockSpec((tt, d), lambda j, i, k: (k, 0)),
                      o_spec=pl.BlockSpec((None, f_blk, d), lambda j, i, k: (j, 0, 0)), acc_shape=(f_blk, d),
                      out_shape=jax.ShapeDtypeStruct((N_DEV // 2, f_blk, d), BF16)).reshape(wd3.shape)
    (d_up0, dffn_w8, dffn_b8), (recv["w_down"],) = _ffn_mid_bwd(up0, dact, ffn_conv_w8, ffn_b8, seq,
                                                                exch=[("scatter", gw_down)])
    gw_up = _matmul("wg_up", h2, d_up0, dims=_TN, grid=(N_DEV, 1, n_tok),
                    a_spec=pl.BlockSpec((tt, d), lambda j, i, k: (k, 0)),
                    b_spec=pl.BlockSpec((None, tt, f_blk), lambda j, i, k: (j, k, 0)),
                    o_spec=pl.BlockSpec((None, d, f_blk), lambda j, i, k: (j, 0, 0)), acc_shape=(d, f_blk),
                    out_shape=jax.ShapeDtypeStruct((N_DEV, d, f_blk), BF16))
    dh2, (recv["w_up"],) = _matmul(
        "mm_up_t", d_up0, w_up3, dims=_NT, grid=(n_row, 1, N_DEV),
        a_spec=pl.BlockSpec((None, tm, f_blk), lambda i, j, k: (k, i, 0)),
        b_spec=pl.BlockSpec((None, d, f_blk), lambda i, j, k: (k, 0, 0)),
        o_spec=row_spec, acc_shape=(tm, d), out_shape=jax.ShapeDtypeStruct((t, d), F32),
        exch=[("scatter", gw_up)])
    dx1, dx1b, dg_ffn = _rms_bwd("rms_ffn_bwd", x1, g_ffn, dh2, dx2, want_bf16=True)

    dmerged = _mm_rows("mm_o_t", dx1b, w_om, dims=_NT, out_dtype=BF16)
    gw_o = _mm_wgrad("wg_o", merged, dx1b, out_dtype=BF16).reshape(wo3.shape)
    dya, dyb, dgates = _merge_bwd(z, ya, yb, dmerged)
    dpa = _mm_rows("mm_a_out_t", dya, w_a, dims=_NT, out_dtype=BF16)
    dcs = _mm_rows("mm_b_out_t", dyb, w_b, dims=_NT, out_dtype=BF16)
    gw_a = _mm_wgrad("wg_a_out", pa, dya, out_dtype=BF16).reshape(wa3.shape)
    gw_b = _mm_wgrad("wg_b_out", cs, dyb, out_dtype=BF16).reshape(wb3.shape)
    (dz, dlvg, dlvb, dws, dbs_full, dconv_w, dconv_b, dlbg, dlbb), (recv["w_o"], recv["w_a_out"], recv["w_b_out"]) = _branch_bwd(
        z, c_saved, dpa, dcs, dgates, ln_v_g, ln_v_b, ws_m, ws_mt, bias_full, conv_w_full, ln_b_g, ln_b_b, seq,
        exch=[("scatter", gw_o), ("scatter", gw_a), ("scatter", gw_b)])
    gw_in = _matmul("wg_in", h1, dz, dims=_TN, grid=(N_DEV, 1, n_tok),
                    a_spec=pl.BlockSpec((tt, d), lambda j, i, k: (k, 0)),
                    b_spec=pl.BlockSpec((tt, n_in), lambda j, i, k: (k, j)),
                    o_spec=pl.BlockSpec((None, d, n_in), lambda j, i, k: (j, 0, 0)), acc_shape=(d, n_in),
                    out_shape=jax.ShapeDtypeStruct((N_DEV, d, n_in), BF16))
    db_s = dbs_full[:, :GROUPS].T
    rep_partial = _pack_replicated(
        dict(ln_v_g=dlvg, ln_v_b=dlvb, b_s=db_s, conv_b_b=dconv_b, ln_b_g=dlbg, ln_b_b=dlbb, g_ffn=dg_ffn,
             g_pg=dg_pg, g_ple=dg_ple, g_final=dg_final, ffn_conv_b=dffn_b8, w_s=dws), loss=loss_v[0, 0])
    dconv_w8 = dconv_w.reshape(CONV_B, N_DEV, -1).transpose(1, 0, 2)
    small_partial = _pack_sharded_small(dconv_w8, dffn_w8)
    dh1, (recv["w_in"], recv_small, recv_rep) = _matmul(
        "mm_in_t", dz, w_in3, dims=_NT, grid=(n_row, 1, N_DEV),
        a_spec=pl.BlockSpec((tm, n_in), lambda i, j, k: (i, k)),
        b_spec=pl.BlockSpec((None, d, n_in), lambda i, j, k: (k, 0, 0)),
        o_spec=row_spec, acc_shape=(tm, d), out_shape=jax.ShapeDtypeStruct((t, d), F32),
        exch=[("scatter", gw_in), ("scatter", small_partial), ("gather", rep_partial)])
    grad_x, dg_mix = _rms_bwd("rms_mix_bwd", x0, g_mix, dh1, dx1, want_bf16=False)
    (recv_g_mix,) = _exchange("exchange_g_mix", [("gather", _rows8(dg_mix))])

    grads, deltas, new_m, new_v = {}, {}, {}, {}
    by_kind = (grads, deltas, new_m, new_v)

    def two_d(a):
        a = sq(a)
        return a.reshape(-1, a.shape[-1])

    for k in _BIG:
        parts = recv[k].reshape(N_DEV, -1, recv[k].shape[-1])
        outs = _adamw("adamw_" + k, parts, two_d(wts[k]), two_d(mom[k]), two_d(var[k]), 128)
        for tgt, o in zip(by_kind, outs):
            tgt[k] = o.reshape(shapes[k])

    small = [_pack_sharded_small(sq(s["conv_b_w"]), sq(s["ffn_conv_w"])) for s in (wts, mom, var)]
    for tgt, o in zip(by_kind, _adamw("adamw_conv", recv_small, small[0], small[1], small[2], 56)):
        cw, fw = _unpack_sharded_small(o)
        tgt["conv_b_w"] = cw.reshape(shapes["conv_b_w"])
        tgt["ffn_conv_w"] = fw.reshape(shapes["ffn_conv_w"])

    rep = [_pack_replicated(s) for s in (wts, mom, var)]
    loss = None
    for tgt, o in zip(by_kind, _adamw("adamw_replicated", recv_rep, rep[0], rep[1], rep[2], 152)):
        un = _unpack_replicated(o, shapes)
        if tgt is grads:
            loss = un["loss"]
        for k in _VEC_NAMES + ("ffn_conv_b", "w_s"):
            tgt[k] = un[k]

    gm = [_rows8(s["g_mix"]) for s in (wts, mom, var)]
    for tgt, o in zip(by_kind, _adamw("adamw_g_mix", recv_g_mix, gm[0], gm[1], gm[2], 8)):
        tgt["g_mix"] = o[0, :].reshape(shapes["g_mix"])

    return (loss, grad_x.reshape(x.shape), *[grads[k] for k in _WEIGHTS], *[deltas[k] for k in _WEIGHTS],
            *[new_m[k] for k in _WEIGHTS], *[new_v[k] for k in _WEIGHTS])
```

```python
import math

import jax
import jax.numpy as jnp
from jax import lax
from jax.experimental import pallas as pl
from jax.experimental.pallas import tpu as pltpu

F32 = jnp.float32
BF16 = jnp.bfloat16

N_DEV = 8
EPS_RMS = 1e-6
EPS_LN = 1e-5
CHUNK = 128
GROUPS = 8
CONV_B = 31
CONV_F = 3
HALO_B = 32
HALO_F = 8
ROWS_F = 16
SUB = 8

ADAM_LR = 0.001
ADAM_B1 = 0.9
ADAM_B2 = 0.999
ADAM_EPS = 1e-08
ADAM_WD = 0.01
ADAM_STEP = 10

VMEM_LIMIT = 56 * 1024 * 1024
TM_MM = 1024
TT_MM = 512
TM_EW = 256

_NN = (((1,), (0,)), ((), ()))
_NT = (((1,), (1,)), ((), ()))
_TN = (((0,), (0,)), ((), ()))
MESH = pl.DeviceIdType.MESH
HBM_SPEC = pl.BlockSpec(memory_space=pltpu.HBM)


def _params(n_axes):
    return pltpu.CompilerParams(dimension_semantics=("arbitrary",) * n_axes, vmem_limit_bytes=VMEM_LIMIT)


def _gelu(x):
    k = math.sqrt(2.0 / math.pi)
    return 0.5 * x * (1.0 + jnp.tanh(k * (x + 0.044715 * (x * x * x))))


def _gelu_and_grad(x):
    k = math.sqrt(2.0 / math.pi)
    x2 = x * x
    t = jnp.tanh(k * (x + 0.044715 * (x2 * x)))
    g = 0.5 * x * (1.0 + t)
    dg = 0.5 * (1.0 + t) + 0.5 * x * (1.0 - t * t) * (k * (1.0 + 3.0 * 0.044715 * x2))
    return g, dg


def _sigmoid(x):
    return 1.0 / (1.0 + jnp.exp(-x))


def _rowsum(x):
    return jnp.sum(x, axis=0, keepdims=True)


def _mean(x):
    return jnp.mean(x, axis=-1, keepdims=True)


def _exchange_io(exch):
    n = len(exch)
    out_shape = [jax.ShapeDtypeStruct(v.shape if kind == "scatter" else (N_DEV,) + v.shape, v.dtype) for kind, v in exch]
    scratch = [pltpu.SemaphoreType.DMA((7 * n,)), pltpu.SemaphoreType.DMA((7 * n,)), pltpu.SemaphoreType.DMA((n,))] if n else []
    return [HBM_SPEC] * n, [HBM_SPEC] * n, out_shape, scratch


def _exchange_step(kinds, x_refs, o_refs, send_sems, recv_sems, local_sems):
    n = len(kinds)
    x, y, c = lax.axis_index("x"), lax.axis_index("y"), lax.axis_index("c")
    me = 4 * x + 2 * y + c

    def src(a, to_slot):
        return x_refs[a].at[to_slot] if kinds[a] == "scatter" else x_refs[a]

    mine = [pltpu.make_async_copy(src(a, me), o_refs[a].at[me], local_sems.at[a]) for a in range(n)]
    sends, recvs = [], []
    for m in range(1, N_DEV):
        mx, my, mc = (m >> 2) & 1, (m >> 1) & 1, m & 1
        px, py, pc = (1 - x if mx else x), (1 - y if my else y), (1 - c if mc else c)
        peer = 4 * px + 2 * py + pc
        for a in range(n):
            k = a * 7 + m - 1
            sends.append(pltpu.make_async_remote_copy(
                src_ref=src(a, peer), dst_ref=o_refs[a].at[me], send_sem=send_sems.at[k], recv_sem=recv_sems.at[k],
                device_id=(px, py, pc), device_id_type=MESH))
            recvs.append(pltpu.make_async_remote_copy(
                src_ref=src(a, peer), dst_ref=o_refs[a].at[peer], send_sem=send_sems.at[k], recv_sem=recv_sems.at[k],
                device_id=(px, py, pc), device_id_type=MESH))

    def start():
        for cp in mine + sends:
            cp.start()

    def finish():
        for cp in recvs:
            cp.wait_recv()
        for cp in sends:
            cp.wait_send()
        for cp in mine:
            cp.wait()

    return start, finish


def _exchange(name, exch):
    n = len(exch)
    kinds = [k for k, _ in exch]
    in_specs, out_specs, out_shape, scratch = _exchange_io(exch)

    def body(*refs):
        start, finish = _exchange_step(kinds, refs[:n], refs[n:2 * n], *refs[2 * n:])
        start()
        finish()

    return pl.pallas_call(body, in_specs=in_specs, out_specs=out_specs, out_shape=out_shape, scratch_shapes=scratch,
                          name=name)(*[v for _, v in exch])


def _matmul(name, a, b, *, dims, grid, a_spec, b_spec, o_spec, acc_shape, out_shape, res=None, res_spec=None, exch=()):
    nk = grid[2]
    has_res = res is not None
    n_in = 3 if has_res else 2
    n_ex = len(exch)
    kinds = [k for k, _ in exch]
    ex_in, ex_out, ex_shape, ex_scratch = _exchange_io(exch)

    def body(*refs):
        if has_res:
            a_ref, b_ref, r_ref = refs[:3]
        else:
            a_ref, b_ref = refs[:2]
            r_ref = None
        o_ref = refs[n_in + n_ex]
        if n_ex:
            pid = [pl.program_id(ax) for ax in range(3)]
            ex_start, ex_finish = _exchange_step(kinds, refs[n_in:n_in + n_ex], refs[n_in + n_ex + 1:n_in + 2 * n_ex + 1],
                                                 *refs[len(refs) - 3:])
            pl.when((pid[0] == 0) & (pid[1] == 0) & (pid[2] == 0))(ex_start)
        part = lax.dot_general(a_ref[...].astype(BF16), b_ref[...].astype(BF16), dims, preferred_element_type=F32)

        def finish(acc):
            if has_res:
                acc = acc + r_ref[...]
            o_ref[...] = acc.astype(o_ref.dtype)

        if nk == 1:
            finish(part)
        else:
            acc_ref = refs[n_in + 2 * n_ex + 1]
            k = pl.program_id(2)

            @pl.when(k == 0)
            def _():
                acc_ref[...] = part

            @pl.when(k > 0)
            def _():
                acc_ref[...] += part

            @pl.when(k == nk - 1)
            def _():
                finish(acc_ref[...])
        if n_ex:
            pl.when((pid[0] == grid[0] - 1) & (pid[1] == grid[1] - 1) & (pid[2] == grid[2] - 1))(ex_finish)

    in_specs = [a_spec, b_spec] + ([res_spec] if has_res else []) + ex_in
    args = (a, b) + ((res,) if has_res else ()) + tuple(v for _, v in exch)
    scratch = ([pltpu.VMEM(acc_shape, F32)] if nk > 1 else []) + ex_scratch
    outs = pl.pallas_call(body, grid=grid, in_specs=in_specs, out_specs=[o_spec] + ex_out, out_shape=[out_shape] + ex_shape,
                          scratch_shapes=scratch, name=name, compiler_params=_params(3))(*args)
    return (outs[0], outs[1:]) if n_ex else outs[0]


def _mm_rows(name, a, w, *, dims, out_dtype, res=None):
    t, k = a.shape
    n = w.shape[1] if dims == _NN else w.shape[0]
    tm = min(TM_MM, t)
    row = pl.BlockSpec((tm, n), lambda i, j, kk: (i, 0))
    return _matmul(name, a, w, dims=dims, grid=(t // tm, 1, 1),
                   a_spec=pl.BlockSpec((tm, k), lambda i, j, kk: (i, 0)),
                   b_spec=pl.BlockSpec(w.shape, lambda i, j, kk: (0, 0)),
                   o_spec=row, acc_shape=(tm, n), out_shape=jax.ShapeDtypeStruct((t, n), out_dtype),
                   res=res, res_spec=row if res is not None else None)


def _mm_wgrad(name, a, b, *, out_dtype):
    t, m = a.shape
    n = b.shape[1]
    tt = min(TT_MM, t)
    return _matmul(name, a, b, dims=_TN, grid=(1, 1, t // tt),
                   a_spec=pl.BlockSpec((tt, m), lambda i, j, kk: (kk, 0)),
                   b_spec=pl.BlockSpec((tt, n), lambda i, j, kk: (kk, 0)),
                   o_spec=pl.BlockSpec((m, n), lambda i, j, kk: (0, 0)),
                   acc_shape=(m, n), out_shape=jax.ShapeDtypeStruct((m, n), out_dtype))


def _vec_spec(d):
    return pl.BlockSpec((1, d), lambda i: (0, 0))


def _row_call(name, body, *, n_steps, in_specs, out_specs, out_shape, scratch, args, exch=()):
    n_in, n_out, n_scr, n_ex = len(in_specs), len(out_specs), len(scratch), len(exch)
    kinds = [k for k, _ in exch]
    ex_in, ex_out, ex_shape, ex_scratch = _exchange_io(exch)

    def wrapped(*refs):
        ins, rest = refs[:n_in], refs[n_in:]
        x_refs, rest = rest[:n_ex], rest[n_ex:]
        outs, rest = rest[:n_out], rest[n_out:]
        o_refs, rest = rest[:n_ex], rest[n_ex:]
        scr, sems = rest[:n_scr], rest[n_scr:]
        if n_ex:
            ex_start, ex_finish = _exchange_step(kinds, x_refs, o_refs, *sems)
            pl.when(pl.program_id(0) == 0)(ex_start)
        body(*ins, *outs, *scr)
        if n_ex:
            pl.when(pl.program_id(0) == n_steps - 1)(ex_finish)

    res = pl.pallas_call(wrapped, grid=(n_steps,), in_specs=list(in_specs) + ex_in, out_specs=list(out_specs) + ex_out,
                         out_shape=list(out_shape) + ex_shape, scratch_shapes=list(scratch) + ex_scratch, name=name,
                         compiler_params=_params(1))(*args, *[v for _, v in exch])
    return res[:n_out], res[n_out:]


def _rms_fwd(name, x, g):
    t, d = x.shape
    tm = min(TM_EW * 2, t)

    def body(x_ref, g_ref, h_ref):
        xv = x_ref[...]
        rstd = lax.rsqrt(_mean(xv * xv) + EPS_RMS)
        h_ref[...] = ((xv * rstd) * g_ref[...]).astype(BF16)

    row = pl.BlockSpec((tm, d), lambda i: (i, 0))
    return pl.pallas_call(body, grid=(t // tm,), in_specs=[row, _vec_spec(d)], out_specs=row,
                          out_shape=jax.ShapeDtypeStruct((t, d), BF16), name=name, compiler_params=_params(1))(x, g)


def _rms_bwd(name, x, g, dh, dres, *, want_bf16):
    t, d = x.shape
    tm = min(TM_EW * 2, t)
    n = t // tm

    def body(x_ref, g_ref, dh_ref, dres_ref, dx_ref, *rest):
        dg_ref = rest[-1]
        i = pl.program_id(0)
        xv = x_ref[...]
        rstd = lax.rsqrt(_mean(xv * xv) + EPS_RMS)
        nrm = xv * rstd
        dhv = dh_ref[...].astype(F32)
        dn = dhv * g_ref[...]
        dx = dres_ref[...] + rstd * (dn - nrm * _mean(dn * nrm))
        dx_ref[...] = dx
        if want_bf16:
            rest[0][...] = dx.astype(BF16)
        part = _rowsum(dhv * nrm)

        @pl.when(i == 0)
        def _():
            dg_ref[...] = part

        @pl.when(i > 0)
        def _():
            dg_ref[...] += part

    row = pl.BlockSpec((tm, d), lambda i: (i, 0))
    out_specs = [row] + ([row] if want_bf16 else []) + [_vec_spec(d)]
    out_shape = ([jax.ShapeDtypeStruct((t, d), F32)] + ([jax.ShapeDtypeStruct((t, d), BF16)] if want_bf16 else [])
                 + [jax.ShapeDtypeStruct((1, d), F32)])
    return pl.pallas_call(body, grid=(n,), in_specs=[row, _vec_spec(d), row, row], out_specs=out_specs,
                          out_shape=out_shape, name=name, compiler_params=_params(1))(x, g, dh, dres)


def _fill_shifted(buf_ref, sh_ref):
    n = sh_ref.shape[1]
    for p in range(1, SUB):
        sh_ref[p - 1] = buf_ref[p:p + n, :]


def _branch_fwd(z, ln_v_g, ln_v_b, ws_m, bias_full, conv_w, conv_b, ln_b_g, ln_b_b, seq, exch=()):
    t = z.shape[0]
    w = 1024
    tm = min(TM_EW, seq)
    tiles_per_seq = seq // tm
    n_chunks = tm // CHUNK

    def body(z_ref, lvg_ref, lvb_ref, ws_ref, bias_ref, cw_ref, cb_ref, lbg_ref, lbb_ref,
             pa_ref, cs_ref, c_ref, hist_ref, buf_ref, mix_ref, sh_ref, wb_ref):
        i = pl.program_id(0)
        u = z_ref[:, 0:w].astype(F32)
        v = z_ref[:, w:2 * w].astype(F32)
        ug = _gelu(u)
        vg = _gelu(v)
        dv = vg - _mean(vg)
        vhat = dv * lax.rsqrt(_mean(dv * dv) + EPS_LN)
        vn = (vhat * lvg_ref[...] + lvb_ref[...]).astype(BF16)
        for ci in range(n_chunks):
            rows = slice(ci * CHUNK, (ci + 1) * CHUNK)
            for g in range(GROUPS):
                cols = slice(g * CHUNK, (g + 1) * CHUNK)
                mix_ref[rows, cols] = lax.dot_general(ws_ref[g], vn[rows, cols], _NN, preferred_element_type=F32)
            mix_ref[rows, :] += bias_ref[...]
        pa_ref[...] = (ug * mix_ref[...]).astype(BF16)

        a = z_ref[:, 2 * w:3 * w].astype(F32)
        gl = z_ref[:, 3 * w:4 * w].astype(F32)
        glu = a * _sigmoid(gl)

        @pl.when(i % tiles_per_seq == 0)
        def _():
            hist_ref[...] = jnp.zeros_like(hist_ref)

        buf_ref[0:HALO_B, :] = hist_ref[...]
        buf_ref[HALO_B:, :] = glu
        hist_ref[...] = glu[tm - HALO_B:, :]
        _fill_shifted(buf_ref, sh_ref)

        @pl.when(i == 0)
        def _():
            for k in range(CONV_B):
                wb_ref[k] = jnp.broadcast_to(cw_ref[k:k + 1, :], (SUB, w))

        groups = 4

        def strip(si, _):
            s = pl.multiple_of(si * (groups * SUB), groups * SUB)
            acc = [jnp.zeros((SUB, w), F32) + cb_ref[...] for _ in range(groups)]
            for k in range(CONV_B):
                whole, part = divmod(HALO_B - (CONV_B - 1) + k, SUB)
                wk = wb_ref[k]
                for g in range(groups):
                    at = pl.ds(s + SUB * (whole + g), SUB)
                    acc[g] = acc[g] + wk * (buf_ref[at, :] if part == 0 else sh_ref[part - 1, at, :])
            for g in range(0, groups, 2):
                at = pl.ds(s + SUB * g, 2 * SUB)
                c = jnp.concatenate(acc[g:g + 2], axis=0)
                c_ref[at, :] = c
                dc = c - _mean(c)
                chat = dc * lax.rsqrt(_mean(dc * dc) + EPS_LN)
                cn = chat * lbg_ref[...] + lbb_ref[...]
                cs_ref[at, :] = (cn * _sigmoid(cn)).astype(BF16)
            return 0

        lax.fori_loop(0, tm // (groups * SUB), strip, 0)

    row = pl.BlockSpec((tm, w), lambda i: (i, 0))
    in_specs = [pl.BlockSpec((tm, 4 * w), lambda i: (i, 0)), _vec_spec(w), _vec_spec(w),
                pl.BlockSpec((GROUPS, CHUNK, CHUNK), lambda i: (0, 0, 0)), pl.BlockSpec((CHUNK, w), lambda i: (0, 0)),
                pl.BlockSpec((CONV_B, w), lambda i: (0, 0)), _vec_spec(w), _vec_spec(w), _vec_spec(w)]
    return _row_call(
        "branch_fwd", body, n_steps=t // tm, in_specs=in_specs, out_specs=[row, row, row],
        out_shape=[jax.ShapeDtypeStruct((t, w), BF16), jax.ShapeDtypeStruct((t, w), BF16), jax.ShapeDtypeStruct((t, w), F32)],
        scratch=[pltpu.VMEM((HALO_B, w), F32), pltpu.VMEM((HALO_B + tm, w), F32), pltpu.VMEM((tm, w), F32),
                 pltpu.VMEM((SUB - 1, HALO_B + tm - SUB, w), F32), pltpu.VMEM((CONV_B, SUB, w), F32)],
        args=(z, ln_v_g, ln_v_b, ws_m, bias_full, conv_w, conv_b, ln_b_g, ln_b_b), exch=exch)


def _merge_fwd(z, ya, yb):
    t, w = ya.shape
    tm = min(TM_EW * 2, t)

    def body(ga_ref, gb_ref, ya_ref, yb_ref, o_ref):
        sa = _sigmoid(ga_ref[...].astype(F32))
        sb = _sigmoid(gb_ref[...].astype(F32))
        o_ref[...] = (sa * ya_ref[...].astype(F32) + sb * yb_ref[...].astype(F32)).astype(BF16)

    row = pl.BlockSpec((tm, w), lambda i: (i, 0))
    in_specs = [pl.BlockSpec((tm, w), lambda i: (i, 4)), pl.BlockSpec((tm, w), lambda i: (i, 5)), row, row]
    return pl.pallas_call(body, grid=(t // tm,), in_specs=in_specs, out_specs=row,
                          out_shape=jax.ShapeDtypeStruct((t, w), BF16), name="merge_fwd",
                          compiler_params=_params(1))(z, z, ya, yb)


def _merge_bwd(z, ya, yb, dmerged):
    t, w = ya.shape
    tm = min(TM_EW * 2, t)

    def body(ga_ref, gb_ref, ya_ref, yb_ref, dm_ref, dya_ref, dyb_ref, dg_ref):
        sa = _sigmoid(ga_ref[...].astype(F32))
        sb = _sigmoid(gb_ref[...].astype(F32))
        dm = dm_ref[...].astype(F32)
        dya_ref[...] = (dm * sa).astype(BF16)
        dyb_ref[...] = (dm * sb).astype(BF16)
        dg_ref[:, 0:w] = (dm * ya_ref[...].astype(F32) * sa * (1.0 - sa)).astype(BF16)
        dg_ref[:, w:2 * w] = (dm * yb_ref[...].astype(F32) * sb * (1.0 - sb)).astype(BF16)

    row = pl.BlockSpec((tm, w), lambda i: (i, 0))
    in_specs = [pl.BlockSpec((tm, w), lambda i: (i, 4)), pl.BlockSpec((tm, w), lambda i: (i, 5)), row, row, row]
    return pl.pallas_call(
        body, grid=(t // tm,), in_specs=in_specs, out_specs=[row, row, pl.BlockSpec((tm, 2 * w), lambda i: (i, 0))],
        out_shape=[jax.ShapeDtypeStruct((t, w), BF16), jax.ShapeDtypeStruct((t, w), BF16),
                   jax.ShapeDtypeStruct((t, 2 * w), BF16)],
        name="merge_bwd", compiler_params=_params(1))(z, z, ya, yb, dmerged)


def _branch_bwd(z, c_saved, dpa, dcs, dgates, ln_v_g, ln_v_b, ws_m, ws_mt, bias_full, conv_w, ln_b_g, ln_b_b, seq, exch=()):
    t = z.shape[0]
    w = 1024
    tm = min(TM_EW, seq)
    tiles_per_seq = seq // tm
    n_tiles = t // tm
    n_chunks = tm // CHUNK

    def body(z_ref, c_ref, dpa_ref, dcs_ref, dgt_ref, lvg_ref, lvb_ref, ws_ref, wst_ref, bias_ref, cw_ref,
             lbg_ref, lbb_ref,
             dz_ref, dlvg_ref, dlvb_ref, dws_ref, dbs_ref, dcw_ref, dcb_ref, dlbg_ref, dlbb_ref,
             carry_ref, glu_ref, dbuf_ref, mix_ref, dvn_ref, dbs_acc_ref, dglu_ref, sh_ref, wb_ref, dwacc_ref):
        i = pl.program_id(0)
        r = n_tiles - 1 - i

        @pl.when(i == 0)
        def _():
            for ref in (dlvg_ref, dlvb_ref, dws_ref, dbs_acc_ref, dwacc_ref, dcb_ref, dlbg_ref, dlbb_ref):
                ref[...] = jnp.zeros_like(ref)

        u = z_ref[:, 0:w].astype(F32)
        v = z_ref[:, w:2 * w].astype(F32)
        ug, dug = _gelu_and_grad(u)
        vg, dvg = _gelu_and_grad(v)
        dv0 = vg - _mean(vg)
        rstd_v = lax.rsqrt(_mean(dv0 * dv0) + EPS_LN)
        vhat = dv0 * rstd_v
        vn = (vhat * lvg_ref[...] + lvb_ref[...]).astype(BF16)
        dpa = dpa_ref[...].astype(F32)
        dmix = dpa * ug
        dmix_b = dmix.astype(BF16)
        for ci in range(n_chunks):
            rows = slice(ci * CHUNK, (ci + 1) * CHUNK)
            for g in range(GROUPS):
                cols = slice(g * CHUNK, (g + 1) * CHUNK)
                mix_ref[rows, cols] = lax.dot_general(ws_ref[g], vn[rows, cols], _NN, preferred_element_type=F32)
                dvn_ref[rows, cols] = lax.dot_general(wst_ref[g], dmix_b[rows, cols], _NN, preferred_element_type=F32)
                dws_ref[g] += lax.dot_general(dmix_b[rows, cols], vn[rows, cols], _NT, preferred_element_type=F32)
            mix_ref[rows, :] += bias_ref[...]
            dbs_acc_ref[...] += dmix[rows, :]
        dz_ref[:, 0:w] = (dpa * mix_ref[...] * dug).astype(BF16)
        dvn = dvn_ref[...]
        dlvg_ref[...] += _rowsum(dvn * vhat)
        dlvb_ref[...] += _rowsum(dvn)
        dvh = dvn * lvg_ref[...]
        dvg_in = rstd_v * (dvh - _mean(dvh) - vhat * _mean(dvh * vhat))
        dz_ref[:, w:2 * w] = (dvg_in * dvg).astype(BF16)

        c = c_ref[...]
        dc0 = c - _mean(c)
        rstd_c = lax.rsqrt(_mean(dc0 * dc0) + EPS_LN)
        chat = dc0 * rstd_c
        cn = chat * lbg_ref[...] + lbb_ref[...]
        sg = _sigmoid(cn)
        dcn = dcs_ref[...].astype(F32) * (sg * (1.0 + cn * (1.0 - sg)))
        dlbg_ref[...] += _rowsum(dcn * chat)
        dlbb_ref[...] += _rowsum(dcn)
        dch = dcn * lbg_ref[...]
        dc = rstd_c * (dch - _mean(dch) - chat * _mean(dch * chat))
        dcb_ref[...] += _rowsum(dc)

        a = z_ref[:, 2 * w:3 * w].astype(F32)
        gl = z_ref[:, 3 * w:4 * w].astype(F32)
        sgl = _sigmoid(gl)
        glu_ref[...] = a * sgl

        @pl.when(r % tiles_per_seq == tiles_per_seq - 1)
        def _():
            carry_ref[...] = jnp.zeros_like(carry_ref)

        dbuf_ref[0:tm, :] = dc
        dbuf_ref[tm:, :] = carry_ref[...]
        carry_ref[...] = dc[0:HALO_B, :]
        _fill_shifted(dbuf_ref, sh_ref)

        @pl.when(i == 0)
        def _():
            for k in range(CONV_B):
                wb_ref[k] = jnp.broadcast_to(cw_ref[k:k + 1, :], (SUB, w))

        groups = 2

        def strip(si, _):
            s = pl.multiple_of(si * (groups * SUB), groups * SUB)
            glu_rows = [glu_ref[pl.ds(s + SUB * g, SUB), :] for g in range(groups)]
            acc = [jnp.zeros((SUB, w), F32) for _ in range(groups)]
            for k in range(CONV_B):
                whole, part = divmod(CONV_B - 1 - k, SUB)
                wk = wb_ref[k]
                dw_part = jnp.zeros((SUB, w), F32)
                for g in range(groups):
                    at = pl.ds(s + SUB * (whole + g), SUB)
                    d_rows = dbuf_ref[at, :] if part == 0 else sh_ref[part - 1, at, :]
                    acc[g] = acc[g] + wk * d_rows
                    dw_part = dw_part + d_rows * glu_rows[g]
                dwacc_ref[k] += dw_part
            dglu_ref[pl.ds(s, groups * SUB), :] = jnp.concatenate(acc, axis=0)
            return 0

        lax.fori_loop(0, tm // (groups * SUB), strip, 0)
        dglu = dglu_ref[...]
        dz_ref[:, 2 * w:3 * w] = (dglu * sgl).astype(BF16)
        dz_ref[:, 3 * w:4 * w] = (dglu * a * sgl * (1.0 - sgl)).astype(BF16)
        dz_ref[:, 4 * w:6 * w] = dgt_ref[...]

        @pl.when(i == n_tiles - 1)
        def _():
            tri = lax.broadcasted_iota(jnp.int32, (CHUNK, CHUNK), 0) >= lax.broadcasted_iota(jnp.int32, (CHUNK, CHUNK), 1)
            lane = lax.broadcasted_iota(jnp.int32, (CHUNK, CHUNK), 1)
            dbs = jnp.zeros((CHUNK, CHUNK), F32)
            for g in range(GROUPS):
                dws_ref[g] = jnp.where(tri, dws_ref[g], 0.0)
                group_sum = jnp.sum(dbs_acc_ref[:, g * CHUNK:(g + 1) * CHUNK], axis=1, keepdims=True)
                dbs = jnp.where(lane == g, group_sum, dbs)
            dbs_ref[...] = dbs
            for k in range(CONV_B):
                dcw_ref[k:k + 1, :] = _rowsum(dwacc_ref[k])

    def rev(i):
        return n_tiles - 1 - i

    row = pl.BlockSpec((tm, w), lambda i: (rev(i), 0))
    full = lambda shape: pl.BlockSpec(shape, lambda i: (0,) * len(shape))
    in_specs = [pl.BlockSpec((tm, 4 * w), lambda i: (rev(i), 0)),
                row, row, row, pl.BlockSpec((tm, 2 * w), lambda i: (rev(i), 0)),
                _vec_spec(w), _vec_spec(w), full((GROUPS, CHUNK, CHUNK)), full((GROUPS, CHUNK, CHUNK)), full((CHUNK, w)),
                full((CONV_B, w)), _vec_spec(w), _vec_spec(w)]
    out_specs = [pl.BlockSpec((tm, 6 * w), lambda i: (rev(i), 0)), _vec_spec(w), _vec_spec(w), full((GROUPS, CHUNK, CHUNK)),
                 full((CHUNK, CHUNK)), full((CONV_B, w)), _vec_spec(w), _vec_spec(w), _vec_spec(w)]
    vec = jax.ShapeDtypeStruct((1, w), F32)
    out_shape = [jax.ShapeDtypeStruct((t, 6 * w), BF16), vec, vec, jax.ShapeDtypeStruct((GROUPS, CHUNK, CHUNK), F32),
                 jax.ShapeDtypeStruct((CHUNK, CHUNK), F32), jax.ShapeDtypeStruct((CONV_B, w), F32), vec, vec, vec]
    scratch = [pltpu.VMEM((HALO_B, w), F32), pltpu.VMEM((tm, w), F32), pltpu.VMEM((tm + HALO_B, w), F32),
               pltpu.VMEM((tm, w), F32), pltpu.VMEM((tm, w), F32), pltpu.VMEM((CHUNK, w), F32), pltpu.VMEM((tm, w), F32),
               pltpu.VMEM((SUB - 1, HALO_B + tm - SUB, w), F32), pltpu.VMEM((CONV_B, SUB, w), F32),
               pltpu.VMEM((CONV_B, SUB, w), F32)]
    return _row_call("branch_bwd", body, n_steps=n_tiles, in_specs=in_specs, out_specs=out_specs, out_shape=out_shape,
                     scratch=scratch, exch=exch,
                     args=(z, c_saved, dpa, dcs, dgates, ln_v_g, ln_v_b, ws_m, ws_mt, bias_full, conv_w, ln_b_g, ln_b_b))


def _conv3_window(prev8, x):
    win = jnp.concatenate([prev8, x], axis=0)
    n = x.shape[0]
    return [win[HALO_F - 2:HALO_F - 2 + n], win[HALO_F - 1:HALO_F - 1 + n], x]


def _ffn_mid_fwd(up0, conv_w, conv_b, seq):
    nb, t, f = up0.shape
    half = nb // 2
    tm = min(TM_EW, seq)
    tiles_per_seq = seq // tm
    n_strips = tm // ROWS_F

    def body(up_ref, w_ref, b_ref, act_ref, upc_ref, hist_ref):
        i = pl.program_id(0)

        @pl.when(i % tiles_per_seq == 0)
        def _():
            hist_ref[...] = jnp.zeros_like(hist_ref)

        for j in range(half):
            jv = j + half
            wg = [w_ref[j, k:k + 1, :] for k in range(CONV_F)]
            wv = [w_ref[jv, k:k + 1, :] for k in range(CONV_F)]
            bg, bv = b_ref[j], b_ref[jv]

            def strip(c, carry):
                rows = pl.ds(pl.multiple_of(c * ROWS_F, ROWS_F), ROWS_F)
                xg = up_ref[j, rows, :].astype(F32)
                xv = up_ref[jv, rows, :].astype(F32)
                sg = _conv3_window(carry[0], xg)
                sv = _conv3_window(carry[1], xv)
                gate = bg + wg[0] * sg[0] + wg[1] * sg[1] + wg[2] * sg[2]
                val = bv + wv[0] * sv[0] + wv[1] * sv[1] + wv[2] * sv[2]
                act_ref[j, rows, :] = (_gelu(gate) * val).astype(BF16)
                upc_ref[j, rows, :] = gate.astype(BF16)
                upc_ref[jv, rows, :] = val.astype(BF16)
                return xg[ROWS_F - HALO_F:], xv[ROWS_F - HALO_F:]

            last = lax.fori_loop(0, n_strips, strip, (hist_ref[j], hist_ref[jv]))
            hist_ref[j] = last[0]
            hist_ref[jv] = last[1]

    return pl.pallas_call(
        body, grid=(t // tm,),
        in_specs=[pl.BlockSpec((nb, tm, f), lambda i: (0, i, 0)), pl.BlockSpec((nb, CONV_F, f), lambda i: (0, 0, 0)),
                  pl.BlockSpec((nb, 1, f), lambda i: (0, 0, 0))],
        out_specs=[pl.BlockSpec((half, tm, f), lambda i: (0, i, 0)), pl.BlockSpec((nb, tm, f), lambda i: (0, i, 0))],
        out_shape=[jax.ShapeDtypeStruct((half, t, f), BF16), jax.ShapeDtypeStruct((nb, t, f), BF16)],
        scratch_shapes=[pltpu.VMEM((nb, HALO_F, f), F32)],
        name="ffn_mid_fwd", compiler_params=_params(1))(up0, conv_w, conv_b)


def _ffn_mid_bwd(up0, upc, dact, conv_w, seq, exch=()):
    nb, t, f = up0.shape
    half = nb // 2
    tm = min(TM_EW, seq)
    tiles_per_seq = seq // tm
    n_tiles = t // tm
    n_strips = tm // ROWS_F

    def body(up_ref, upc_ref, da_ref, w_ref, dup_ref, dw_ref, db_ref, carry_ref, dwacc_ref, dbacc_ref):
        i = pl.program_id(0)
        r = n_tiles - 1 - i

        @pl.when(i == 0)
        def _():
            dwacc_ref[...] = jnp.zeros_like(dwacc_ref)
            dbacc_ref[...] = jnp.zeros_like(dbacc_ref)

        @pl.when(r % tiles_per_seq == tiles_per_seq - 1)
        def _():
            carry_ref[...] = jnp.zeros_like(carry_ref)

        for j in range(half):
            jv = j + half
            wg = [w_ref[j, k:k + 1, :] for k in range(CONV_F)]
            wv = [w_ref[jv, k:k + 1, :] for k in range(CONV_F)]

            def strip(ci, carry):
                rows = pl.ds(pl.multiple_of((n_strips - 1 - ci) * ROWS_F, ROWS_F), ROWS_F)
                val = upc_ref[jv, rows, :].astype(F32)
                gg, dgg = _gelu_and_grad(upc_ref[j, rows, :].astype(F32))
                da = da_ref[j, rows, :].astype(F32)
                d_gate = da * val * dgg
                d_val = da * gg
                for blk, d, nxt, wk in ((j, d_gate, carry[0], wg), (jv, d_val, carry[1], wv)):
                    dbacc_ref[blk] += d
                    dwin = jnp.concatenate([d, nxt], axis=0)
                    shifted = [dwin[2:2 + ROWS_F], dwin[1:1 + ROWS_F], d]
                    x = up_ref[blk, rows, :].astype(F32)
                    for k in range(CONV_F):
                        dwacc_ref[blk, k] += shifted[k] * x
                    dx = wk[0] * shifted[0] + wk[1] * shifted[1] + wk[2] * shifted[2]
                    dup_ref[blk, rows, :] = dx.astype(BF16)
                return d_gate[0:HALO_F], d_val[0:HALO_F]

            carry = lax.fori_loop(0, n_strips, strip, (carry_ref[j], carry_ref[jv]))
            carry_ref[j] = carry[0]
            carry_ref[jv] = carry[1]

        @pl.when(i == n_tiles - 1)
        def _():
            for blk in range(nb):
                db_ref[blk] = _rowsum(dbacc_ref[blk])
                for k in range(CONV_F):
                    dw_ref[blk, k:k + 1, :] = _rowsum(dwacc_ref[blk, k])

    def rev(i):
        return n_tiles - 1 - i

    return _row_call(
        "ffn_mid_bwd", body, n_steps=n_tiles,
        in_specs=[pl.BlockSpec((nb, tm, f), lambda i: (0, rev(i), 0)), pl.BlockSpec((nb, tm, f), lambda i: (0, rev(i), 0)),
                  pl.BlockSpec((half, tm, f), lambda i: (0, rev(i), 0)),
                  pl.BlockSpec((nb, CONV_F, f), lambda i: (0, 0, 0))],
        out_specs=[pl.BlockSpec((nb, tm, f), lambda i: (0, rev(i), 0)), pl.BlockSpec((nb, CONV_F, f), lambda i: (0, 0, 0)),
                   pl.BlockSpec((nb, 1, f), lambda i: (0, 0, 0))],
        out_shape=[jax.ShapeDtypeStruct((nb, t, f), BF16), jax.ShapeDtypeStruct((nb, CONV_F, f), F32),
                   jax.ShapeDtypeStruct((nb, 1, f), F32)],
        scratch=[pltpu.VMEM((nb, HALO_F, f), F32), pltpu.VMEM((nb, CONV_F, ROWS_F, f), F32), pltpu.VMEM((nb, ROWS_F, f), F32)],
        args=(up0, upc, dact, conv_w), exch=exch)


def _head(x2, q, r, target, g_ple, g_final):
    t, d = x2.shape
    tm = min(TM_EW * 2, t)

    def body(x2_ref, q_ref, r_ref, tg_ref, gple_ref, gfin_ref, dx3_ref, dq_ref, dr_ref, loss_ref, dgfin_ref, dgple_ref):
        i = pl.program_id(0)
        pg = _sigmoid(q_ref[...])
        rv = r_ref[...]
        rstd_r = lax.rsqrt(_mean(rv * rv) + EPS_RMS)
        nr = rv * rstd_r
        pe = nr * gple_ref[...]
        x3 = x2_ref[...] + pe * pg
        rstd3 = lax.rsqrt(_mean(x3 * x3) + EPS_RMS)
        n3 = x3 * rstd3
        err = n3 * gfin_ref[...] - tg_ref[...]
        loss_part = jnp.sum(_rowsum(err * err), axis=1, keepdims=True) * (0.5 / d)
        dy = err * (1.0 / d)
        dn3 = dy * gfin_ref[...]
        dx3 = rstd3 * (dn3 - n3 * _mean(dn3 * n3))
        dx3_ref[...] = dx3
        dq_ref[...] = (dx3 * pe * pg * (1.0 - pg)).astype(BF16)
        dpe = dx3 * pg
        dnr = dpe * gple_ref[...]
        dr_ref[...] = (rstd_r * (dnr - nr * _mean(dnr * nr))).astype(BF16)
        dgfin = _rowsum(dy * n3)
        dgple = _rowsum(dpe * nr)

        @pl.when(i == 0)
        def _():
            loss_ref[...] = jnp.zeros_like(loss_ref) + loss_part
            dgfin_ref[...] = dgfin
            dgple_ref[...] = dgple

        @pl.when(i > 0)
        def _():
            loss_ref[...] += loss_part
            dgfin_ref[...] += dgfin
            dgple_ref[...] += dgple

    row = pl.BlockSpec((tm, d), lambda i: (i, 0))
    vec = jax.ShapeDtypeStruct((1, d), F32)
    return pl.pallas_call(
        body, grid=(t // tm,), in_specs=[row, row, row, row, _vec_spec(d), _vec_spec(d)],
        out_specs=[row, row, row, _vec_spec(d), _vec_spec(d), _vec_spec(d)],
        out_shape=[jax.ShapeDtypeStruct((t, d), F32), jax.ShapeDtypeStruct((t, d), BF16), jax.ShapeDtypeStruct((t, d), BF16),
                   vec, vec, vec],
        name="head", compiler_params=_params(1))(x2, q, r, target, g_ple, g_final)


def _all_gather(xs):
    n = len(xs)

    def body(*refs):
        x_refs, o_refs = refs[:n], refs[n:2 * n]
        send_sems, recv_sems, local_sems = refs[2 * n:]
        x, y, c = lax.axis_index("x"), lax.axis_index("y"), lax.axis_index("c")
        me, sibling = (x, y, c), (x, y, 1 - c)
        chips = [(1 - x, y), (x, 1 - y), (1 - x, 1 - y)]

        def slot(pos):
            return 4 * pos[0] + 2 * pos[1] + pos[2]

        def copy(a, k, block, to, src=None):
            dst = o_refs[a].at[slot(block)]
            return pltpu.make_async_remote_copy(
                src_ref=dst if src is None else src, dst_ref=dst, send_sem=send_sems.at[a * 7 + k],
                recv_sem=recv_sems.at[a * 7 + k], device_id=to, device_id_type=MESH)

        mine = [pltpu.make_async_copy(x_refs[a], o_refs[a].at[slot(me)], local_sems.at[a]) for a in range(n)]
        for cp in mine:
            cp.start()
        first = []
        for a in range(n):
            first.append(copy(a, 0, me, sibling, src=x_refs[a]))
            first += [copy(a, 1 + j, me, (*chip, c), src=x_refs[a]) for j, chip in enumerate(chips)]
        for cp in first:
            cp.start()
        passed = []
        for j, chip in enumerate(chips):
            for a in range(n):
                copy(a, 1 + j, (*chip, c), me).wait_recv()
                cp = copy(a, 4 + j, (*chip, c), sibling)
                cp.start()
                passed.append(cp)
        for a in range(n):
            copy(a, 0, sibling, me).wait_recv()
        for j, chip in enumerate(chips):
            for a in range(n):
                copy(a, 4 + j, (*chip, 1 - c), me).wait_recv()
        for cp in first + passed:
            cp.wait_send()
        for cp in mine:
            cp.wait()

    return pl.pallas_call(
        body, in_specs=[HBM_SPEC] * n, out_specs=[HBM_SPEC] * n,
        out_shape=[jax.ShapeDtypeStruct((N_DEV,) + v.shape, v.dtype) for v in xs],
        scratch_shapes=[pltpu.SemaphoreType.DMA((7 * n,)), pltpu.SemaphoreType.DMA((7 * n,)), pltpu.SemaphoreType.DMA((n,))],
        name="all_gather_weights")(*xs)


def _adamw(name, parts, w, m, v, rows_per_step):
    r, c = w.shape
    tr = r if r <= rows_per_step else (rows_per_step if r % rows_per_step == 0 else r // 2)
    c1 = 1.0 - ADAM_B1 ** ADAM_STEP
    c2 = 1.0 - ADAM_B2 ** ADAM_STEP

    def body(p_ref, w_ref, m_ref, v_ref, g_ref, d_ref, mo_ref, vo_ref):
        g = p_ref[0].astype(F32)
        for s in range(1, N_DEV):
            g = g + p_ref[s].astype(F32)
        m_new = ADAM_B1 * m_ref[...] + (1.0 - ADAM_B1) * g
        v_new = ADAM_B2 * v_ref[...] + (1.0 - ADAM_B2) * (g * g)
        g_ref[...] = g
        mo_ref[...] = m_new
        vo_ref[...] = v_new
        d_ref[...] = -ADAM_LR * ((m_new / c1) / (jnp.sqrt(v_new / c2) + ADAM_EPS) + ADAM_WD * w_ref[...])

    row = pl.BlockSpec((tr, c), lambda i: (i, 0))
    out = jax.ShapeDtypeStruct((r, c), F32)
    return pl.pallas_call(body, grid=(r // tr,), in_specs=[pl.BlockSpec((N_DEV, tr, c), lambda i: (0, i, 0)), row, row, row],
                          out_specs=[row, row, row, row], out_shape=[out, out, out, out], name=name,
                          compiler_params=_params(1))(parts, w, m, v)


_VEC_NAMES = ("ln_v_g", "ln_v_b", "b_s", "conv_b_b", "ln_b_g", "ln_b_b", "g_ffn", "g_pg", "g_ple", "g_final")
_LOSS_ROW = len(_VEC_NAMES)


def _pack_replicated(d, loss=None):
    head = jnp.zeros((16, 1024), F32)
    for i, k in enumerate(_VEC_NAMES):
        head = head.at[i].set(d[k].reshape(1024))
    if loss is not None:
        head = head.at[_LOSS_ROW, 0].set(loss)
    fb = jnp.pad(d["ffn_conv_b"].reshape(-1), (0, 6144 - 5632)).reshape(6, 1024)
    return jnp.concatenate([head, jnp.pad(fb, ((0, 2), (0, 0))), d["w_s"].reshape(128, 1024)], axis=0)


def _unpack_replicated(pk, shapes):
    out = {k: pk[i, :].reshape(shapes[k]) for i, k in enumerate(_VEC_NAMES)}
    out["loss"] = pk[_LOSS_ROW, 0]
    out["ffn_conv_b"] = pk[16:22, :].reshape(-1)[:5632].reshape(shapes["ffn_conv_b"])
    out["w_s"] = pk[24:152, :].reshape(shapes["w_s"])
    return out


def _rows8(vec):
    return jnp.pad(vec.reshape(1, 1024), ((0, 7), (0, 0)))


def _pack_sharded_small(conv_b_w, ffn_conv_w):
    lead = conv_b_w.shape[:-2]
    pad0 = [(0, 0)] * len(lead)
    a = jnp.pad(conv_b_w, pad0 + [(0, 1), (0, 0)])
    b = jnp.pad(ffn_conv_w.reshape(lead + (CONV_F * 704,)), pad0 + [(0, 24 * 128 - CONV_F * 704)]).reshape(lead + (24, 128))
    return jnp.concatenate([a, b], axis=-2)


def _unpack_sharded_small(pk):
    lead = pk.shape[:-2]
    conv_b_w = pk[..., 0:CONV_B, :]
    ffn = pk[..., 32:56, :].reshape(lead + (24 * 128,))[..., :CONV_F * 704].reshape(lead + (CONV_F, 704))
    return conv_b_w, ffn


_WEIGHTS = ("g_mix", "w_in", "ln_v_g", "ln_v_b", "w_s", "b_s", "w_a_out", "conv_b_w", "conv_b_b", "ln_b_g", "ln_b_b",
            "w_b_out", "w_o", "g_ffn", "w_up", "ffn_conv_w", "ffn_conv_b", "w_down", "g_pg", "w_pg", "w_ple", "g_ple",
            "g_final")
_BIG = ("w_in", "w_a_out", "w_b_out", "w_o", "w_up", "w_down", "w_pg", "w_ple")


def kernel(x, p, g_mix, w_in, ln_v_g, ln_v_b, w_s, b_s, w_a_out, conv_b_w, conv_b_b, ln_b_g, ln_b_b, w_b_out, w_o, g_ffn, w_up, ffn_conv_w, ffn_conv_b, w_down, g_pg, w_pg, w_ple, g_ple, g_final, loss_target, m_g_mix, m_w_in, m_ln_v_g, m_ln_v_b, m_w_s, m_b_s, m_w_a_out, m_conv_b_w, m_conv_b_b, m_ln_b_g, m_ln_b_b, m_w_b_out, m_w_o, m_g_ffn, m_w_up, m_ffn_conv_w, m_ffn_conv_b, m_w_down, m_g_pg, m_w_pg, m_w_ple, m_g_ple, m_g_final, v_g_mix, v_w_in, v_ln_v_g, v_ln_v_b, v_w_s, v_b_s, v_w_a_out, v_conv_b_w, v_conv_b_b, v_ln_b_g, v_ln_b_b, v_w_b_out, v_w_o, v_g_ffn, v_w_up, v_ffn_conv_w, v_ffn_conv_b, v_w_down, v_g_pg, v_w_pg, v_w_ple, v_g_ple, v_g_final):
    local = dict(locals())
    wts = {k: local[k] for k in _WEIGHTS}
    mom = {k: local["m_" + k] for k in _WEIGHTS}
    var = {k: local["v_" + k] for k in _WEIGHTS}
    shapes = {k: wts[k].shape for k in _WEIGHTS}

    bsz, seq, d = x.shape
    t = bsz * seq
    x0 = x.reshape(t, d)
    p0 = p.reshape(t, p.shape[-1])
    target = loss_target.reshape(t, d)
    tm = min(TM_MM, t)
    tt = min(TT_MM, t)
    n_row = t // tm
    n_tok = t // tt

    def sq(a):
        return a.reshape(a.shape[1:])

    shard = {k: sq(wts[k]).astype(BF16) for k in _BIG}
    w_in3, small8 = _all_gather([shard["w_in"], _pack_sharded_small(sq(conv_b_w), sq(ffn_conv_w))])
    conv_b_w8, ffn_conv_w8 = _unpack_sharded_small(small8)
    conv_w_full = conv_b_w8.transpose(1, 0, 2).reshape(CONV_B, N_DEV * conv_b_w8.shape[-1])
    n_in = w_in3.shape[2]
    f_blk = shard["w_up"].shape[1]

    ws_m = jnp.where(jnp.tril(jnp.ones((CHUNK, CHUNK), bool))[None], sq(w_s), 0.0).astype(BF16)
    ws_mt = jnp.swapaxes(ws_m, 1, 2)
    bias_full = jnp.broadcast_to(sq(b_s).T[:, :, None], (CHUNK, GROUPS, CHUNK)).reshape(CHUNK, GROUPS * CHUNK)
    ffn_b8 = ffn_conv_b.reshape(N_DEV, 1, f_blk)

    h1 = _rms_fwd("rms_mix", x0, g_mix)
    z, (wa3, wb3, wo3) = _matmul(
        "mm_in", h1, w_in3, dims=_NN, grid=(N_DEV, n_row, 1),
        a_spec=pl.BlockSpec((tm, d), lambda j, i, k: (i, 0)),
        b_spec=pl.BlockSpec((None, d, n_in), lambda j, i, k: (j, 0, 0)),
        o_spec=pl.BlockSpec((tm, n_in), lambda j, i, k: (i, j)), acc_shape=(tm, n_in),
        out_shape=jax.ShapeDtypeStruct((t, N_DEV * n_in), BF16),
        exch=[("gather", shard[k]) for k in ("w_a_out", "w_b_out", "w_o")])
    w_a = wa3.reshape(-1, d)
    w_b = wb3.reshape(-1, d)
    w_om = wo3.reshape(-1, d)
    (pa, cs, c_saved), (w_up3, wd3, wpg3, wple3) = _branch_fwd(
        z, ln_v_g, ln_v_b, ws_m, bias_full, conv_w_full, conv_b_b, ln_b_g, ln_b_b, seq,
        exch=[("gather", shard[k]) for k in ("w_up", "w_down", "w_pg", "w_ple")])
    w_pgm = wpg3.reshape(-1, d)
    w_d4 = wd3.reshape(N_DEV // 2, f_blk, d)
    w_plem = wple3.transpose(1, 0, 2).reshape(wple3.shape[1], d)
    ya = _mm_rows("mm_a_out", pa, w_a, dims=_NN, out_dtype=BF16)
    yb = _mm_rows("mm_b_out", cs, w_b, dims=_NN, out_dtype=BF16)
    merged = _merge_fwd(z, ya, yb)
    x1 = _mm_rows("mm_o", merged, w_om, dims=_NN, out_dtype=F32, res=x0)
    h2 = _rms_fwd("rms_ffn", x1, g_ffn)
    up0 = _matmul("mm_up", h2, w_up3, dims=_NN, grid=(N_DEV, n_row, 1),
                  a_spec=pl.BlockSpec((tm, d), lambda j, i, k: (i, 0)),
                  b_spec=pl.BlockSpec((None, d, f_blk), lambda j, i, k: (j, 0, 0)),
                  o_spec=pl.BlockSpec((None, tm, f_blk), lambda j, i, k: (j, i, 0)), acc_shape=(tm, f_blk),
                  out_shape=jax.ShapeDtypeStruct((N_DEV, t, f_blk), BF16))
    act, upc = _ffn_mid_fwd(up0, ffn_conv_w8, ffn_b8, seq)
    row_spec = pl.BlockSpec((tm, d), lambda i, j, k: (i, 0))
    x2 = _matmul("mm_down", act, w_d4, dims=_NN, grid=(n_row, 1, N_DEV // 2),
                 a_spec=pl.BlockSpec((None, tm, f_blk), lambda i, j, k: (k, i, 0)),
                 b_spec=pl.BlockSpec((None, f_blk, d), lambda i, j, k: (k, 0, 0)),
                 o_spec=row_spec, acc_shape=(tm, d), out_shape=jax.ShapeDtypeStruct((t, d), F32),
                 res=x1, res_spec=row_spec)
    hq = _rms_fwd("rms_pg", x2, g_pg)
    q = _mm_rows("mm_pg", hq, w_pgm, dims=_NN, out_dtype=F32)
    r = _mm_rows("mm_ple", p0, w_plem, dims=_NN, out_dtype=F32)

    dx3, dq, dr, loss_v, dg_final, dg_ple = _head(x2, q, r, target, g_ple, g_final.reshape(1, d))

    recv = {}
    gw_pg = _mm_wgrad("wg_pg", hq, dq, out_dtype=BF16).reshape(wpg3.shape)
    dw_ple = _mm_wgrad("wg_ple", p0, dr, out_dtype=BF16)
    gw_ple = dw_ple.reshape(dw_ple.shape[0], N_DEV, -1).transpose(1, 0, 2)
    dhq = _mm_rows("mm_pg_t", dq, w_pgm, dims=_NT, out_dtype=F32)
    dx2, dx2b, dg_pg = _rms_bwd("rms_pg_bwd", x2, g_pg, dhq, dx3, want_bf16=True)

    dact, (recv["w_pg"], recv["w_ple"]) = _matmul(
        "mm_down_t", dx2b, w_d4, dims=_NT, grid=(N_DEV // 2, n_row, 1),
        a_spec=pl.BlockSpec((tm, d), lambda j, i, k: (i, 0)),
        b_spec=pl.BlockSpec((None, f_blk, d), lambda j, i, k: (j, 0, 0)),
        o_spec=pl.BlockSpec((None, tm, f_blk), lambda j, i, k: (j, i, 0)), acc_shape=(tm, f_blk),
        out_shape=jax.ShapeDtypeStruct((N_DEV // 2, t, f_blk), BF16),
        exch=[("scatter", gw_pg), ("scatter", gw_ple)])
    gw_down = _matmul("wg_down", act, dx2b, dims=_TN, grid=(N_DEV // 2, 1, n_tok),
                      a_spec=pl.BlockSpec((None, tt, f_blk), lambda j, i, k: (j, k, 0)),
                      b_spec=pl.BlockSpec((tt, d), lambda j, i, k: (k, 0)),
                      o_spec=pl.BlockSpec((None, f_blk, d), lambda j, i, k: (j, 0, 0)), acc_shape=(f_blk, d),
                      out_shape=jax.ShapeDtypeStruct((N_DEV // 2, f_blk, d), BF16)).reshape(wd3.shape)
    (d_up0, dffn_w8, dffn_b8), (recv["w_down"],) = _ffn_mid_bwd(up0, upc, dact, ffn_conv_w8, seq,
                                                                exch=[("scatter", gw_down)])
    gw_up = _matmul("wg_up", h2, d_up0, dims=_TN, grid=(N_DEV, 1, n_tok),
                    a_spec=pl.BlockSpec((tt, d), lambda j, i, k: (k, 0)),
                    b_spec=pl.BlockSpec((None, tt, f_blk), lambda j, i, k: (j, k, 0)),
                    o_spec=pl.BlockSpec((None, d, f_blk), lambda j, i, k: (j, 0, 0)), acc_shape=(d, f_blk),
                    out_shape=jax.ShapeDtypeStruct((N_DEV, d, f_blk), BF16))
    dh2, (recv["w_up"],) = _matmul(
        "mm_up_t", d_up0, w_up3, dims=_NT, grid=(n_row, 1, N_DEV),
        a_spec=pl.BlockSpec((None, tm, f_blk), lambda i, j, k: (k, i, 0)),
        b_spec=pl.BlockSpec((None, d, f_blk), lambda i, j, k: (k, 0, 0)),
        o_spec=row_spec, acc_shape=(tm, d), out_shape=jax.ShapeDtypeStruct((t, d), F32),
        exch=[("scatter", gw_up)])
    dx1, dx1b, dg_ffn = _rms_bwd("rms_ffn_bwd", x1, g_ffn, dh2, dx2, want_bf16=True)

    dmerged = _mm_rows("mm_o_t", dx1b, w_om, dims=_NT, out_dtype=BF16)
    gw_o = _mm_wgrad("wg_o", merged, dx1b, out_dtype=BF16).reshape(wo3.shape)
    dya, dyb, dgates = _merge_bwd(z, ya, yb, dmerged)
    dpa = _mm_rows("mm_a_out_t", dya, w_a, dims=_NT, out_dtype=BF16)
    dcs = _mm_rows("mm_b_out_t", dyb, w_b, dims=_NT, out_dtype=BF16)
    gw_a = _mm_wgrad("wg_a_out", pa, dya, out_dtype=BF16).reshape(wa3.shape)
    gw_b = _mm_wgrad("wg_b_out", cs, dyb, out_dtype=BF16).reshape(wb3.shape)
    (dz, dlvg, dlvb, dws, dbs_full, dconv_w, dconv_b, dlbg, dlbb), (recv["w_o"], recv["w_a_out"], recv["w_b_out"]) = _branch_bwd(
        z, c_saved, dpa, dcs, dgates, ln_v_g, ln_v_b, ws_m, ws_mt, bias_full, conv_w_full, ln_b_g, ln_b_b, seq,
        exch=[("scatter", gw_o), ("scatter", gw_a), ("scatter", gw_b)])
    gw_in = _matmul("wg_in", h1, dz, dims=_TN, grid=(N_DEV, 1, n_tok),
                    a_spec=pl.BlockSpec((tt, d), lambda j, i, k: (k, 0)),
                    b_spec=pl.BlockSpec((tt, n_in), lambda j, i, k: (k, j)),
                    o_spec=pl.BlockSpec((None, d, n_in), lambda j, i, k: (j, 0, 0)), acc_shape=(d, n_in),
                    out_shape=jax.ShapeDtypeStruct((N_DEV, d, n_in), BF16))
    db_s = dbs_full[:, :GROUPS].T
    rep_partial = _pack_replicated(
        dict(ln_v_g=dlvg, ln_v_b=dlvb, b_s=db_s, conv_b_b=dconv_b, ln_b_g=dlbg, ln_b_b=dlbb, g_ffn=dg_ffn,
             g_pg=dg_pg, g_ple=dg_ple, g_final=dg_final, ffn_conv_b=dffn_b8, w_s=dws), loss=loss_v[0, 0])
    dconv_w8 = dconv_w.reshape(CONV_B, N_DEV, -1).transpose(1, 0, 2)
    small_partial = _pack_sharded_small(dconv_w8, dffn_w8)
    dh1, (recv["w_in"], recv_small, recv_rep) = _matmul(
        "mm_in_t", dz, w_in3, dims=_NT, grid=(n_row, 1, N_DEV),
        a_spec=pl.BlockSpec((tm, n_in), lambda i, j, k: (i, k)),
        b_spec=pl.BlockSpec((None, d, n_in), lambda i, j, k: (k, 0, 0)),
        o_spec=row_spec, acc_shape=(tm, d), out_shape=jax.ShapeDtypeStruct((t, d), F32),
        exch=[("scatter", gw_in), ("scatter", small_partial), ("gather", rep_partial)])
    grad_x, dg_mix = _rms_bwd("rms_mix_bwd", x0, g_mix, dh1, dx1, want_bf16=False)
    (recv_g_mix,) = _exchange("exchange_g_mix", [("gather", _rows8(dg_mix))])

    grads, deltas, new_m, new_v = {}, {}, {}, {}
    by_kind = (grads, deltas, new_m, new_v)

    def two_d(a):
        a = sq(a)
        return a.reshape(-1, a.shape[-1])

    for k in _BIG:
        parts = recv[k].reshape(N_DEV, -1, recv[k].shape[-1])
        outs = _adamw("adamw_" + k, parts, two_d(wts[k]), two_d(mom[k]), two_d(var[k]), 128)
        for tgt, o in zip(by_kind, outs):
            tgt[k] = o.reshape(shapes[k])

    small = [_pack_sharded_small(sq(s["conv_b_w"]), sq(s["ffn_conv_w"])) for s in (wts, mom, var)]
    for tgt, o in zip(by_kind, _adamw("adamw_conv", recv_small, small[0], small[1], small[2], 56)):
        cw, fw = _unpack_sharded_small(o)
        tgt["conv_b_w"] = cw.reshape(shapes["conv_b_w"])
        tgt["ffn_conv_w"] = fw.reshape(shapes["ffn_conv_w"])

    rep = [_pack_replicated(s) for s in (wts, mom, var)]
    loss = None
    for tgt, o in zip(by_kind, _adamw("adamw_replicated", recv_rep, rep[0], rep[1], rep[2], 152)):
        un = _unpack_replicated(o, shapes)
        if tgt is grads:
            loss = un["loss"]
        for k in _VEC_NAMES + ("ffn_conv_b", "w_s"):
            tgt[k] = un[k]

    gm = [_rows8(s["g_mix"]) for s in (wts, mom, var)]
    for tgt, o in zip(by_kind, _adamw("adamw_g_mix", recv_g_mix, gm[0], gm[1], gm[2], 8)):
        tgt["g_mix"] = o[0, :].reshape(shapes["g_mix"])

    return (loss, grad_x.reshape(x.shape), *[grads[k] for k in _WEIGHTS], *[deltas[k] for k in _WEIGHTS],
            *[new_m[k] for k in _WEIGHTS], *[new_v[k] for k in _WEIGHTS])
```

```python
import math

import jax
import jax.numpy as jnp
from jax import lax
from jax.experimental import pallas as pl
from jax.experimental.pallas import tpu as pltpu

F32 = jnp.float32
BF16 = jnp.bfloat16

N_DEV = 8
EPS_RMS = 1e-6
EPS_LN = 1e-5
CHUNK = 128
GROUPS = 8
CONV_B = 31
CONV_F = 3
HALO_B = 32
HALO_F = 8
ROWS_F = 16
SUB = 8

ADAM_LR = 0.001
ADAM_B1 = 0.9
ADAM_B2 = 0.999
ADAM_EPS = 1e-08
ADAM_WD = 0.01
ADAM_STEP = 10

VMEM_LIMIT = 56 * 1024 * 1024
TM_MM = 1024
TM_WIDE = 2048
TT_MM = 2048
TM_EW = 256
EPI_STRIP = 256

_NN = (((1,), (0,)), ((), ()))
_NT = (((1,), (1,)), ((), ()))
_TN = (((0,), (0,)), ((), ()))
MESH = pl.DeviceIdType.MESH
HBM_SPEC = pl.BlockSpec(memory_space=pltpu.HBM)


def _params(n_axes):
    return pltpu.CompilerParams(dimension_semantics=("arbitrary",) * n_axes, vmem_limit_bytes=VMEM_LIMIT)


def _gelu(x):
    k = math.sqrt(2.0 / math.pi)
    return 0.5 * x * (1.0 + jnp.tanh(k * (x + 0.044715 * (x * x * x))))


def _gelu_and_grad(x):
    k = math.sqrt(2.0 / math.pi)
    x2 = x * x
    t = jnp.tanh(k * (x + 0.044715 * (x2 * x)))
    g = 0.5 * x * (1.0 + t)
    dg = 0.5 * (1.0 + t) + 0.5 * x * (1.0 - t * t) * (k * (1.0 + 3.0 * 0.044715 * x2))
    return g, dg


def _sigmoid(x):
    return 1.0 / (1.0 + jnp.exp(-x))


def _rowsum(x):
    return jnp.sum(x, axis=0, keepdims=True)


def _mean(x):
    return jnp.mean(x, axis=-1, keepdims=True)


def _exchange_io(exch):
    n = len(exch)
    out_shape = [jax.ShapeDtypeStruct(v.shape if kind == "scatter" else (N_DEV,) + v.shape, v.dtype) for kind, v in exch]
    scratch = [pltpu.SemaphoreType.DMA((7 * n,)), pltpu.SemaphoreType.DMA((7 * n,)), pltpu.SemaphoreType.DMA((n,))] if n else []
    return [HBM_SPEC] * n, [HBM_SPEC] * n, out_shape, scratch


def _exchange_step(kinds, x_refs, o_refs, send_sems, recv_sems, local_sems):
    n = len(kinds)
    x, y, c = lax.axis_index("x"), lax.axis_index("y"), lax.axis_index("c")
    me = 4 * x + 2 * y + c

    def src(a, to_slot):
        return x_refs[a].at[to_slot] if kinds[a] == "scatter" else x_refs[a]

    mine = [pltpu.make_async_copy(src(a, me), o_refs[a].at[me], local_sems.at[a]) for a in range(n)]
    sends, recvs = [], []
    for m in range(1, N_DEV):
        mx, my, mc = (m >> 2) & 1, (m >> 1) & 1, m & 1
        px, py, pc = (1 - x if mx else x), (1 - y if my else y), (1 - c if mc else c)
        peer = 4 * px + 2 * py + pc
        for a in range(n):
            k = a * 7 + m - 1
            sends.append(pltpu.make_async_remote_copy(
                src_ref=src(a, peer), dst_ref=o_refs[a].at[me], send_sem=send_sems.at[k], recv_sem=recv_sems.at[k],
                device_id=(px, py, pc), device_id_type=MESH))
            recvs.append(pltpu.make_async_remote_copy(
                src_ref=src(a, peer), dst_ref=o_refs[a].at[peer], send_sem=send_sems.at[k], recv_sem=recv_sems.at[k],
                device_id=(px, py, pc), device_id_type=MESH))

    def start():
        for cp in mine + sends:
            cp.start()

    def finish():
        for cp in recvs:
            cp.wait_recv()
        for cp in sends:
            cp.wait_send()
        for cp in mine:
            cp.wait()

    return start, finish


def _exchange(name, exch):
    n = len(exch)
    kinds = [k for k, _ in exch]
    in_specs, out_specs, out_shape, scratch = _exchange_io(exch)

    def body(*refs):
        start, finish = _exchange_step(kinds, refs[:n], refs[n:2 * n], *refs[2 * n:])
        start()
        finish()

    return pl.pallas_call(body, in_specs=in_specs, out_specs=out_specs, out_shape=out_shape, scratch_shapes=scratch,
                          name=name)(*[v for _, v in exch])


class _Epilogue:
    def __init__(self, fn, ins=(), in_specs=(), out_specs=(), out_shape=(), strip=None):
        self.fn, self.ins, self.in_specs = fn, list(ins), list(in_specs)
        self.out_specs, self.out_shape = list(out_specs), list(out_shape)
        self.strip = strip


def _matmul(name, a, b, *, dims, grid, a_spec, b_spec, acc_shape, o_spec=None, out_shape=None, epi=None, exch=()):
    nk = grid[2]
    plain = epi is None
    if plain:
        def store(acc, ins, outs, i):
            outs[0][...] = acc.astype(outs[0].dtype)
        epi = _Epilogue(store, out_specs=[o_spec], out_shape=[out_shape])
    n_in = 2 + len(epi.ins)
    n_out = len(epi.out_specs)
    n_ex = len(exch)
    kinds = [k for k, _ in exch]
    ex_in, ex_out, ex_shape, ex_scratch = _exchange_io(exch)

    def body(*refs):
        a_ref, b_ref = refs[:2]
        step0 = pl.program_id(0)
        epi_ins, rest = refs[2:n_in], refs[n_in:]
        x_refs, rest = rest[:n_ex], rest[n_ex:]
        outs, rest = rest[:n_out], rest[n_out:]
        o_refs, scr = rest[:n_ex], rest[n_ex:]
        if n_ex:
            pid = [pl.program_id(ax) for ax in range(3)]
            ex_start, ex_finish = _exchange_step(kinds, x_refs, o_refs, *scr[len(scr) - 3:])
            pl.when((pid[0] == 0) & (pid[1] == 0) & (pid[2] == 0))(ex_start)
        part = lax.dot_general(a_ref[...].astype(BF16), b_ref[...].astype(BF16), dims, preferred_element_type=F32)

        def run_epilogue(rows_of_acc):
            rows = acc_shape[0]
            strip = rows if epi.strip is None else min(epi.strip, rows)
            for s in range(0, rows, strip):
                def view(ref):
                    return ref.at[pl.ds(s, strip)] if ref.shape[0] == rows else ref
                first = (step0 == 0) if s == 0 else False
                epi.fn(rows_of_acc(s, strip), [view(r) for r in epi_ins], [view(r) for r in outs], first)

        if nk == 1:
            run_epilogue(lambda s, n: part[s:s + n])
        else:
            acc_ref = scr[0]
            k = pl.program_id(2)

            @pl.when(k == 0)
            def _():
                acc_ref[...] = part

            @pl.when(k > 0)
            def _():
                acc_ref[...] += part

            @pl.when(k == nk - 1)
            def _():
                run_epilogue(lambda s, n: acc_ref[pl.ds(s, n), :])
        if n_ex:
            pl.when((pid[0] == grid[0] - 1) & (pid[1] == grid[1] - 1) & (pid[2] == grid[2] - 1))(ex_finish)

    scratch = ([pltpu.VMEM(acc_shape, F32)] if nk > 1 else []) + ex_scratch
    res = pl.pallas_call(body, grid=grid, in_specs=[a_spec, b_spec] + epi.in_specs + ex_in,
                         out_specs=epi.out_specs + ex_out, out_shape=epi.out_shape + ex_shape, scratch_shapes=scratch,
                         name=name, compiler_params=_params(3))(a, b, *epi.ins, *[v for _, v in exch])
    main = res[0] if plain else res[:n_out]
    return (main, res[n_out:]) if n_ex else main


def _mm_rows(name, a, w, *, dims, tm, out_dtype=None, epi=None):
    t, k = a.shape
    n = w.shape[1] if dims == _NN else w.shape[0]
    tm = min(tm, t)
    return _matmul(name, a, w, dims=dims, grid=(t // tm, 1, 1),
                   a_spec=pl.BlockSpec((tm, k), lambda i, j, kk: (i, 0)),
                   b_spec=pl.BlockSpec(w.shape, lambda i, j, kk: (0, 0)),
                   o_spec=pl.BlockSpec((tm, n), lambda i, j, kk: (i, 0)), acc_shape=(tm, n),
                   out_shape=jax.ShapeDtypeStruct((t, n), out_dtype) if epi is None else None, epi=epi)


def _mm_wgrad(name, a, b, *, out_dtype, tt):
    t, m = a.shape
    n = b.shape[1]
    tt = min(tt, t)
    return _matmul(name, a, b, dims=_TN, grid=(1, 1, t // tt),
                   a_spec=pl.BlockSpec((tt, m), lambda i, j, kk: (kk, 0)),
                   b_spec=pl.BlockSpec((tt, n), lambda i, j, kk: (kk, 0)),
                   o_spec=pl.BlockSpec((m, n), lambda i, j, kk: (0, 0)),
                   acc_shape=(m, n), out_shape=jax.ShapeDtypeStruct((m, n), out_dtype))


def _row3(tm, d, col=0):
    return pl.BlockSpec((tm, d), lambda i, j, k: (i, col))


def _vec3(d):
    return pl.BlockSpec((1, d), lambda i, j, k: (0, 0))


def _accumulate_over_rows(ref, part, first):
    if first is False:
        ref[...] += part
        return

    @pl.when(first)
    def _():
        ref[...] = part + jnp.zeros_like(ref)

    @pl.when(jnp.logical_not(first))
    def _():
        ref[...] += part


def _epi_residual_rms(res, g, tm):
    t, d = res.shape

    def fn(acc, ins, outs, i):
        res_ref, g_ref = ins
        xv = acc + res_ref[...]
        outs[0][...] = xv
        rstd = lax.rsqrt(_mean(xv * xv) + EPS_RMS)
        outs[1][...] = ((xv * rstd) * g_ref[...]).astype(BF16)

    return _Epilogue(fn, strip=EPI_STRIP, ins=[res, g], in_specs=[_row3(tm, d), _vec3(d)], out_specs=[_row3(tm, d), _row3(tm, d)],
                     out_shape=[jax.ShapeDtypeStruct((t, d), F32), jax.ShapeDtypeStruct((t, d), BF16)])


def _epi_rms_bwd(x, g, dres, tm, *, want_bf16):
    t, d = x.shape

    def fn(acc, ins, outs, i):
        x_ref, g_ref, dres_ref = ins
        xv = x_ref[...]
        rstd = lax.rsqrt(_mean(xv * xv) + EPS_RMS)
        nrm = xv * rstd
        dn = acc * g_ref[...]
        dx = dres_ref[...] + rstd * (dn - nrm * _mean(dn * nrm))
        outs[0][...] = dx
        if want_bf16:
            outs[1][...] = dx.astype(BF16)
        _accumulate_over_rows(outs[-1], _rowsum(acc * nrm), i)

    row = _row3(tm, d)
    n_dx = 2 if want_bf16 else 1
    return _Epilogue(fn, strip=EPI_STRIP, ins=[x, g, dres], in_specs=[row, _vec3(d), row], out_specs=[row] * n_dx + [_vec3(d)],
                     out_shape=[jax.ShapeDtypeStruct((t, d), F32)] + [jax.ShapeDtypeStruct((t, d), BF16)] * (n_dx - 1)
                     + [jax.ShapeDtypeStruct((1, d), F32)])


def _epi_merge_fwd(z, ya, tm):
    t, w = ya.shape

    def fn(acc, ins, outs, i):
        ga_ref, gb_ref, ya_ref = ins
        outs[0][...] = acc.astype(BF16)
        sa = _sigmoid(ga_ref[...].astype(F32))
        sb = _sigmoid(gb_ref[...].astype(F32))
        outs[1][...] = (sa * ya_ref[...].astype(F32) + sb * acc).astype(BF16)

    row = _row3(tm, w)
    return _Epilogue(fn, strip=EPI_STRIP, ins=[z, z, ya], in_specs=[_row3(tm, w, 4), _row3(tm, w, 5), row], out_specs=[row, row],
                     out_shape=[jax.ShapeDtypeStruct((t, w), BF16)] * 2)


def _epi_merge_bwd(z, ya, yb, tm):
    t, w = ya.shape

    def fn(acc, ins, outs, i):
        ga_ref, gb_ref, ya_ref, yb_ref = ins
        sa = _sigmoid(ga_ref[...].astype(F32))
        sb = _sigmoid(gb_ref[...].astype(F32))
        outs[0][...] = (acc * sa).astype(BF16)
        outs[1][...] = (acc * sb).astype(BF16)
        outs[2][:, 0:w] = (acc * ya_ref[...].astype(F32) * sa * (1.0 - sa)).astype(BF16)
        outs[2][:, w:2 * w] = (acc * yb_ref[...].astype(F32) * sb * (1.0 - sb)).astype(BF16)

    row = _row3(tm, w)
    return _Epilogue(fn, strip=EPI_STRIP, ins=[z, z, ya, yb], in_specs=[_row3(tm, w, 4), _row3(tm, w, 5), row, row],
                     out_specs=[row, row, _row3(tm, 2 * w)],
                     out_shape=[jax.ShapeDtypeStruct((t, w), BF16)] * 2 + [jax.ShapeDtypeStruct((t, 2 * w), BF16)])


def _epi_head(x2, r, target, g_ple, g_final, tm):
    t, d = x2.shape

    def fn(acc, ins, outs, i):
        x2_ref, r_ref, tg_ref, gple_ref, gfin_ref = ins
        dx3_ref, dq_ref, dr_ref, loss_ref, dgfin_ref, dgple_ref = outs
        pg = _sigmoid(acc)
        rv = r_ref[...]
        rstd_r = lax.rsqrt(_mean(rv * rv) + EPS_RMS)
        nr = rv * rstd_r
        pe = nr * gple_ref[...]
        x3 = x2_ref[...] + pe * pg
        rstd3 = lax.rsqrt(_mean(x3 * x3) + EPS_RMS)
        n3 = x3 * rstd3
        err = n3 * gfin_ref[...] - tg_ref[...]
        loss_part = jnp.sum(_rowsum(err * err), axis=1, keepdims=True) * (0.5 / d)
        dy = err * (1.0 / d)
        dn3 = dy * gfin_ref[...]
        dx3 = rstd3 * (dn3 - n3 * _mean(dn3 * n3))
        dx3_ref[...] = dx3
        dq_ref[...] = (dx3 * pe * pg * (1.0 - pg)).astype(BF16)
        dpe = dx3 * pg
        dnr = dpe * gple_ref[...]
        dr_ref[...] = (rstd_r * (dnr - nr * _mean(dnr * nr))).astype(BF16)
        _accumulate_over_rows(loss_ref, loss_part, i)
        _accumulate_over_rows(dgfin_ref, _rowsum(dy * n3), i)
        _accumulate_over_rows(dgple_ref, _rowsum(dpe * nr), i)

    row = _row3(tm, d)
    vec = jax.ShapeDtypeStruct((1, d), F32)
    return _Epilogue(fn, strip=EPI_STRIP, ins=[x2, r, target, g_ple, g_final], in_specs=[row, row, row, _vec3(d), _vec3(d)],
                     out_specs=[row, row, row, _vec3(d), _vec3(d), _vec3(d)],
                     out_shape=[jax.ShapeDtypeStruct((t, d), F32), jax.ShapeDtypeStruct((t, d), BF16),
                                jax.ShapeDtypeStruct((t, d), BF16), vec, vec, vec])


def _vec_spec(d):
    return pl.BlockSpec((1, d), lambda i: (0, 0))


def _row_call(name, body, *, n_steps, in_specs, out_specs, out_shape, scratch, args, exch=()):
    n_in, n_out, n_scr, n_ex = len(in_specs), len(out_specs), len(scratch), len(exch)
    kinds = [k for k, _ in exch]
    ex_in, ex_out, ex_shape, ex_scratch = _exchange_io(exch)

    def wrapped(*refs):
        ins, rest = refs[:n_in], refs[n_in:]
        x_refs, rest = rest[:n_ex], rest[n_ex:]
        outs, rest = rest[:n_out], rest[n_out:]
        o_refs, rest = rest[:n_ex], rest[n_ex:]
        scr, sems = rest[:n_scr], rest[n_scr:]
        if n_ex:
            ex_start, ex_finish = _exchange_step(kinds, x_refs, o_refs, *sems)
            pl.when(pl.program_id(0) == 0)(ex_start)
        body(*ins, *outs, *scr)
        if n_ex:
            pl.when(pl.program_id(0) == n_steps - 1)(ex_finish)

    res = pl.pallas_call(wrapped, grid=(n_steps,), in_specs=list(in_specs) + ex_in, out_specs=list(out_specs) + ex_out,
                         out_shape=list(out_shape) + ex_shape, scratch_shapes=list(scratch) + ex_scratch, name=name,
                         compiler_params=_params(1))(*args, *[v for _, v in exch])
    return res[:n_out], res[n_out:]


def _rms_fwd(name, x, g):
    t, d = x.shape
    tm = min(TM_EW * 2, t)

    def body(x_ref, g_ref, h_ref):
        xv = x_ref[...]
        rstd = lax.rsqrt(_mean(xv * xv) + EPS_RMS)
        h_ref[...] = ((xv * rstd) * g_ref[...]).astype(BF16)

    row = pl.BlockSpec((tm, d), lambda i: (i, 0))
    return pl.pallas_call(body, grid=(t // tm,), in_specs=[row, _vec_spec(d)], out_specs=row,
                          out_shape=jax.ShapeDtypeStruct((t, d), BF16), name=name, compiler_params=_params(1))(x, g)


def _fill_shifted(buf_ref, sh_ref):
    n = sh_ref.shape[1]
    for p in range(1, SUB):
        sh_ref[p - 1] = buf_ref[p:p + n, :]


def _branch_fwd(z, ln_v_g, ln_v_b, ws_m, bias_full, conv_w, conv_b, ln_b_g, ln_b_b, seq, exch=()):
    t = z.shape[0]
    w = 1024
    tm = min(TM_EW, seq)
    tiles_per_seq = seq // tm
    n_chunks = tm // CHUNK

    def body(z_ref, lvg_ref, lvb_ref, ws_ref, bias_ref, cw_ref, cb_ref, lbg_ref, lbb_ref,
             pa_ref, cs_ref, c_ref, hist_ref, buf_ref, mix_ref, sh_ref, wb_ref):
        i = pl.program_id(0)
        u = z_ref[:, 0:w].astype(F32)
        v = z_ref[:, w:2 * w].astype(F32)
        ug = _gelu(u)
        vg = _gelu(v)
        dv = vg - _mean(vg)
        vhat = dv * lax.rsqrt(_mean(dv * dv) + EPS_LN)
        vn = (vhat * lvg_ref[...] + lvb_ref[...]).astype(BF16)
        for ci in range(n_chunks):
            rows = slice(ci * CHUNK, (ci + 1) * CHUNK)
            for g in range(GROUPS):
                cols = slice(g * CHUNK, (g + 1) * CHUNK)
                mix_ref[rows, cols] = lax.dot_general(ws_ref[g], vn[rows, cols], _NN, preferred_element_type=F32)
            mix_ref[rows, :] += bias_ref[...]
        pa_ref[...] = (ug * mix_ref[...]).astype(BF16)

        a = z_ref[:, 2 * w:3 * w].astype(F32)
        gl = z_ref[:, 3 * w:4 * w].astype(F32)
        glu = a * _sigmoid(gl)

        @pl.when(i % tiles_per_seq == 0)
        def _():
            hist_ref[...] = jnp.zeros_like(hist_ref)

        buf_ref[0:HALO_B, :] = hist_ref[...]
        buf_ref[HALO_B:, :] = glu
        hist_ref[...] = glu[tm - HALO_B:, :]
        _fill_shifted(buf_ref, sh_ref)

        @pl.when(i == 0)
        def _():
            for k in range(CONV_B):
                wb_ref[k] = jnp.broadcast_to(cw_ref[k:k + 1, :], (SUB, w))

        groups = 4

        def strip(si, _):
            s = pl.multiple_of(si * (groups * SUB), groups * SUB)
            acc = [jnp.zeros((SUB, w), F32) + cb_ref[...] for _ in range(groups)]
            for k in range(CONV_B):
                whole, part = divmod(HALO_B - (CONV_B - 1) + k, SUB)
                wk = wb_ref[k]
                for g in range(groups):
                    at = pl.ds(s + SUB * (whole + g), SUB)
                    acc[g] = acc[g] + wk * (buf_ref[at, :] if part == 0 else sh_ref[part - 1, at, :])
            for g in range(0, groups, 2):
                at = pl.ds(s + SUB * g, 2 * SUB)
                c = jnp.concatenate(acc[g:g + 2], axis=0)
                c_ref[at, :] = c
                dc = c - _mean(c)
                chat = dc * lax.rsqrt(_mean(dc * dc) + EPS_LN)
                cn = chat * lbg_ref[...] + lbb_ref[...]
                cs_ref[at, :] = (cn * _sigmoid(cn)).astype(BF16)
            return 0

        lax.fori_loop(0, tm // (groups * SUB), strip, 0)

    row = pl.BlockSpec((tm, w), lambda i: (i, 0))
    in_specs = [pl.BlockSpec((tm, 4 * w), lambda i: (i, 0)), _vec_spec(w), _vec_spec(w),
                pl.BlockSpec((GROUPS, CHUNK, CHUNK), lambda i: (0, 0, 0)), pl.BlockSpec((CHUNK, w), lambda i: (0, 0)),
                pl.BlockSpec((CONV_B, w), lambda i: (0, 0)), _vec_spec(w), _vec_spec(w), _vec_spec(w)]
    return _row_call(
        "branch_fwd", body, n_steps=t // tm, in_specs=in_specs, out_specs=[row, row, row],
        out_shape=[jax.ShapeDtypeStruct((t, w), BF16), jax.ShapeDtypeStruct((t, w), BF16), jax.ShapeDtypeStruct((t, w), F32)],
        scratch=[pltpu.VMEM((HALO_B, w), F32), pltpu.VMEM((HALO_B + tm, w), F32), pltpu.VMEM((tm, w), F32),
                 pltpu.VMEM((SUB - 1, HALO_B + tm - SUB, w), F32), pltpu.VMEM((CONV_B, SUB, w), F32)],
        args=(z, ln_v_g, ln_v_b, ws_m, bias_full, conv_w, conv_b, ln_b_g, ln_b_b), exch=exch)


def _branch_bwd(z, c_saved, dpa, dcs, dgates, ln_v_g, ln_v_b, ws_m, ws_mt, bias_full, conv_w, ln_b_g, ln_b_b, seq, exch=()):
    t = z.shape[0]
    w = 1024
    tm = min(TM_EW, seq)
    tiles_per_seq = seq // tm
    n_tiles = t // tm
    n_chunks = tm // CHUNK

    def body(z_ref, c_ref, dpa_ref, dcs_ref, dgt_ref, lvg_ref, lvb_ref, ws_ref, wst_ref, bias_ref, cw_ref,
             lbg_ref, lbb_ref,
             dz_ref, dlvg_ref, dlvb_ref, dws_ref, dbs_ref, dcw_ref, dcb_ref, dlbg_ref, dlbb_ref,
             carry_ref, glu_ref, dbuf_ref, mix_ref, dvn_ref, dbs_acc_ref, dglu_ref, sh_ref, wb_ref, dwacc_ref):
        i = pl.program_id(0)
        r = n_tiles - 1 - i

        @pl.when(i == 0)
        def _():
            for ref in (dlvg_ref, dlvb_ref, dws_ref, dbs_acc_ref, dwacc_ref, dcb_ref, dlbg_ref, dlbb_ref):
                ref[...] = jnp.zeros_like(ref)

        u = z_ref[:, 0:w].astype(F32)
        v = z_ref[:, w:2 * w].astype(F32)
        ug, dug = _gelu_and_grad(u)
        vg, dvg = _gelu_and_grad(v)
        dv0 = vg - _mean(vg)
        rstd_v = lax.rsqrt(_mean(dv0 * dv0) + EPS_LN)
        vhat = dv0 * rstd_v
        vn = (vhat * lvg_ref[...] + lvb_ref[...]).astype(BF16)
        dpa = dpa_ref[...].astype(F32)
        dmix = dpa * ug
        dmix_b = dmix.astype(BF16)
        for ci in range(n_chunks):
            rows = slice(ci * CHUNK, (ci + 1) * CHUNK)
            for g in range(GROUPS):
                cols = slice(g * CHUNK, (g + 1) * CHUNK)
                mix_ref[rows, cols] = lax.dot_general(ws_ref[g], vn[rows, cols], _NN, preferred_element_type=F32)
                dvn_ref[rows, cols] = lax.dot_general(wst_ref[g], dmix_b[rows, cols], _NN, preferred_element_type=F32)
                dws_ref[g] += lax.dot_general(dmix_b[rows, cols], vn[rows, cols], _NT, preferred_element_type=F32)
            mix_ref[rows, :] += bias_ref[...]
            dbs_acc_ref[...] += dmix[rows, :]
        dz_ref[:, 0:w] = (dpa * mix_ref[...] * dug).astype(BF16)
        dvn = dvn_ref[...]
        dlvg_ref[...] += _rowsum(dvn * vhat)
        dlvb_ref[...] += _rowsum(dvn)
        dvh = dvn * lvg_ref[...]
        dvg_in = rstd_v * (dvh - _mean(dvh) - vhat * _mean(dvh * vhat))
        dz_ref[:, w:2 * w] = (dvg_in * dvg).astype(BF16)

        c = c_ref[...]
        dc0 = c - _mean(c)
        rstd_c = lax.rsqrt(_mean(dc0 * dc0) + EPS_LN)
        chat = dc0 * rstd_c
        cn = chat * lbg_ref[...] + lbb_ref[...]
        sg = _sigmoid(cn)
        dcn = dcs_ref[...].astype(F32) * (sg * (1.0 + cn * (1.0 - sg)))
        dlbg_ref[...] += _rowsum(dcn * chat)
        dlbb_ref[...] += _rowsum(dcn)
        dch = dcn * lbg_ref[...]
        dc = rstd_c * (dch - _mean(dch) - chat * _mean(dch * chat))
        dcb_ref[...] += _rowsum(dc)

        a = z_ref[:, 2 * w:3 * w].astype(F32)
        gl = z_ref[:, 3 * w:4 * w].astype(F32)
        sgl = _sigmoid(gl)
        glu_ref[...] = a * sgl

        @pl.when(r % tiles_per_seq == tiles_per_seq - 1)
        def _():
            carry_ref[...] = jnp.zeros_like(carry_ref)

        dbuf_ref[0:tm, :] = dc
        dbuf_ref[tm:, :] = carry_ref[...]
        carry_ref[...] = dc[0:HALO_B, :]
        _fill_shifted(dbuf_ref, sh_ref)

        @pl.when(i == 0)
        def _():
            for k in range(CONV_B):
                wb_ref[k] = jnp.broadcast_to(cw_ref[k:k + 1, :], (SUB, w))

        groups = 2

        def strip(si, _):
            s = pl.multiple_of(si * (groups * SUB), groups * SUB)
            glu_rows = [glu_ref[pl.ds(s + SUB * g, SUB), :] for g in range(groups)]
            acc = [jnp.zeros((SUB, w), F32) for _ in range(groups)]
            for k in range(CONV_B):
                whole, part = divmod(CONV_B - 1 - k, SUB)
                wk = wb_ref[k]
                dw_part = jnp.zeros((SUB, w), F32)
                for g in range(groups):
                    at = pl.ds(s + SUB * (whole + g), SUB)
                    d_rows = dbuf_ref[at, :] if part == 0 else sh_ref[part - 1, at, :]
                    acc[g] = acc[g] + wk * d_rows
                    dw_part = dw_part + d_rows * glu_rows[g]
                dwacc_ref[k] += dw_part
            dglu_ref[pl.ds(s, groups * SUB), :] = jnp.concatenate(acc, axis=0)
            return 0

        lax.fori_loop(0, tm // (groups * SUB), strip, 0)
        dglu = dglu_ref[...]
        dz_ref[:, 2 * w:3 * w] = (dglu * sgl).astype(BF16)
        dz_ref[:, 3 * w:4 * w] = (dglu * a * sgl * (1.0 - sgl)).astype(BF16)
        dz_ref[:, 4 * w:6 * w] = dgt_ref[...]

        @pl.when(i == n_tiles - 1)
        def _():
            tri = lax.broadcasted_iota(jnp.int32, (CHUNK, CHUNK), 0) >= lax.broadcasted_iota(jnp.int32, (CHUNK, CHUNK), 1)
            lane = lax.broadcasted_iota(jnp.int32, (CHUNK, CHUNK), 1)
            dbs = jnp.zeros((CHUNK, CHUNK), F32)
            for g in range(GROUPS):
                dws_ref[g] = jnp.where(tri, dws_ref[g], 0.0)
                group_sum = jnp.sum(dbs_acc_ref[:, g * CHUNK:(g + 1) * CHUNK], axis=1, keepdims=True)
                dbs = jnp.where(lane == g, group_sum, dbs)
            dbs_ref[...] = dbs
            for k in range(CONV_B):
                dcw_ref[k:k + 1, :] = _rowsum(dwacc_ref[k])

    def rev(i):
        return n_tiles - 1 - i

    row = pl.BlockSpec((tm, w), lambda i: (rev(i), 0))
    full = lambda shape: pl.BlockSpec(shape, lambda i: (0,) * len(shape))
    in_specs = [pl.BlockSpec((tm, 4 * w), lambda i: (rev(i), 0)),
                row, row, row, pl.BlockSpec((tm, 2 * w), lambda i: (rev(i), 0)),
                _vec_spec(w), _vec_spec(w), full((GROUPS, CHUNK, CHUNK)), full((GROUPS, CHUNK, CHUNK)), full((CHUNK, w)),
                full((CONV_B, w)), _vec_spec(w), _vec_spec(w)]
    out_specs = [pl.BlockSpec((tm, 6 * w), lambda i: (rev(i), 0)), _vec_spec(w), _vec_spec(w), full((GROUPS, CHUNK, CHUNK)),
                 full((CHUNK, CHUNK)), full((CONV_B, w)), _vec_spec(w), _vec_spec(w), _vec_spec(w)]
    vec = jax.ShapeDtypeStruct((1, w), F32)
    out_shape = [jax.ShapeDtypeStruct((t, 6 * w), BF16), vec, vec, jax.ShapeDtypeStruct((GROUPS, CHUNK, CHUNK), F32),
                 jax.ShapeDtypeStruct((CHUNK, CHUNK), F32), jax.ShapeDtypeStruct((CONV_B, w), F32), vec, vec, vec]
    scratch = [pltpu.VMEM((HALO_B, w), F32), pltpu.VMEM((tm, w), F32), pltpu.VMEM((tm + HALO_B, w), F32),
               pltpu.VMEM((tm, w), F32), pltpu.VMEM((tm, w), F32), pltpu.VMEM((CHUNK, w), F32), pltpu.VMEM((tm, w), F32),
               pltpu.VMEM((SUB - 1, HALO_B + tm - SUB, w), F32), pltpu.VMEM((CONV_B, SUB, w), F32),
               pltpu.VMEM((CONV_B, SUB, w), F32)]
    return _row_call("branch_bwd", body, n_steps=n_tiles, in_specs=in_specs, out_specs=out_specs, out_shape=out_shape,
                     scratch=scratch, exch=exch,
                     args=(z, c_saved, dpa, dcs, dgates, ln_v_g, ln_v_b, ws_m, ws_mt, bias_full, conv_w, ln_b_g, ln_b_b))


def _conv3_window(prev8, x):
    win = jnp.concatenate([prev8, x], axis=0)
    n = x.shape[0]
    return [win[HALO_F - 2:HALO_F - 2 + n], win[HALO_F - 1:HALO_F - 1 + n], x]


def _ffn_mid_fwd(up0, conv_w, conv_b, seq):
    nb, t, f = up0.shape
    half = nb // 2
    tm = min(TM_EW, seq)
    tiles_per_seq = seq // tm
    n_strips = tm // ROWS_F

    def body(up_ref, w_ref, b_ref, act_ref, upc_ref, hist_ref):
        i = pl.program_id(0)

        @pl.when(i % tiles_per_seq == 0)
        def _():
            hist_ref[...] = jnp.zeros_like(hist_ref)

        for j in range(half):
            jv = j + half
            wg = [w_ref[j, k:k + 1, :] for k in range(CONV_F)]
            wv = [w_ref[jv, k:k + 1, :] for k in range(CONV_F)]
            bg, bv = b_ref[j], b_ref[jv]

            def strip(c, carry):
                rows = pl.ds(pl.multiple_of(c * ROWS_F, ROWS_F), ROWS_F)
                xg = up_ref[j, rows, :].astype(F32)
                xv = up_ref[jv, rows, :].astype(F32)
                sg = _conv3_window(carry[0], xg)
                sv = _conv3_window(carry[1], xv)
                gate = bg + wg[0] * sg[0] + wg[1] * sg[1] + wg[2] * sg[2]
                val = bv + wv[0] * sv[0] + wv[1] * sv[1] + wv[2] * sv[2]
                act_ref[j, rows, :] = (_gelu(gate) * val).astype(BF16)
                upc_ref[j, rows, :] = gate.astype(BF16)
                upc_ref[jv, rows, :] = val.astype(BF16)
                return xg[ROWS_F - HALO_F:], xv[ROWS_F - HALO_F:]

            last = lax.fori_loop(0, n_strips, strip, (hist_ref[j], hist_ref[jv]))
            hist_ref[j] = last[0]
            hist_ref[jv] = last[1]

    return pl.pallas_call(
        body, grid=(t // tm,),
        in_specs=[pl.BlockSpec((nb, tm, f), lambda i: (0, i, 0)), pl.BlockSpec((nb, CONV_F, f), lambda i: (0, 0, 0)),
                  pl.BlockSpec((nb, 1, f), lambda i: (0, 0, 0))],
        out_specs=[pl.BlockSpec((half, tm, f), lambda i: (0, i, 0)), pl.BlockSpec((nb, tm, f), lambda i: (0, i, 0))],
        out_shape=[jax.ShapeDtypeStruct((half, t, f), BF16), jax.ShapeDtypeStruct((nb, t, f), BF16)],
        scratch_shapes=[pltpu.VMEM((nb, HALO_F, f), F32)],
        name="ffn_mid_fwd", compiler_params=_params(1))(up0, conv_w, conv_b)


def _ffn_mid_bwd(up0, upc, dact, conv_w, seq, exch=()):
    nb, t, f = up0.shape
    half = nb // 2
    tm = min(TM_EW, seq)
    tiles_per_seq = seq // tm
    n_tiles = t // tm
    n_strips = tm // ROWS_F

    def body(up_ref, upc_ref, da_ref, w_ref, dup_ref, dw_ref, db_ref, carry_ref, dwacc_ref, dbacc_ref):
        i = pl.program_id(0)
        r = n_tiles - 1 - i

        @pl.when(i == 0)
        def _():
            dwacc_ref[...] = jnp.zeros_like(dwacc_ref)
            dbacc_ref[...] = jnp.zeros_like(dbacc_ref)

        @pl.when(r % tiles_per_seq == tiles_per_seq - 1)
        def _():
            carry_ref[...] = jnp.zeros_like(carry_ref)

        for j in range(half):
            jv = j + half
            wg = [w_ref[j, k:k + 1, :] for k in range(CONV_F)]
            wv = [w_ref[jv, k:k + 1, :] for k in range(CONV_F)]

            def strip(ci, carry):
                rows = pl.ds(pl.multiple_of((n_strips - 1 - ci) * ROWS_F, ROWS_F), ROWS_F)
                val = upc_ref[jv, rows, :].astype(F32)
                gg, dgg = _gelu_and_grad(upc_ref[j, rows, :].astype(F32))
                da = da_ref[j, rows, :].astype(F32)
                d_gate = da * val * dgg
                d_val = da * gg
                for blk, d, nxt, wk in ((j, d_gate, carry[0], wg), (jv, d_val, carry[1], wv)):
                    dbacc_ref[blk] += d
                    dwin = jnp.concatenate([d, nxt], axis=0)
                    shifted = [dwin[2:2 + ROWS_F], dwin[1:1 + ROWS_F], d]
                    x = up_ref[blk, rows, :].astype(F32)
                    for k in range(CONV_F):
                        dwacc_ref[blk, k] += shifted[k] * x
                    dx = wk[0] * shifted[0] + wk[1] * shifted[1] + wk[2] * shifted[2]
                    dup_ref[blk, rows, :] = dx.astype(BF16)
                return d_gate[0:HALO_F], d_val[0:HALO_F]

            carry = lax.fori_loop(0, n_strips, strip, (carry_ref[j], carry_ref[jv]))
            carry_ref[j] = carry[0]
            carry_ref[jv] = carry[1]

        @pl.when(i == n_tiles - 1)
        def _():
            for blk in range(nb):
                db_ref[blk] = _rowsum(dbacc_ref[blk])
                for k in range(CONV_F):
                    dw_ref[blk, k:k + 1, :] = _rowsum(dwacc_ref[blk, k])

    def rev(i):
        return n_tiles - 1 - i

    return _row_call(
        "ffn_mid_bwd", body, n_steps=n_tiles,
        in_specs=[pl.BlockSpec((nb, tm, f), lambda i: (0, rev(i), 0)), pl.BlockSpec((nb, tm, f), lambda i: (0, rev(i), 0)),
                  pl.BlockSpec((half, tm, f), lambda i: (0, rev(i), 0)),
                  pl.BlockSpec((nb, CONV_F, f), lambda i: (0, 0, 0))],
        out_specs=[pl.BlockSpec((nb, tm, f), lambda i: (0, rev(i), 0)), pl.BlockSpec((nb, CONV_F, f), lambda i: (0, 0, 0)),
                   pl.BlockSpec((nb, 1, f), lambda i: (0, 0, 0))],
        out_shape=[jax.ShapeDtypeStruct((nb, t, f), BF16), jax.ShapeDtypeStruct((nb, CONV_F, f), F32),
                   jax.ShapeDtypeStruct((nb, 1, f), F32)],
        scratch=[pltpu.VMEM((nb, HALO_F, f), F32), pltpu.VMEM((nb, CONV_F, ROWS_F, f), F32), pltpu.VMEM((nb, ROWS_F, f), F32)],
        args=(up0, upc, dact, conv_w), exch=exch)


def _all_gather(xs):
    n = len(xs)

    def body(*refs):
        x_refs, o_refs = refs[:n], refs[n:2 * n]
        send_sems, recv_sems, local_sems = refs[2 * n:]
        x, y, c = lax.axis_index("x"), lax.axis_index("y"), lax.axis_index("c")
        me, sibling = (x, y, c), (x, y, 1 - c)
        chips = [(1 - x, y), (x, 1 - y), (1 - x, 1 - y)]

        def slot(pos):
            return 4 * pos[0] + 2 * pos[1] + pos[2]

        def copy(a, k, block, to, src=None):
            dst = o_refs[a].at[slot(block)]
            return pltpu.make_async_remote_copy(
                src_ref=dst if src is None else src, dst_ref=dst, send_sem=send_sems.at[a * 7 + k],
                recv_sem=recv_sems.at[a * 7 + k], device_id=to, device_id_type=MESH)

        mine = [pltpu.make_async_copy(x_refs[a], o_refs[a].at[slot(me)], local_sems.at[a]) for a in range(n)]
        for cp in mine:
            cp.start()
        first = []
        for a in range(n):
            first.append(copy(a, 0, me, sibling, src=x_refs[a]))
            first += [copy(a, 1 + j, me, (*chip, c), src=x_refs[a]) for j, chip in enumerate(chips)]
        for cp in first:
            cp.start()
        passed = []
        for j, chip in enumerate(chips):
            for a in range(n):
                copy(a, 1 + j, (*chip, c), me).wait_recv()
                cp = copy(a, 4 + j, (*chip, c), sibling)
                cp.start()
                passed.append(cp)
        for a in range(n):
            copy(a, 0, sibling, me).wait_recv()
        for j, chip in enumerate(chips):
            for a in range(n):
                copy(a, 4 + j, (*chip, 1 - c), me).wait_recv()
        for cp in first + passed:
            cp.wait_send()
        for cp in mine:
            cp.wait()

    return pl.pallas_call(
        body, in_specs=[HBM_SPEC] * n, out_specs=[HBM_SPEC] * n,
        out_shape=[jax.ShapeDtypeStruct((N_DEV,) + v.shape, v.dtype) for v in xs],
        scratch_shapes=[pltpu.SemaphoreType.DMA((7 * n,)), pltpu.SemaphoreType.DMA((7 * n,)), pltpu.SemaphoreType.DMA((n,))],
        name="all_gather_weights")(*xs)


def _adamw(name, parts, w, m, v, rows_per_step):
    r, c = w.shape
    tr = r if r <= rows_per_step else (rows_per_step if r % rows_per_step == 0 else r // 2)
    c1 = 1.0 - ADAM_B1 ** ADAM_STEP
    c2 = 1.0 - ADAM_B2 ** ADAM_STEP

    def body(p_ref, w_ref, m_ref, v_ref, g_ref, d_ref, mo_ref, vo_ref):
        g = p_ref[0].astype(F32)
        for s in range(1, N_DEV):
            g = g + p_ref[s].astype(F32)
        m_new = ADAM_B1 * m_ref[...] + (1.0 - ADAM_B1) * g
        v_new = ADAM_B2 * v_ref[...] + (1.0 - ADAM_B2) * (g * g)
        g_ref[...] = g
        mo_ref[...] = m_new
        vo_ref[...] = v_new
        d_ref[...] = -ADAM_LR * ((m_new / c1) / (jnp.sqrt(v_new / c2) + ADAM_EPS) + ADAM_WD * w_ref[...])

    row = pl.BlockSpec((tr, c), lambda i: (i, 0))
    out = jax.ShapeDtypeStruct((r, c), F32)
    return pl.pallas_call(body, grid=(r // tr,), in_specs=[pl.BlockSpec((N_DEV, tr, c), lambda i: (0, i, 0)), row, row, row],
                          out_specs=[row, row, row, row], out_shape=[out, out, out, out], name=name,
                          compiler_params=_params(1))(parts, w, m, v)


_VEC_NAMES = ("ln_v_g", "ln_v_b", "b_s", "conv_b_b", "ln_b_g", "ln_b_b", "g_ffn", "g_pg", "g_ple", "g_final")
_LOSS_ROW = len(_VEC_NAMES)


def _pack_replicated(d, loss=None):
    head = jnp.zeros((16, 1024), F32)
    for i, k in enumerate(_VEC_NAMES):
        head = head.at[i].set(d[k].reshape(1024))
    if loss is not None:
        head = head.at[_LOSS_ROW, 0].set(loss)
    fb = jnp.pad(d["ffn_conv_b"].reshape(-1), (0, 6144 - 5632)).reshape(6, 1024)
    return jnp.concatenate([head, jnp.pad(fb, ((0, 2), (0, 0))), d["w_s"].reshape(128, 1024)], axis=0)


def _unpack_replicated(pk, shapes):
    out = {k: pk[i, :].reshape(shapes[k]) for i, k in enumerate(_VEC_NAMES)}
    out["loss"] = pk[_LOSS_ROW, 0]
    out["ffn_conv_b"] = pk[16:22, :].reshape(-1)[:5632].reshape(shapes["ffn_conv_b"])
    out["w_s"] = pk[24:152, :].reshape(shapes["w_s"])
    return out


def _rows8(vec):
    return jnp.pad(vec.reshape(1, 1024), ((0, 7), (0, 0)))


def _pack_sharded_small(conv_b_w, ffn_conv_w):
    lead = conv_b_w.shape[:-2]
    pad0 = [(0, 0)] * len(lead)
    a = jnp.pad(conv_b_w, pad0 + [(0, 1), (0, 0)])
    b = jnp.pad(ffn_conv_w.reshape(lead + (CONV_F * 704,)), pad0 + [(0, 24 * 128 - CONV_F * 704)]).reshape(lead + (24, 128))
    return jnp.concatenate([a, b], axis=-2)


def _unpack_sharded_small(pk):
    lead = pk.shape[:-2]
    conv_b_w = pk[..., 0:CONV_B, :]
    ffn = pk[..., 32:56, :].reshape(lead + (24 * 128,))[..., :CONV_F * 704].reshape(lead + (CONV_F, 704))
    return conv_b_w, ffn


_WEIGHTS = ("g_mix", "w_in", "ln_v_g", "ln_v_b", "w_s", "b_s", "w_a_out", "conv_b_w", "conv_b_b", "ln_b_g", "ln_b_b",
            "w_b_out", "w_o", "g_ffn", "w_up", "ffn_conv_w", "ffn_conv_b", "w_down", "g_pg", "w_pg", "w_ple", "g_ple",
            "g_final")
_BIG = ("w_in", "w_a_out", "w_b_out", "w_o", "w_up", "w_down", "w_pg", "w_ple")


def kernel(x, p, g_mix, w_in, ln_v_g, ln_v_b, w_s, b_s, w_a_out, conv_b_w, conv_b_b, ln_b_g, ln_b_b, w_b_out, w_o, g_ffn, w_up, ffn_conv_w, ffn_conv_b, w_down, g_pg, w_pg, w_ple, g_ple, g_final, loss_target, m_g_mix, m_w_in, m_ln_v_g, m_ln_v_b, m_w_s, m_b_s, m_w_a_out, m_conv_b_w, m_conv_b_b, m_ln_b_g, m_ln_b_b, m_w_b_out, m_w_o, m_g_ffn, m_w_up, m_ffn_conv_w, m_ffn_conv_b, m_w_down, m_g_pg, m_w_pg, m_w_ple, m_g_ple, m_g_final, v_g_mix, v_w_in, v_ln_v_g, v_ln_v_b, v_w_s, v_b_s, v_w_a_out, v_conv_b_w, v_conv_b_b, v_ln_b_g, v_ln_b_b, v_w_b_out, v_w_o, v_g_ffn, v_w_up, v_ffn_conv_w, v_ffn_conv_b, v_w_down, v_g_pg, v_w_pg, v_w_ple, v_g_ple, v_g_final):
    local = dict(locals())
    wts = {k: local[k] for k in _WEIGHTS}
    mom = {k: local["m_" + k] for k in _WEIGHTS}
    var = {k: local["v_" + k] for k in _WEIGHTS}
    shapes = {k: wts[k].shape for k in _WEIGHTS}

    bsz, seq, d = x.shape
    t = bsz * seq
    x0 = x.reshape(t, d)
    p0 = p.reshape(t, p.shape[-1])
    target = loss_target.reshape(t, d)
    tm = min(TM_MM, t)
    tm_wide = min(TM_WIDE, t)
    tt = min(TT_MM, t)
    n_row = t // tm
    n_tok = t // tt

    def sq(a):
        return a.reshape(a.shape[1:])

    shard = {k: sq(wts[k]).astype(BF16) for k in _BIG}
    w_in3, small8 = _all_gather([shard["w_in"], _pack_sharded_small(sq(conv_b_w), sq(ffn_conv_w))])
    conv_b_w8, ffn_conv_w8 = _unpack_sharded_small(small8)
    conv_w_full = conv_b_w8.transpose(1, 0, 2).reshape(CONV_B, N_DEV * conv_b_w8.shape[-1])
    n_in = w_in3.shape[2]
    f_blk = shard["w_up"].shape[1]

    ws_m = jnp.where(jnp.tril(jnp.ones((CHUNK, CHUNK), bool))[None], sq(w_s), 0.0).astype(BF16)
    ws_mt = jnp.swapaxes(ws_m, 1, 2)
    bias_full = jnp.broadcast_to(sq(b_s).T[:, :, None], (CHUNK, GROUPS, CHUNK)).reshape(CHUNK, GROUPS * CHUNK)
    ffn_b8 = ffn_conv_b.reshape(N_DEV, 1, f_blk)

    h1 = _rms_fwd("rms_mix", x0, g_mix)
    z, (wa3, wb3, wo3) = _matmul(
        "mm_in", h1, w_in3, dims=_NN, grid=(N_DEV, t // tm_wide, 1),
        a_spec=pl.BlockSpec((tm_wide, d), lambda j, i, k: (i, 0)),
        b_spec=pl.BlockSpec((None, d, n_in), lambda j, i, k: (j, 0, 0)),
        o_spec=pl.BlockSpec((tm_wide, n_in), lambda j, i, k: (i, j)), acc_shape=(tm_wide, n_in),
        out_shape=jax.ShapeDtypeStruct((t, N_DEV * n_in), BF16),
        exch=[("gather", shard[k]) for k in ("w_a_out", "w_b_out", "w_o")])
    w_a = wa3.reshape(-1, d)
    w_b = wb3.reshape(-1, d)
    w_om = wo3.reshape(-1, d)
    (pa, cs, c_saved), (w_up3, wd3, wpg3, wple3) = _branch_fwd(
        z, ln_v_g, ln_v_b, ws_m, bias_full, conv_w_full, conv_b_b, ln_b_g, ln_b_b, seq,
        exch=[("gather", shard[k]) for k in ("w_up", "w_down", "w_pg", "w_ple")])
    w_pgm = wpg3.reshape(-1, d)
    w_d4 = wd3.reshape(N_DEV // 2, f_blk, d)
    w_plem = wple3.transpose(1, 0, 2).reshape(wple3.shape[1], d)
    ya = _mm_rows("mm_a_out", pa, w_a, dims=_NN, tm=tm, out_dtype=BF16)
    yb, merged = _mm_rows("mm_b_out", cs, w_b, dims=_NN, tm=tm, epi=_epi_merge_fwd(z, ya, tm))
    x1, h2 = _mm_rows("mm_o", merged, w_om, dims=_NN, tm=tm, epi=_epi_residual_rms(x0, g_ffn, tm))
    up0 = _matmul("mm_up", h2, w_up3, dims=_NN, grid=(N_DEV, t // tm_wide, 1),
                  a_spec=pl.BlockSpec((tm_wide, d), lambda j, i, k: (i, 0)),
                  b_spec=pl.BlockSpec((None, d, f_blk), lambda j, i, k: (j, 0, 0)),
                  o_spec=pl.BlockSpec((None, tm_wide, f_blk), lambda j, i, k: (j, i, 0)), acc_shape=(tm_wide, f_blk),
                  out_shape=jax.ShapeDtypeStruct((N_DEV, t, f_blk), BF16))
    act, upc = _ffn_mid_fwd(up0, ffn_conv_w8, ffn_b8, seq)
    x2, hq = _matmul("mm_down", act, w_d4, dims=_NN, grid=(n_row, 1, N_DEV // 2),
                     a_spec=pl.BlockSpec((None, tm, f_blk), lambda i, j, k: (k, i, 0)),
                     b_spec=pl.BlockSpec((None, f_blk, d), lambda i, j, k: (k, 0, 0)),
                     acc_shape=(tm, d), epi=_epi_residual_rms(x1, g_pg, tm))
    r = _mm_rows("mm_ple", p0, w_plem, dims=_NN, tm=tm, out_dtype=F32)

    dx3, dq, dr, loss_v, dg_final, dg_ple = _mm_rows(
        "mm_pg", hq, w_pgm, dims=_NN, tm=tm // 2, epi=_epi_head(x2, r, target, g_ple, g_final.reshape(1, d), tm // 2))

    recv = {}
    gw_pg = _mm_wgrad("wg_pg", hq, dq, out_dtype=BF16, tt=tt).reshape(wpg3.shape)
    dw_ple = _mm_wgrad("wg_ple", p0, dr, out_dtype=BF16, tt=tt)
    gw_ple = dw_ple.reshape(dw_ple.shape[0], N_DEV, -1).transpose(1, 0, 2)
    dx2, dx2b, dg_pg = _mm_rows("mm_pg_t", dq, w_pgm, dims=_NT, tm=tm, epi=_epi_rms_bwd(x2, g_pg, dx3, tm, want_bf16=True))

    dact, (recv["w_pg"], recv["w_ple"]) = _matmul(
        "mm_down_t", dx2b, w_d4, dims=_NT, grid=(N_DEV // 2, n_row, 1),
        a_spec=pl.BlockSpec((tm, d), lambda j, i, k: (i, 0)),
        b_spec=pl.BlockSpec((None, f_blk, d), lambda j, i, k: (j, 0, 0)),
        o_spec=pl.BlockSpec((None, tm, f_blk), lambda j, i, k: (j, i, 0)), acc_shape=(tm, f_blk),
        out_shape=jax.ShapeDtypeStruct((N_DEV // 2, t, f_blk), BF16),
        exch=[("scatter", gw_pg), ("scatter", gw_ple)])
    gw_down = _matmul("wg_down", act, dx2b, dims=_TN, grid=(N_DEV // 2, 1, n_tok),
                      a_spec=pl.BlockSpec((None, tt, f_blk), lambda j, i, k: (j, k, 0)),
                      b_spec=pl.BlockSpec((tt, d), lambda j, i, k: (k, 0)),
                      o_spec=pl.BlockSpec((None, f_blk, d), lambda j, i, k: (j, 0, 0)), acc_shape=(f_blk, d),
                      out_shape=jax.ShapeDtypeStruct((N_DEV // 2, f_blk, d), BF16)).reshape(wd3.shape)
    (d_up0, dffn_w8, dffn_b8), (recv["w_down"],) = _ffn_mid_bwd(up0, upc, dact, ffn_conv_w8, seq,
                                                                exch=[("scatter", gw_down)])
    gw_up = _matmul("wg_up", h2, d_up0, dims=_TN, grid=(N_DEV, 1, n_tok),
                    a_spec=pl.BlockSpec((tt, d), lambda j, i, k: (k, 0)),
                    b_spec=pl.BlockSpec((None, tt, f_blk), lambda j, i, k: (j, k, 0)),
                    o_spec=pl.BlockSpec((None, d, f_blk), lambda j, i, k: (j, 0, 0)), acc_shape=(d, f_blk),
                    out_shape=jax.ShapeDtypeStruct((N_DEV, d, f_blk), BF16))
    (dx1, dx1b, dg_ffn), (recv["w_up"],) = _matmul(
        "mm_up_t", d_up0, w_up3, dims=_NT, grid=(n_row, 1, N_DEV),
        a_spec=pl.BlockSpec((None, tm, f_blk), lambda i, j, k: (k, i, 0)),
        b_spec=pl.BlockSpec((None, d, f_blk), lambda i, j, k: (k, 0, 0)),
        acc_shape=(tm, d), epi=_epi_rms_bwd(x1, g_ffn, dx2, tm, want_bf16=True), exch=[("scatter", gw_up)])

    dya, dyb, dgates = _mm_rows("mm_o_t", dx1b, w_om, dims=_NT, tm=tm, epi=_epi_merge_bwd(z, ya, yb, tm))
    gw_o = _mm_wgrad("wg_o", merged, dx1b, out_dtype=BF16, tt=tt).reshape(wo3.shape)
    dpa = _mm_rows("mm_a_out_t", dya, w_a, dims=_NT, tm=tm, out_dtype=BF16)
    dcs = _mm_rows("mm_b_out_t", dyb, w_b, dims=_NT, tm=tm, out_dtype=BF16)
    gw_a = _mm_wgrad("wg_a_out", pa, dya, out_dtype=BF16, tt=tt).reshape(wa3.shape)
    gw_b = _mm_wgrad("wg_b_out", cs, dyb, out_dtype=BF16, tt=tt).reshape(wb3.shape)
    (dz, dlvg, dlvb, dws, dbs_full, dconv_w, dconv_b, dlbg, dlbb), (recv["w_o"], recv["w_a_out"], recv["w_b_out"]) = _branch_bwd(
        z, c_saved, dpa, dcs, dgates, ln_v_g, ln_v_b, ws_m, ws_mt, bias_full, conv_w_full, ln_b_g, ln_b_b, seq,
        exch=[("scatter", gw_o), ("scatter", gw_a), ("scatter", gw_b)])
    gw_in = _matmul("wg_in", h1, dz, dims=_TN, grid=(N_DEV, 1, n_tok),
                    a_spec=pl.BlockSpec((tt, d), lambda j, i, k: (k, 0)),
                    b_spec=pl.BlockSpec((tt, n_in), lambda j, i, k: (k, j)),
                    o_spec=pl.BlockSpec((None, d, n_in), lambda j, i, k: (j, 0, 0)), acc_shape=(d, n_in),
                    out_shape=jax.ShapeDtypeStruct((N_DEV, d, n_in), BF16))
    db_s = dbs_full[:, :GROUPS].T
    rep_partial = _pack_replicated(
        dict(ln_v_g=dlvg, ln_v_b=dlvb, b_s=db_s, conv_b_b=dconv_b, ln_b_g=dlbg, ln_b_b=dlbb, g_ffn=dg_ffn,
             g_pg=dg_pg, g_ple=dg_ple, g_final=dg_final, ffn_conv_b=dffn_b8, w_s=dws), loss=loss_v[0, 0])
    dconv_w8 = dconv_w.reshape(CONV_B, N_DEV, -1).transpose(1, 0, 2)
    small_partial = _pack_sharded_small(dconv_w8, dffn_w8)
    (grad_x, dg_mix), (recv["w_in"], recv_small, recv_rep) = _matmul(
        "mm_in_t", dz, w_in3, dims=_NT, grid=(n_row, 1, N_DEV),
        a_spec=pl.BlockSpec((tm, n_in), lambda i, j, k: (i, k)),
        b_spec=pl.BlockSpec((None, d, n_in), lambda i, j, k: (k, 0, 0)),
        acc_shape=(tm, d), epi=_epi_rms_bwd(x0, g_mix, dx1, tm, want_bf16=False),
        exch=[("scatter", gw_in), ("scatter", small_partial), ("gather", rep_partial)])
    (recv_g_mix,) = _exchange("exchange_g_mix", [("gather", _rows8(dg_mix))])

    grads, deltas, new_m, new_v = {}, {}, {}, {}
    by_kind = (grads, deltas, new_m, new_v)

    def two_d(a):
        a = sq(a)
        return a.reshape(-1, a.shape[-1])

    for k in _BIG:
        parts = recv[k].reshape(N_DEV, -1, recv[k].shape[-1])
        outs = _adamw("adamw_" + k, parts, two_d(wts[k]), two_d(mom[k]), two_d(var[k]), 128)
        for tgt, o in zip(by_kind, outs):
            tgt[k] = o.reshape(shapes[k])

    small = [_pack_sharded_small(sq(s["conv_b_w"]), sq(s["ffn_conv_w"])) for s in (wts, mom, var)]
    for tgt, o in zip(by_kind, _adamw("adamw_conv", recv_small, small[0], small[1], small[2], 56)):
        cw, fw = _unpack_sharded_small(o)
        tgt["conv_b_w"] = cw.reshape(shapes["conv_b_w"])
        tgt["ffn_conv_w"] = fw.reshape(shapes["ffn_conv_w"])

    rep = [_pack_replicated(s) for s in (wts, mom, var)]
    loss = None
    for tgt, o in zip(by_kind, _adamw("adamw_replicated", recv_rep, rep[0], rep[1], rep[2], 152)):
        un = _unpack_replicated(o, shapes)
        if tgt is grads:
            loss = un["loss"]
        for k in _VEC_NAMES + ("ffn_conv_b", "w_s"):
            tgt[k] = un[k]

    gm = [_rows8(s["g_mix"]) for s in (wts, mom, var)]
    for tgt, o in zip(by_kind, _adamw("adamw_g_mix", recv_g_mix, gm[0], gm[1], gm[2], 8)):
        tgt["g_mix"] = o[0, :].reshape(shapes["g_mix"])

    return (loss, grad_x.reshape(x.shape), *[grads[k] for k in _WEIGHTS], *[deltas[k] for k in _WEIGHTS],
            *[new_m[k] for k in _WEIGHTS], *[new_v[k] for k in _WEIGHTS])
```

```python
import math

import jax
import jax.numpy as jnp
from jax import lax
from jax.experimental import pallas as pl
from jax.experimental.pallas import tpu as pltpu

F32 = jnp.float32
BF16 = jnp.bfloat16

N_DEV = 8
EPS_RMS = 1e-6
EPS_LN = 1e-5
CHUNK = 128
GROUPS = 8
CONV_B = 31
CONV_F = 3
HALO_B = 32
HALO_F = 8
ROWS_F = 16
SUB = 8

ADAM_LR = 0.001
ADAM_B1 = 0.9
ADAM_B2 = 0.999
ADAM_EPS = 1e-08
ADAM_WD = 0.01
ADAM_STEP = 10

VMEM_LIMIT = 56 * 1024 * 1024
TM_MM = 1024
TM_WIDE = 2048
TT_MM = 2048
TM_EW = 256
EPI_STRIP = 256

_NN = (((1,), (0,)), ((), ()))
_NT = (((1,), (1,)), ((), ()))
_TN = (((0,), (0,)), ((), ()))
MESH = pl.DeviceIdType.MESH
HBM_SPEC = pl.BlockSpec(memory_space=pltpu.HBM)


def _params(n_axes):
    return pltpu.CompilerParams(dimension_semantics=("arbitrary",) * n_axes, vmem_limit_bytes=VMEM_LIMIT)


def _gelu(x):
    k = math.sqrt(2.0 / math.pi)
    return 0.5 * x * (1.0 + jnp.tanh(k * (x + 0.044715 * (x * x * x))))


def _gelu_and_grad(x):
    k = math.sqrt(2.0 / math.pi)
    x2 = x * x
    t = jnp.tanh(k * (x + 0.044715 * (x2 * x)))
    g = 0.5 * x * (1.0 + t)
    dg = 0.5 * (1.0 + t) + 0.5 * x * (1.0 - t * t) * (k * (1.0 + 3.0 * 0.044715 * x2))
    return g, dg


def _sigmoid(x):
    return 1.0 / (1.0 + jnp.exp(-x))


def _rowsum(x):
    return jnp.sum(x, axis=0, keepdims=True)


def _mean(x):
    return jnp.mean(x, axis=-1, keepdims=True)


def _exchange_io(exch):
    n = len(exch)
    out_shape = [jax.ShapeDtypeStruct(v.shape if kind == "scatter" else (N_DEV,) + v.shape, v.dtype) for kind, v in exch]
    scratch = [pltpu.SemaphoreType.DMA((7 * n,)), pltpu.SemaphoreType.DMA((7 * n,)), pltpu.SemaphoreType.DMA((n,))] if n else []
    return [HBM_SPEC] * n, [HBM_SPEC] * n, out_shape, scratch


def _exchange_step(kinds, x_refs, o_refs, send_sems, recv_sems, local_sems):
    n = len(kinds)
    x, y, c = lax.axis_index("x"), lax.axis_index("y"), lax.axis_index("c")
    me = 4 * x + 2 * y + c

    def src(a, to_slot):
        return x_refs[a].at[to_slot] if kinds[a] == "scatter" else x_refs[a]

    mine = [pltpu.make_async_copy(src(a, me), o_refs[a].at[me], local_sems.at[a]) for a in range(n)]
    sends, recvs = [], []
    for m in range(1, N_DEV):
        mx, my, mc = (m >> 2) & 1, (m >> 1) & 1, m & 1
        px, py, pc = (1 - x if mx else x), (1 - y if my else y), (1 - c if mc else c)
        peer = 4 * px + 2 * py + pc
        for a in range(n):
            k = a * 7 + m - 1
            sends.append(pltpu.make_async_remote_copy(
                src_ref=src(a, peer), dst_ref=o_refs[a].at[me], send_sem=send_sems.at[k], recv_sem=recv_sems.at[k],
                device_id=(px, py, pc), device_id_type=MESH))
            recvs.append(pltpu.make_async_remote_copy(
                src_ref=src(a, peer), dst_ref=o_refs[a].at[peer], send_sem=send_sems.at[k], recv_sem=recv_sems.at[k],
                device_id=(px, py, pc), device_id_type=MESH))

    def start():
        for cp in mine + sends:
            cp.start()

    def finish():
        for cp in recvs:
            cp.wait_recv()
        for cp in sends:
            cp.wait_send()
        for cp in mine:
            cp.wait()

    return start, finish


def _exchange(name, exch):
    n = len(exch)
    kinds = [k for k, _ in exch]
    in_specs, out_specs, out_shape, scratch = _exchange_io(exch)

    def body(*refs):
        start, finish = _exchange_step(kinds, refs[:n], refs[n:2 * n], *refs[2 * n:])
        start()
        finish()

    return pl.pallas_call(body, in_specs=in_specs, out_specs=out_specs, out_shape=out_shape, scratch_shapes=scratch,
                          name=name)(*[v for _, v in exch])


class _Epilogue:
    def __init__(self, fn, ins=(), in_specs=(), out_specs=(), out_shape=(), strip=None):
        self.fn, self.ins, self.in_specs = fn, list(ins), list(in_specs)
        self.out_specs, self.out_shape = list(out_specs), list(out_shape)
        self.strip = strip


def _matmul(name, a, b, *, dims, grid, a_spec, b_spec, acc_shape, o_spec=None, out_shape=None, epi=None, exch=()):
    nk = grid[2]
    plain = epi is None
    if plain:
        def store(acc, ins, outs, i):
            outs[0][...] = acc.astype(outs[0].dtype)
        epi = _Epilogue(store, out_specs=[o_spec], out_shape=[out_shape])
    n_in = 2 + len(epi.ins)
    n_out = len(epi.out_specs)
    n_ex = len(exch)
    kinds = [k for k, _ in exch]
    ex_in, ex_out, ex_shape, ex_scratch = _exchange_io(exch)

    def body(*refs):
        a_ref, b_ref = refs[:2]
        step0 = pl.program_id(0)
        epi_ins, rest = refs[2:n_in], refs[n_in:]
        x_refs, rest = rest[:n_ex], rest[n_ex:]
        outs, rest = rest[:n_out], rest[n_out:]
        o_refs, scr = rest[:n_ex], rest[n_ex:]
        if n_ex:
            pid = [pl.program_id(ax) for ax in range(3)]
            ex_start, ex_finish = _exchange_step(kinds, x_refs, o_refs, *scr[len(scr) - 3:])
            pl.when((pid[0] == 0) & (pid[1] == 0) & (pid[2] == 0))(ex_start)
        part = lax.dot_general(a_ref[...].astype(BF16), b_ref[...].astype(BF16), dims, preferred_element_type=F32)

        def run_epilogue(rows_of_acc):
            rows = acc_shape[0]
            strip = rows if epi.strip is None else min(epi.strip, rows)
            for s in range(0, rows, strip):
                def view(ref):
                    return ref.at[pl.ds(s, strip)] if ref.shape[0] == rows else ref
                first = (step0 == 0) if s == 0 else False
                epi.fn(rows_of_acc(s, strip), [view(r) for r in epi_ins], [view(r) for r in outs], first)

        if nk == 1:
            run_epilogue(lambda s, n: part[s:s + n])
        else:
            acc_ref = scr[0]
            k = pl.program_id(2)

            @pl.when(k == 0)
            def _():
                acc_ref[...] = part

            @pl.when(k > 0)
            def _():
                acc_ref[...] += part

            @pl.when(k == nk - 1)
            def _():
                run_epilogue(lambda s, n: acc_ref[pl.ds(s, n), :])
        if n_ex:
            pl.when((pid[0] == grid[0] - 1) & (pid[1] == grid[1] - 1) & (pid[2] == grid[2] - 1))(ex_finish)

    scratch = ([pltpu.VMEM(acc_shape, F32)] if nk > 1 else []) + ex_scratch
    res = pl.pallas_call(body, grid=grid, in_specs=[a_spec, b_spec] + epi.in_specs + ex_in,
                         out_specs=epi.out_specs + ex_out, out_shape=epi.out_shape + ex_shape, scratch_shapes=scratch,
                         name=name, compiler_params=_params(3))(a, b, *epi.ins, *[v for _, v in exch])
    main = res[0] if plain else res[:n_out]
    return (main, res[n_out:]) if n_ex else main


def _mm_rows(name, a, w, *, dims, tm, out_dtype=None, epi=None):
    t, k = a.shape
    n = w.shape[1] if dims == _NN else w.shape[0]
    tm = min(tm, t)
    return _matmul(name, a, w, dims=dims, grid=(t // tm, 1, 1),
                   a_spec=pl.BlockSpec((tm, k), lambda i, j, kk: (i, 0)),
                   b_spec=pl.BlockSpec(w.shape, lambda i, j, kk: (0, 0)),
                   o_spec=pl.BlockSpec((tm, n), lambda i, j, kk: (i, 0)), acc_shape=(tm, n),
                   out_shape=jax.ShapeDtypeStruct((t, n), out_dtype) if epi is None else None, epi=epi)


def _mm_wgrad(name, a, b, *, out_dtype, tt):
    t, m = a.shape
    n = b.shape[1]
    tt = min(tt, t)
    return _matmul(name, a, b, dims=_TN, grid=(1, 1, t // tt),
                   a_spec=pl.BlockSpec((tt, m), lambda i, j, kk: (kk, 0)),
                   b_spec=pl.BlockSpec((tt, n), lambda i, j, kk: (kk, 0)),
                   o_spec=pl.BlockSpec((m, n), lambda i, j, kk: (0, 0)),
                   acc_shape=(m, n), out_shape=jax.ShapeDtypeStruct((m, n), out_dtype))


def _row3(tm, d, col=0):
    return pl.BlockSpec((tm, d), lambda i, j, k: (i, col))


def _vec3(d):
    return pl.BlockSpec((1, d), lambda i, j, k: (0, 0))


def _accumulate_over_rows(ref, part, first):
    if first is False:
        ref[...] += part
        return

    @pl.when(first)
    def _():
        ref[...] = part + jnp.zeros_like(ref)

    @pl.when(jnp.logical_not(first))
    def _():
        ref[...] += part


def _epi_residual_rms(res, g, tm):
    t, d = res.shape

    def fn(acc, ins, outs, i):
        res_ref, g_ref = ins
        xv = acc + res_ref[...]
        outs[0][...] = xv
        rstd = lax.rsqrt(_mean(xv * xv) + EPS_RMS)
        outs[1][...] = ((xv * rstd) * g_ref[...]).astype(BF16)

    return _Epilogue(fn, strip=EPI_STRIP, ins=[res, g], in_specs=[_row3(tm, d), _vec3(d)], out_specs=[_row3(tm, d), _row3(tm, d)],
                     out_shape=[jax.ShapeDtypeStruct((t, d), F32), jax.ShapeDtypeStruct((t, d), BF16)])


def _epi_rms_bwd(x, g, dres, tm, *, want_bf16):
    t, d = x.shape

    def fn(acc, ins, outs, i):
        x_ref, g_ref, dres_ref = ins
        xv = x_ref[...]
        rstd = lax.rsqrt(_mean(xv * xv) + EPS_RMS)
        nrm = xv * rstd
        dn = acc * g_ref[...]
        dx = dres_ref[...] + rstd * (dn - nrm * _mean(dn * nrm))
        outs[0][...] = dx
        if want_bf16:
            outs[1][...] = dx.astype(BF16)
        _accumulate_over_rows(outs[-1], _rowsum(acc * nrm), i)

    row = _row3(tm, d)
    n_dx = 2 if want_bf16 else 1
    return _Epilogue(fn, strip=EPI_STRIP, ins=[x, g, dres], in_specs=[row, _vec3(d), row], out_specs=[row] * n_dx + [_vec3(d)],
                     out_shape=[jax.ShapeDtypeStruct((t, d), F32)] + [jax.ShapeDtypeStruct((t, d), BF16)] * (n_dx - 1)
                     + [jax.ShapeDtypeStruct((1, d), F32)])


def _epi_merge_fwd(z, ya, tm):
    t, w = ya.shape

    def fn(acc, ins, outs, i):
        ga_ref, gb_ref, ya_ref = ins
        outs[0][...] = acc.astype(BF16)
        sa = _sigmoid(ga_ref[...].astype(F32))
        sb = _sigmoid(gb_ref[...].astype(F32))
        outs[1][...] = (sa * ya_ref[...].astype(F32) + sb * acc).astype(BF16)

    row = _row3(tm, w)
    return _Epilogue(fn, strip=EPI_STRIP, ins=[z, z, ya], in_specs=[_row3(tm, w, 4), _row3(tm, w, 5), row], out_specs=[row, row],
                     out_shape=[jax.ShapeDtypeStruct((t, w), BF16)] * 2)


def _epi_merge_bwd(z, ya, yb, tm):
    t, w = ya.shape

    def fn(acc, ins, outs, i):
        ga_ref, gb_ref, ya_ref, yb_ref = ins
        sa = _sigmoid(ga_ref[...].astype(F32))
        sb = _sigmoid(gb_ref[...].astype(F32))
        outs[0][...] = (acc * sa).astype(BF16)
        outs[1][...] = (acc * sb).astype(BF16)
        outs[2][:, 0:w] = (acc * ya_ref[...].astype(F32) * sa * (1.0 - sa)).astype(BF16)
        outs[2][:, w:2 * w] = (acc * yb_ref[...].astype(F32) * sb * (1.0 - sb)).astype(BF16)

    row = _row3(tm, w)
    return _Epilogue(fn, strip=EPI_STRIP, ins=[z, z, ya, yb], in_specs=[_row3(tm, w, 4), _row3(tm, w, 5), row, row],
                     out_specs=[row, row, _row3(tm, 2 * w)],
                     out_shape=[jax.ShapeDtypeStruct((t, w), BF16)] * 2 + [jax.ShapeDtypeStruct((t, 2 * w), BF16)])


def _epi_head(x2, r, target, g_ple, g_final, tm):
    t, d = x2.shape

    def fn(acc, ins, outs, i):
        x2_ref, r_ref, tg_ref, gple_ref, gfin_ref = ins
        dx3_ref, dq_ref, dr_ref, loss_ref, dgfin_ref, dgple_ref = outs
        pg = _sigmoid(acc)
        rv = r_ref[...]
        rstd_r = lax.rsqrt(_mean(rv * rv) + EPS_RMS)
        nr = rv * rstd_r
        pe = nr * gple_ref[...]
        x3 = x2_ref[...] + pe * pg
        rstd3 = lax.rsqrt(_mean(x3 * x3) + EPS_RMS)
        n3 = x3 * rstd3
        err = n3 * gfin_ref[...] - tg_ref[...]
        loss_part = jnp.sum(_rowsum(err * err), axis=1, keepdims=True) * (0.5 / d)
        dy = err * (1.0 / d)
        dn3 = dy * gfin_ref[...]
        dx3 = rstd3 * (dn3 - n3 * _mean(dn3 * n3))
        dx3_ref[...] = dx3
        dq_ref[...] = (dx3 * pe * pg * (1.0 - pg)).astype(BF16)
        dpe = dx3 * pg
        dnr = dpe * gple_ref[...]
        dr_ref[...] = (rstd_r * (dnr - nr * _mean(dnr * nr))).astype(BF16)
        _accumulate_over_rows(loss_ref, loss_part, i)
        _accumulate_over_rows(dgfin_ref, _rowsum(dy * n3), i)
        _accumulate_over_rows(dgple_ref, _rowsum(dpe * nr), i)

    row = _row3(tm, d)
    vec = jax.ShapeDtypeStruct((1, d), F32)
    return _Epilogue(fn, strip=EPI_STRIP, ins=[x2, r, target, g_ple, g_final], in_specs=[row, row, row, _vec3(d), _vec3(d)],
                     out_specs=[row, row, row, _vec3(d), _vec3(d), _vec3(d)],
                     out_shape=[jax.ShapeDtypeStruct((t, d), F32), jax.ShapeDtypeStruct((t, d), BF16),
                                jax.ShapeDtypeStruct((t, d), BF16), vec, vec, vec])


def _vec_spec(d):
    return pl.BlockSpec((1, d), lambda i: (0, 0))


def _row_call(name, body, *, n_steps, in_specs, out_specs, out_shape, scratch, args, exch=()):
    n_in, n_out, n_scr, n_ex = len(in_specs), len(out_specs), len(scratch), len(exch)
    kinds = [k for k, _ in exch]
    ex_in, ex_out, ex_shape, ex_scratch = _exchange_io(exch)

    def wrapped(*refs):
        ins, rest = refs[:n_in], refs[n_in:]
        x_refs, rest = rest[:n_ex], rest[n_ex:]
        outs, rest = rest[:n_out], rest[n_out:]
        o_refs, rest = rest[:n_ex], rest[n_ex:]
        scr, sems = rest[:n_scr], rest[n_scr:]
        if n_ex:
            ex_start, ex_finish = _exchange_step(kinds, x_refs, o_refs, *sems)
            pl.when(pl.program_id(0) == 0)(ex_start)
        body(*ins, *outs, *scr)
        if n_ex:
            pl.when(pl.program_id(0) == n_steps - 1)(ex_finish)

    res = pl.pallas_call(wrapped, grid=(n_steps,), in_specs=list(in_specs) + ex_in, out_specs=list(out_specs) + ex_out,
                         out_shape=list(out_shape) + ex_shape, scratch_shapes=list(scratch) + ex_scratch, name=name,
                         compiler_params=_params(1))(*args, *[v for _, v in exch])
    return res[:n_out], res[n_out:]


def _rms_fwd(name, x, g):
    t, d = x.shape
    tm = min(TM_EW * 2, t)

    def body(x_ref, g_ref, h_ref):
        xv = x_ref[...]
        rstd = lax.rsqrt(_mean(xv * xv) + EPS_RMS)
        h_ref[...] = ((xv * rstd) * g_ref[...]).astype(BF16)

    row = pl.BlockSpec((tm, d), lambda i: (i, 0))
    return pl.pallas_call(body, grid=(t // tm,), in_specs=[row, _vec_spec(d)], out_specs=row,
                          out_shape=jax.ShapeDtypeStruct((t, d), BF16), name=name, compiler_params=_params(1))(x, g)


def _fill_shifted(buf_ref, sh_ref):
    n = sh_ref.shape[1]
    for p in range(1, SUB):
        sh_ref[p - 1] = buf_ref[p:p + n, :]


def _branch_fwd(z, ln_v_g, ln_v_b, ws_m, bias_full, conv_w, conv_b, ln_b_g, ln_b_b, seq, exch=()):
    t = z.shape[0]
    w = 1024
    tm = min(TM_EW, seq)
    tiles_per_seq = seq // tm
    n_chunks = tm // CHUNK

    def body(z_ref, lvg_ref, lvb_ref, ws_ref, bias_ref, cw_ref, cb_ref, lbg_ref, lbb_ref,
             pa_ref, cs_ref, c_ref, hist_ref, buf_ref, mix_ref, sh_ref, wb_ref):
        i = pl.program_id(0)
        u = z_ref[:, 0:w].astype(F32)
        v = z_ref[:, w:2 * w].astype(F32)
        ug = _gelu(u)
        vg = _gelu(v)
        dv = vg - _mean(vg)
        vhat = dv * lax.rsqrt(_mean(dv * dv) + EPS_LN)
        vn = (vhat * lvg_ref[...] + lvb_ref[...]).astype(BF16)
        for ci in range(n_chunks):
            rows = slice(ci * CHUNK, (ci + 1) * CHUNK)
            for g in range(GROUPS):
                cols = slice(g * CHUNK, (g + 1) * CHUNK)
                mix_ref[rows, cols] = lax.dot_general(ws_ref[g], vn[rows, cols], _NN, preferred_element_type=F32)
            mix_ref[rows, :] += bias_ref[...]
        pa_ref[...] = (ug * mix_ref[...]).astype(BF16)

        a = z_ref[:, 2 * w:3 * w].astype(F32)
        gl = z_ref[:, 3 * w:4 * w].astype(F32)
        glu = a * _sigmoid(gl)

        @pl.when(i % tiles_per_seq == 0)
        def _():
            hist_ref[...] = jnp.zeros_like(hist_ref)

        buf_ref[0:HALO_B, :] = hist_ref[...]
        buf_ref[HALO_B:, :] = glu
        hist_ref[...] = glu[tm - HALO_B:, :]
        _fill_shifted(buf_ref, sh_ref)

        @pl.when(i == 0)
        def _():
            for k in range(CONV_B):
                wb_ref[k] = jnp.broadcast_to(cw_ref[k:k + 1, :], (SUB, w))

        groups = 4

        def strip(si, _):
            s = pl.multiple_of(si * (groups * SUB), groups * SUB)
            acc = [jnp.zeros((SUB, w), F32) + cb_ref[...] for _ in range(groups)]
            for k in range(CONV_B):
                whole, part = divmod(HALO_B - (CONV_B - 1) + k, SUB)
                wk = wb_ref[k]
                for g in range(groups):
                    at = pl.ds(s + SUB * (whole + g), SUB)
                    acc[g] = acc[g] + wk * (buf_ref[at, :] if part == 0 else sh_ref[part - 1, at, :])
            for g in range(0, groups, 2):
                at = pl.ds(s + SUB * g, 2 * SUB)
                c = jnp.concatenate(acc[g:g + 2], axis=0)
                c_ref[at, :] = c
                dc = c - _mean(c)
                chat = dc * lax.rsqrt(_mean(dc * dc) + EPS_LN)
                cn = chat * lbg_ref[...] + lbb_ref[...]
                cs_ref[at, :] = (cn * _sigmoid(cn)).astype(BF16)
            return 0

        lax.fori_loop(0, tm // (groups * SUB), strip, 0)

    row = pl.BlockSpec((tm, w), lambda i: (i, 0))
    in_specs = [pl.BlockSpec((tm, 4 * w), lambda i: (i, 0)), _vec_spec(w), _vec_spec(w),
                pl.BlockSpec((GROUPS, CHUNK, CHUNK), lambda i: (0, 0, 0)), pl.BlockSpec((CHUNK, w), lambda i: (0, 0)),
                pl.BlockSpec((CONV_B, w), lambda i: (0, 0)), _vec_spec(w), _vec_spec(w), _vec_spec(w)]
    return _row_call(
        "branch_fwd", body, n_steps=t // tm, in_specs=in_specs, out_specs=[row, row, row],
        out_shape=[jax.ShapeDtypeStruct((t, w), BF16), jax.ShapeDtypeStruct((t, w), BF16), jax.ShapeDtypeStruct((t, w), F32)],
        scratch=[pltpu.VMEM((HALO_B, w), F32), pltpu.VMEM((HALO_B + tm, w), F32), pltpu.VMEM((tm, w), F32),
                 pltpu.VMEM((SUB - 1, HALO_B + tm - SUB, w), F32), pltpu.VMEM((CONV_B, SUB, w), F32)],
        args=(z, ln_v_g, ln_v_b, ws_m, bias_full, conv_w, conv_b, ln_b_g, ln_b_b), exch=exch)


def _branch_bwd(z, c_saved, dpa, dcs, dgates, ln_v_g, ln_v_b, ws_m, ws_mt, bias_full, conv_w, ln_b_g, ln_b_b, seq, exch=()):
    t = z.shape[0]
    w = 1024
    tm = min(TM_EW, seq)
    tiles_per_seq = seq // tm
    n_tiles = t // tm
    n_chunks = tm // CHUNK

    def body(z_ref, c_ref, dpa_ref, dcs_ref, dgt_ref, lvg_ref, lvb_ref, ws_ref, wst_ref, bias_ref, cw_ref,
             lbg_ref, lbb_ref,
             dz_ref, dlvg_ref, dlvb_ref, dws_ref, dbs_ref, dcw_ref, dcb_ref, dlbg_ref, dlbb_ref,
             carry_ref, glu_ref, dbuf_ref, mix_ref, dvn_ref, dbs_acc_ref, dglu_ref, sh_ref, wb_ref, dwacc_ref):
        i = pl.program_id(0)
        r = n_tiles - 1 - i

        @pl.when(i == 0)
        def _():
            for ref in (dlvg_ref, dlvb_ref, dws_ref, dbs_acc_ref, dwacc_ref, dcb_ref, dlbg_ref, dlbb_ref):
                ref[...] = jnp.zeros_like(ref)

        u = z_ref[:, 0:w].astype(F32)
        v = z_ref[:, w:2 * w].astype(F32)
        ug, dug = _gelu_and_grad(u)
        vg, dvg = _gelu_and_grad(v)
        dv0 = vg - _mean(vg)
        rstd_v = lax.rsqrt(_mean(dv0 * dv0) + EPS_LN)
        vhat = dv0 * rstd_v
        vn = (vhat * lvg_ref[...] + lvb_ref[...]).astype(BF16)
        dpa = dpa_ref[...].astype(F32)
        dmix = dpa * ug
        dmix_b = dmix.astype(BF16)
        for ci in range(n_chunks):
            rows = slice(ci * CHUNK, (ci + 1) * CHUNK)
            for g in range(GROUPS):
                cols = slice(g * CHUNK, (g + 1) * CHUNK)
                mix_ref[rows, cols] = lax.dot_general(ws_ref[g], vn[rows, cols], _NN, preferred_element_type=F32)
                dvn_ref[rows, cols] = lax.dot_general(wst_ref[g], dmix_b[rows, cols], _NN, preferred_element_type=F32)
                dws_ref[g] += lax.dot_general(dmix_b[rows, cols], vn[rows, cols], _NT, preferred_element_type=F32)
            mix_ref[rows, :] += bias_ref[...]
            dbs_acc_ref[...] += dmix[rows, :]
        dz_ref[:, 0:w] = (dpa * mix_ref[...] * dug).astype(BF16)
        dvn = dvn_ref[...]
        dlvg_ref[...] += _rowsum(dvn * vhat)
        dlvb_ref[...] += _rowsum(dvn)
        dvh = dvn * lvg_ref[...]
        dvg_in = rstd_v * (dvh - _mean(dvh) - vhat * _mean(dvh * vhat))
        dz_ref[:, w:2 * w] = (dvg_in * dvg).astype(BF16)

        c = c_ref[...]
        dc0 = c - _mean(c)
        rstd_c = lax.rsqrt(_mean(dc0 * dc0) + EPS_LN)
        chat = dc0 * rstd_c
        cn = chat * lbg_ref[...] + lbb_ref[...]
        sg = _sigmoid(cn)
        dcn = dcs_ref[...].astype(F32) * (sg * (1.0 + cn * (1.0 - sg)))
        dlbg_ref[...] += _rowsum(dcn * chat)
        dlbb_ref[...] += _rowsum(dcn)
        dch = dcn * lbg_ref[...]
        dc = rstd_c * (dch - _mean(dch) - chat * _mean(dch * chat))
        dcb_ref[...] += _rowsum(dc)

        a = z_ref[:, 2 * w:3 * w].astype(F32)
        gl = z_ref[:, 3 * w:4 * w].astype(F32)
        sgl = _sigmoid(gl)
        glu_ref[...] = a * sgl

        @pl.when(r % tiles_per_seq == tiles_per_seq - 1)
        def _():
            carry_ref[...] = jnp.zeros_like(carry_ref)

        dbuf_ref[0:tm, :] = dc
        dbuf_ref[tm:, :] = carry_ref[...]
        carry_ref[...] = dc[0:HALO_B, :]
        _fill_shifted(dbuf_ref, sh_ref)

        @pl.when(i == 0)
        def _():
            for k in range(CONV_B):
                wb_ref[k] = jnp.broadcast_to(cw_ref[k:k + 1, :], (SUB, w))

        groups = 2

        def strip(si, _):
            s = pl.multiple_of(si * (groups * SUB), groups * SUB)
            glu_rows = [glu_ref[pl.ds(s + SUB * g, SUB), :] for g in range(groups)]
            acc = [jnp.zeros((SUB, w), F32) for _ in range(groups)]
            for k in range(CONV_B):
                whole, part = divmod(CONV_B - 1 - k, SUB)
                wk = wb_ref[k]
                dw_part = jnp.zeros((SUB, w), F32)
                for g in range(groups):
                    at = pl.ds(s + SUB * (whole + g), SUB)
                    d_rows = dbuf_ref[at, :] if part == 0 else sh_ref[part - 1, at, :]
                    acc[g] = acc[g] + wk * d_rows
                    dw_part = dw_part + d_rows * glu_rows[g]
                dwacc_ref[k] += dw_part
            dglu_ref[pl.ds(s, groups * SUB), :] = jnp.concatenate(acc, axis=0)
            return 0

        lax.fori_loop(0, tm // (groups * SUB), strip, 0)
        dglu = dglu_ref[...]
        dz_ref[:, 2 * w:3 * w] = (dglu * sgl).astype(BF16)
        dz_ref[:, 3 * w:4 * w] = (dglu * a * sgl * (1.0 - sgl)).astype(BF16)
        dz_ref[:, 4 * w:6 * w] = dgt_ref[...]

        @pl.when(i == n_tiles - 1)
        def _():
            tri = lax.broadcasted_iota(jnp.int32, (CHUNK, CHUNK), 0) >= lax.broadcasted_iota(jnp.int32, (CHUNK, CHUNK), 1)
            lane = lax.broadcasted_iota(jnp.int32, (CHUNK, CHUNK), 1)
            dbs = jnp.zeros((CHUNK, CHUNK), F32)
            for g in range(GROUPS):
                dws_ref[g] = jnp.where(tri, dws_ref[g], 0.0)
                group_sum = jnp.sum(dbs_acc_ref[:, g * CHUNK:(g + 1) * CHUNK], axis=1, keepdims=True)
                dbs = jnp.where(lane == g, group_sum, dbs)
            dbs_ref[...] = dbs
            for k in range(CONV_B):
                dcw_ref[k:k + 1, :] = _rowsum(dwacc_ref[k])

    def rev(i):
        return n_tiles - 1 - i

    row = pl.BlockSpec((tm, w), lambda i: (rev(i), 0))
    full = lambda shape: pl.BlockSpec(shape, lambda i: (0,) * len(shape))
    in_specs = [pl.BlockSpec((tm, 4 * w), lambda i: (rev(i), 0)),
                row, row, row, pl.BlockSpec((tm, 2 * w), lambda i: (rev(i), 0)),
                _vec_spec(w), _vec_spec(w), full((GROUPS, CHUNK, CHUNK)), full((GROUPS, CHUNK, CHUNK)), full((CHUNK, w)),
                full((CONV_B, w)), _vec_spec(w), _vec_spec(w)]
    out_specs = [pl.BlockSpec((tm, 6 * w), lambda i: (rev(i), 0)), _vec_spec(w), _vec_spec(w), full((GROUPS, CHUNK, CHUNK)),
                 full((CHUNK, CHUNK)), full((CONV_B, w)), _vec_spec(w), _vec_spec(w), _vec_spec(w)]
    vec = jax.ShapeDtypeStruct((1, w), F32)
    out_shape = [jax.ShapeDtypeStruct((t, 6 * w), BF16), vec, vec, jax.ShapeDtypeStruct((GROUPS, CHUNK, CHUNK), F32),
                 jax.ShapeDtypeStruct((CHUNK, CHUNK), F32), jax.ShapeDtypeStruct((CONV_B, w), F32), vec, vec, vec]
    scratch = [pltpu.VMEM((HALO_B, w), F32), pltpu.VMEM((tm, w), F32), pltpu.VMEM((tm + HALO_B, w), F32),
               pltpu.VMEM((tm, w), F32), pltpu.VMEM((tm, w), F32), pltpu.VMEM((CHUNK, w), F32), pltpu.VMEM((tm, w), F32),
               pltpu.VMEM((SUB - 1, HALO_B + tm - SUB, w), F32), pltpu.VMEM((CONV_B, SUB, w), F32),
               pltpu.VMEM((CONV_B, SUB, w), F32)]
    return _row_call("branch_bwd", body, n_steps=n_tiles, in_specs=in_specs, out_specs=out_specs, out_shape=out_shape,
                     scratch=scratch, exch=exch,
                     args=(z, c_saved, dpa, dcs, dgates, ln_v_g, ln_v_b, ws_m, ws_mt, bias_full, conv_w, ln_b_g, ln_b_b))


def _conv3_window(prev8, x):
    win = jnp.concatenate([prev8, x], axis=0)
    n = x.shape[0]
    return [win[HALO_F - 2:HALO_F - 2 + n], win[HALO_F - 1:HALO_F - 1 + n], x]


def _ffn_mid_fwd(up0, conv_w, conv_b, seq):
    nb, t, f = up0.shape
    half = nb // 2
    tm = min(TM_EW, seq)
    tiles_per_seq = seq // tm
    n_strips = tm // ROWS_F

    def body(up_ref, w_ref, b_ref, act_ref, upc_ref, hist_ref):
        i = pl.program_id(0)

        @pl.when(i % tiles_per_seq == 0)
        def _():
            hist_ref[...] = jnp.zeros_like(hist_ref)

        for j in range(half):
            jv = j + half
            wg = [w_ref[j, k:k + 1, :] for k in range(CONV_F)]
            wv = [w_ref[jv, k:k + 1, :] for k in range(CONV_F)]
            bg, bv = b_ref[j], b_ref[jv]

            def strip(c, carry):
                rows = pl.ds(pl.multiple_of(c * ROWS_F, ROWS_F), ROWS_F)
                xg = up_ref[j, rows, :].astype(F32)
                xv = up_ref[jv, rows, :].astype(F32)
                sg = _conv3_window(carry[0], xg)
                sv = _conv3_window(carry[1], xv)
                gate = bg + wg[0] * sg[0] + wg[1] * sg[1] + wg[2] * sg[2]
                val = bv + wv[0] * sv[0] + wv[1] * sv[1] + wv[2] * sv[2]
                act_ref[j, rows, :] = (_gelu(gate) * val).astype(BF16)
                upc_ref[j, rows, :] = gate.astype(BF16)
                upc_ref[jv, rows, :] = val.astype(BF16)
                return xg[ROWS_F - HALO_F:], xv[ROWS_F - HALO_F:]

            last = lax.fori_loop(0, n_strips, strip, (hist_ref[j], hist_ref[jv]))
            hist_ref[j] = last[0]
            hist_ref[jv] = last[1]

    return pl.pallas_call(
        body, grid=(t // tm,),
        in_specs=[pl.BlockSpec((nb, tm, f), lambda i: (0, i, 0)), pl.BlockSpec((nb, CONV_F, f), lambda i: (0, 0, 0)),
                  pl.BlockSpec((nb, 1, f), lambda i: (0, 0, 0))],
        out_specs=[pl.BlockSpec((half, tm, f), lambda i: (0, i, 0)), pl.BlockSpec((nb, tm, f), lambda i: (0, i, 0))],
        out_shape=[jax.ShapeDtypeStruct((half, t, f), BF16), jax.ShapeDtypeStruct((nb, t, f), BF16)],
        scratch_shapes=[pltpu.VMEM((nb, HALO_F, f), F32)],
        name="ffn_mid_fwd", compiler_params=_params(1))(up0, conv_w, conv_b)


def _ffn_mid_bwd(up0, upc, dact, conv_w, seq, exch=()):
    nb, t, f = up0.shape
    half = nb // 2
    tm = min(TM_EW, seq)
    tiles_per_seq = seq // tm
    n_tiles = t // tm
    n_strips = tm // ROWS_F

    def body(up_ref, upc_ref, da_ref, w_ref, dup_ref, dw_ref, db_ref, carry_ref, dwacc_ref, dbacc_ref):
        i = pl.program_id(0)
        r = n_tiles - 1 - i

        @pl.when(i == 0)
        def _():
            dwacc_ref[...] = jnp.zeros_like(dwacc_ref)
            dbacc_ref[...] = jnp.zeros_like(dbacc_ref)

        @pl.when(r % tiles_per_seq == tiles_per_seq - 1)
        def _():
            carry_ref[...] = jnp.zeros_like(carry_ref)

        for j in range(half):
            jv = j + half
            wg = [w_ref[j, k:k + 1, :] for k in range(CONV_F)]
            wv = [w_ref[jv, k:k + 1, :] for k in range(CONV_F)]

            def strip(ci, carry):
                rows = pl.ds(pl.multiple_of((n_strips - 1 - ci) * ROWS_F, ROWS_F), ROWS_F)
                val = upc_ref[jv, rows, :].astype(F32)
                gg, dgg = _gelu_and_grad(upc_ref[j, rows, :].astype(F32))
                da = da_ref[j, rows, :].astype(F32)
                d_gate = da * val * dgg
                d_val = da * gg
                for blk, d, nxt, wk in ((j, d_gate, carry[0], wg), (jv, d_val, carry[1], wv)):
                    dbacc_ref[blk] += d
                    dwin = jnp.concatenate([d, nxt], axis=0)
                    shifted = [dwin[2:2 + ROWS_F], dwin[1:1 + ROWS_F], d]
                    x = up_ref[blk, rows, :].astype(F32)
                    for k in range(CONV_F):
                        dwacc_ref[blk, k] += shifted[k] * x
                    dx = wk[0] * shifted[0] + wk[1] * shifted[1] + wk[2] * shifted[2]
                    dup_ref[blk, rows, :] = dx.astype(BF16)
                return d_gate[0:HALO_F], d_val[0:HALO_F]

            carry = lax.fori_loop(0, n_strips, strip, (carry_ref[j], carry_ref[jv]))
            carry_ref[j] = carry[0]
            carry_ref[jv] = carry[1]

        @pl.when(i == n_tiles - 1)
        def _():
            for blk in range(nb):
                db_ref[blk] = _rowsum(dbacc_ref[blk])
                for k in range(CONV_F):
                    dw_ref[blk, k:k + 1, :] = _rowsum(dwacc_ref[blk, k])

    def rev(i):
        return n_tiles - 1 - i

    return _row_call(
        "ffn_mid_bwd", body, n_steps=n_tiles,
        in_specs=[pl.BlockSpec((nb, tm, f), lambda i: (0, rev(i), 0)), pl.BlockSpec((nb, tm, f), lambda i: (0, rev(i), 0)),
                  pl.BlockSpec((half, tm, f), lambda i: (0, rev(i), 0)),
                  pl.BlockSpec((nb, CONV_F, f), lambda i: (0, 0, 0))],
        out_specs=[pl.BlockSpec((nb, tm, f), lambda i: (0, rev(i), 0)), pl.BlockSpec((nb, CONV_F, f), lambda i: (0, 0, 0)),
                   pl.BlockSpec((nb, 1, f), lambda i: (0, 0, 0))],
        out_shape=[jax.ShapeDtypeStruct((nb, t, f), BF16), jax.ShapeDtypeStruct((nb, CONV_F, f), F32),
                   jax.ShapeDtypeStruct((nb, 1, f), F32)],
        scratch=[pltpu.VMEM((nb, HALO_F, f), F32), pltpu.VMEM((nb, CONV_F, ROWS_F, f), F32), pltpu.VMEM((nb, ROWS_F, f), F32)],
        args=(up0, upc, dact, conv_w), exch=exch)


def _all_gather(xs):
    n = len(xs)

    def body(*refs):
        x_refs, o_refs = refs[:n], refs[n:2 * n]
        send_sems, recv_sems, local_sems = refs[2 * n:]
        x, y, c = lax.axis_index("x"), lax.axis_index("y"), lax.axis_index("c")
        me, sibling = (x, y, c), (x, y, 1 - c)
        chips = [(1 - x, y), (x, 1 - y), (1 - x, 1 - y)]

        def slot(pos):
            return 4 * pos[0] + 2 * pos[1] + pos[2]

        def copy(a, k, block, to, src=None):
            dst = o_refs[a].at[slot(block)]
            return pltpu.make_async_remote_copy(
                src_ref=dst if src is None else src, dst_ref=dst, send_sem=send_sems.at[a * 7 + k],
                recv_sem=recv_sems.at[a * 7 + k], device_id=to, device_id_type=MESH)

        mine = [pltpu.make_async_copy(x_refs[a], o_refs[a].at[slot(me)], local_sems.at[a]) for a in range(n)]
        for cp in mine:
            cp.start()
        first = []
        for a in range(n):
            first.append(copy(a, 0, me, sibling, src=x_refs[a]))
            first += [copy(a, 1 + j, me, (*chip, c), src=x_refs[a]) for j, chip in enumerate(chips)]
        for cp in first:
            cp.start()
        passed = []
        for j, chip in enumerate(chips):
            for a in range(n):
                copy(a, 1 + j, (*chip, c), me).wait_recv()
                cp = copy(a, 4 + j, (*chip, c), sibling)
                cp.start()
                passed.append(cp)
        for a in range(n):
            copy(a, 0, sibling, me).wait_recv()
        for j, chip in enumerate(chips):
            for a in range(n):
                copy(a, 4 + j, (*chip, 1 - c), me).wait_recv()
        for cp in first + passed:
            cp.wait_send()
        for cp in mine:
            cp.wait()

    return pl.pallas_call(
        body, in_specs=[HBM_SPEC] * n, out_specs=[HBM_SPEC] * n,
        out_shape=[jax.ShapeDtypeStruct((N_DEV,) + v.shape, v.dtype) for v in xs],
        scratch_shapes=[pltpu.SemaphoreType.DMA((7 * n,)), pltpu.SemaphoreType.DMA((7 * n,)), pltpu.SemaphoreType.DMA((n,))],
        name="all_gather_weights")(*xs)


def _adamw_update(g, w, m, v):
    c1 = 1.0 - ADAM_B1 ** ADAM_STEP
    c2 = 1.0 - ADAM_B2 ** ADAM_STEP
    m_new = ADAM_B1 * m + (1.0 - ADAM_B1) * g
    v_new = ADAM_B2 * v + (1.0 - ADAM_B2) * (g * g)
    delta = -ADAM_LR * ((m_new / c1) / (jnp.sqrt(v_new / c2) + ADAM_EPS) + ADAM_WD * w)
    return m_new, v_new, delta


def _adamw(name, parts, w, m, v, rows_per_step):
    r, c = w.shape
    tr = r if r <= rows_per_step else (rows_per_step if r % rows_per_step == 0 else r // 2)

    def body(p_ref, w_ref, m_ref, v_ref, g_ref, d_ref, mo_ref, vo_ref):
        g = p_ref[0].astype(F32)
        for s in range(1, N_DEV):
            g = g + p_ref[s].astype(F32)
        m_new, v_new, delta = _adamw_update(g, w_ref[...], m_ref[...], v_ref[...])
        g_ref[...] = g
        mo_ref[...] = m_new
        vo_ref[...] = v_new
        d_ref[...] = delta

    row = pl.BlockSpec((tr, c), lambda i: (i, 0))
    out = jax.ShapeDtypeStruct((r, c), F32)
    return pl.pallas_call(body, grid=(r // tr,), in_specs=[pl.BlockSpec((N_DEV, tr, c), lambda i: (0, i, 0)), row, row, row],
                          out_specs=[row, row, row, row], out_shape=[out, out, out, out], name=name,
                          compiler_params=_params(1))(parts, w, m, v)


_VEC_NAMES = ("ln_v_g", "ln_v_b", "b_s", "conv_b_b", "ln_b_g", "ln_b_b", "g_ffn", "g_pg", "g_ple", "g_final")
_LOSS_ROW = len(_VEC_NAMES)
_REP_LAYOUT = dict({k: (i, 1, 1024) for i, k in enumerate(_VEC_NAMES)}, ffn_conv_b=(16, 8, 704), w_s=(24, 128, 1024))


def _pack_replicated_grads(d, loss_row):
    head = jnp.concatenate([d[k].reshape(1, 1024) for k in _VEC_NAMES] + [loss_row], axis=1).reshape(_LOSS_ROW + 1, 1024)
    return jnp.concatenate([jnp.pad(head, ((0, 16 - _LOSS_ROW - 1), (0, 0))),
                            jnp.pad(d["ffn_conv_b"].reshape(8, 704), ((0, 0), (0, 1024 - 704))),
                            d["w_s"].reshape(128, 1024)], axis=0)


def _rows8(vec):
    return jnp.pad(vec.reshape(1, 1024), ((0, 7), (0, 0)))


def _adamw_small(name, parts, items, loss_row=None):
    n = len(items)

    def body(p_ref, *refs):
        ins, outs = refs[:3 * n], refs[3 * n:]
        for a, (row, rows, cols) in enumerate(it[:3] for it in items):
            g = p_ref[0, row:row + rows, 0:cols]
            for s in range(1, N_DEV):
                g = g + p_ref[s, row:row + rows, 0:cols]
            m_new, v_new, delta = _adamw_update(g, ins[3 * a][...], ins[3 * a + 1][...], ins[3 * a + 2][...])
            for ref, val in zip(outs[4 * a:4 * a + 4], (g, delta, m_new, v_new)):
                ref[...] = val
        if loss_row is not None:
            total = p_ref[0, loss_row:loss_row + 1, 0:128]
            for s in range(1, N_DEV):
                total = total + p_ref[s, loss_row:loss_row + 1, 0:128]
            outs[-1][...] = total

    out_shape = [jax.ShapeDtypeStruct((rows, cols), F32) for _, rows, cols, *_ in items for _ in range(4)]
    if loss_row is not None:
        out_shape.append(jax.ShapeDtypeStruct((1, 128), F32))
    flat = pl.pallas_call(body, out_shape=out_shape, name=name,
                          compiler_params=pltpu.CompilerParams(vmem_limit_bytes=VMEM_LIMIT))(
        parts, *[arr for it in items for arr in it[3:]])
    res = [tuple(flat[4 * a:4 * a + 4]) for a in range(n)]
    return res + [flat[-1]] if loss_row is not None else res


def _pack_sharded_small(conv_b_w, ffn_conv_w):
    lead = conv_b_w.shape[:-2]
    pad0 = [(0, 0)] * len(lead)
    a = jnp.pad(conv_b_w, pad0 + [(0, 1), (0, 0)])
    b = jnp.pad(ffn_conv_w.reshape(lead + (CONV_F * 704,)), pad0 + [(0, 24 * 128 - CONV_F * 704)]).reshape(lead + (24, 128))
    return jnp.concatenate([a, b], axis=-2)


def _unpack_sharded_small(pk):
    lead = pk.shape[:-2]
    conv_b_w = pk[..., 0:CONV_B, :]
    ffn = pk[..., 32:56, :].reshape(lead + (24 * 128,))[..., :CONV_F * 704].reshape(lead + (CONV_F, 704))
    return conv_b_w, ffn


_WEIGHTS = ("g_mix", "w_in", "ln_v_g", "ln_v_b", "w_s", "b_s", "w_a_out", "conv_b_w", "conv_b_b", "ln_b_g", "ln_b_b",
            "w_b_out", "w_o", "g_ffn", "w_up", "ffn_conv_w", "ffn_conv_b", "w_down", "g_pg", "w_pg", "w_ple", "g_ple",
            "g_final")
_BIG = ("w_in", "w_a_out", "w_b_out", "w_o", "w_up", "w_down", "w_pg", "w_ple")


def kernel(x, p, g_mix, w_in, ln_v_g, ln_v_b, w_s, b_s, w_a_out, conv_b_w, conv_b_b, ln_b_g, ln_b_b, w_b_out, w_o, g_ffn, w_up, ffn_conv_w, ffn_conv_b, w_down, g_pg, w_pg, w_ple, g_ple, g_final, loss_target, m_g_mix, m_w_in, m_ln_v_g, m_ln_v_b, m_w_s, m_b_s, m_w_a_out, m_conv_b_w, m_conv_b_b, m_ln_b_g, m_ln_b_b, m_w_b_out, m_w_o, m_g_ffn, m_w_up, m_ffn_conv_w, m_ffn_conv_b, m_w_down, m_g_pg, m_w_pg, m_w_ple, m_g_ple, m_g_final, v_g_mix, v_w_in, v_ln_v_g, v_ln_v_b, v_w_s, v_b_s, v_w_a_out, v_conv_b_w, v_conv_b_b, v_ln_b_g, v_ln_b_b, v_w_b_out, v_w_o, v_g_ffn, v_w_up, v_ffn_conv_w, v_ffn_conv_b, v_w_down, v_g_pg, v_w_pg, v_w_ple, v_g_ple, v_g_final):
    local = dict(locals())
    wts = {k: local[k] for k in _WEIGHTS}
    mom = {k: local["m_" + k] for k in _WEIGHTS}
    var = {k: local["v_" + k] for k in _WEIGHTS}
    shapes = {k: wts[k].shape for k in _WEIGHTS}

    bsz, seq, d = x.shape
    t = bsz * seq
    x0 = x.reshape(t, d)
    p0 = p.reshape(t, p.shape[-1])
    target = loss_target.reshape(t, d)
    tm = min(TM_MM, t)
    tm_wide = min(TM_WIDE, t)
    tt = min(TT_MM, t)
    n_row = t // tm
    n_tok = t // tt

    def sq(a):
        return a.reshape(a.shape[1:])

    shard = {k: sq(wts[k]).astype(BF16) for k in _BIG}
    w_in3, small8 = _all_gather([shard["w_in"], _pack_sharded_small(sq(conv_b_w), sq(ffn_conv_w))])
    conv_b_w8, ffn_conv_w8 = _unpack_sharded_small(small8)
    conv_w_full = conv_b_w8.transpose(1, 0, 2).reshape(CONV_B, N_DEV * conv_b_w8.shape[-1])
    n_in = w_in3.shape[2]
    f_blk = shard["w_up"].shape[1]

    ws_m = jnp.where(jnp.tril(jnp.ones((CHUNK, CHUNK), bool))[None], sq(w_s), 0.0).astype(BF16)
    ws_mt = jnp.swapaxes(ws_m, 1, 2)
    bias_full = jnp.broadcast_to(sq(b_s).T[:, :, None], (CHUNK, GROUPS, CHUNK)).reshape(CHUNK, GROUPS * CHUNK)
    ffn_b8 = ffn_conv_b.reshape(N_DEV, 1, f_blk)

    h1 = _rms_fwd("rms_mix", x0, g_mix)
    z, (wa3, wb3, wo3) = _matmul(
        "mm_in", h1, w_in3, dims=_NN, grid=(N_DEV, t // tm_wide, 1),
        a_spec=pl.BlockSpec((tm_wide, d), lambda j, i, k: (i, 0)),
        b_spec=pl.BlockSpec((None, d, n_in), lambda j, i, k: (j, 0, 0)),
        o_spec=pl.BlockSpec((tm_wide, n_in), lambda j, i, k: (i, j)), acc_shape=(tm_wide, n_in),
        out_shape=jax.ShapeDtypeStruct((t, N_DEV * n_in), BF16),
        exch=[("gather", shard[k]) for k in ("w_a_out", "w_b_out", "w_o")])
    w_a = wa3.reshape(-1, d)
    w_b = wb3.reshape(-1, d)
    w_om = wo3.reshape(-1, d)
    (pa, cs, c_saved), (w_up3, wd3, wpg3, wple3) = _branch_fwd(
        z, ln_v_g, ln_v_b, ws_m, bias_full, conv_w_full, conv_b_b, ln_b_g, ln_b_b, seq,
        exch=[("gather", shard[k]) for k in ("w_up", "w_down", "w_pg", "w_ple")])
    w_pgm = wpg3.reshape(-1, d)
    w_d4 = wd3.reshape(N_DEV // 2, f_blk, d)
    w_plem = wple3.transpose(1, 0, 2).reshape(wple3.shape[1], d)
    ya = _mm_rows("mm_a_out", pa, w_a, dims=_NN, tm=tm, out_dtype=BF16)
    yb, merged = _mm_rows("mm_b_out", cs, w_b, dims=_NN, tm=tm, epi=_epi_merge_fwd(z, ya, tm))
    x1, h2 = _mm_rows("mm_o", merged, w_om, dims=_NN, tm=tm, epi=_epi_residual_rms(x0, g_ffn, tm))
    up0 = _matmul("mm_up", h2, w_up3, dims=_NN, grid=(N_DEV, t // tm_wide, 1),
                  a_spec=pl.BlockSpec((tm_wide, d), lambda j, i, k: (i, 0)),
                  b_spec=pl.BlockSpec((None, d, f_blk), lambda j, i, k: (j, 0, 0)),
                  o_spec=pl.BlockSpec((None, tm_wide, f_blk), lambda j, i, k: (j, i, 0)), acc_shape=(tm_wide, f_blk),
                  out_shape=jax.ShapeDtypeStruct((N_DEV, t, f_blk), BF16))
    act, upc = _ffn_mid_fwd(up0, ffn_conv_w8, ffn_b8, seq)
    x2, hq = _matmul("mm_down", act, w_d4, dims=_NN, grid=(n_row, 1, N_DEV // 2),
                     a_spec=pl.BlockSpec((None, tm, f_blk), lambda i, j, k: (k, i, 0)),
                     b_spec=pl.BlockSpec((None, f_blk, d), lambda i, j, k: (k, 0, 0)),
                     acc_shape=(tm, d), epi=_epi_residual_rms(x1, g_pg, tm))
    r = _mm_rows("mm_ple", p0, w_plem, dims=_NN, tm=tm, out_dtype=F32)

    dx3, dq, dr, loss_v, dg_final, dg_ple = _mm_rows(
        "mm_pg", hq, w_pgm, dims=_NN, tm=tm // 2, epi=_epi_head(x2, r, target, g_ple, g_final.reshape(1, d), tm // 2))

    recv = {}
    gw_pg = _mm_wgrad("wg_pg", hq, dq, out_dtype=BF16, tt=tt).reshape(wpg3.shape)
    dw_ple = _mm_wgrad("wg_ple", p0, dr, out_dtype=BF16, tt=tt)
    gw_ple = dw_ple.reshape(dw_ple.shape[0], N_DEV, -1).transpose(1, 0, 2)
    dx2, dx2b, dg_pg = _mm_rows("mm_pg_t", dq, w_pgm, dims=_NT, tm=tm, epi=_epi_rms_bwd(x2, g_pg, dx3, tm, want_bf16=True))

    dact, (recv["w_pg"], recv["w_ple"]) = _matmul(
        "mm_down_t", dx2b, w_d4, dims=_NT, grid=(N_DEV // 2, n_row, 1),
        a_spec=pl.BlockSpec((tm, d), lambda j, i, k: (i, 0)),
        b_spec=pl.BlockSpec((None, f_blk, d), lambda j, i, k: (j, 0, 0)),
        o_spec=pl.BlockSpec((None, tm, f_blk), lambda j, i, k: (j, i, 0)), acc_shape=(tm, f_blk),
        out_shape=jax.ShapeDtypeStruct((N_DEV // 2, t, f_blk), BF16),
        exch=[("scatter", gw_pg), ("scatter", gw_ple)])
    gw_down = _matmul("wg_down", act, dx2b, dims=_TN, grid=(N_DEV // 2, 1, n_tok),
                      a_spec=pl.BlockSpec((None, tt, f_blk), lambda j, i, k: (j, k, 0)),
                      b_spec=pl.BlockSpec((tt, d), lambda j, i, k: (k, 0)),
                      o_spec=pl.BlockSpec((None, f_blk, d), lambda j, i, k: (j, 0, 0)), acc_shape=(f_blk, d),
                      out_shape=jax.ShapeDtypeStruct((N_DEV // 2, f_blk, d), BF16)).reshape(wd3.shape)
    (d_up0, dffn_w8, dffn_b8), (recv["w_down"],) = _ffn_mid_bwd(up0, upc, dact, ffn_conv_w8, seq,
                                                                exch=[("scatter", gw_down)])
    gw_up = _matmul("wg_up", h2, d_up0, dims=_TN, grid=(N_DEV, 1, n_tok),
                    a_spec=pl.BlockSpec((tt, d), lambda j, i, k: (k, 0)),
                    b_spec=pl.BlockSpec((None, tt, f_blk), lambda j, i, k: (j, k, 0)),
                    o_spec=pl.BlockSpec((None, d, f_blk), lambda j, i, k: (j, 0, 0)), acc_shape=(d, f_blk),
                    out_shape=jax.ShapeDtypeStruct((N_DEV, d, f_blk), BF16))
    (dx1, dx1b, dg_ffn), (recv["w_up"],) = _matmul(
        "mm_up_t", d_up0, w_up3, dims=_NT, grid=(n_row, 1, N_DEV),
        a_spec=pl.BlockSpec((None, tm, f_blk), lambda i, j, k: (k, i, 0)),
        b_spec=pl.BlockSpec((None, d, f_blk), lambda i, j, k: (k, 0, 0)),
        acc_shape=(tm, d), epi=_epi_rms_bwd(x1, g_ffn, dx2, tm, want_bf16=True), exch=[("scatter", gw_up)])

    dya, dyb, dgates = _mm_rows("mm_o_t", dx1b, w_om, dims=_NT, tm=tm, epi=_epi_merge_bwd(z, ya, yb, tm))
    gw_o = _mm_wgrad("wg_o", merged, dx1b, out_dtype=BF16, tt=tt).reshape(wo3.shape)
    dpa = _mm_rows("mm_a_out_t", dya, w_a, dims=_NT, tm=tm, out_dtype=BF16)
    dcs = _mm_rows("mm_b_out_t", dyb, w_b, dims=_NT, tm=tm, out_dtype=BF16)
    gw_a = _mm_wgrad("wg_a_out", pa, dya, out_dtype=BF16, tt=tt).reshape(wa3.shape)
    gw_b = _mm_wgrad("wg_b_out", cs, dyb, out_dtype=BF16, tt=tt).reshape(wb3.shape)
    (dz, dlvg, dlvb, dws, dbs_full, dconv_w, dconv_b, dlbg, dlbb), (recv["w_o"], recv["w_a_out"], recv["w_b_out"]) = _branch_bwd(
        z, c_saved, dpa, dcs, dgates, ln_v_g, ln_v_b, ws_m, ws_mt, bias_full, conv_w_full, ln_b_g, ln_b_b, seq,
        exch=[("scatter", gw_o), ("scatter", gw_a), ("scatter", gw_b)])
    db_s = dbs_full[:, :GROUPS].T
    rep_partial = _pack_replicated_grads(
        dict(ln_v_g=dlvg, ln_v_b=dlvb, b_s=db_s, conv_b_b=dconv_b, ln_b_g=dlbg, ln_b_b=dlbb, g_ffn=dg_ffn,
             g_pg=dg_pg, g_ple=dg_ple, g_final=dg_final, ffn_conv_b=dffn_b8, w_s=dws), loss_v)
    dconv_w8 = dconv_w.reshape(CONV_B, N_DEV, -1).transpose(1, 0, 2)
    small_partial = _pack_sharded_small(dconv_w8, dffn_w8)
    gw_in, (recv_small, recv_rep) = _matmul(
        "wg_in", h1, dz, dims=_TN, grid=(N_DEV, 1, n_tok),
        a_spec=pl.BlockSpec((tt, d), lambda j, i, k: (k, 0)),
        b_spec=pl.BlockSpec((tt, n_in), lambda j, i, k: (k, j)),
        o_spec=pl.BlockSpec((None, d, n_in), lambda j, i, k: (j, 0, 0)), acc_shape=(d, n_in),
        out_shape=jax.ShapeDtypeStruct((N_DEV, d, n_in), BF16),
        exch=[("scatter", small_partial), ("gather", rep_partial)])
    (grad_x, dg_mix), (recv["w_in"],) = _matmul(
        "mm_in_t", dz, w_in3, dims=_NT, grid=(n_row, 1, N_DEV),
        a_spec=pl.BlockSpec((tm, n_in), lambda i, j, k: (i, k)),
        b_spec=pl.BlockSpec((None, d, n_in), lambda i, j, k: (k, 0, 0)),
        acc_shape=(tm, d), epi=_epi_rms_bwd(x0, g_mix, dx1, tm, want_bf16=False), exch=[("scatter", gw_in)])
    (recv_g_mix,) = _exchange("exchange_g_mix", [("gather", _rows8(dg_mix))])

    grads, deltas, new_m, new_v = {}, {}, {}, {}
    by_kind = (grads, deltas, new_m, new_v)

    def two_d(a):
        a = sq(a)
        return a.reshape(-1, a.shape[-1])

    for k in _BIG:
        parts = recv[k].reshape(N_DEV, -1, recv[k].shape[-1])
        outs = _adamw("adamw_" + k, parts, two_d(wts[k]), two_d(mom[k]), two_d(var[k]), 128)
        for tgt, o in zip(by_kind, outs):
            tgt[k] = o.reshape(shapes[k])

    small = [_pack_sharded_small(sq(s["conv_b_w"]), sq(s["ffn_conv_w"])) for s in (wts, mom, var)]
    for tgt, o in zip(by_kind, _adamw("adamw_conv", recv_small, small[0], small[1], small[2], 56)):
        cw, fw = _unpack_sharded_small(o)
        tgt["conv_b_w"] = cw.reshape(shapes["conv_b_w"])
        tgt["ffn_conv_w"] = fw.reshape(shapes["ffn_conv_w"])

    names = list(_REP_LAYOUT)
    items = [_REP_LAYOUT[k] + tuple(s[k].reshape(_REP_LAYOUT[k][1:]) for s in (wts, mom, var)) for k in names]
    *rep_outs, loss_sum = _adamw_small("adamw_replicated", recv_rep, items, loss_row=_LOSS_ROW)
    for k, outs in zip(names, rep_outs):
        for tgt, o in zip(by_kind, outs):
            tgt[k] = o.reshape(shapes[k])
    loss = loss_sum[0, 0]

    g_mix_item = (0, 1, 1024) + tuple(s["g_mix"].reshape(1, 1024) for s in (wts, mom, var))
    for tgt, o in zip(by_kind, _adamw_small("adamw_g_mix", recv_g_mix, [g_mix_item])[0]):
        tgt["g_mix"] = o.reshape(shapes["g_mix"])

    return (loss, grad_x.reshape(x.shape), *[grads[k] for k in _WEIGHTS], *[deltas[k] for k in _WEIGHTS],
            *[new_m[k] for k in _WEIGHTS], *[new_v[k] for k in _WEIGHTS])
```

```python
import math

import jax
import jax.numpy as jnp
from jax import lax
from jax.experimental import pallas as pl
from jax.experimental.pallas import tpu as pltpu

F32 = jnp.float32
BF16 = jnp.bfloat16

N_DEV = 8
EPS_RMS = 1e-6
EPS_LN = 1e-5
CHUNK = 128
GROUPS = 8
CONV_B = 31
CONV_F = 3
HALO_B = 32
HALO_F = 8
ROWS_F = 16
SUB = 8

ADAM_LR = 0.001
ADAM_B1 = 0.9
ADAM_B2 = 0.999
ADAM_EPS = 1e-08
ADAM_WD = 0.01
ADAM_STEP = 10

VMEM_LIMIT = 56 * 1024 * 1024
TM_MM = 1024
TM_WIDE = 2048
TT_MM = 2048
TM_EW = 256
EPI_STRIP = 256

_NN = (((1,), (0,)), ((), ()))
_NT = (((1,), (1,)), ((), ()))
_TN = (((0,), (0,)), ((), ()))
MESH = pl.DeviceIdType.MESH
HBM_SPEC = pl.BlockSpec(memory_space=pltpu.HBM)


def _params(n_axes):
    return pltpu.CompilerParams(dimension_semantics=("arbitrary",) * n_axes, vmem_limit_bytes=VMEM_LIMIT)


def _gelu(x):
    k = math.sqrt(2.0 / math.pi)
    return 0.5 * x * (1.0 + jnp.tanh(k * (x + 0.044715 * (x * x * x))))


def _gelu_and_grad(x):
    k = math.sqrt(2.0 / math.pi)
    x2 = x * x
    t = jnp.tanh(k * (x + 0.044715 * (x2 * x)))
    g = 0.5 * x * (1.0 + t)
    dg = 0.5 * (1.0 + t) + 0.5 * x * (1.0 - t * t) * (k * (1.0 + 3.0 * 0.044715 * x2))
    return g, dg


def _sigmoid(x):
    return 1.0 / (1.0 + jnp.exp(-x))


def _rowsum(x):
    return jnp.sum(x, axis=0, keepdims=True)


def _mean(x):
    return jnp.mean(x, axis=-1, keepdims=True)


def _exchange_io(exch):
    n = len(exch)
    out_shape = [jax.ShapeDtypeStruct(v.shape if kind == "scatter" else (N_DEV,) + v.shape, v.dtype) for kind, v in exch]
    scratch = [pltpu.SemaphoreType.DMA((7 * n,)), pltpu.SemaphoreType.DMA((7 * n,)), pltpu.SemaphoreType.DMA((n,))] if n else []
    return [HBM_SPEC] * n, [HBM_SPEC] * n, out_shape, scratch


def _exchange_step(kinds, x_refs, o_refs, send_sems, recv_sems, local_sems):
    n = len(kinds)
    x, y, c = lax.axis_index("x"), lax.axis_index("y"), lax.axis_index("c")
    me = 4 * x + 2 * y + c

    def src(a, to_slot):
        return x_refs[a].at[to_slot] if kinds[a] == "scatter" else x_refs[a]

    mine = [pltpu.make_async_copy(src(a, me), o_refs[a].at[me], local_sems.at[a]) for a in range(n)]
    sends, recvs = [], []
    for m in range(1, N_DEV):
        mx, my, mc = (m >> 2) & 1, (m >> 1) & 1, m & 1
        px, py, pc = (1 - x if mx else x), (1 - y if my else y), (1 - c if mc else c)
        peer = 4 * px + 2 * py + pc
        for a in range(n):
            k = a * 7 + m - 1
            sends.append(pltpu.make_async_remote_copy(
                src_ref=src(a, peer), dst_ref=o_refs[a].at[me], send_sem=send_sems.at[k], recv_sem=recv_sems.at[k],
                device_id=(px, py, pc), device_id_type=MESH))
            recvs.append(pltpu.make_async_remote_copy(
                src_ref=src(a, peer), dst_ref=o_refs[a].at[peer], send_sem=send_sems.at[k], recv_sem=recv_sems.at[k],
                device_id=(px, py, pc), device_id_type=MESH))

    def start():
        for cp in mine + sends:
            cp.start()

    def finish():
        for cp in recvs:
            cp.wait_recv()
        for cp in sends:
            cp.wait_send()
        for cp in mine:
            cp.wait()

    return start, finish


def _exchange(name, exch):
    n = len(exch)
    kinds = [k for k, _ in exch]
    in_specs, out_specs, out_shape, scratch = _exchange_io(exch)

    def body(*refs):
        start, finish = _exchange_step(kinds, refs[:n], refs[n:2 * n], *refs[2 * n:])
        start()
        finish()

    return pl.pallas_call(body, in_specs=in_specs, out_specs=out_specs, out_shape=out_shape, scratch_shapes=scratch,
                          name=name)(*[v for _, v in exch])


class _Epilogue:
    def __init__(self, fn, ins=(), in_specs=(), out_specs=(), out_shape=(), strip=None):
        self.fn, self.ins, self.in_specs = fn, list(ins), list(in_specs)
        self.out_specs, self.out_shape = list(out_specs), list(out_shape)
        self.strip = strip


def _matmul(name, a, b, *, dims, grid, a_spec, b_spec, acc_shape, o_spec=None, out_shape=None, epi=None, exch=()):
    nk = grid[2]
    plain = epi is None
    if plain:
        def store(acc, ins, outs, i):
            outs[0][...] = acc.astype(outs[0].dtype)
        epi = _Epilogue(store, out_specs=[o_spec], out_shape=[out_shape])
    n_in = 2 + len(epi.ins)
    n_out = len(epi.out_specs)
    n_ex = len(exch)
    kinds = [k for k, _ in exch]
    ex_in, ex_out, ex_shape, ex_scratch = _exchange_io(exch)

    def body(*refs):
        a_ref, b_ref = refs[:2]
        step0 = pl.program_id(0)
        epi_ins, rest = refs[2:n_in], refs[n_in:]
        x_refs, rest = rest[:n_ex], rest[n_ex:]
        outs, rest = rest[:n_out], rest[n_out:]
        o_refs, scr = rest[:n_ex], rest[n_ex:]
        if n_ex:
            pid = [pl.program_id(ax) for ax in range(3)]
            ex_start, ex_finish = _exchange_step(kinds, x_refs, o_refs, *scr[len(scr) - 3:])
            pl.when((pid[0] == 0) & (pid[1] == 0) & (pid[2] == 0))(ex_start)
        part = lax.dot_general(a_ref[...].astype(BF16), b_ref[...].astype(BF16), dims, preferred_element_type=F32)

        def run_epilogue(rows_of_acc):
            rows = acc_shape[0]
            strip = rows if epi.strip is None else min(epi.strip, rows)
            for s in range(0, rows, strip):
                def view(ref):
                    return ref.at[pl.ds(s, strip)] if ref.shape[0] == rows else ref
                first = (step0 == 0) if s == 0 else False
                epi.fn(rows_of_acc(s, strip), [view(r) for r in epi_ins], [view(r) for r in outs], first)

        if nk == 1:
            run_epilogue(lambda s, n: part[s:s + n])
        else:
            acc_ref = scr[0]
            k = pl.program_id(2)

            @pl.when(k == 0)
            def _():
                acc_ref[...] = part

            @pl.when(k > 0)
            def _():
                acc_ref[...] += part

            @pl.when(k == nk - 1)
            def _():
                run_epilogue(lambda s, n: acc_ref[pl.ds(s, n), :])
        if n_ex:
            pl.when((pid[0] == grid[0] - 1) & (pid[1] == grid[1] - 1) & (pid[2] == grid[2] - 1))(ex_finish)

    scratch = ([pltpu.VMEM(acc_shape, F32)] if nk > 1 else []) + ex_scratch
    res = pl.pallas_call(body, grid=grid, in_specs=[a_spec, b_spec] + epi.in_specs + ex_in,
                         out_specs=epi.out_specs + ex_out, out_shape=epi.out_shape + ex_shape, scratch_shapes=scratch,
                         name=name, compiler_params=_params(3))(a, b, *epi.ins, *[v for _, v in exch])
    main = res[0] if plain else res[:n_out]
    return (main, res[n_out:]) if n_ex else main


def _mm_rows(name, a, w, *, dims, tm, out_dtype=None, epi=None):
    t, k = a.shape
    n = w.shape[1] if dims == _NN else w.shape[0]
    tm = min(tm, t)
    return _matmul(name, a, w, dims=dims, grid=(t // tm, 1, 1),
                   a_spec=pl.BlockSpec((tm, k), lambda i, j, kk: (i, 0)),
                   b_spec=pl.BlockSpec(w.shape, lambda i, j, kk: (0, 0)),
                   o_spec=pl.BlockSpec((tm, n), lambda i, j, kk: (i, 0)), acc_shape=(tm, n),
                   out_shape=jax.ShapeDtypeStruct((t, n), out_dtype) if epi is None else None, epi=epi)


def _mm_wgrad(name, a, b, *, out_dtype, tt):
    t, m = a.shape
    n = b.shape[1]
    tt = min(tt, t)
    return _matmul(name, a, b, dims=_TN, grid=(1, 1, t // tt),
                   a_spec=pl.BlockSpec((tt, m), lambda i, j, kk: (kk, 0)),
                   b_spec=pl.BlockSpec((tt, n), lambda i, j, kk: (kk, 0)),
                   o_spec=pl.BlockSpec((m, n), lambda i, j, kk: (0, 0)),
                   acc_shape=(m, n), out_shape=jax.ShapeDtypeStruct((m, n), out_dtype))


def _row3(tm, d, col=0):
    return pl.BlockSpec((tm, d), lambda i, j, k: (i, col))


def _vec3(d):
    return pl.BlockSpec((1, d), lambda i, j, k: (0, 0))


def _accumulate_over_rows(ref, part, first):
    if first is False:
        ref[...] += part
        return

    @pl.when(first)
    def _():
        ref[...] = part + jnp.zeros_like(ref)

    @pl.when(jnp.logical_not(first))
    def _():
        ref[...] += part


def _epi_residual_rms(res, g, tm):
    t, d = res.shape

    def fn(acc, ins, outs, i):
        res_ref, g_ref = ins
        xv = acc + res_ref[...]
        outs[0][...] = xv
        rstd = lax.rsqrt(_mean(xv * xv) + EPS_RMS)
        outs[1][...] = ((xv * rstd) * g_ref[...]).astype(BF16)

    return _Epilogue(fn, strip=EPI_STRIP, ins=[res, g], in_specs=[_row3(tm, d), _vec3(d)], out_specs=[_row3(tm, d), _row3(tm, d)],
                     out_shape=[jax.ShapeDtypeStruct((t, d), F32), jax.ShapeDtypeStruct((t, d), BF16)])


def _epi_rms_bwd(x, g, dres, tm, *, want_bf16):
    t, d = x.shape

    def fn(acc, ins, outs, i):
        x_ref, g_ref, dres_ref = ins
        xv = x_ref[...]
        rstd = lax.rsqrt(_mean(xv * xv) + EPS_RMS)
        nrm = xv * rstd
        dn = acc * g_ref[...]
        dx = dres_ref[...] + rstd * (dn - nrm * _mean(dn * nrm))
        outs[0][...] = dx
        if want_bf16:
            outs[1][...] = dx.astype(BF16)
        _accumulate_over_rows(outs[-1], _rowsum(acc * nrm), i)

    row = _row3(tm, d)
    n_dx = 2 if want_bf16 else 1
    return _Epilogue(fn, strip=EPI_STRIP, ins=[x, g, dres], in_specs=[row, _vec3(d), row], out_specs=[row] * n_dx + [_vec3(d)],
                     out_shape=[jax.ShapeDtypeStruct((t, d), F32)] + [jax.ShapeDtypeStruct((t, d), BF16)] * (n_dx - 1)
                     + [jax.ShapeDtypeStruct((1, d), F32)])


def _epi_merge_fwd(z, ya, tm):
    t, w = ya.shape

    def fn(acc, ins, outs, i):
        ga_ref, gb_ref, ya_ref = ins
        outs[0][...] = acc.astype(BF16)
        sa = _sigmoid(ga_ref[...].astype(F32))
        sb = _sigmoid(gb_ref[...].astype(F32))
        outs[1][...] = (sa * ya_ref[...].astype(F32) + sb * acc).astype(BF16)

    row = _row3(tm, w)
    return _Epilogue(fn, strip=EPI_STRIP, ins=[z, z, ya], in_specs=[_row3(tm, w, 4), _row3(tm, w, 5), row], out_specs=[row, row],
                     out_shape=[jax.ShapeDtypeStruct((t, w), BF16)] * 2)


def _epi_merge_bwd(z, ya, yb, tm):
    t, w = ya.shape

    def fn(acc, ins, outs, i):
        ga_ref, gb_ref, ya_ref, yb_ref = ins
        sa = _sigmoid(ga_ref[...].astype(F32))
        sb = _sigmoid(gb_ref[...].astype(F32))
        outs[0][...] = (acc * sa).astype(BF16)
        outs[1][...] = (acc * sb).astype(BF16)
        outs[2][:, 0:w] = (acc * ya_ref[...].astype(F32) * sa * (1.0 - sa)).astype(BF16)
        outs[2][:, w:2 * w] = (acc * yb_ref[...].astype(F32) * sb * (1.0 - sb)).astype(BF16)

    row = _row3(tm, w)
    return _Epilogue(fn, strip=EPI_STRIP, ins=[z, z, ya, yb], in_specs=[_row3(tm, w, 4), _row3(tm, w, 5), row, row],
                     out_specs=[row, row, _row3(tm, 2 * w)],
                     out_shape=[jax.ShapeDtypeStruct((t, w), BF16)] * 2 + [jax.ShapeDtypeStruct((t, 2 * w), BF16)])


def _epi_head(x2, r, target, g_ple, g_final, tm):
    t, d = x2.shape

    def fn(acc, ins, outs, i):
        x2_ref, r_ref, tg_ref, gple_ref, gfin_ref = ins
        dx3_ref, dq_ref, dr_ref, loss_ref, dgfin_ref, dgple_ref = outs
        pg = _sigmoid(acc)
        rv = r_ref[...]
        rstd_r = lax.rsqrt(_mean(rv * rv) + EPS_RMS)
        nr = rv * rstd_r
        pe = nr * gple_ref[...]
        x3 = x2_ref[...] + pe * pg
        rstd3 = lax.rsqrt(_mean(x3 * x3) + EPS_RMS)
        n3 = x3 * rstd3
        err = n3 * gfin_ref[...] - tg_ref[...]
        loss_part = jnp.sum(_rowsum(err * err), axis=1, keepdims=True) * (0.5 / d)
        dy = err * (1.0 / d)
        dn3 = dy * gfin_ref[...]
        dx3 = rstd3 * (dn3 - n3 * _mean(dn3 * n3))
        dx3_ref[...] = dx3
        dq_ref[...] = (dx3 * pe * pg * (1.0 - pg)).astype(BF16)
        dpe = dx3 * pg
        dnr = dpe * gple_ref[...]
        dr_ref[...] = (rstd_r * (dnr - nr * _mean(dnr * nr))).astype(BF16)
        _accumulate_over_rows(loss_ref, loss_part, i)
        _accumulate_over_rows(dgfin_ref, _rowsum(dy * n3), i)
        _accumulate_over_rows(dgple_ref, _rowsum(dpe * nr), i)

    row = _row3(tm, d)
    vec = jax.ShapeDtypeStruct((1, d), F32)
    return _Epilogue(fn, strip=EPI_STRIP, ins=[x2, r, target, g_ple, g_final], in_specs=[row, row, row, _vec3(d), _vec3(d)],
                     out_specs=[row, row, row, _vec3(d), _vec3(d), _vec3(d)],
                     out_shape=[jax.ShapeDtypeStruct((t, d), F32), jax.ShapeDtypeStruct((t, d), BF16),
                                jax.ShapeDtypeStruct((t, d), BF16), vec, vec, vec])


def _vec_spec(d):
    return pl.BlockSpec((1, d), lambda i: (0, 0))


def _row_call(name, body, *, n_steps, in_specs, out_specs, out_shape, scratch, args, exch=()):
    n_in, n_out, n_scr, n_ex = len(in_specs), len(out_specs), len(scratch), len(exch)
    kinds = [k for k, _ in exch]
    ex_in, ex_out, ex_shape, ex_scratch = _exchange_io(exch)

    def wrapped(*refs):
        ins, rest = refs[:n_in], refs[n_in:]
        x_refs, rest = rest[:n_ex], rest[n_ex:]
        outs, rest = rest[:n_out], rest[n_out:]
        o_refs, rest = rest[:n_ex], rest[n_ex:]
        scr, sems = rest[:n_scr], rest[n_scr:]
        if n_ex:
            ex_start, ex_finish = _exchange_step(kinds, x_refs, o_refs, *sems)
            pl.when(pl.program_id(0) == 0)(ex_start)
        body(*ins, *outs, *scr)
        if n_ex:
            pl.when(pl.program_id(0) == n_steps - 1)(ex_finish)

    res = pl.pallas_call(wrapped, grid=(n_steps,), in_specs=list(in_specs) + ex_in, out_specs=list(out_specs) + ex_out,
                         out_shape=list(out_shape) + ex_shape, scratch_shapes=list(scratch) + ex_scratch, name=name,
                         compiler_params=_params(1))(*args, *[v for _, v in exch])
    return res[:n_out], res[n_out:]


def _rms_fwd(name, x, g):
    t, d = x.shape
    tm = min(TM_EW * 2, t)

    def body(x_ref, g_ref, h_ref):
        xv = x_ref[...]
        rstd = lax.rsqrt(_mean(xv * xv) + EPS_RMS)
        h_ref[...] = ((xv * rstd) * g_ref[...]).astype(BF16)

    row = pl.BlockSpec((tm, d), lambda i: (i, 0))
    return pl.pallas_call(body, grid=(t // tm,), in_specs=[row, _vec_spec(d)], out_specs=row,
                          out_shape=jax.ShapeDtypeStruct((t, d), BF16), name=name, compiler_params=_params(1))(x, g)


def _fill_shifted(buf_ref, sh_ref):
    n = sh_ref.shape[1]
    for p in range(1, SUB):
        sh_ref[p - 1] = buf_ref[p:p + n, :]


def _branch_fwd(z, ln_v_g, ln_v_b, ws_m, bias_full, conv_w, conv_b, ln_b_g, ln_b_b, seq, exch=()):
    t = z.shape[0]
    w = 1024
    tm = min(TM_EW, seq)
    tiles_per_seq = seq // tm
    n_chunks = tm // CHUNK

    def body(z_ref, lvg_ref, lvb_ref, ws_ref, bias_ref, cw_ref, cb_ref, lbg_ref, lbb_ref,
             pa_ref, cs_ref, c_ref, hist_ref, buf_ref, mix_ref, sh_ref, wb_ref):
        i = pl.program_id(0)
        u = z_ref[:, 0:w].astype(F32)
        v = z_ref[:, w:2 * w].astype(F32)
        ug = _gelu(u)
        vg = _gelu(v)
        dv = vg - _mean(vg)
        vhat = dv * lax.rsqrt(_mean(dv * dv) + EPS_LN)
        vn = (vhat * lvg_ref[...] + lvb_ref[...]).astype(BF16)
        for ci in range(n_chunks):
            rows = slice(ci * CHUNK, (ci + 1) * CHUNK)
            for g in range(GROUPS):
                cols = slice(g * CHUNK, (g + 1) * CHUNK)
                mix_ref[rows, cols] = lax.dot_general(ws_ref[g], vn[rows, cols], _NN, preferred_element_type=F32)
            mix_ref[rows, :] += bias_ref[...]
        pa_ref[...] = (ug * mix_ref[...]).astype(BF16)

        a = z_ref[:, 2 * w:3 * w].astype(F32)
        gl = z_ref[:, 3 * w:4 * w].astype(F32)
        glu = a * _sigmoid(gl)

        @pl.when(i % tiles_per_seq == 0)
        def _():
            hist_ref[...] = jnp.zeros_like(hist_ref)

        buf_ref[0:HALO_B, :] = hist_ref[...]
        buf_ref[HALO_B:, :] = glu
        hist_ref[...] = glu[tm - HALO_B:, :]
        _fill_shifted(buf_ref, sh_ref)

        @pl.when(i == 0)
        def _():
            for k in range(CONV_B):
                wb_ref[k] = jnp.broadcast_to(cw_ref[k:k + 1, :], (SUB, w))

        groups = 4

        def strip(si, _):
            s = pl.multiple_of(si * (groups * SUB), groups * SUB)
            acc = [jnp.zeros((SUB, w), F32) + cb_ref[...] for _ in range(groups)]
            for k in range(CONV_B):
                whole, part = divmod(HALO_B - (CONV_B - 1) + k, SUB)
                wk = wb_ref[k]
                for g in range(groups):
                    at = pl.ds(s + SUB * (whole + g), SUB)
                    acc[g] = acc[g] + wk * (buf_ref[at, :] if part == 0 else sh_ref[part - 1, at, :])
            for g in range(0, groups, 2):
                at = pl.ds(s + SUB * g, 2 * SUB)
                c = jnp.concatenate(acc[g:g + 2], axis=0)
                c_ref[at, :] = c
                dc = c - _mean(c)
                chat = dc * lax.rsqrt(_mean(dc * dc) + EPS_LN)
                cn = chat * lbg_ref[...] + lbb_ref[...]
                cs_ref[at, :] = (cn * _sigmoid(cn)).astype(BF16)
            return 0

        lax.fori_loop(0, tm // (groups * SUB), strip, 0)

    row = pl.BlockSpec((tm, w), lambda i: (i, 0))
    in_specs = [pl.BlockSpec((tm, 4 * w), lambda i: (i, 0)), _vec_spec(w), _vec_spec(w),
                pl.BlockSpec((GROUPS, CHUNK, CHUNK), lambda i: (0, 0, 0)), pl.BlockSpec((CHUNK, w), lambda i: (0, 0)),
                pl.BlockSpec((CONV_B, w), lambda i: (0, 0)), _vec_spec(w), _vec_spec(w), _vec_spec(w)]
    return _row_call(
        "branch_fwd", body, n_steps=t // tm, in_specs=in_specs, out_specs=[row, row, row],
        out_shape=[jax.ShapeDtypeStruct((t, w), BF16), jax.ShapeDtypeStruct((t, w), BF16), jax.ShapeDtypeStruct((t, w), F32)],
        scratch=[pltpu.VMEM((HALO_B, w), F32), pltpu.VMEM((HALO_B + tm, w), F32), pltpu.VMEM((tm, w), F32),
                 pltpu.VMEM((SUB - 1, HALO_B + tm - SUB, w), F32), pltpu.VMEM((CONV_B, SUB, w), F32)],
        args=(z, ln_v_g, ln_v_b, ws_m, bias_full, conv_w, conv_b, ln_b_g, ln_b_b), exch=exch)


def _branch_bwd(z, c_saved, dpa, dcs, dgates, ln_v_g, ln_v_b, ws_m, ws_mt, bias_full, conv_w, ln_b_g, ln_b_b, seq, exch=()):
    t = z.shape[0]
    w = 1024
    tm = min(TM_EW, seq)
    tiles_per_seq = seq // tm
    n_tiles = t // tm
    n_chunks = tm // CHUNK

    def body(z_ref, c_ref, dpa_ref, dcs_ref, dgt_ref, lvg_ref, lvb_ref, ws_ref, wst_ref, bias_ref, cw_ref,
             lbg_ref, lbb_ref,
             dz_ref, dlvg_ref, dlvb_ref, dws_ref, dbs_ref, dcw_ref, dcb_ref, dlbg_ref, dlbb_ref,
             carry_ref, glu_ref, dbuf_ref, mix_ref, dvn_ref, dbs_acc_ref, dglu_ref, sh_ref, wb_ref, dwacc_ref):
        i = pl.program_id(0)
        r = n_tiles - 1 - i

        @pl.when(i == 0)
        def _():
            for ref in (dlvg_ref, dlvb_ref, dws_ref, dbs_acc_ref, dwacc_ref, dcb_ref, dlbg_ref, dlbb_ref):
                ref[...] = jnp.zeros_like(ref)

        u = z_ref[:, 0:w].astype(F32)
        v = z_ref[:, w:2 * w].astype(F32)
        ug, dug = _gelu_and_grad(u)
        vg, dvg = _gelu_and_grad(v)
        dv0 = vg - _mean(vg)
        rstd_v = lax.rsqrt(_mean(dv0 * dv0) + EPS_LN)
        vhat = dv0 * rstd_v
        vn = (vhat * lvg_ref[...] + lvb_ref[...]).astype(BF16)
        dpa = dpa_ref[...].astype(F32)
        dmix = dpa * ug
        dmix_b = dmix.astype(BF16)
        for ci in range(n_chunks):
            rows = slice(ci * CHUNK, (ci + 1) * CHUNK)
            for g in range(GROUPS):
                cols = slice(g * CHUNK, (g + 1) * CHUNK)
                mix_ref[rows, cols] = lax.dot_general(ws_ref[g], vn[rows, cols], _NN, preferred_element_type=F32)
                dvn_ref[rows, cols] = lax.dot_general(wst_ref[g], dmix_b[rows, cols], _NN, preferred_element_type=F32)
                dws_ref[g] += lax.dot_general(dmix_b[rows, cols], vn[rows, cols], _NT, preferred_element_type=F32)
            mix_ref[rows, :] += bias_ref[...]
            dbs_acc_ref[...] += dmix[rows, :]
        dz_ref[:, 0:w] = (dpa * mix_ref[...] * dug).astype(BF16)
        dvn = dvn_ref[...]
        dlvg_ref[...] += _rowsum(dvn * vhat)
        dlvb_ref[...] += _rowsum(dvn)
        dvh = dvn * lvg_ref[...]
        dvg_in = rstd_v * (dvh - _mean(dvh) - vhat * _mean(dvh * vhat))
        dz_ref[:, w:2 * w] = (dvg_in * dvg).astype(BF16)

        c = c_ref[...]
        dc0 = c - _mean(c)
        rstd_c = lax.rsqrt(_mean(dc0 * dc0) + EPS_LN)
        chat = dc0 * rstd_c
        cn = chat * lbg_ref[...] + lbb_ref[...]
        sg = _sigmoid(cn)
        dcn = dcs_ref[...].astype(F32) * (sg * (1.0 + cn * (1.0 - sg)))
        dlbg_ref[...] += _rowsum(dcn * chat)
        dlbb_ref[...] += _rowsum(dcn)
        dch = dcn * lbg_ref[...]
        dc = rstd_c * (dch - _mean(dch) - chat * _mean(dch * chat))
        dcb_ref[...] += _rowsum(dc)

        a = z_ref[:, 2 * w:3 * w].astype(F32)
        gl = z_ref[:, 3 * w:4 * w].astype(F32)
        sgl = _sigmoid(gl)
        glu_ref[...] = a * sgl

        @pl.when(r % tiles_per_seq == tiles_per_seq - 1)
        def _():
            carry_ref[...] = jnp.zeros_like(carry_ref)

        dbuf_ref[0:tm, :] = dc
        dbuf_ref[tm:, :] = carry_ref[...]
        carry_ref[...] = dc[0:HALO_B, :]
        _fill_shifted(dbuf_ref, sh_ref)

        @pl.when(i == 0)
        def _():
            for k in range(CONV_B):
                wb_ref[k] = jnp.broadcast_to(cw_ref[k:k + 1, :], (SUB, w))

        groups = 2

        def strip(si, _):
            s = pl.multiple_of(si * (groups * SUB), groups * SUB)
            glu_rows = [glu_ref[pl.ds(s + SUB * g, SUB), :] for g in range(groups)]
            acc = [jnp.zeros((SUB, w), F32) for _ in range(groups)]
            for k in range(CONV_B):
                whole, part = divmod(CONV_B - 1 - k, SUB)
                wk = wb_ref[k]
                dw_part = jnp.zeros((SUB, w), F32)
                for g in range(groups):
                    at = pl.ds(s + SUB * (whole + g), SUB)
                    d_rows = dbuf_ref[at, :] if part == 0 else sh_ref[part - 1, at, :]
                    acc[g] = acc[g] + wk * d_rows
                    dw_part = dw_part + d_rows * glu_rows[g]
                dwacc_ref[k] += dw_part
            dglu_ref[pl.ds(s, groups * SUB), :] = jnp.concatenate(acc, axis=0)
            return 0

        lax.fori_loop(0, tm // (groups * SUB), strip, 0)
        dglu = dglu_ref[...]
        dz_ref[:, 2 * w:3 * w] = (dglu * sgl).astype(BF16)
        dz_ref[:, 3 * w:4 * w] = (dglu * a * sgl * (1.0 - sgl)).astype(BF16)
        dz_ref[:, 4 * w:6 * w] = dgt_ref[...]

        @pl.when(i == n_tiles - 1)
        def _():
            tri = lax.broadcasted_iota(jnp.int32, (CHUNK, CHUNK), 0) >= lax.broadcasted_iota(jnp.int32, (CHUNK, CHUNK), 1)
            lane = lax.broadcasted_iota(jnp.int32, (CHUNK, CHUNK), 1)
            dbs = jnp.zeros((CHUNK, CHUNK), F32)
            for g in range(GROUPS):
                dws_ref[g] = jnp.where(tri, dws_ref[g], 0.0)
                group_sum = jnp.sum(dbs_acc_ref[:, g * CHUNK:(g + 1) * CHUNK], axis=1, keepdims=True)
                dbs = jnp.where(lane == g, group_sum, dbs)
            dbs_ref[...] = dbs
            for k in range(CONV_B):
                dcw_ref[k:k + 1, :] = _rowsum(dwacc_ref[k])

    def rev(i):
        return n_tiles - 1 - i

    row = pl.BlockSpec((tm, w), lambda i: (rev(i), 0))
    full = lambda shape: pl.BlockSpec(shape, lambda i: (0,) * len(shape))
    in_specs = [pl.BlockSpec((tm, 4 * w), lambda i: (rev(i), 0)),
                row, row, row, pl.BlockSpec((tm, 2 * w), lambda i: (rev(i), 0)),
                _vec_spec(w), _vec_spec(w), full((GROUPS, CHUNK, CHUNK)), full((GROUPS, CHUNK, CHUNK)), full((CHUNK, w)),
                full((CONV_B, w)), _vec_spec(w), _vec_spec(w)]
    out_specs = [pl.BlockSpec((tm, 6 * w), lambda i: (rev(i), 0)), _vec_spec(w), _vec_spec(w), full((GROUPS, CHUNK, CHUNK)),
                 full((CHUNK, CHUNK)), full((CONV_B, w)), _vec_spec(w), _vec_spec(w), _vec_spec(w)]
    vec = jax.ShapeDtypeStruct((1, w), F32)
    out_shape = [jax.ShapeDtypeStruct((t, 6 * w), BF16), vec, vec, jax.ShapeDtypeStruct((GROUPS, CHUNK, CHUNK), F32),
                 jax.ShapeDtypeStruct((CHUNK, CHUNK), F32), jax.ShapeDtypeStruct((CONV_B, w), F32), vec, vec, vec]
    scratch = [pltpu.VMEM((HALO_B, w), F32), pltpu.VMEM((tm, w), F32), pltpu.VMEM((tm + HALO_B, w), F32),
               pltpu.VMEM((tm, w), F32), pltpu.VMEM((tm, w), F32), pltpu.VMEM((CHUNK, w), F32), pltpu.VMEM((tm, w), F32),
               pltpu.VMEM((SUB - 1, HALO_B + tm - SUB, w), F32), pltpu.VMEM((CONV_B, SUB, w), F32),
               pltpu.VMEM((CONV_B, SUB, w), F32)]
    return _row_call("branch_bwd", body, n_steps=n_tiles, in_specs=in_specs, out_specs=out_specs, out_shape=out_shape,
                     scratch=scratch, exch=exch,
                     args=(z, c_saved, dpa, dcs, dgates, ln_v_g, ln_v_b, ws_m, ws_mt, bias_full, conv_w, ln_b_g, ln_b_b))


def _conv3_window(prev8, x):
    win = jnp.concatenate([prev8, x], axis=0)
    n = x.shape[0]
    return [win[HALO_F - 2:HALO_F - 2 + n], win[HALO_F - 1:HALO_F - 1 + n], x]


def _ffn_mid_fwd(up0, conv_w, conv_b, seq):
    nb, t, f = up0.shape
    half = nb // 2
    tm = min(TM_EW, seq)
    tiles_per_seq = seq // tm
    n_strips = tm // ROWS_F

    def body(up_ref, w_ref, b_ref, act_ref, upc_ref, hist_ref):
        i = pl.program_id(0)

        @pl.when(i % tiles_per_seq == 0)
        def _():
            hist_ref[...] = jnp.zeros_like(hist_ref)

        for j in range(half):
            jv = j + half
            wg = [w_ref[j, k:k + 1, :] for k in range(CONV_F)]
            wv = [w_ref[jv, k:k + 1, :] for k in range(CONV_F)]
            bg, bv = b_ref[j], b_ref[jv]

            def strip(c, carry):
                rows = pl.ds(pl.multiple_of(c * ROWS_F, ROWS_F), ROWS_F)
                xg = up_ref[j, rows, :].astype(F32)
                xv = up_ref[jv, rows, :].astype(F32)
                sg = _conv3_window(carry[0], xg)
                sv = _conv3_window(carry[1], xv)
                gate = bg + wg[0] * sg[0] + wg[1] * sg[1] + wg[2] * sg[2]
                val = bv + wv[0] * sv[0] + wv[1] * sv[1] + wv[2] * sv[2]
                act_ref[j, rows, :] = (_gelu(gate) * val).astype(BF16)
                upc_ref[j, rows, :] = gate.astype(BF16)
                upc_ref[jv, rows, :] = val.astype(BF16)
                return xg[ROWS_F - HALO_F:], xv[ROWS_F - HALO_F:]

            last = lax.fori_loop(0, n_strips, strip, (hist_ref[j], hist_ref[jv]))
            hist_ref[j] = last[0]
            hist_ref[jv] = last[1]

    return pl.pallas_call(
        body, grid=(t // tm,),
        in_specs=[pl.BlockSpec((nb, tm, f), lambda i: (0, i, 0)), pl.BlockSpec((nb, CONV_F, f), lambda i: (0, 0, 0)),
                  pl.BlockSpec((nb, 1, f), lambda i: (0, 0, 0))],
        out_specs=[pl.BlockSpec((half, tm, f), lambda i: (0, i, 0)), pl.BlockSpec((nb, tm, f), lambda i: (0, i, 0))],
        out_shape=[jax.ShapeDtypeStruct((half, t, f), BF16), jax.ShapeDtypeStruct((nb, t, f), BF16)],
        scratch_shapes=[pltpu.VMEM((nb, HALO_F, f), F32)],
        name="ffn_mid_fwd", compiler_params=_params(1))(up0, conv_w, conv_b)


def _ffn_mid_bwd(up0, upc, dact, conv_w, seq, exch=()):
    nb, t, f = up0.shape
    half = nb // 2
    tm = min(TM_EW, seq)
    tiles_per_seq = seq // tm
    n_tiles = t // tm
    n_strips = tm // ROWS_F

    def body(up_ref, upc_ref, da_ref, w_ref, dup_ref, dw_ref, db_ref, carry_ref, dwacc_ref, dbacc_ref):
        i = pl.program_id(0)
        r = n_tiles - 1 - i

        @pl.when(i == 0)
        def _():
            dwacc_ref[...] = jnp.zeros_like(dwacc_ref)
            dbacc_ref[...] = jnp.zeros_like(dbacc_ref)

        @pl.when(r % tiles_per_seq == tiles_per_seq - 1)
        def _():
            carry_ref[...] = jnp.zeros_like(carry_ref)

        for j in range(half):
            jv = j + half
            wg = [w_ref[j, k:k + 1, :] for k in range(CONV_F)]
            wv = [w_ref[jv, k:k + 1, :] for k in range(CONV_F)]

            def strip(ci, carry):
                rows = pl.ds(pl.multiple_of((n_strips - 1 - ci) * ROWS_F, ROWS_F), ROWS_F)
                val = upc_ref[jv, rows, :].astype(F32)
                gg, dgg = _gelu_and_grad(upc_ref[j, rows, :].astype(F32))
                da = da_ref[j, rows, :].astype(F32)
                d_gate = da * val * dgg
                d_val = da * gg
                for blk, d, nxt, wk in ((j, d_gate, carry[0], wg), (jv, d_val, carry[1], wv)):
                    dbacc_ref[blk] += d
                    dwin = jnp.concatenate([d, nxt], axis=0)
                    shifted = [dwin[2:2 + ROWS_F], dwin[1:1 + ROWS_F], d]
                    x = up_ref[blk, rows, :].astype(F32)
                    for k in range(CONV_F):
                        dwacc_ref[blk, k] += shifted[k] * x
                    dx = wk[0] * shifted[0] + wk[1] * shifted[1] + wk[2] * shifted[2]
                    dup_ref[blk, rows, :] = dx.astype(BF16)
                return d_gate[0:HALO_F], d_val[0:HALO_F]

            carry = lax.fori_loop(0, n_strips, strip, (carry_ref[j], carry_ref[jv]))
            carry_ref[j] = carry[0]
            carry_ref[jv] = carry[1]

        @pl.when(i == n_tiles - 1)
        def _():
            for blk in range(nb):
                db_ref[blk] = _rowsum(dbacc_ref[blk])
                for k in range(CONV_F):
                    dw_ref[blk, k:k + 1, :] = _rowsum(dwacc_ref[blk, k])

    def rev(i):
        return n_tiles - 1 - i

    return _row_call(
        "ffn_mid_bwd", body, n_steps=n_tiles,
        in_specs=[pl.BlockSpec((nb, tm, f), lambda i: (0, rev(i), 0)), pl.BlockSpec((nb, tm, f), lambda i: (0, rev(i), 0)),
                  pl.BlockSpec((half, tm, f), lambda i: (0, rev(i), 0)),
                  pl.BlockSpec((nb, CONV_F, f), lambda i: (0, 0, 0))],
        out_specs=[pl.BlockSpec((nb, tm, f), lambda i: (0, rev(i), 0)), pl.BlockSpec((nb, CONV_F, f), lambda i: (0, 0, 0)),
                   pl.BlockSpec((nb, 1, f), lambda i: (0, 0, 0))],
        out_shape=[jax.ShapeDtypeStruct((nb, t, f), BF16), jax.ShapeDtypeStruct((nb, CONV_F, f), F32),
                   jax.ShapeDtypeStruct((nb, 1, f), F32)],
        scratch=[pltpu.VMEM((nb, HALO_F, f), F32), pltpu.VMEM((nb, CONV_F, ROWS_F, f), F32), pltpu.VMEM((nb, ROWS_F, f), F32)],
        args=(up0, upc, dact, conv_w), exch=exch)


def _all_gather(xs):
    n = len(xs)

    def body(*refs):
        x_refs, o_refs = refs[:n], refs[n:2 * n]
        send_sems, recv_sems, local_sems = refs[2 * n:]
        x, y, c = lax.axis_index("x"), lax.axis_index("y"), lax.axis_index("c")
        me, sibling = (x, y, c), (x, y, 1 - c)
        chips = [(1 - x, y), (x, 1 - y), (1 - x, 1 - y)]

        def slot(pos):
            return 4 * pos[0] + 2 * pos[1] + pos[2]

        def copy(a, k, block, to, src=None):
            dst = o_refs[a].at[slot(block)]
            return pltpu.make_async_remote_copy(
                src_ref=dst if src is None else src, dst_ref=dst, send_sem=send_sems.at[a * 7 + k],
                recv_sem=recv_sems.at[a * 7 + k], device_id=to, device_id_type=MESH)

        mine = [pltpu.make_async_copy(x_refs[a], o_refs[a].at[slot(me)], local_sems.at[a]) for a in range(n)]
        for cp in mine:
            cp.start()
        first = []
        for a in range(n):
            first.append(copy(a, 0, me, sibling, src=x_refs[a]))
            first += [copy(a, 1 + j, me, (*chip, c), src=x_refs[a]) for j, chip in enumerate(chips)]
        for cp in first:
            cp.start()
        passed = []
        for j, chip in enumerate(chips):
            for a in range(n):
                copy(a, 1 + j, (*chip, c), me).wait_recv()
                cp = copy(a, 4 + j, (*chip, c), sibling)
                cp.start()
                passed.append(cp)
        for a in range(n):
            copy(a, 0, sibling, me).wait_recv()
        for j, chip in enumerate(chips):
            for a in range(n):
                copy(a, 4 + j, (*chip, 1 - c), me).wait_recv()
        for cp in first + passed:
            cp.wait_send()
        for cp in mine:
            cp.wait()

    return pl.pallas_call(
        body, in_specs=[HBM_SPEC] * n, out_specs=[HBM_SPEC] * n,
        out_shape=[jax.ShapeDtypeStruct((N_DEV,) + v.shape, v.dtype) for v in xs],
        scratch_shapes=[pltpu.SemaphoreType.DMA((7 * n,)), pltpu.SemaphoreType.DMA((7 * n,)), pltpu.SemaphoreType.DMA((n,))],
        name="all_gather_weights")(*xs)


def _adamw_update(g, w, m, v):
    c1 = 1.0 - ADAM_B1 ** ADAM_STEP
    c2 = 1.0 - ADAM_B2 ** ADAM_STEP
    m_new = ADAM_B1 * m + (1.0 - ADAM_B1) * g
    v_new = ADAM_B2 * v + (1.0 - ADAM_B2) * (g * g)
    delta = -ADAM_LR * ((m_new / c1) / (jnp.sqrt(v_new / c2) + ADAM_EPS) + ADAM_WD * w)
    return m_new, v_new, delta


def _adamw(name, parts, w, m, v, rows_per_step):
    r, c = w.shape
    tr = r if r <= rows_per_step else (rows_per_step if r % rows_per_step == 0 else r // 2)

    def body(p_ref, w_ref, m_ref, v_ref, g_ref, d_ref, mo_ref, vo_ref):
        g = p_ref[0].astype(F32)
        for s in range(1, N_DEV):
            g = g + p_ref[s].astype(F32)
        m_new, v_new, delta = _adamw_update(g, w_ref[...], m_ref[...], v_ref[...])
        g_ref[...] = g
        mo_ref[...] = m_new
        vo_ref[...] = v_new
        d_ref[...] = delta

    row = pl.BlockSpec((tr, c), lambda i: (i, 0))
    out = jax.ShapeDtypeStruct((r, c), F32)
    return pl.pallas_call(body, grid=(r // tr,), in_specs=[pl.BlockSpec((N_DEV, tr, c), lambda i: (0, i, 0)), row, row, row],
                          out_specs=[row, row, row, row], out_shape=[out, out, out, out], name=name,
                          compiler_params=_params(1))(parts, w, m, v)


_VEC_NAMES = ("ln_v_g", "ln_v_b", "b_s", "conv_b_b", "ln_b_g", "ln_b_b", "g_ffn", "g_pg", "g_ple", "g_final")
_LOSS_ROW = len(_VEC_NAMES)
_REP_LAYOUT = dict({k: (i, 1, 1024) for i, k in enumerate(_VEC_NAMES)}, ffn_conv_b=(16, 8, 704), w_s=(24, 128, 1024))


def _pack_replicated_grads(d, loss_row):
    head = jnp.concatenate([d[k].reshape(1, 1024) for k in _VEC_NAMES] + [loss_row], axis=1).reshape(_LOSS_ROW + 1, 1024)
    return jnp.concatenate([jnp.pad(head, ((0, 16 - _LOSS_ROW - 1), (0, 0))),
                            jnp.pad(d["ffn_conv_b"].reshape(8, 704), ((0, 0), (0, 1024 - 704))),
                            d["w_s"].reshape(128, 1024)], axis=0)


def _rows8(vec):
    return jnp.pad(vec.reshape(1, 1024), ((0, 7), (0, 0)))


def _adamw_small(name, parts, items, loss_row=None):
    n = len(items)

    def body(p_ref, *refs):
        ins, outs = refs[:3 * n], refs[3 * n:]
        for a, (row, rows, cols) in enumerate(it[:3] for it in items):
            g = p_ref[0, row:row + rows, 0:cols]
            for s in range(1, N_DEV):
                g = g + p_ref[s, row:row + rows, 0:cols]
            m_new, v_new, delta = _adamw_update(g, ins[3 * a][...], ins[3 * a + 1][...], ins[3 * a + 2][...])
            for ref, val in zip(outs[4 * a:4 * a + 4], (g, delta, m_new, v_new)):
                ref[...] = val
        if loss_row is not None:
            total = p_ref[0, loss_row:loss_row + 1, 0:128]
            for s in range(1, N_DEV):
                total = total + p_ref[s, loss_row:loss_row + 1, 0:128]
            outs[-1][...] = total

    out_shape = [jax.ShapeDtypeStruct((rows, cols), F32) for _, rows, cols, *_ in items for _ in range(4)]
    if loss_row is not None:
        out_shape.append(jax.ShapeDtypeStruct((1, 128), F32))
    flat = pl.pallas_call(body, out_shape=out_shape, name=name,
                          compiler_params=pltpu.CompilerParams(vmem_limit_bytes=VMEM_LIMIT))(
        parts, *[arr for it in items for arr in it[3:]])
    res = [tuple(flat[4 * a:4 * a + 4]) for a in range(n)]
    return res + [flat[-1]] if loss_row is not None else res


def _pack_sharded_small(conv_b_w, ffn_conv_w):
    lead = conv_b_w.shape[:-2]
    pad0 = [(0, 0)] * len(lead)
    a = jnp.pad(conv_b_w, pad0 + [(0, 1), (0, 0)])
    b = jnp.pad(ffn_conv_w.reshape(lead + (CONV_F * 704,)), pad0 + [(0, 24 * 128 - CONV_F * 704)]).reshape(lead + (24, 128))
    return jnp.concatenate([a, b], axis=-2)


def _unpack_sharded_small(pk):
    lead = pk.shape[:-2]
    conv_b_w = pk[..., 0:CONV_B, :]
    ffn = pk[..., 32:56, :].reshape(lead + (24 * 128,))[..., :CONV_F * 704].reshape(lead + (CONV_F, 704))
    return conv_b_w, ffn


_WEIGHTS = ("g_mix", "w_in", "ln_v_g", "ln_v_b", "w_s", "b_s", "w_a_out", "conv_b_w", "conv_b_b", "ln_b_g", "ln_b_b",
            "w_b_out", "w_o", "g_ffn", "w_up", "ffn_conv_w", "ffn_conv_b", "w_down", "g_pg", "w_pg", "w_ple", "g_ple",
            "g_final")
_BIG = ("w_in", "w_a_out", "w_b_out", "w_o", "w_up", "w_down", "w_pg", "w_ple")


def kernel(x, p, g_mix, w_in, ln_v_g, ln_v_b, w_s, b_s, w_a_out, conv_b_w, conv_b_b, ln_b_g, ln_b_b, w_b_out, w_o, g_ffn, w_up, ffn_conv_w, ffn_conv_b, w_down, g_pg, w_pg, w_ple, g_ple, g_final, loss_target, m_g_mix, m_w_in, m_ln_v_g, m_ln_v_b, m_w_s, m_b_s, m_w_a_out, m_conv_b_w, m_conv_b_b, m_ln_b_g, m_ln_b_b, m_w_b_out, m_w_o, m_g_ffn, m_w_up, m_ffn_conv_w, m_ffn_conv_b, m_w_down, m_g_pg, m_w_pg, m_w_ple, m_g_ple, m_g_final, v_g_mix, v_w_in, v_ln_v_g, v_ln_v_b, v_w_s, v_b_s, v_w_a_out, v_conv_b_w, v_conv_b_b, v_ln_b_g, v_ln_b_b, v_w_b_out, v_w_o, v_g_ffn, v_w_up, v_ffn_conv_w, v_ffn_conv_b, v_w_down, v_g_pg, v_w_pg, v_w_ple, v_g_ple, v_g_final):
    local = dict(locals())
    wts = {k: local[k] for k in _WEIGHTS}
    mom = {k: local["m_" + k] for k in _WEIGHTS}
    var = {k: local["v_" + k] for k in _WEIGHTS}
    shapes = {k: wts[k].shape for k in _WEIGHTS}

    bsz, seq, d = x.shape
    t = bsz * seq
    x0 = x.reshape(t, d)
    p0 = p.reshape(t, p.shape[-1])
    target = loss_target.reshape(t, d)
    tm = min(TM_MM, t)
    tm_wide = min(TM_WIDE, t)
    tt = min(TT_MM, t)
    n_row = t // tm
    n_tok = t // tt

    def sq(a):
        return a.reshape(a.shape[1:])

    shard = {k: sq(wts[k]).astype(BF16) for k in _BIG}
    w_in3, small8 = _all_gather([shard["w_in"], _pack_sharded_small(sq(conv_b_w), sq(ffn_conv_w))])
    conv_b_w8, ffn_conv_w8 = _unpack_sharded_small(small8)
    conv_w_full = conv_b_w8.transpose(1, 0, 2).reshape(CONV_B, N_DEV * conv_b_w8.shape[-1])
    n_in = w_in3.shape[2]
    f_dev = shard["w_up"].shape[1]
    n_blk, f_blk = N_DEV // 2, 2 * f_dev

    def pair_blocks(a):
        return a.reshape(n_blk, 2, a.shape[1], f_dev).transpose(0, 2, 1, 3).reshape(n_blk, a.shape[1], f_blk)

    def unpair_blocks(a):
        return a.reshape(n_blk, a.shape[1], 2, f_dev).transpose(0, 2, 1, 3).reshape(N_DEV, a.shape[1], f_dev)

    ws_m = jnp.where(jnp.tril(jnp.ones((CHUNK, CHUNK), bool))[None], sq(w_s), 0.0).astype(BF16)
    ws_mt = jnp.swapaxes(ws_m, 1, 2)
    bias_full = jnp.broadcast_to(sq(b_s).T[:, :, None], (CHUNK, GROUPS, CHUNK)).reshape(CHUNK, GROUPS * CHUNK)
    ffn_b = ffn_conv_b.reshape(n_blk, 1, f_blk)
    ffn_w = pair_blocks(ffn_conv_w8)

    h1 = _rms_fwd("rms_mix", x0, g_mix)
    z, (wa3, wb3, wo3) = _matmul(
        "mm_in", h1, w_in3, dims=_NN, grid=(N_DEV, t // tm_wide, 1),
        a_spec=pl.BlockSpec((tm_wide, d), lambda j, i, k: (i, 0)),
        b_spec=pl.BlockSpec((None, d, n_in), lambda j, i, k: (j, 0, 0)),
        o_spec=pl.BlockSpec((tm_wide, n_in), lambda j, i, k: (i, j)), acc_shape=(tm_wide, n_in),
        out_shape=jax.ShapeDtypeStruct((t, N_DEV * n_in), BF16),
        exch=[("gather", shard[k]) for k in ("w_a_out", "w_b_out", "w_o")])
    w_a = wa3.reshape(-1, d)
    w_b = wb3.reshape(-1, d)
    w_om = wo3.reshape(-1, d)
    (pa, cs, c_saved), (w_up3, wd3, wpg3, wple3) = _branch_fwd(
        z, ln_v_g, ln_v_b, ws_m, bias_full, conv_w_full, conv_b_b, ln_b_g, ln_b_b, seq,
        exch=[("gather", shard[k]) for k in ("w_up", "w_down", "w_pg", "w_ple")])
    w_pgm = wpg3.reshape(-1, d)
    w_upb = pair_blocks(w_up3)
    w_db = wd3.reshape(n_blk // 2, f_blk, d)
    w_plem = wple3.transpose(1, 0, 2).reshape(wple3.shape[1], d)
    ya = _mm_rows("mm_a_out", pa, w_a, dims=_NN, tm=tm, out_dtype=BF16)
    yb, merged = _mm_rows("mm_b_out", cs, w_b, dims=_NN, tm=tm, epi=_epi_merge_fwd(z, ya, tm))
    x1, h2 = _mm_rows("mm_o", merged, w_om, dims=_NN, tm=tm, epi=_epi_residual_rms(x0, g_ffn, tm))
    up0 = _matmul("mm_up", h2, w_upb, dims=_NN, grid=(n_blk, t // tm_wide, 1),
                  a_spec=pl.BlockSpec((tm_wide, d), lambda j, i, k: (i, 0)),
                  b_spec=pl.BlockSpec((None, d, f_blk), lambda j, i, k: (j, 0, 0)),
                  o_spec=pl.BlockSpec((None, tm_wide, f_blk), lambda j, i, k: (j, i, 0)), acc_shape=(tm_wide, f_blk),
                  out_shape=jax.ShapeDtypeStruct((n_blk, t, f_blk), BF16))
    act, upc = _ffn_mid_fwd(up0, ffn_w, ffn_b, seq)
    x2, hq = _matmul("mm_down", act, w_db, dims=_NN, grid=(n_row, 1, n_blk // 2),
                     a_spec=pl.BlockSpec((None, tm, f_blk), lambda i, j, k: (k, i, 0)),
                     b_spec=pl.BlockSpec((None, f_blk, d), lambda i, j, k: (k, 0, 0)),
                     acc_shape=(tm, d), epi=_epi_residual_rms(x1, g_pg, tm))
    r = _mm_rows("mm_ple", p0, w_plem, dims=_NN, tm=tm, out_dtype=F32)

    dx3, dq, dr, loss_v, dg_final, dg_ple = _mm_rows(
        "mm_pg", hq, w_pgm, dims=_NN, tm=tm // 2, epi=_epi_head(x2, r, target, g_ple, g_final.reshape(1, d), tm // 2))

    recv = {}
    gw_pg = _mm_wgrad("wg_pg", hq, dq, out_dtype=BF16, tt=tt).reshape(wpg3.shape)
    dw_ple = _mm_wgrad("wg_ple", p0, dr, out_dtype=BF16, tt=tt)
    gw_ple = dw_ple.reshape(dw_ple.shape[0], N_DEV, -1).transpose(1, 0, 2)
    dx2, dx2b, dg_pg = _mm_rows("mm_pg_t", dq, w_pgm, dims=_NT, tm=tm, epi=_epi_rms_bwd(x2, g_pg, dx3, tm, want_bf16=True))

    dact, (recv["w_pg"], recv["w_ple"]) = _matmul(
        "mm_down_t", dx2b, w_db, dims=_NT, grid=(n_blk // 2, n_row, 1),
        a_spec=pl.BlockSpec((tm, d), lambda j, i, k: (i, 0)),
        b_spec=pl.BlockSpec((None, f_blk, d), lambda j, i, k: (j, 0, 0)),
        o_spec=pl.BlockSpec((None, tm, f_blk), lambda j, i, k: (j, i, 0)), acc_shape=(tm, f_blk),
        out_shape=jax.ShapeDtypeStruct((n_blk // 2, t, f_blk), BF16),
        exch=[("scatter", gw_pg), ("scatter", gw_ple)])
    gw_down = _matmul("wg_down", act, dx2b, dims=_TN, grid=(n_blk // 2, 1, n_tok),
                      a_spec=pl.BlockSpec((None, tt, f_blk), lambda j, i, k: (j, k, 0)),
                      b_spec=pl.BlockSpec((tt, d), lambda j, i, k: (k, 0)),
                      o_spec=pl.BlockSpec((None, f_blk, d), lambda j, i, k: (j, 0, 0)), acc_shape=(f_blk, d),
                      out_shape=jax.ShapeDtypeStruct((n_blk // 2, f_blk, d), BF16)).reshape(wd3.shape)
    (d_up0, dffn_wb, dffn_b), (recv["w_down"],) = _ffn_mid_bwd(up0, upc, dact, ffn_w, seq, exch=[("scatter", gw_down)])
    dffn_w8 = unpair_blocks(dffn_wb)
    gw_up = _matmul("wg_up", h2, d_up0, dims=_TN, grid=(n_blk, 1, n_tok),
                    a_spec=pl.BlockSpec((tt, d), lambda j, i, k: (k, 0)),
                    b_spec=pl.BlockSpec((None, tt, f_blk), lambda j, i, k: (j, k, 0)),
                    o_spec=pl.BlockSpec((None, d, f_blk), lambda j, i, k: (j, 0, 0)), acc_shape=(d, f_blk),
                    out_shape=jax.ShapeDtypeStruct((n_blk, d, f_blk), BF16))
    gw_up = unpair_blocks(gw_up)
    (dx1, dx1b, dg_ffn), (recv["w_up"],) = _matmul(
        "mm_up_t", d_up0, w_upb, dims=_NT, grid=(n_row, 1, n_blk),
        a_spec=pl.BlockSpec((None, tm, f_blk), lambda i, j, k: (k, i, 0)),
        b_spec=pl.BlockSpec((None, d, f_blk), lambda i, j, k: (k, 0, 0)),
        acc_shape=(tm, d), epi=_epi_rms_bwd(x1, g_ffn, dx2, tm, want_bf16=True), exch=[("scatter", gw_up)])

    dya, dyb, dgates = _mm_rows("mm_o_t", dx1b, w_om, dims=_NT, tm=tm, epi=_epi_merge_bwd(z, ya, yb, tm))
    gw_o = _mm_wgrad("wg_o", merged, dx1b, out_dtype=BF16, tt=tt).reshape(wo3.shape)
    dpa = _mm_rows("mm_a_out_t", dya, w_a, dims=_NT, tm=tm, out_dtype=BF16)
    dcs = _mm_rows("mm_b_out_t", dyb, w_b, dims=_NT, tm=tm, out_dtype=BF16)
    gw_a = _mm_wgrad("wg_a_out", pa, dya, out_dtype=BF16, tt=tt).reshape(wa3.shape)
    gw_b = _mm_wgrad("wg_b_out", cs, dyb, out_dtype=BF16, tt=tt).reshape(wb3.shape)
    (dz, dlvg, dlvb, dws, dbs_full, dconv_w, dconv_b, dlbg, dlbb), (recv["w_o"], recv["w_a_out"], recv["w_b_out"]) = _branch_bwd(
        z, c_saved, dpa, dcs, dgates, ln_v_g, ln_v_b, ws_m, ws_mt, bias_full, conv_w_full, ln_b_g, ln_b_b, seq,
        exch=[("scatter", gw_o), ("scatter", gw_a), ("scatter", gw_b)])
    db_s = dbs_full[:, :GROUPS].T
    rep_partial = _pack_replicated_grads(
        dict(ln_v_g=dlvg, ln_v_b=dlvb, b_s=db_s, conv_b_b=dconv_b, ln_b_g=dlbg, ln_b_b=dlbb, g_ffn=dg_ffn,
             g_pg=dg_pg, g_ple=dg_ple, g_final=dg_final, ffn_conv_b=dffn_b, w_s=dws), loss_v)
    dconv_w8 = dconv_w.reshape(CONV_B, N_DEV, -1).transpose(1, 0, 2)
    small_partial = _pack_sharded_small(dconv_w8, dffn_w8)
    gw_in, (recv_small, recv_rep) = _matmul(
        "wg_in", h1, dz, dims=_TN, grid=(N_DEV, 1, n_tok),
        a_spec=pl.BlockSpec((tt, d), lambda j, i, k: (k, 0)),
        b_spec=pl.BlockSpec((tt, n_in), lambda j, i, k: (k, j)),
        o_spec=pl.BlockSpec((None, d, n_in), lambda j, i, k: (j, 0, 0)), acc_shape=(d, n_in),
        out_shape=jax.ShapeDtypeStruct((N_DEV, d, n_in), BF16),
        exch=[("scatter", small_partial), ("gather", rep_partial)])
    (grad_x, dg_mix), (recv["w_in"],) = _matmul(
        "mm_in_t", dz, w_in3, dims=_NT, grid=(n_row, 1, N_DEV),
        a_spec=pl.BlockSpec((tm, n_in), lambda i, j, k: (i, k)),
        b_spec=pl.BlockSpec((None, d, n_in), lambda i, j, k: (k, 0, 0)),
        acc_shape=(tm, d), epi=_epi_rms_bwd(x0, g_mix, dx1, tm, want_bf16=False), exch=[("scatter", gw_in)])
    (recv_g_mix,) = _exchange("exchange_g_mix", [("gather", _rows8(dg_mix))])

    grads, deltas, new_m, new_v = {}, {}, {}, {}
    by_kind = (grads, deltas, new_m, new_v)

    def two_d(a):
        a = sq(a)
        return a.reshape(-1, a.shape[-1])

    for k in _BIG:
        parts = recv[k].reshape(N_DEV, -1, recv[k].shape[-1])
        outs = _adamw("adamw_" + k, parts, two_d(wts[k]), two_d(mom[k]), two_d(var[k]), 128)
        for tgt, o in zip(by_kind, outs):
            tgt[k] = o.reshape(shapes[k])

    small = [_pack_sharded_small(sq(s["conv_b_w"]), sq(s["ffn_conv_w"])) for s in (wts, mom, var)]
    for tgt, o in zip(by_kind, _adamw("adamw_conv", recv_small, small[0], small[1], small[2], 56)):
        cw, fw = _unpack_sharded_small(o)
        tgt["conv_b_w"] = cw.reshape(shapes["conv_b_w"])
        tgt["ffn_conv_w"] = fw.reshape(shapes["ffn_conv_w"])

    names = list(_REP_LAYOUT)
    items = [_REP_LAYOUT[k] + tuple(s[k].reshape(_REP_LAYOUT[k][1:]) for s in (wts, mom, var)) for k in names]
    *rep_outs, loss_sum = _adamw_small("adamw_replicated", recv_rep, items, loss_row=_LOSS_ROW)
    for k, outs in zip(names, rep_outs):
        for tgt, o in zip(by_kind, outs):
            tgt[k] = o.reshape(shapes[k])
    loss = loss_sum[0, 0]

    g_mix_item = (0, 1, 1024) + tuple(s["g_mix"].reshape(1, 1024) for s in (wts, mom, var))
    for tgt, o in zip(by_kind, _adamw_small("adamw_g_mix", recv_g_mix, [g_mix_item])[0]):
        tgt["g_mix"] = o.reshape(shapes["g_mix"])

    return (loss, grad_x.reshape(x.shape), *[grads[k] for k in _WEIGHTS], *[deltas[k] for k in _WEIGHTS],
            *[new_m[k] for k in _WEIGHTS], *[new_v[k] for k in _WEIGHTS])
```

```python
import math

import jax
import jax.numpy as jnp
from jax import lax
from jax.experimental import pallas as pl
from jax.experimental.pallas import tpu as pltpu

F32 = jnp.float32
BF16 = jnp.bfloat16

N_DEV = 8
EPS_RMS = 1e-6
EPS_LN = 1e-5
CHUNK = 128
GROUPS = 8
CONV_B = 31
CONV_F = 3
HALO_B = 32
HALO_F = 8
ROWS_F = 16
SUB = 8

ADAM_LR = 0.001
ADAM_B1 = 0.9
ADAM_B2 = 0.999
ADAM_EPS = 1e-08
ADAM_WD = 0.01
ADAM_STEP = 10

VMEM_LIMIT = 56 * 1024 * 1024
TM_MM = 1024
TM_WIDE = 2048
TT_MM = 2048
TM_EW = 256
TM_FFN = 512
ADAMW_ROWS = 512
EPI_STRIP = 256

_NN = (((1,), (0,)), ((), ()))
_NT = (((1,), (1,)), ((), ()))
_TN = (((0,), (0,)), ((), ()))
MESH = pl.DeviceIdType.MESH
HBM_SPEC = pl.BlockSpec(memory_space=pltpu.HBM)


def _params(n_axes):
    return pltpu.CompilerParams(dimension_semantics=("arbitrary",) * n_axes, vmem_limit_bytes=VMEM_LIMIT)


def _gelu(x):
    k = math.sqrt(2.0 / math.pi)
    return 0.5 * x * (1.0 + jnp.tanh(k * (x + 0.044715 * (x * x * x))))


def _gelu_and_grad(x):
    k = math.sqrt(2.0 / math.pi)
    x2 = x * x
    t = jnp.tanh(k * (x + 0.044715 * (x2 * x)))
    g = 0.5 * x * (1.0 + t)
    dg = 0.5 * (1.0 + t) + 0.5 * x * (1.0 - t * t) * (k * (1.0 + 3.0 * 0.044715 * x2))
    return g, dg


def _sigmoid(x):
    return 1.0 / (1.0 + jnp.exp(-x))


def _rowsum(x):
    return jnp.sum(x, axis=0, keepdims=True)


def _mean(x):
    return jnp.mean(x, axis=-1, keepdims=True)


def _exchange_io(exch):
    n = len(exch)
    out_shape = [jax.ShapeDtypeStruct(v.shape if kind == "scatter" else (N_DEV,) + v.shape, v.dtype) for kind, v in exch]
    scratch = [pltpu.SemaphoreType.DMA((7 * n,)), pltpu.SemaphoreType.DMA((7 * n,)), pltpu.SemaphoreType.DMA((n,))] if n else []
    return [HBM_SPEC] * n, [HBM_SPEC] * n, out_shape, scratch


def _exchange_step(kinds, x_refs, o_refs, send_sems, recv_sems, local_sems):
    n = len(kinds)
    x, y, c = lax.axis_index("x"), lax.axis_index("y"), lax.axis_index("c")
    me = 4 * x + 2 * y + c

    def src(a, to_slot):
        return x_refs[a].at[to_slot] if kinds[a] == "scatter" else x_refs[a]

    mine = [pltpu.make_async_copy(src(a, me), o_refs[a].at[me], local_sems.at[a]) for a in range(n)]
    sends, recvs = [], []
    for m in range(1, N_DEV):
        mx, my, mc = (m >> 2) & 1, (m >> 1) & 1, m & 1
        px, py, pc = (1 - x if mx else x), (1 - y if my else y), (1 - c if mc else c)
        peer = 4 * px + 2 * py + pc
        for a in range(n):
            k = a * 7 + m - 1
            sends.append(pltpu.make_async_remote_copy(
                src_ref=src(a, peer), dst_ref=o_refs[a].at[me], send_sem=send_sems.at[k], recv_sem=recv_sems.at[k],
                device_id=(px, py, pc), device_id_type=MESH))
            recvs.append(pltpu.make_async_remote_copy(
                src_ref=src(a, peer), dst_ref=o_refs[a].at[peer], send_sem=send_sems.at[k], recv_sem=recv_sems.at[k],
                device_id=(px, py, pc), device_id_type=MESH))

    def start():
        for cp in mine + sends:
            cp.start()

    def finish():
        for cp in recvs:
            cp.wait_recv()
        for cp in sends:
            cp.wait_send()
        for cp in mine:
            cp.wait()

    return start, finish


def _exchange(name, exch):
    n = len(exch)
    kinds = [k for k, _ in exch]
    in_specs, out_specs, out_shape, scratch = _exchange_io(exch)

    def body(*refs):
        start, finish = _exchange_step(kinds, refs[:n], refs[n:2 * n], *refs[2 * n:])
        start()
        finish()

    return pl.pallas_call(body, in_specs=in_specs, out_specs=out_specs, out_shape=out_shape, scratch_shapes=scratch,
                          name=name)(*[v for _, v in exch])


class _Epilogue:
    def __init__(self, fn, ins=(), in_specs=(), out_specs=(), out_shape=(), strip=None):
        self.fn, self.ins, self.in_specs = fn, list(ins), list(in_specs)
        self.out_specs, self.out_shape = list(out_specs), list(out_shape)
        self.strip = strip


def _matmul(name, a, b, *, dims, grid, a_spec, b_spec, acc_shape, o_spec=None, out_shape=None, epi=None, exch=()):
    nk = grid[2]
    plain = epi is None
    if plain:
        def store(acc, ins, outs, i):
            outs[0][...] = acc.astype(outs[0].dtype)
        epi = _Epilogue(store, out_specs=[o_spec], out_shape=[out_shape])
    n_in = 2 + len(epi.ins)
    n_out = len(epi.out_specs)
    n_ex = len(exch)
    kinds = [k for k, _ in exch]
    ex_in, ex_out, ex_shape, ex_scratch = _exchange_io(exch)

    def body(*refs):
        a_ref, b_ref = refs[:2]
        step0 = pl.program_id(0)
        epi_ins, rest = refs[2:n_in], refs[n_in:]
        x_refs, rest = rest[:n_ex], rest[n_ex:]
        outs, rest = rest[:n_out], rest[n_out:]
        o_refs, scr = rest[:n_ex], rest[n_ex:]
        if n_ex:
            pid = [pl.program_id(ax) for ax in range(3)]
            ex_start, ex_finish = _exchange_step(kinds, x_refs, o_refs, *scr[len(scr) - 3:])
            pl.when((pid[0] == 0) & (pid[1] == 0) & (pid[2] == 0))(ex_start)
        part = lax.dot_general(a_ref[...].astype(BF16), b_ref[...].astype(BF16), dims, preferred_element_type=F32)

        def run_epilogue(rows_of_acc):
            rows = acc_shape[0]
            strip = rows if epi.strip is None else min(epi.strip, rows)
            for s in range(0, rows, strip):
                def view(ref):
                    return ref.at[pl.ds(s, strip)] if ref.shape[0] == rows else ref
                first = (step0 == 0) if s == 0 else False
                epi.fn(rows_of_acc(s, strip), [view(r) for r in epi_ins], [view(r) for r in outs], first)

        if nk == 1:
            run_epilogue(lambda s, n: part[s:s + n])
        else:
            acc_ref = scr[0]
            k = pl.program_id(2)

            @pl.when(k == 0)
            def _():
                acc_ref[...] = part

            @pl.when(k > 0)
            def _():
                acc_ref[...] += part

            @pl.when(k == nk - 1)
            def _():
                run_epilogue(lambda s, n: acc_ref[pl.ds(s, n), :])
        if n_ex:
            pl.when((pid[0] == grid[0] - 1) & (pid[1] == grid[1] - 1) & (pid[2] == grid[2] - 1))(ex_finish)

    scratch = ([pltpu.VMEM(acc_shape, F32)] if nk > 1 else []) + ex_scratch
    res = pl.pallas_call(body, grid=grid, in_specs=[a_spec, b_spec] + epi.in_specs + ex_in,
                         out_specs=epi.out_specs + ex_out, out_shape=epi.out_shape + ex_shape, scratch_shapes=scratch,
                         name=name, compiler_params=_params(3))(a, b, *epi.ins, *[v for _, v in exch])
    main = res[0] if plain else res[:n_out]
    return (main, res[n_out:]) if n_ex else main


def _mm_rows(name, a, w, *, dims, tm, out_dtype=None, epi=None):
    t, k = a.shape
    n = w.shape[1] if dims == _NN else w.shape[0]
    tm = min(tm, t)
    return _matmul(name, a, w, dims=dims, grid=(t // tm, 1, 1),
                   a_spec=pl.BlockSpec((tm, k), lambda i, j, kk: (i, 0)),
                   b_spec=pl.BlockSpec(w.shape, lambda i, j, kk: (0, 0)),
                   o_spec=pl.BlockSpec((tm, n), lambda i, j, kk: (i, 0)), acc_shape=(tm, n),
                   out_shape=jax.ShapeDtypeStruct((t, n), out_dtype) if epi is None else None, epi=epi)


def _mm_wgrad(name, a, b, *, out_dtype, tt):
    t, m = a.shape
    n = b.shape[1]
    tt = min(tt, t)
    return _matmul(name, a, b, dims=_TN, grid=(1, 1, t // tt),
                   a_spec=pl.BlockSpec((tt, m), lambda i, j, kk: (kk, 0)),
                   b_spec=pl.BlockSpec((tt, n), lambda i, j, kk: (kk, 0)),
                   o_spec=pl.BlockSpec((m, n), lambda i, j, kk: (0, 0)),
                   acc_shape=(m, n), out_shape=jax.ShapeDtypeStruct((m, n), out_dtype))


def _row3(tm, d, col=0):
    return pl.BlockSpec((tm, d), lambda i, j, k: (i, col))


def _vec3(d):
    return pl.BlockSpec((1, d), lambda i, j, k: (0, 0))


def _accumulate_over_rows(ref, part, first):
    if first is False:
        ref[...] += part
        return

    @pl.when(first)
    def _():
        ref[...] = part + jnp.zeros_like(ref)

    @pl.when(jnp.logical_not(first))
    def _():
        ref[...] += part


def _epi_residual_rms(res, g, tm):
    t, d = res.shape

    def fn(acc, ins, outs, i):
        res_ref, g_ref = ins
        xv = acc + res_ref[...]
        outs[0][...] = xv
        rstd = lax.rsqrt(_mean(xv * xv) + EPS_RMS)
        outs[1][...] = ((xv * rstd) * g_ref[...]).astype(BF16)

    return _Epilogue(fn, strip=EPI_STRIP, ins=[res, g], in_specs=[_row3(tm, d), _vec3(d)], out_specs=[_row3(tm, d), _row3(tm, d)],
                     out_shape=[jax.ShapeDtypeStruct((t, d), F32), jax.ShapeDtypeStruct((t, d), BF16)])


def _epi_rms_bwd(x, g, dres, tm, *, want_bf16):
    t, d = x.shape

    def fn(acc, ins, outs, i):
        x_ref, g_ref, dres_ref = ins
        xv = x_ref[...]
        rstd = lax.rsqrt(_mean(xv * xv) + EPS_RMS)
        nrm = xv * rstd
        dn = acc * g_ref[...]
        dx = dres_ref[...] + rstd * (dn - nrm * _mean(dn * nrm))
        outs[0][...] = dx
        if want_bf16:
            outs[1][...] = dx.astype(BF16)
        _accumulate_over_rows(outs[-1], _rowsum(acc * nrm), i)

    row = _row3(tm, d)
    n_dx = 2 if want_bf16 else 1
    return _Epilogue(fn, strip=EPI_STRIP, ins=[x, g, dres], in_specs=[row, _vec3(d), row], out_specs=[row] * n_dx + [_vec3(d)],
                     out_shape=[jax.ShapeDtypeStruct((t, d), F32)] + [jax.ShapeDtypeStruct((t, d), BF16)] * (n_dx - 1)
                     + [jax.ShapeDtypeStruct((1, d), F32)])


def _epi_merge_fwd(z, ya, tm):
    t, w = ya.shape

    def fn(acc, ins, outs, i):
        ga_ref, gb_ref, ya_ref = ins
        outs[0][...] = acc.astype(BF16)
        sa = _sigmoid(ga_ref[...].astype(F32))
        sb = _sigmoid(gb_ref[...].astype(F32))
        outs[1][...] = (sa * ya_ref[...].astype(F32) + sb * acc).astype(BF16)

    row = _row3(tm, w)
    return _Epilogue(fn, strip=EPI_STRIP, ins=[z, z, ya], in_specs=[_row3(tm, w, 4), _row3(tm, w, 5), row], out_specs=[row, row],
                     out_shape=[jax.ShapeDtypeStruct((t, w), BF16)] * 2)


def _epi_merge_bwd(z, ya, yb, tm):
    t, w = ya.shape

    def fn(acc, ins, outs, i):
        ga_ref, gb_ref, ya_ref, yb_ref = ins
        sa = _sigmoid(ga_ref[...].astype(F32))
        sb = _sigmoid(gb_ref[...].astype(F32))
        outs[0][...] = (acc * sa).astype(BF16)
        outs[1][...] = (acc * sb).astype(BF16)
        outs[2][:, 0:w] = (acc * ya_ref[...].astype(F32) * sa * (1.0 - sa)).astype(BF16)
        outs[2][:, w:2 * w] = (acc * yb_ref[...].astype(F32) * sb * (1.0 - sb)).astype(BF16)

    row = _row3(tm, w)
    return _Epilogue(fn, strip=EPI_STRIP, ins=[z, z, ya, yb], in_specs=[_row3(tm, w, 4), _row3(tm, w, 5), row, row],
                     out_specs=[row, row, _row3(tm, 2 * w)],
                     out_shape=[jax.ShapeDtypeStruct((t, w), BF16)] * 2 + [jax.ShapeDtypeStruct((t, 2 * w), BF16)])


def _epi_head(x2, r, target, g_ple, g_final, tm):
    t, d = x2.shape

    def fn(acc, ins, outs, i):
        x2_ref, r_ref, tg_ref, gple_ref, gfin_ref = ins
        dx3_ref, dq_ref, dr_ref, loss_ref, dgfin_ref, dgple_ref = outs
        pg = _sigmoid(acc)
        rv = r_ref[...]
        rstd_r = lax.rsqrt(_mean(rv * rv) + EPS_RMS)
        nr = rv * rstd_r
        pe = nr * gple_ref[...]
        x3 = x2_ref[...] + pe * pg
        rstd3 = lax.rsqrt(_mean(x3 * x3) + EPS_RMS)
        n3 = x3 * rstd3
        err = n3 * gfin_ref[...] - tg_ref[...]
        loss_part = jnp.sum(_rowsum(err * err), axis=1, keepdims=True) * (0.5 / d)
        dy = err * (1.0 / d)
        dn3 = dy * gfin_ref[...]
        dx3 = rstd3 * (dn3 - n3 * _mean(dn3 * n3))
        dx3_ref[...] = dx3
        dq_ref[...] = (dx3 * pe * pg * (1.0 - pg)).astype(BF16)
        dpe = dx3 * pg
        dnr = dpe * gple_ref[...]
        dr_ref[...] = (rstd_r * (dnr - nr * _mean(dnr * nr))).astype(BF16)
        _accumulate_over_rows(loss_ref, loss_part, i)
        _accumulate_over_rows(dgfin_ref, _rowsum(dy * n3), i)
        _accumulate_over_rows(dgple_ref, _rowsum(dpe * nr), i)

    row = _row3(tm, d)
    vec = jax.ShapeDtypeStruct((1, d), F32)
    return _Epilogue(fn, strip=EPI_STRIP, ins=[x2, r, target, g_ple, g_final], in_specs=[row, row, row, _vec3(d), _vec3(d)],
                     out_specs=[row, row, row, _vec3(d), _vec3(d), _vec3(d)],
                     out_shape=[jax.ShapeDtypeStruct((t, d), F32), jax.ShapeDtypeStruct((t, d), BF16),
                                jax.ShapeDtypeStruct((t, d), BF16), vec, vec, vec])


def _vec_spec(d):
    return pl.BlockSpec((1, d), lambda i: (0, 0))


def _row_call(name, body, *, n_steps, in_specs, out_specs, out_shape, scratch, args, exch=()):
    n_in, n_out, n_scr, n_ex = len(in_specs), len(out_specs), len(scratch), len(exch)
    kinds = [k for k, _ in exch]
    ex_in, ex_out, ex_shape, ex_scratch = _exchange_io(exch)

    def wrapped(*refs):
        ins, rest = refs[:n_in], refs[n_in:]
        x_refs, rest = rest[:n_ex], rest[n_ex:]
        outs, rest = rest[:n_out], rest[n_out:]
        o_refs, rest = rest[:n_ex], rest[n_ex:]
        scr, sems = rest[:n_scr], rest[n_scr:]
        if n_ex:
            ex_start, ex_finish = _exchange_step(kinds, x_refs, o_refs, *sems)
            pl.when(pl.program_id(0) == 0)(ex_start)
        body(*ins, *outs, *scr)
        if n_ex:
            pl.when(pl.program_id(0) == n_steps - 1)(ex_finish)

    res = pl.pallas_call(wrapped, grid=(n_steps,), in_specs=list(in_specs) + ex_in, out_specs=list(out_specs) + ex_out,
                         out_shape=list(out_shape) + ex_shape, scratch_shapes=list(scratch) + ex_scratch, name=name,
                         compiler_params=_params(1))(*args, *[v for _, v in exch])
    return res[:n_out], res[n_out:]


def _rms_fwd(name, x, g):
    t, d = x.shape
    tm = min(TM_EW * 2, t)

    def body(x_ref, g_ref, h_ref):
        xv = x_ref[...]
        rstd = lax.rsqrt(_mean(xv * xv) + EPS_RMS)
        h_ref[...] = ((xv * rstd) * g_ref[...]).astype(BF16)

    row = pl.BlockSpec((tm, d), lambda i: (i, 0))
    return pl.pallas_call(body, grid=(t // tm,), in_specs=[row, _vec_spec(d)], out_specs=row,
                          out_shape=jax.ShapeDtypeStruct((t, d), BF16), name=name, compiler_params=_params(1))(x, g)


def _fill_shifted(buf_ref, sh_ref):
    n = sh_ref.shape[1]
    for p in range(1, SUB):
        sh_ref[p - 1] = buf_ref[p:p + n, :]


def _branch_fwd(z, ln_v_g, ln_v_b, ws_m, bias_full, conv_w, conv_b, ln_b_g, ln_b_b, seq, exch=()):
    t = z.shape[0]
    w = 1024
    tm = min(TM_EW, seq)
    tiles_per_seq = seq // tm
    n_chunks = tm // CHUNK

    def body(z_ref, lvg_ref, lvb_ref, ws_ref, bias_ref, cw_ref, cb_ref, lbg_ref, lbb_ref,
             pa_ref, cs_ref, c_ref, hist_ref, buf_ref, mix_ref, sh_ref, wb_ref):
        i = pl.program_id(0)
        u = z_ref[:, 0:w].astype(F32)
        v = z_ref[:, w:2 * w].astype(F32)
        ug = _gelu(u)
        vg = _gelu(v)
        dv = vg - _mean(vg)
        vhat = dv * lax.rsqrt(_mean(dv * dv) + EPS_LN)
        vn = (vhat * lvg_ref[...] + lvb_ref[...]).astype(BF16)
        for ci in range(n_chunks):
            rows = slice(ci * CHUNK, (ci + 1) * CHUNK)
            for g in range(GROUPS):
                cols = slice(g * CHUNK, (g + 1) * CHUNK)
                mix_ref[rows, cols] = lax.dot_general(ws_ref[g], vn[rows, cols], _NN, preferred_element_type=F32)
            mix_ref[rows, :] += bias_ref[...]
        pa_ref[...] = (ug * mix_ref[...]).astype(BF16)

        a = z_ref[:, 2 * w:3 * w].astype(F32)
        gl = z_ref[:, 3 * w:4 * w].astype(F32)
        glu = a * _sigmoid(gl)

        @pl.when(i % tiles_per_seq == 0)
        def _():
            hist_ref[...] = jnp.zeros_like(hist_ref)

        buf_ref[0:HALO_B, :] = hist_ref[...]
        buf_ref[HALO_B:, :] = glu
        hist_ref[...] = glu[tm - HALO_B:, :]
        _fill_shifted(buf_ref, sh_ref)

        @pl.when(i == 0)
        def _():
            for k in range(CONV_B):
                wb_ref[k] = jnp.broadcast_to(cw_ref[k:k + 1, :], (SUB, w))

        groups = 4

        def strip(si, _):
            s = pl.multiple_of(si * (groups * SUB), groups * SUB)
            acc = [jnp.zeros((SUB, w), F32) + cb_ref[...] for _ in range(groups)]
            for k in range(CONV_B):
                whole, part = divmod(HALO_B - (CONV_B - 1) + k, SUB)
                wk = wb_ref[k]
                for g in range(groups):
                    at = pl.ds(s + SUB * (whole + g), SUB)
                    acc[g] = acc[g] + wk * (buf_ref[at, :] if part == 0 else sh_ref[part - 1, at, :])
            for g in range(0, groups, 2):
                at = pl.ds(s + SUB * g, 2 * SUB)
                c = jnp.concatenate(acc[g:g + 2], axis=0)
                c_ref[at, :] = c
                dc = c - _mean(c)
                chat = dc * lax.rsqrt(_mean(dc * dc) + EPS_LN)
                cn = chat * lbg_ref[...] + lbb_ref[...]
                cs_ref[at, :] = (cn * _sigmoid(cn)).astype(BF16)
            return 0

        lax.fori_loop(0, tm // (groups * SUB), strip, 0)

    row = pl.BlockSpec((tm, w), lambda i: (i, 0))
    in_specs = [pl.BlockSpec((tm, 4 * w), lambda i: (i, 0)), _vec_spec(w), _vec_spec(w),
                pl.BlockSpec((GROUPS, CHUNK, CHUNK), lambda i: (0, 0, 0)), pl.BlockSpec((CHUNK, w), lambda i: (0, 0)),
                pl.BlockSpec((CONV_B, w), lambda i: (0, 0)), _vec_spec(w), _vec_spec(w), _vec_spec(w)]
    return _row_call(
        "branch_fwd", body, n_steps=t // tm, in_specs=in_specs, out_specs=[row, row, row],
        out_shape=[jax.ShapeDtypeStruct((t, w), BF16), jax.ShapeDtypeStruct((t, w), BF16), jax.ShapeDtypeStruct((t, w), F32)],
        scratch=[pltpu.VMEM((HALO_B, w), F32), pltpu.VMEM((HALO_B + tm, w), F32), pltpu.VMEM((tm, w), F32),
                 pltpu.VMEM((SUB - 1, HALO_B + tm - SUB, w), F32), pltpu.VMEM((CONV_B, SUB, w), F32)],
        args=(z, ln_v_g, ln_v_b, ws_m, bias_full, conv_w, conv_b, ln_b_g, ln_b_b), exch=exch)


def _branch_bwd(z, c_saved, dpa, dcs, dgates, ln_v_g, ln_v_b, ws_m, ws_mt, bias_full, conv_w, ln_b_g, ln_b_b, seq, exch=()):
    t = z.shape[0]
    w = 1024
    tm = min(TM_EW, seq)
    tiles_per_seq = seq // tm
    n_tiles = t // tm
    n_chunks = tm // CHUNK

    def body(z_ref, c_ref, dpa_ref, dcs_ref, dgt_ref, lvg_ref, lvb_ref, ws_ref, wst_ref, bias_ref, cw_ref,
             lbg_ref, lbb_ref,
             dz_ref, dlvg_ref, dlvb_ref, dws_ref, dbs_ref, dcw_ref, dcb_ref, dlbg_ref, dlbb_ref,
             carry_ref, glu_ref, dbuf_ref, mix_ref, dvn_ref, dbs_acc_ref, dglu_ref, sh_ref, wb_ref, dwacc_ref):
        i = pl.program_id(0)
        r = n_tiles - 1 - i

        @pl.when(i == 0)
        def _():
            for ref in (dlvg_ref, dlvb_ref, dws_ref, dbs_acc_ref, dwacc_ref, dcb_ref, dlbg_ref, dlbb_ref):
                ref[...] = jnp.zeros_like(ref)

        u = z_ref[:, 0:w].astype(F32)
        v = z_ref[:, w:2 * w].astype(F32)
        ug, dug = _gelu_and_grad(u)
        vg, dvg = _gelu_and_grad(v)
        dv0 = vg - _mean(vg)
        rstd_v = lax.rsqrt(_mean(dv0 * dv0) + EPS_LN)
        vhat = dv0 * rstd_v
        vn = (vhat * lvg_ref[...] + lvb_ref[...]).astype(BF16)
        dpa = dpa_ref[...].astype(F32)
        dmix = dpa * ug
        dmix_b = dmix.astype(BF16)
        for ci in range(n_chunks):
            rows = slice(ci * CHUNK, (ci + 1) * CHUNK)
            for g in range(GROUPS):
                cols = slice(g * CHUNK, (g + 1) * CHUNK)
                mix_ref[rows, cols] = lax.dot_general(ws_ref[g], vn[rows, cols], _NN, preferred_element_type=F32)
                dvn_ref[rows, cols] = lax.dot_general(wst_ref[g], dmix_b[rows, cols], _NN, preferred_element_type=F32)
                dws_ref[g] += lax.dot_general(dmix_b[rows, cols], vn[rows, cols], _NT, preferred_element_type=F32)
            mix_ref[rows, :] += bias_ref[...]
            dbs_acc_ref[...] += dmix[rows, :]
        dz_ref[:, 0:w] = (dpa * mix_ref[...] * dug).astype(BF16)
        dvn = dvn_ref[...]
        dlvg_ref[...] += _rowsum(dvn * vhat)
        dlvb_ref[...] += _rowsum(dvn)
        dvh = dvn * lvg_ref[...]
        dvg_in = rstd_v * (dvh - _mean(dvh) - vhat * _mean(dvh * vhat))
        dz_ref[:, w:2 * w] = (dvg_in * dvg).astype(BF16)

        c = c_ref[...]
        dc0 = c - _mean(c)
        rstd_c = lax.rsqrt(_mean(dc0 * dc0) + EPS_LN)
        chat = dc0 * rstd_c
        cn = chat * lbg_ref[...] + lbb_ref[...]
        sg = _sigmoid(cn)
        dcn = dcs_ref[...].astype(F32) * (sg * (1.0 + cn * (1.0 - sg)))
        dlbg_ref[...] += _rowsum(dcn * chat)
        dlbb_ref[...] += _rowsum(dcn)
        dch = dcn * lbg_ref[...]
        dc = rstd_c * (dch - _mean(dch) - chat * _mean(dch * chat))
        dcb_ref[...] += _rowsum(dc)

        a = z_ref[:, 2 * w:3 * w].astype(F32)
        gl = z_ref[:, 3 * w:4 * w].astype(F32)
        sgl = _sigmoid(gl)
        glu_ref[...] = a * sgl

        @pl.when(r % tiles_per_seq == tiles_per_seq - 1)
        def _():
            carry_ref[...] = jnp.zeros_like(carry_ref)

        dbuf_ref[0:tm, :] = dc
        dbuf_ref[tm:, :] = carry_ref[...]
        carry_ref[...] = dc[0:HALO_B, :]
        _fill_shifted(dbuf_ref, sh_ref)

        @pl.when(i == 0)
        def _():
            for k in range(CONV_B):
                wb_ref[k] = jnp.broadcast_to(cw_ref[k:k + 1, :], (SUB, w))

        groups = 2

        def strip(si, _):
            s = pl.multiple_of(si * (groups * SUB), groups * SUB)
            glu_rows = [glu_ref[pl.ds(s + SUB * g, SUB), :] for g in range(groups)]
            acc = [jnp.zeros((SUB, w), F32) for _ in range(groups)]
            for k in range(CONV_B):
                whole, part = divmod(CONV_B - 1 - k, SUB)
                wk = wb_ref[k]
                dw_part = jnp.zeros((SUB, w), F32)
                for g in range(groups):
                    at = pl.ds(s + SUB * (whole + g), SUB)
                    d_rows = dbuf_ref[at, :] if part == 0 else sh_ref[part - 1, at, :]
                    acc[g] = acc[g] + wk * d_rows
                    dw_part = dw_part + d_rows * glu_rows[g]
                dwacc_ref[k] += dw_part
            dglu_ref[pl.ds(s, groups * SUB), :] = jnp.concatenate(acc, axis=0)
            return 0

        lax.fori_loop(0, tm // (groups * SUB), strip, 0)
        dglu = dglu_ref[...]
        dz_ref[:, 2 * w:3 * w] = (dglu * sgl).astype(BF16)
        dz_ref[:, 3 * w:4 * w] = (dglu * a * sgl * (1.0 - sgl)).astype(BF16)
        dz_ref[:, 4 * w:6 * w] = dgt_ref[...]

        @pl.when(i == n_tiles - 1)
        def _():
            tri = lax.broadcasted_iota(jnp.int32, (CHUNK, CHUNK), 0) >= lax.broadcasted_iota(jnp.int32, (CHUNK, CHUNK), 1)
            lane = lax.broadcasted_iota(jnp.int32, (CHUNK, CHUNK), 1)
            dbs = jnp.zeros((CHUNK, CHUNK), F32)
            for g in range(GROUPS):
                dws_ref[g] = jnp.where(tri, dws_ref[g], 0.0)
                group_sum = jnp.sum(dbs_acc_ref[:, g * CHUNK:(g + 1) * CHUNK], axis=1, keepdims=True)
                dbs = jnp.where(lane == g, group_sum, dbs)
            dbs_ref[...] = dbs
            for k in range(CONV_B):
                dcw_ref[k:k + 1, :] = _rowsum(dwacc_ref[k])

    def rev(i):
        return n_tiles - 1 - i

    row = pl.BlockSpec((tm, w), lambda i: (rev(i), 0))
    full = lambda shape: pl.BlockSpec(shape, lambda i: (0,) * len(shape))
    in_specs = [pl.BlockSpec((tm, 4 * w), lambda i: (rev(i), 0)),
                row, row, row, pl.BlockSpec((tm, 2 * w), lambda i: (rev(i), 0)),
                _vec_spec(w), _vec_spec(w), full((GROUPS, CHUNK, CHUNK)), full((GROUPS, CHUNK, CHUNK)), full((CHUNK, w)),
                full((CONV_B, w)), _vec_spec(w), _vec_spec(w)]
    out_specs = [pl.BlockSpec((tm, 6 * w), lambda i: (rev(i), 0)), _vec_spec(w), _vec_spec(w), full((GROUPS, CHUNK, CHUNK)),
                 full((CHUNK, CHUNK)), full((CONV_B, w)), _vec_spec(w), _vec_spec(w), _vec_spec(w)]
    vec = jax.ShapeDtypeStruct((1, w), F32)
    out_shape = [jax.ShapeDtypeStruct((t, 6 * w), BF16), vec, vec, jax.ShapeDtypeStruct((GROUPS, CHUNK, CHUNK), F32),
                 jax.ShapeDtypeStruct((CHUNK, CHUNK), F32), jax.ShapeDtypeStruct((CONV_B, w), F32), vec, vec, vec]
    scratch = [pltpu.VMEM((HALO_B, w), F32), pltpu.VMEM((tm, w), F32), pltpu.VMEM((tm + HALO_B, w), F32),
               pltpu.VMEM((tm, w), F32), pltpu.VMEM((tm, w), F32), pltpu.VMEM((CHUNK, w), F32), pltpu.VMEM((tm, w), F32),
               pltpu.VMEM((SUB - 1, HALO_B + tm - SUB, w), F32), pltpu.VMEM((CONV_B, SUB, w), F32),
               pltpu.VMEM((CONV_B, SUB, w), F32)]
    return _row_call("branch_bwd", body, n_steps=n_tiles, in_specs=in_specs, out_specs=out_specs, out_shape=out_shape,
                     scratch=scratch, exch=exch,
                     args=(z, c_saved, dpa, dcs, dgates, ln_v_g, ln_v_b, ws_m, ws_mt, bias_full, conv_w, ln_b_g, ln_b_b))


def _conv3_window(prev8, x):
    win = jnp.concatenate([prev8, x], axis=0)
    n = x.shape[0]
    return [win[HALO_F - 2:HALO_F - 2 + n], win[HALO_F - 1:HALO_F - 1 + n], x]


def _ffn_mid_fwd(up0, conv_w, conv_b, seq):
    nb, t, f = up0.shape
    half = nb // 2
    tm = min(TM_FFN, seq)
    tiles_per_seq = seq // tm
    n_strips = tm // ROWS_F

    def body(up_ref, w_ref, b_ref, act_ref, upc_ref, hist_ref):
        i = pl.program_id(0)

        @pl.when(i % tiles_per_seq == 0)
        def _():
            hist_ref[...] = jnp.zeros_like(hist_ref)

        for j in range(half):
            jv = j + half
            wg = [w_ref[j, k:k + 1, :] for k in range(CONV_F)]
            wv = [w_ref[jv, k:k + 1, :] for k in range(CONV_F)]
            bg, bv = b_ref[j], b_ref[jv]

            def strip(c, carry):
                rows = pl.ds(pl.multiple_of(c * ROWS_F, ROWS_F), ROWS_F)
                xg = up_ref[j, rows, :].astype(F32)
                xv = up_ref[jv, rows, :].astype(F32)
                sg = _conv3_window(carry[0], xg)
                sv = _conv3_window(carry[1], xv)
                gate = bg + wg[0] * sg[0] + wg[1] * sg[1] + wg[2] * sg[2]
                val = bv + wv[0] * sv[0] + wv[1] * sv[1] + wv[2] * sv[2]
                act_ref[j, rows, :] = (_gelu(gate) * val).astype(BF16)
                upc_ref[j, rows, :] = gate.astype(BF16)
                upc_ref[jv, rows, :] = val.astype(BF16)
                return xg[ROWS_F - HALO_F:], xv[ROWS_F - HALO_F:]

            last = lax.fori_loop(0, n_strips, strip, (hist_ref[j], hist_ref[jv]))
            hist_ref[j] = last[0]
            hist_ref[jv] = last[1]

    return pl.pallas_call(
        body, grid=(t // tm,),
        in_specs=[pl.BlockSpec((nb, tm, f), lambda i: (0, i, 0)), pl.BlockSpec((nb, CONV_F, f), lambda i: (0, 0, 0)),
                  pl.BlockSpec((nb, 1, f), lambda i: (0, 0, 0))],
        out_specs=[pl.BlockSpec((half, tm, f), lambda i: (0, i, 0)), pl.BlockSpec((nb, tm, f), lambda i: (0, i, 0))],
        out_shape=[jax.ShapeDtypeStruct((half, t, f), BF16), jax.ShapeDtypeStruct((nb, t, f), BF16)],
        scratch_shapes=[pltpu.VMEM((nb, HALO_F, f), F32)],
        name="ffn_mid_fwd", compiler_params=_params(1))(up0, conv_w, conv_b)


def _ffn_mid_bwd(up0, upc, dact, conv_w, seq, exch=()):
    nb, t, f = up0.shape
    half = nb // 2
    tm = min(TM_FFN, seq)
    tiles_per_seq = seq // tm
    n_tiles = t // tm
    n_strips = tm // ROWS_F

    def body(up_ref, upc_ref, da_ref, w_ref, dup_ref, dw_ref, db_ref, carry_ref, dwacc_ref, dbacc_ref):
        i = pl.program_id(0)
        r = n_tiles - 1 - i

        @pl.when(i == 0)
        def _():
            dwacc_ref[...] = jnp.zeros_like(dwacc_ref)
            dbacc_ref[...] = jnp.zeros_like(dbacc_ref)

        @pl.when(r % tiles_per_seq == tiles_per_seq - 1)
        def _():
            carry_ref[...] = jnp.zeros_like(carry_ref)

        for j in range(half):
            jv = j + half
            wg = [w_ref[j, k:k + 1, :] for k in range(CONV_F)]
            wv = [w_ref[jv, k:k + 1, :] for k in range(CONV_F)]

            def strip(ci, carry):
                rows = pl.ds(pl.multiple_of((n_strips - 1 - ci) * ROWS_F, ROWS_F), ROWS_F)
                val = upc_ref[jv, rows, :].astype(F32)
                gg, dgg = _gelu_and_grad(upc_ref[j, rows, :].astype(F32))
                da = da_ref[j, rows, :].astype(F32)
                d_gate = da * val * dgg
                d_val = da * gg
                for blk, d, nxt, wk in ((j, d_gate, carry[0], wg), (jv, d_val, carry[1], wv)):
                    dbacc_ref[blk] += d
                    dwin = jnp.concatenate([d, nxt], axis=0)
                    shifted = [dwin[2:2 + ROWS_F], dwin[1:1 + ROWS_F], d]
                    x = up_ref[blk, rows, :].astype(F32)
                    for k in range(CONV_F):
                        dwacc_ref[blk, k] += shifted[k] * x
                    dx = wk[0] * shifted[0] + wk[1] * shifted[1] + wk[2] * shifted[2]
                    dup_ref[blk, rows, :] = dx.astype(BF16)
                return d_gate[0:HALO_F], d_val[0:HALO_F]

            carry = lax.fori_loop(0, n_strips, strip, (carry_ref[j], carry_ref[jv]))
            carry_ref[j] = carry[0]
            carry_ref[jv] = carry[1]

        @pl.when(i == n_tiles - 1)
        def _():
            for blk in range(nb):
                db_ref[blk] = _rowsum(dbacc_ref[blk])
                for k in range(CONV_F):
                    dw_ref[blk, k:k + 1, :] = _rowsum(dwacc_ref[blk, k])

    def rev(i):
        return n_tiles - 1 - i

    return _row_call(
        "ffn_mid_bwd", body, n_steps=n_tiles,
        in_specs=[pl.BlockSpec((nb, tm, f), lambda i: (0, rev(i), 0)), pl.BlockSpec((nb, tm, f), lambda i: (0, rev(i), 0)),
                  pl.BlockSpec((half, tm, f), lambda i: (0, rev(i), 0)),
                  pl.BlockSpec((nb, CONV_F, f), lambda i: (0, 0, 0))],
        out_specs=[pl.BlockSpec((nb, tm, f), lambda i: (0, rev(i), 0)), pl.BlockSpec((nb, CONV_F, f), lambda i: (0, 0, 0)),
                   pl.BlockSpec((nb, 1, f), lambda i: (0, 0, 0))],
        out_shape=[jax.ShapeDtypeStruct((nb, t, f), BF16), jax.ShapeDtypeStruct((nb, CONV_F, f), F32),
                   jax.ShapeDtypeStruct((nb, 1, f), F32)],
        scratch=[pltpu.VMEM((nb, HALO_F, f), F32), pltpu.VMEM((nb, CONV_F, ROWS_F, f), F32), pltpu.VMEM((nb, ROWS_F, f), F32)],
        args=(up0, upc, dact, conv_w), exch=exch)


def _all_gather(xs):
    n = len(xs)

    def body(*refs):
        x_refs, o_refs = refs[:n], refs[n:2 * n]
        send_sems, recv_sems, local_sems = refs[2 * n:]
        x, y, c = lax.axis_index("x"), lax.axis_index("y"), lax.axis_index("c")
        me, sibling = (x, y, c), (x, y, 1 - c)
        chips = [(1 - x, y), (x, 1 - y), (1 - x, 1 - y)]

        def slot(pos):
            return 4 * pos[0] + 2 * pos[1] + pos[2]

        def copy(a, k, block, to, src=None):
            dst = o_refs[a].at[slot(block)]
            return pltpu.make_async_remote_copy(
                src_ref=dst if src is None else src, dst_ref=dst, send_sem=send_sems.at[a * 7 + k],
                recv_sem=recv_sems.at[a * 7 + k], device_id=to, device_id_type=MESH)

        mine = [pltpu.make_async_copy(x_refs[a], o_refs[a].at[slot(me)], local_sems.at[a]) for a in range(n)]
        for cp in mine:
            cp.start()
        first = []
        for a in range(n):
            first.append(copy(a, 0, me, sibling, src=x_refs[a]))
            first += [copy(a, 1 + j, me, (*chip, c), src=x_refs[a]) for j, chip in enumerate(chips)]
        for cp in first:
            cp.start()
        passed = []
        for j, chip in enumerate(chips):
            for a in range(n):
                copy(a, 1 + j, (*chip, c), me).wait_recv()
                cp = copy(a, 4 + j, (*chip, c), sibling)
                cp.start()
                passed.append(cp)
        for a in range(n):
            copy(a, 0, sibling, me).wait_recv()
        for j, chip in enumerate(chips):
            for a in range(n):
                copy(a, 4 + j, (*chip, 1 - c), me).wait_recv()
        for cp in first + passed:
            cp.wait_send()
        for cp in mine:
            cp.wait()

    return pl.pallas_call(
        body, in_specs=[HBM_SPEC] * n, out_specs=[HBM_SPEC] * n,
        out_shape=[jax.ShapeDtypeStruct((N_DEV,) + v.shape, v.dtype) for v in xs],
        scratch_shapes=[pltpu.SemaphoreType.DMA((7 * n,)), pltpu.SemaphoreType.DMA((7 * n,)), pltpu.SemaphoreType.DMA((n,))],
        name="all_gather_weights")(*xs)


def _adamw_update(g, w, m, v):
    c1 = 1.0 - ADAM_B1 ** ADAM_STEP
    c2 = 1.0 - ADAM_B2 ** ADAM_STEP
    m_new = ADAM_B1 * m + (1.0 - ADAM_B1) * g
    v_new = ADAM_B2 * v + (1.0 - ADAM_B2) * (g * g)
    delta = -ADAM_LR * ((m_new / c1) / (jnp.sqrt(v_new / c2) + ADAM_EPS) + ADAM_WD * w)
    return m_new, v_new, delta


def _adamw(name, parts, w, m, v, rows_per_step):
    r, c = w.shape
    tr = r if r <= rows_per_step else (rows_per_step if r % rows_per_step == 0 else r // 2)

    def body(p_ref, w_ref, m_ref, v_ref, g_ref, d_ref, mo_ref, vo_ref):
        g = p_ref[0].astype(F32)
        for s in range(1, N_DEV):
            g = g + p_ref[s].astype(F32)
        m_new, v_new, delta = _adamw_update(g, w_ref[...], m_ref[...], v_ref[...])
        g_ref[...] = g
        mo_ref[...] = m_new
        vo_ref[...] = v_new
        d_ref[...] = delta

    row = pl.BlockSpec((tr, c), lambda i: (i, 0))
    out = jax.ShapeDtypeStruct((r, c), F32)
    return pl.pallas_call(body, grid=(r // tr,), in_specs=[pl.BlockSpec((N_DEV, tr, c), lambda i: (0, i, 0)), row, row, row],
                          out_specs=[row, row, row, row], out_shape=[out, out, out, out], name=name,
                          compiler_params=_params(1))(parts, w, m, v)


_VEC_NAMES = ("ln_v_g", "ln_v_b", "b_s", "conv_b_b", "ln_b_g", "ln_b_b", "g_ffn", "g_pg", "g_ple", "g_final")
_LOSS_ROW = len(_VEC_NAMES)
_REP_LAYOUT = dict({k: (i, 1, 1024) for i, k in enumerate(_VEC_NAMES)}, ffn_conv_b=(16, 8, 704), w_s=(24, 128, 1024))


def _pack_replicated_grads(d, loss_row):
    head = jnp.concatenate([d[k].reshape(1, 1024) for k in _VEC_NAMES] + [loss_row], axis=1).reshape(_LOSS_ROW + 1, 1024)
    return jnp.concatenate([jnp.pad(head, ((0, 16 - _LOSS_ROW - 1), (0, 0))),
                            jnp.pad(d["ffn_conv_b"].reshape(8, 704), ((0, 0), (0, 1024 - 704))),
                            d["w_s"].reshape(128, 1024)], axis=0)


def _rows8(vec):
    return jnp.pad(vec.reshape(1, 1024), ((0, 7), (0, 0)))


def _adamw_small(name, parts, items, loss_row=None):
    n = len(items)

    def body(p_ref, *refs):
        ins, outs = refs[:3 * n], refs[3 * n:]
        for a, (row, rows, cols) in enumerate(it[:3] for it in items):
            g = p_ref[0, row:row + rows, 0:cols]
            for s in range(1, N_DEV):
                g = g + p_ref[s, row:row + rows, 0:cols]
            m_new, v_new, delta = _adamw_update(g, ins[3 * a][...], ins[3 * a + 1][...], ins[3 * a + 2][...])
            for ref, val in zip(outs[4 * a:4 * a + 4], (g, delta, m_new, v_new)):
                ref[...] = val
        if loss_row is not None:
            total = p_ref[0, loss_row:loss_row + 1, 0:128]
            for s in range(1, N_DEV):
                total = total + p_ref[s, loss_row:loss_row + 1, 0:128]
            outs[-1][...] = total

    out_shape = [jax.ShapeDtypeStruct((rows, cols), F32) for _, rows, cols, *_ in items for _ in range(4)]
    if loss_row is not None:
        out_shape.append(jax.ShapeDtypeStruct((1, 128), F32))
    flat = pl.pallas_call(body, out_shape=out_shape, name=name,
                          compiler_params=pltpu.CompilerParams(vmem_limit_bytes=VMEM_LIMIT))(
        parts, *[arr for it in items for arr in it[3:]])
    res = [tuple(flat[4 * a:4 * a + 4]) for a in range(n)]
    return res + [flat[-1]] if loss_row is not None else res


def _pack_sharded_small(conv_b_w, ffn_conv_w):
    lead = conv_b_w.shape[:-2]
    pad0 = [(0, 0)] * len(lead)
    a = jnp.pad(conv_b_w, pad0 + [(0, 1), (0, 0)])
    b = jnp.pad(ffn_conv_w.reshape(lead + (CONV_F * 704,)), pad0 + [(0, 24 * 128 - CONV_F * 704)]).reshape(lead + (24, 128))
    return jnp.concatenate([a, b], axis=-2)


def _unpack_sharded_small(pk):
    lead = pk.shape[:-2]
    conv_b_w = pk[..., 0:CONV_B, :]
    ffn = pk[..., 32:56, :].reshape(lead + (24 * 128,))[..., :CONV_F * 704].reshape(lead + (CONV_F, 704))
    return conv_b_w, ffn


_WEIGHTS = ("g_mix", "w_in", "ln_v_g", "ln_v_b", "w_s", "b_s", "w_a_out", "conv_b_w", "conv_b_b", "ln_b_g", "ln_b_b",
            "w_b_out", "w_o", "g_ffn", "w_up", "ffn_conv_w", "ffn_conv_b", "w_down", "g_pg", "w_pg", "w_ple", "g_ple",
            "g_final")
_BIG = ("w_in", "w_a_out", "w_b_out", "w_o", "w_up", "w_down", "w_pg", "w_ple")


def kernel(x, p, g_mix, w_in, ln_v_g, ln_v_b, w_s, b_s, w_a_out, conv_b_w, conv_b_b, ln_b_g, ln_b_b, w_b_out, w_o, g_ffn, w_up, ffn_conv_w, ffn_conv_b, w_down, g_pg, w_pg, w_ple, g_ple, g_final, loss_target, m_g_mix, m_w_in, m_ln_v_g, m_ln_v_b, m_w_s, m_b_s, m_w_a_out, m_conv_b_w, m_conv_b_b, m_ln_b_g, m_ln_b_b, m_w_b_out, m_w_o, m_g_ffn, m_w_up, m_ffn_conv_w, m_ffn_conv_b, m_w_down, m_g_pg, m_w_pg, m_w_ple, m_g_ple, m_g_final, v_g_mix, v_w_in, v_ln_v_g, v_ln_v_b, v_w_s, v_b_s, v_w_a_out, v_conv_b_w, v_conv_b_b, v_ln_b_g, v_ln_b_b, v_w_b_out, v_w_o, v_g_ffn, v_w_up, v_ffn_conv_w, v_ffn_conv_b, v_w_down, v_g_pg, v_w_pg, v_w_ple, v_g_ple, v_g_final):
    local = dict(locals())
    wts = {k: local[k] for k in _WEIGHTS}
    mom = {k: local["m_" + k] for k in _WEIGHTS}
    var = {k: local["v_" + k] for k in _WEIGHTS}
    shapes = {k: wts[k].shape for k in _WEIGHTS}

    bsz, seq, d = x.shape
    t = bsz * seq
    x0 = x.reshape(t, d)
    p0 = p.reshape(t, p.shape[-1])
    target = loss_target.reshape(t, d)
    tm = min(TM_MM, t)
    tm_wide = min(TM_WIDE, t)
    tt = min(TT_MM, t)
    n_row = t // tm
    n_tok = t // tt

    def sq(a):
        return a.reshape(a.shape[1:])

    shard = {k: sq(wts[k]).astype(BF16) for k in _BIG}
    w_in3, small8 = _all_gather([shard["w_in"], _pack_sharded_small(sq(conv_b_w), sq(ffn_conv_w))])
    conv_b_w8, ffn_conv_w8 = _unpack_sharded_small(small8)
    conv_w_full = conv_b_w8.transpose(1, 0, 2).reshape(CONV_B, N_DEV * conv_b_w8.shape[-1])
    n_in = w_in3.shape[2]
    f_dev = shard["w_up"].shape[1]
    n_blk, f_blk = N_DEV // 2, 2 * f_dev

    def pair_blocks(a):
        return a.reshape(n_blk, 2, a.shape[1], f_dev).transpose(0, 2, 1, 3).reshape(n_blk, a.shape[1], f_blk)

    def unpair_blocks(a):
        return a.reshape(n_blk, a.shape[1], 2, f_dev).transpose(0, 2, 1, 3).reshape(N_DEV, a.shape[1], f_dev)

    ws_m = jnp.where(jnp.tril(jnp.ones((CHUNK, CHUNK), bool))[None], sq(w_s), 0.0).astype(BF16)
    ws_mt = jnp.swapaxes(ws_m, 1, 2)
    bias_full = jnp.broadcast_to(sq(b_s).T[:, :, None], (CHUNK, GROUPS, CHUNK)).reshape(CHUNK, GROUPS * CHUNK)
    ffn_b = ffn_conv_b.reshape(n_blk, 1, f_blk)
    ffn_w = pair_blocks(ffn_conv_w8)

    h1 = _rms_fwd("rms_mix", x0, g_mix)
    z, (wa3, wb3, wo3) = _matmul(
        "mm_in", h1, w_in3, dims=_NN, grid=(N_DEV, t // tm_wide, 1),
        a_spec=pl.BlockSpec((tm_wide, d), lambda j, i, k: (i, 0)),
        b_spec=pl.BlockSpec((None, d, n_in), lambda j, i, k: (j, 0, 0)),
        o_spec=pl.BlockSpec((tm_wide, n_in), lambda j, i, k: (i, j)), acc_shape=(tm_wide, n_in),
        out_shape=jax.ShapeDtypeStruct((t, N_DEV * n_in), BF16),
        exch=[("gather", shard[k]) for k in ("w_a_out", "w_b_out", "w_o")])
    w_a = wa3.reshape(-1, d)
    w_b = wb3.reshape(-1, d)
    w_om = wo3.reshape(-1, d)
    (pa, cs, c_saved), (w_up3, wd3, wpg3, wple3) = _branch_fwd(
        z, ln_v_g, ln_v_b, ws_m, bias_full, conv_w_full, conv_b_b, ln_b_g, ln_b_b, seq,
        exch=[("gather", shard[k]) for k in ("w_up", "w_down", "w_pg", "w_ple")])
    w_pgm = wpg3.reshape(-1, d)
    w_upb = pair_blocks(w_up3)
    w_db = wd3.reshape(n_blk // 2, f_blk, d)
    w_plem = wple3.transpose(1, 0, 2).reshape(wple3.shape[1], d)
    ya = _mm_rows("mm_a_out", pa, w_a, dims=_NN, tm=tm, out_dtype=BF16)
    yb, merged = _mm_rows("mm_b_out", cs, w_b, dims=_NN, tm=tm, epi=_epi_merge_fwd(z, ya, tm))
    x1, h2 = _mm_rows("mm_o", merged, w_om, dims=_NN, tm=tm, epi=_epi_residual_rms(x0, g_ffn, tm))
    up0 = _matmul("mm_up", h2, w_upb, dims=_NN, grid=(n_blk, t // tm_wide, 1),
                  a_spec=pl.BlockSpec((tm_wide, d), lambda j, i, k: (i, 0)),
                  b_spec=pl.BlockSpec((None, d, f_blk), lambda j, i, k: (j, 0, 0)),
                  o_spec=pl.BlockSpec((None, tm_wide, f_blk), lambda j, i, k: (j, i, 0)), acc_shape=(tm_wide, f_blk),
                  out_shape=jax.ShapeDtypeStruct((n_blk, t, f_blk), BF16))
    act, upc = _ffn_mid_fwd(up0, ffn_w, ffn_b, seq)
    x2, hq = _matmul("mm_down", act, w_db, dims=_NN, grid=(n_row, 1, n_blk // 2),
                     a_spec=pl.BlockSpec((None, tm, f_blk), lambda i, j, k: (k, i, 0)),
                     b_spec=pl.BlockSpec((None, f_blk, d), lambda i, j, k: (k, 0, 0)),
                     acc_shape=(tm, d), epi=_epi_residual_rms(x1, g_pg, tm))
    r = _mm_rows("mm_ple", p0, w_plem, dims=_NN, tm=tm, out_dtype=F32)

    dx3, dq, dr, loss_v, dg_final, dg_ple = _mm_rows(
        "mm_pg", hq, w_pgm, dims=_NN, tm=tm // 2, epi=_epi_head(x2, r, target, g_ple, g_final.reshape(1, d), tm // 2))

    recv = {}
    gw_pg = _mm_wgrad("wg_pg", hq, dq, out_dtype=BF16, tt=tt).reshape(wpg3.shape)
    dw_ple = _mm_wgrad("wg_ple", p0, dr, out_dtype=BF16, tt=tt)
    gw_ple = dw_ple.reshape(dw_ple.shape[0], N_DEV, -1).transpose(1, 0, 2)
    dx2, dx2b, dg_pg = _mm_rows("mm_pg_t", dq, w_pgm, dims=_NT, tm=tm, epi=_epi_rms_bwd(x2, g_pg, dx3, tm, want_bf16=True))

    dact, (recv["w_pg"], recv["w_ple"]) = _matmul(
        "mm_down_t", dx2b, w_db, dims=_NT, grid=(n_blk // 2, n_row, 1),
        a_spec=pl.BlockSpec((tm, d), lambda j, i, k: (i, 0)),
        b_spec=pl.BlockSpec((None, f_blk, d), lambda j, i, k: (j, 0, 0)),
        o_spec=pl.BlockSpec((None, tm, f_blk), lambda j, i, k: (j, i, 0)), acc_shape=(tm, f_blk),
        out_shape=jax.ShapeDtypeStruct((n_blk // 2, t, f_blk), BF16),
        exch=[("scatter", gw_pg), ("scatter", gw_ple)])
    gw_down = _matmul("wg_down", act, dx2b, dims=_TN, grid=(n_blk // 2, 1, n_tok),
                      a_spec=pl.BlockSpec((None, tt, f_blk), lambda j, i, k: (j, k, 0)),
                      b_spec=pl.BlockSpec((tt, d), lambda j, i, k: (k, 0)),
                      o_spec=pl.BlockSpec((None, f_blk, d), lambda j, i, k: (j, 0, 0)), acc_shape=(f_blk, d),
                      out_shape=jax.ShapeDtypeStruct((n_blk // 2, f_blk, d), BF16)).reshape(wd3.shape)
    (d_up0, dffn_wb, dffn_b), (recv["w_down"],) = _ffn_mid_bwd(up0, upc, dact, ffn_w, seq, exch=[("scatter", gw_down)])
    dffn_w8 = unpair_blocks(dffn_wb)
    gw_up = _matmul("wg_up", h2, d_up0, dims=_TN, grid=(n_blk, 1, n_tok),
                    a_spec=pl.BlockSpec((tt, d), lambda j, i, k: (k, 0)),
                    b_spec=pl.BlockSpec((None, tt, f_blk), lambda j, i, k: (j, k, 0)),
                    o_spec=pl.BlockSpec((None, d, f_blk), lambda j, i, k: (j, 0, 0)), acc_shape=(d, f_blk),
                    out_shape=jax.ShapeDtypeStruct((n_blk, d, f_blk), BF16))
    gw_up = unpair_blocks(gw_up)
    (dx1, dx1b, dg_ffn), (recv["w_up"],) = _matmul(
        "mm_up_t", d_up0, w_upb, dims=_NT, grid=(n_row, 1, n_blk),
        a_spec=pl.BlockSpec((None, tm, f_blk), lambda i, j, k: (k, i, 0)),
        b_spec=pl.BlockSpec((None, d, f_blk), lambda i, j, k: (k, 0, 0)),
        acc_shape=(tm, d), epi=_epi_rms_bwd(x1, g_ffn, dx2, tm, want_bf16=True), exch=[("scatter", gw_up)])

    dya, dyb, dgates = _mm_rows("mm_o_t", dx1b, w_om, dims=_NT, tm=tm, epi=_epi_merge_bwd(z, ya, yb, tm))
    gw_o = _mm_wgrad("wg_o", merged, dx1b, out_dtype=BF16, tt=tt).reshape(wo3.shape)
    dpa = _mm_rows("mm_a_out_t", dya, w_a, dims=_NT, tm=tm, out_dtype=BF16)
    dcs = _mm_rows("mm_b_out_t", dyb, w_b, dims=_NT, tm=tm, out_dtype=BF16)
    gw_a = _mm_wgrad("wg_a_out", pa, dya, out_dtype=BF16, tt=tt).reshape(wa3.shape)
    gw_b = _mm_wgrad("wg_b_out", cs, dyb, out_dtype=BF16, tt=tt).reshape(wb3.shape)
    (dz, dlvg, dlvb, dws, dbs_full, dconv_w, dconv_b, dlbg, dlbb), (recv["w_o"], recv["w_a_out"], recv["w_b_out"]) = _branch_bwd(
        z, c_saved, dpa, dcs, dgates, ln_v_g, ln_v_b, ws_m, ws_mt, bias_full, conv_w_full, ln_b_g, ln_b_b, seq,
        exch=[("scatter", gw_o), ("scatter", gw_a), ("scatter", gw_b)])
    db_s = dbs_full[:, :GROUPS].T
    rep_partial = _pack_replicated_grads(
        dict(ln_v_g=dlvg, ln_v_b=dlvb, b_s=db_s, conv_b_b=dconv_b, ln_b_g=dlbg, ln_b_b=dlbb, g_ffn=dg_ffn,
             g_pg=dg_pg, g_ple=dg_ple, g_final=dg_final, ffn_conv_b=dffn_b, w_s=dws), loss_v)
    dconv_w8 = dconv_w.reshape(CONV_B, N_DEV, -1).transpose(1, 0, 2)
    small_partial = _pack_sharded_small(dconv_w8, dffn_w8)
    gw_in, (recv_small, recv_rep) = _matmul(
        "wg_in", h1, dz, dims=_TN, grid=(N_DEV, 1, n_tok),
        a_spec=pl.BlockSpec((tt, d), lambda j, i, k: (k, 0)),
        b_spec=pl.BlockSpec((tt, n_in), lambda j, i, k: (k, j)),
        o_spec=pl.BlockSpec((None, d, n_in), lambda j, i, k: (j, 0, 0)), acc_shape=(d, n_in),
        out_shape=jax.ShapeDtypeStruct((N_DEV, d, n_in), BF16),
        exch=[("scatter", small_partial), ("gather", rep_partial)])
    w_inb = w_in3.reshape(N_DEV // 2, 2, d, n_in).transpose(0, 2, 1, 3).reshape(N_DEV // 2, d, 2 * n_in)
    (grad_x, dg_mix), (recv["w_in"],) = _matmul(
        "mm_in_t", dz, w_inb, dims=_NT, grid=(n_row, 1, N_DEV // 2),
        a_spec=pl.BlockSpec((tm, 2 * n_in), lambda i, j, k: (i, k)),
        b_spec=pl.BlockSpec((None, d, 2 * n_in), lambda i, j, k: (k, 0, 0)),
        acc_shape=(tm, d), epi=_epi_rms_bwd(x0, g_mix, dx1, tm, want_bf16=False), exch=[("scatter", gw_in)])
    (recv_g_mix,) = _exchange("exchange_g_mix", [("gather", _rows8(dg_mix))])

    grads, deltas, new_m, new_v = {}, {}, {}, {}
    by_kind = (grads, deltas, new_m, new_v)

    def two_d(a):
        a = sq(a)
        return a.reshape(-1, a.shape[-1])

    for k in _BIG:
        parts = recv[k].reshape(N_DEV, -1, recv[k].shape[-1])
        outs = _adamw("adamw_" + k, parts, two_d(wts[k]), two_d(mom[k]), two_d(var[k]), ADAMW_ROWS)
        for tgt, o in zip(by_kind, outs):
            tgt[k] = o.reshape(shapes[k])

    small = [_pack_sharded_small(sq(s["conv_b_w"]), sq(s["ffn_conv_w"])) for s in (wts, mom, var)]
    for tgt, o in zip(by_kind, _adamw("adamw_conv", recv_small, small[0], small[1], small[2], 56)):
        cw, fw = _unpack_sharded_small(o)
        tgt["conv_b_w"] = cw.reshape(shapes["conv_b_w"])
        tgt["ffn_conv_w"] = fw.reshape(shapes["ffn_conv_w"])

    names = list(_REP_LAYOUT)
    items = [_REP_LAYOUT[k] + tuple(s[k].reshape(_REP_LAYOUT[k][1:]) for s in (wts, mom, var)) for k in names]
    *rep_outs, loss_sum = _adamw_small("adamw_replicated", recv_rep, items, loss_row=_LOSS_ROW)
    for k, outs in zip(names, rep_outs):
        for tgt, o in zip(by_kind, outs):
            tgt[k] = o.reshape(shapes[k])
    loss = loss_sum[0, 0]

    g_mix_item = (0, 1, 1024) + tuple(s["g_mix"].reshape(1, 1024) for s in (wts, mom, var))
    for tgt, o in zip(by_kind, _adamw_small("adamw_g_mix", recv_g_mix, [g_mix_item])[0]):
        tgt["g_mix"] = o.reshape(shapes["g_mix"])

    return (loss, grad_x.reshape(x.shape), *[grads[k] for k in _WEIGHTS], *[deltas[k] for k in _WEIGHTS],
            *[new_m[k] for k in _WEIGHTS], *[new_v[k] for k in _WEIGHTS])
```

```python
import math

import jax
import jax.numpy as jnp
from jax import lax
from jax.experimental import pallas as pl
from jax.experimental.pallas import tpu as pltpu

F32 = jnp.float32
BF16 = jnp.bfloat16

N_DEV = 8
EPS_RMS = 1e-6
EPS_LN = 1e-5
CHUNK = 128
GROUPS = 8
CONV_B = 31
CONV_F = 3
HALO_B = 32
HALO_F = 8
ROWS_F = 16
SUB = 8

ADAM_LR = 0.001
ADAM_B1 = 0.9
ADAM_B2 = 0.999
ADAM_EPS = 1e-08
ADAM_WD = 0.01
ADAM_STEP = 10

VMEM_LIMIT = 56 * 1024 * 1024
TM_MM = 1024
TM_WIDE = 2048
TT_MM = 2048
TM_EW = 256
TM_FFN = 512
ADAMW_ROWS = 512
EPI_STRIP = 256

_NN = (((1,), (0,)), ((), ()))
_NT = (((1,), (1,)), ((), ()))
_TN = (((0,), (0,)), ((), ()))
MESH = pl.DeviceIdType.MESH
HBM_SPEC = pl.BlockSpec(memory_space=pltpu.HBM)


def _params(n_axes):
    return pltpu.CompilerParams(dimension_semantics=("arbitrary",) * n_axes, vmem_limit_bytes=VMEM_LIMIT)


def _gelu(x):
    k = math.sqrt(2.0 / math.pi)
    return 0.5 * x * (1.0 + jnp.tanh(k * (x + 0.044715 * (x * x * x))))


def _gelu_and_grad(x):
    k = math.sqrt(2.0 / math.pi)
    x2 = x * x
    t = jnp.tanh(k * (x + 0.044715 * (x2 * x)))
    g = 0.5 * x * (1.0 + t)
    dg = 0.5 * (1.0 + t) + 0.5 * x * (1.0 - t * t) * (k * (1.0 + 3.0 * 0.044715 * x2))
    return g, dg


def _sigmoid(x):
    return 1.0 / (1.0 + jnp.exp(-x))


def _rowsum(x):
    return jnp.sum(x, axis=0, keepdims=True)


def _mean(x):
    return jnp.mean(x, axis=-1, keepdims=True)


def _exchange_io(exch):
    n = len(exch)
    out_shape = [jax.ShapeDtypeStruct(v.shape if kind == "scatter" else (N_DEV,) + v.shape, v.dtype) for kind, v in exch]
    scratch = [pltpu.SemaphoreType.DMA((7 * n,)), pltpu.SemaphoreType.DMA((7 * n,)), pltpu.SemaphoreType.DMA((n,))] if n else []
    return [HBM_SPEC] * n, [HBM_SPEC] * n, out_shape, scratch


def _exchange_step(kinds, x_refs, o_refs, send_sems, recv_sems, local_sems):
    n = len(kinds)
    x, y, c = lax.axis_index("x"), lax.axis_index("y"), lax.axis_index("c")
    me = 4 * x + 2 * y + c

    def src(a, to_slot):
        return x_refs[a].at[to_slot] if kinds[a] == "scatter" else x_refs[a]

    mine = [pltpu.make_async_copy(src(a, me), o_refs[a].at[me], local_sems.at[a]) for a in range(n)]
    sends, recvs = [], []
    for m in range(1, N_DEV):
        mx, my, mc = (m >> 2) & 1, (m >> 1) & 1, m & 1
        px, py, pc = (1 - x if mx else x), (1 - y if my else y), (1 - c if mc else c)
        peer = 4 * px + 2 * py + pc
        for a in range(n):
            k = a * 7 + m - 1
            sends.append(pltpu.make_async_remote_copy(
                src_ref=src(a, peer), dst_ref=o_refs[a].at[me], send_sem=send_sems.at[k], recv_sem=recv_sems.at[k],
                device_id=(px, py, pc), device_id_type=MESH))
            recvs.append(pltpu.make_async_remote_copy(
                src_ref=src(a, peer), dst_ref=o_refs[a].at[peer], send_sem=send_sems.at[k], recv_sem=recv_sems.at[k],
                device_id=(px, py, pc), device_id_type=MESH))

    def start():
        for cp in mine + sends:
            cp.start()

    def finish():
        for cp in recvs:
            cp.wait_recv()
        for cp in sends:
            cp.wait_send()
        for cp in mine:
            cp.wait()

    return start, finish


def _exchange(name, exch):
    n = len(exch)
    kinds = [k for k, _ in exch]
    in_specs, out_specs, out_shape, scratch = _exchange_io(exch)

    def body(*refs):
        start, finish = _exchange_step(kinds, refs[:n], refs[n:2 * n], *refs[2 * n:])
        start()
        finish()

    return pl.pallas_call(body, in_specs=in_specs, out_specs=out_specs, out_shape=out_shape, scratch_shapes=scratch,
                          name=name)(*[v for _, v in exch])


class _Epilogue:
    def __init__(self, fn, ins=(), in_specs=(), out_specs=(), out_shape=(), strip=None):
        self.fn, self.ins, self.in_specs = fn, list(ins), list(in_specs)
        self.out_specs, self.out_shape = list(out_specs), list(out_shape)
        self.strip = strip


def _matmul(name, a, b, *, dims, grid, a_spec, b_spec, acc_shape, o_spec=None, out_shape=None, epi=None, exch=(), split=1):
    nk = grid[2]
    plain = epi is None
    if plain:
        def store(acc, ins, outs, i):
            outs[0][...] = acc.astype(outs[0].dtype)
        epi = _Epilogue(store, out_specs=[o_spec], out_shape=[out_shape])
    n_in = 2 + len(epi.ins)
    n_out = len(epi.out_specs)
    n_ex = len(exch)
    kinds = [k for k, _ in exch]
    ex_in, ex_out, ex_shape, ex_scratch = _exchange_io(exch)

    def body(*refs):
        a_ref, b_ref = refs[:2]
        step0 = pl.program_id(0)
        epi_ins, rest = refs[2:n_in], refs[n_in:]
        x_refs, rest = rest[:n_ex], rest[n_ex:]
        outs, rest = rest[:n_out], rest[n_out:]
        o_refs, scr = rest[:n_ex], rest[n_ex:]
        if n_ex:
            pid = [pl.program_id(ax) for ax in range(3)]
            ex_start, ex_finish = _exchange_step(kinds, x_refs, o_refs, *scr[len(scr) - 3:])
            pl.when((pid[0] == 0) & (pid[1] == 0) & (pid[2] == 0))(ex_start)
        if split == 1:
            part = lax.dot_general(a_ref[...].astype(BF16), b_ref[...].astype(BF16), dims, preferred_element_type=F32)
        else:
            kk = b_ref.shape[-1]
            part = None
            for s in range(split):
                a_s = a_ref[s] if len(a_ref.shape) == 3 else a_ref[:, s * kk:(s + 1) * kk]
                p_s = lax.dot_general(a_s.astype(BF16), b_ref[s].astype(BF16), dims, preferred_element_type=F32)
                part = p_s if part is None else part + p_s

        def run_epilogue(rows_of_acc):
            rows = acc_shape[0]
            strip = rows if epi.strip is None else min(epi.strip, rows)
            for s in range(0, rows, strip):
                def view(ref):
                    return ref.at[pl.ds(s, strip)] if ref.shape[0] == rows else ref
                first = (step0 == 0) if s == 0 else False
                epi.fn(rows_of_acc(s, strip), [view(r) for r in epi_ins], [view(r) for r in outs], first)

        if nk == 1:
            run_epilogue(lambda s, n: part[s:s + n])
        else:
            acc_ref = scr[0]
            k = pl.program_id(2)

            @pl.when(k == 0)
            def _():
                acc_ref[...] = part

            @pl.when(k > 0)
            def _():
                acc_ref[...] += part

            @pl.when(k == nk - 1)
            def _():
                run_epilogue(lambda s, n: acc_ref[pl.ds(s, n), :])
        if n_ex:
            pl.when((pid[0] == grid[0] - 1) & (pid[1] == grid[1] - 1) & (pid[2] == grid[2] - 1))(ex_finish)

    scratch = ([pltpu.VMEM(acc_shape, F32)] if nk > 1 else []) + ex_scratch
    res = pl.pallas_call(body, grid=grid, in_specs=[a_spec, b_spec] + epi.in_specs + ex_in,
                         out_specs=epi.out_specs + ex_out, out_shape=epi.out_shape + ex_shape, scratch_shapes=scratch,
                         name=name, compiler_params=_params(3))(a, b, *epi.ins, *[v for _, v in exch])
    main = res[0] if plain else res[:n_out]
    return (main, res[n_out:]) if n_ex else main


def _mm_rows(name, a, w, *, dims, tm, out_dtype=None, epi=None):
    t, k = a.shape
    n = w.shape[1] if dims == _NN else w.shape[0]
    tm = min(tm, t)
    return _matmul(name, a, w, dims=dims, grid=(t // tm, 1, 1),
                   a_spec=pl.BlockSpec((tm, k), lambda i, j, kk: (i, 0)),
                   b_spec=pl.BlockSpec(w.shape, lambda i, j, kk: (0, 0)),
                   o_spec=pl.BlockSpec((tm, n), lambda i, j, kk: (i, 0)), acc_shape=(tm, n),
                   out_shape=jax.ShapeDtypeStruct((t, n), out_dtype) if epi is None else None, epi=epi)


def _mm_wgrad(name, a, b, *, out_dtype, tt):
    t, m = a.shape
    n = b.shape[1]
    tt = min(tt, t)
    return _matmul(name, a, b, dims=_TN, grid=(1, 1, t // tt),
                   a_spec=pl.BlockSpec((tt, m), lambda i, j, kk: (kk, 0)),
                   b_spec=pl.BlockSpec((tt, n), lambda i, j, kk: (kk, 0)),
                   o_spec=pl.BlockSpec((m, n), lambda i, j, kk: (0, 0)),
                   acc_shape=(m, n), out_shape=jax.ShapeDtypeStruct((m, n), out_dtype))


def _row3(tm, d, col=0):
    return pl.BlockSpec((tm, d), lambda i, j, k: (i, col))


def _vec3(d):
    return pl.BlockSpec((1, d), lambda i, j, k: (0, 0))


def _accumulate_over_rows(ref, part, first):
    if first is False:
        ref[...] += part
        return

    @pl.when(first)
    def _():
        ref[...] = part + jnp.zeros_like(ref)

    @pl.when(jnp.logical_not(first))
    def _():
        ref[...] += part


def _epi_residual_rms(res, g, tm):
    t, d = res.shape

    def fn(acc, ins, outs, i):
        res_ref, g_ref = ins
        xv = acc + res_ref[...]
        outs[0][...] = xv
        rstd = lax.rsqrt(_mean(xv * xv) + EPS_RMS)
        outs[1][...] = ((xv * rstd) * g_ref[...]).astype(BF16)

    return _Epilogue(fn, strip=EPI_STRIP, ins=[res, g], in_specs=[_row3(tm, d), _vec3(d)], out_specs=[_row3(tm, d), _row3(tm, d)],
                     out_shape=[jax.ShapeDtypeStruct((t, d), F32), jax.ShapeDtypeStruct((t, d), BF16)])


def _epi_rms_bwd(x, g, dres, tm, *, want_bf16):
    t, d = x.shape

    def fn(acc, ins, outs, i):
        x_ref, g_ref, dres_ref = ins
        xv = x_ref[...]
        rstd = lax.rsqrt(_mean(xv * xv) + EPS_RMS)
        nrm = xv * rstd
        dn = acc * g_ref[...]
        dx = dres_ref[...] + rstd * (dn - nrm * _mean(dn * nrm))
        outs[0][...] = dx
        if want_bf16:
            outs[1][...] = dx.astype(BF16)
        _accumulate_over_rows(outs[-1], _rowsum(acc * nrm), i)

    row = _row3(tm, d)
    n_dx = 2 if want_bf16 else 1
    return _Epilogue(fn, strip=EPI_STRIP, ins=[x, g, dres], in_specs=[row, _vec3(d), row], out_specs=[row] * n_dx + [_vec3(d)],
                     out_shape=[jax.ShapeDtypeStruct((t, d), F32)] + [jax.ShapeDtypeStruct((t, d), BF16)] * (n_dx - 1)
                     + [jax.ShapeDtypeStruct((1, d), F32)])


def _epi_merge_fwd(z, ya, tm):
    t, w = ya.shape

    def fn(acc, ins, outs, i):
        ga_ref, gb_ref, ya_ref = ins
        outs[0][...] = acc.astype(BF16)
        sa = _sigmoid(ga_ref[...].astype(F32))
        sb = _sigmoid(gb_ref[...].astype(F32))
        outs[1][...] = (sa * ya_ref[...].astype(F32) + sb * acc).astype(BF16)

    row = _row3(tm, w)
    return _Epilogue(fn, strip=EPI_STRIP, ins=[z, z, ya], in_specs=[_row3(tm, w, 4), _row3(tm, w, 5), row], out_specs=[row, row],
                     out_shape=[jax.ShapeDtypeStruct((t, w), BF16)] * 2)


def _epi_merge_bwd(z, ya, yb, tm):
    t, w = ya.shape

    def fn(acc, ins, outs, i):
        ga_ref, gb_ref, ya_ref, yb_ref = ins
        sa = _sigmoid(ga_ref[...].astype(F32))
        sb = _sigmoid(gb_ref[...].astype(F32))
        outs[0][...] = (acc * sa).astype(BF16)
        outs[1][...] = (acc * sb).astype(BF16)
        outs[2][:, 0:w] = (acc * ya_ref[...].astype(F32) * sa * (1.0 - sa)).astype(BF16)
        outs[2][:, w:2 * w] = (acc * yb_ref[...].astype(F32) * sb * (1.0 - sb)).astype(BF16)

    row = _row3(tm, w)
    return _Epilogue(fn, strip=EPI_STRIP, ins=[z, z, ya, yb], in_specs=[_row3(tm, w, 4), _row3(tm, w, 5), row, row],
                     out_specs=[row, row, _row3(tm, 2 * w)],
                     out_shape=[jax.ShapeDtypeStruct((t, w), BF16)] * 2 + [jax.ShapeDtypeStruct((t, 2 * w), BF16)])


def _epi_head(x2, r, target, g_ple, g_final, tm):
    t, d = x2.shape

    def fn(acc, ins, outs, i):
        x2_ref, r_ref, tg_ref, gple_ref, gfin_ref = ins
        dx3_ref, dq_ref, dr_ref, loss_ref, dgfin_ref, dgple_ref = outs
        pg = _sigmoid(acc)
        rv = r_ref[...]
        rstd_r = lax.rsqrt(_mean(rv * rv) + EPS_RMS)
        nr = rv * rstd_r
        pe = nr * gple_ref[...]
        x3 = x2_ref[...] + pe * pg
        rstd3 = lax.rsqrt(_mean(x3 * x3) + EPS_RMS)
        n3 = x3 * rstd3
        err = n3 * gfin_ref[...] - tg_ref[...]
        loss_part = jnp.sum(_rowsum(err * err), axis=1, keepdims=True) * (0.5 / d)
        dy = err * (1.0 / d)
        dn3 = dy * gfin_ref[...]
        dx3 = rstd3 * (dn3 - n3 * _mean(dn3 * n3))
        dx3_ref[...] = dx3
        dq_ref[...] = (dx3 * pe * pg * (1.0 - pg)).astype(BF16)
        dpe = dx3 * pg
        dnr = dpe * gple_ref[...]
        dr_ref[...] = (rstd_r * (dnr - nr * _mean(dnr * nr))).astype(BF16)
        _accumulate_over_rows(loss_ref, loss_part, i)
        _accumulate_over_rows(dgfin_ref, _rowsum(dy * n3), i)
        _accumulate_over_rows(dgple_ref, _rowsum(dpe * nr), i)

    row = _row3(tm, d)
    vec = jax.ShapeDtypeStruct((1, d), F32)
    return _Epilogue(fn, strip=EPI_STRIP, ins=[x2, r, target, g_ple, g_final], in_specs=[row, row, row, _vec3(d), _vec3(d)],
                     out_specs=[row, row, row, _vec3(d), _vec3(d), _vec3(d)],
                     out_shape=[jax.ShapeDtypeStruct((t, d), F32), jax.ShapeDtypeStruct((t, d), BF16),
                                jax.ShapeDtypeStruct((t, d), BF16), vec, vec, vec])


def _vec_spec(d):
    return pl.BlockSpec((1, d), lambda i: (0, 0))


def _row_call(name, body, *, n_steps, in_specs, out_specs, out_shape, scratch, args, exch=()):
    n_in, n_out, n_scr, n_ex = len(in_specs), len(out_specs), len(scratch), len(exch)
    kinds = [k for k, _ in exch]
    ex_in, ex_out, ex_shape, ex_scratch = _exchange_io(exch)

    def wrapped(*refs):
        ins, rest = refs[:n_in], refs[n_in:]
        x_refs, rest = rest[:n_ex], rest[n_ex:]
        outs, rest = rest[:n_out], rest[n_out:]
        o_refs, rest = rest[:n_ex], rest[n_ex:]
        scr, sems = rest[:n_scr], rest[n_scr:]
        if n_ex:
            ex_start, ex_finish = _exchange_step(kinds, x_refs, o_refs, *sems)
            pl.when(pl.program_id(0) == 0)(ex_start)
        body(*ins, *outs, *scr)
        if n_ex:
            pl.when(pl.program_id(0) == n_steps - 1)(ex_finish)

    res = pl.pallas_call(wrapped, grid=(n_steps,), in_specs=list(in_specs) + ex_in, out_specs=list(out_specs) + ex_out,
                         out_shape=list(out_shape) + ex_shape, scratch_shapes=list(scratch) + ex_scratch, name=name,
                         compiler_params=_params(1))(*args, *[v for _, v in exch])
    return res[:n_out], res[n_out:]


def _rms_fwd(name, x, g):
    t, d = x.shape
    tm = min(TM_EW * 2, t)

    def body(x_ref, g_ref, h_ref):
        xv = x_ref[...]
        rstd = lax.rsqrt(_mean(xv * xv) + EPS_RMS)
        h_ref[...] = ((xv * rstd) * g_ref[...]).astype(BF16)

    row = pl.BlockSpec((tm, d), lambda i: (i, 0))
    return pl.pallas_call(body, grid=(t // tm,), in_specs=[row, _vec_spec(d)], out_specs=row,
                          out_shape=jax.ShapeDtypeStruct((t, d), BF16), name=name, compiler_params=_params(1))(x, g)


def _fill_shifted(buf_ref, sh_ref):
    n = sh_ref.shape[1]
    for p in range(1, SUB):
        sh_ref[p - 1] = buf_ref[p:p + n, :]


def _branch_fwd(z, ln_v_g, ln_v_b, ws_m, bias_full, conv_w, conv_b, ln_b_g, ln_b_b, seq, exch=()):
    t = z.shape[0]
    w = 1024
    tm = min(TM_EW, seq)
    tiles_per_seq = seq // tm
    n_chunks = tm // CHUNK

    def body(z_ref, lvg_ref, lvb_ref, ws_ref, bias_ref, cw_ref, cb_ref, lbg_ref, lbb_ref,
             pa_ref, cs_ref, c_ref, hist_ref, buf_ref, mix_ref, sh_ref, wb_ref):
        i = pl.program_id(0)
        u = z_ref[:, 0:w].astype(F32)
        v = z_ref[:, w:2 * w].astype(F32)
        ug = _gelu(u)
        vg = _gelu(v)
        dv = vg - _mean(vg)
        vhat = dv * lax.rsqrt(_mean(dv * dv) + EPS_LN)
        vn = (vhat * lvg_ref[...] + lvb_ref[...]).astype(BF16)
        for ci in range(n_chunks):
            rows = slice(ci * CHUNK, (ci + 1) * CHUNK)
            for g in range(GROUPS):
                cols = slice(g * CHUNK, (g + 1) * CHUNK)
                mix_ref[rows, cols] = lax.dot_general(ws_ref[g], vn[rows, cols], _NN, preferred_element_type=F32)
            mix_ref[rows, :] += bias_ref[...]
        pa_ref[...] = (ug * mix_ref[...]).astype(BF16)

        a = z_ref[:, 2 * w:3 * w].astype(F32)
        gl = z_ref[:, 3 * w:4 * w].astype(F32)
        glu = a * _sigmoid(gl)

        @pl.when(i % tiles_per_seq == 0)
        def _():
            hist_ref[...] = jnp.zeros_like(hist_ref)

        buf_ref[0:HALO_B, :] = hist_ref[...]
        buf_ref[HALO_B:, :] = glu
        hist_ref[...] = glu[tm - HALO_B:, :]
        _fill_shifted(buf_ref, sh_ref)

        @pl.when(i == 0)
        def _():
            for k in range(CONV_B):
                wb_ref[k] = jnp.broadcast_to(cw_ref[k:k + 1, :], (SUB, w))

        groups = 4

        def strip(si, _):
            s = pl.multiple_of(si * (groups * SUB), groups * SUB)
            acc = [jnp.zeros((SUB, w), F32) + cb_ref[...] for _ in range(groups)]
            for k in range(CONV_B):
                whole, part = divmod(HALO_B - (CONV_B - 1) + k, SUB)
                wk = wb_ref[k]
                for g in range(groups):
                    at = pl.ds(s + SUB * (whole + g), SUB)
                    acc[g] = acc[g] + wk * (buf_ref[at, :] if part == 0 else sh_ref[part - 1, at, :])
            for g in range(0, groups, 2):
                at = pl.ds(s + SUB * g, 2 * SUB)
                c = jnp.concatenate(acc[g:g + 2], axis=0)
                c_ref[at, :] = c
                dc = c - _mean(c)
                chat = dc * lax.rsqrt(_mean(dc * dc) + EPS_LN)
                cn = chat * lbg_ref[...] + lbb_ref[...]
                cs_ref[at, :] = (cn * _sigmoid(cn)).astype(BF16)
            return 0

        lax.fori_loop(0, tm // (groups * SUB), strip, 0)

    row = pl.BlockSpec((tm, w), lambda i: (i, 0))
    in_specs = [pl.BlockSpec((tm, 4 * w), lambda i: (i, 0)), _vec_spec(w), _vec_spec(w),
                pl.BlockSpec((GROUPS, CHUNK, CHUNK), lambda i: (0, 0, 0)), pl.BlockSpec((CHUNK, w), lambda i: (0, 0)),
                pl.BlockSpec((CONV_B, w), lambda i: (0, 0)), _vec_spec(w), _vec_spec(w), _vec_spec(w)]
    return _row_call(
        "branch_fwd", body, n_steps=t // tm, in_specs=in_specs, out_specs=[row, row, row],
        out_shape=[jax.ShapeDtypeStruct((t, w), BF16), jax.ShapeDtypeStruct((t, w), BF16), jax.ShapeDtypeStruct((t, w), F32)],
        scratch=[pltpu.VMEM((HALO_B, w), F32), pltpu.VMEM((HALO_B + tm, w), F32), pltpu.VMEM((tm, w), F32),
                 pltpu.VMEM((SUB - 1, HALO_B + tm - SUB, w), F32), pltpu.VMEM((CONV_B, SUB, w), F32)],
        args=(z, ln_v_g, ln_v_b, ws_m, bias_full, conv_w, conv_b, ln_b_g, ln_b_b), exch=exch)


def _branch_bwd(z, c_saved, dpa, dcs, dgates, ln_v_g, ln_v_b, ws_m, ws_mt, bias_full, conv_w, ln_b_g, ln_b_b, seq, exch=()):
    t = z.shape[0]
    w = 1024
    tm = min(TM_EW, seq)
    tiles_per_seq = seq // tm
    n_tiles = t // tm
    n_chunks = tm // CHUNK

    def body(z_ref, c_ref, dpa_ref, dcs_ref, dgt_ref, lvg_ref, lvb_ref, ws_ref, wst_ref, bias_ref, cw_ref,
             lbg_ref, lbb_ref,
             dz_ref, dlvg_ref, dlvb_ref, dws_ref, dbs_ref, dcw_ref, dcb_ref, dlbg_ref, dlbb_ref,
             carry_ref, glu_ref, dbuf_ref, mix_ref, dvn_ref, dbs_acc_ref, dglu_ref, sh_ref, wb_ref, dwacc_ref):
        i = pl.program_id(0)
        r = n_tiles - 1 - i

        @pl.when(i == 0)
        def _():
            for ref in (dlvg_ref, dlvb_ref, dws_ref, dbs_acc_ref, dwacc_ref, dcb_ref, dlbg_ref, dlbb_ref):
                ref[...] = jnp.zeros_like(ref)

        u = z_ref[:, 0:w].astype(F32)
        v = z_ref[:, w:2 * w].astype(F32)
        ug, dug = _gelu_and_grad(u)
        vg, dvg = _gelu_and_grad(v)
        dv0 = vg - _mean(vg)
        rstd_v = lax.rsqrt(_mean(dv0 * dv0) + EPS_LN)
        vhat = dv0 * rstd_v
        vn = (vhat * lvg_ref[...] + lvb_ref[...]).astype(BF16)
        dpa = dpa_ref[...].astype(F32)
        dmix = dpa * ug
        dmix_b = dmix.astype(BF16)
        for ci in range(n_chunks):
            rows = slice(ci * CHUNK, (ci + 1) * CHUNK)
            for g in range(GROUPS):
                cols = slice(g * CHUNK, (g + 1) * CHUNK)
                mix_ref[rows, cols] = lax.dot_general(ws_ref[g], vn[rows, cols], _NN, preferred_element_type=F32)
                dvn_ref[rows, cols] = lax.dot_general(wst_ref[g], dmix_b[rows, cols], _NN, preferred_element_type=F32)
                dws_ref[g] += lax.dot_general(dmix_b[rows, cols], vn[rows, cols], _NT, preferred_element_type=F32)
            mix_ref[rows, :] += bias_ref[...]
            dbs_acc_ref[...] += dmix[rows, :]
        dz_ref[:, 0:w] = (dpa * mix_ref[...] * dug).astype(BF16)
        dvn = dvn_ref[...]
        dlvg_ref[...] += _rowsum(dvn * vhat)
        dlvb_ref[...] += _rowsum(dvn)
        dvh = dvn * lvg_ref[...]
        dvg_in = rstd_v * (dvh - _mean(dvh) - vhat * _mean(dvh * vhat))
        dz_ref[:, w:2 * w] = (dvg_in * dvg).astype(BF16)

        c = c_ref[...]
        dc0 = c - _mean(c)
        rstd_c = lax.rsqrt(_mean(dc0 * dc0) + EPS_LN)
        chat = dc0 * rstd_c
        cn = chat * lbg_ref[...] + lbb_ref[...]
        sg = _sigmoid(cn)
        dcn = dcs_ref[...].astype(F32) * (sg * (1.0 + cn * (1.0 - sg)))
        dlbg_ref[...] += _rowsum(dcn * chat)
        dlbb_ref[...] += _rowsum(dcn)
        dch = dcn * lbg_ref[...]
        dc = rstd_c * (dch - _mean(dch) - chat * _mean(dch * chat))
        dcb_ref[...] += _rowsum(dc)

        a = z_ref[:, 2 * w:3 * w].astype(F32)
        gl = z_ref[:, 3 * w:4 * w].astype(F32)
        sgl = _sigmoid(gl)
        glu_ref[...] = a * sgl

        @pl.when(r % tiles_per_seq == tiles_per_seq - 1)
        def _():
            carry_ref[...] = jnp.zeros_like(carry_ref)

        dbuf_ref[0:tm, :] = dc
        dbuf_ref[tm:, :] = carry_ref[...]
        carry_ref[...] = dc[0:HALO_B, :]
        _fill_shifted(dbuf_ref, sh_ref)

        @pl.when(i == 0)
        def _():
            for k in range(CONV_B):
                wb_ref[k] = jnp.broadcast_to(cw_ref[k:k + 1, :], (SUB, w))

        groups = 2

        def strip(si, _):
            s = pl.multiple_of(si * (groups * SUB), groups * SUB)
            glu_rows = [glu_ref[pl.ds(s + SUB * g, SUB), :] for g in range(groups)]
            acc = [jnp.zeros((SUB, w), F32) for _ in range(groups)]
            for k in range(CONV_B):
                whole, part = divmod(CONV_B - 1 - k, SUB)
                wk = wb_ref[k]
                dw_part = jnp.zeros((SUB, w), F32)
                for g in range(groups):
                    at = pl.ds(s + SUB * (whole + g), SUB)
                    d_rows = dbuf_ref[at, :] if part == 0 else sh_ref[part - 1, at, :]
                    acc[g] = acc[g] + wk * d_rows
                    dw_part = dw_part + d_rows * glu_rows[g]
                dwacc_ref[k] += dw_part
            dglu_ref[pl.ds(s, groups * SUB), :] = jnp.concatenate(acc, axis=0)
            return 0

        lax.fori_loop(0, tm // (groups * SUB), strip, 0)
        dglu = dglu_ref[...]
        dz_ref[:, 2 * w:3 * w] = (dglu * sgl).astype(BF16)
        dz_ref[:, 3 * w:4 * w] = (dglu * a * sgl * (1.0 - sgl)).astype(BF16)
        dz_ref[:, 4 * w:6 * w] = dgt_ref[...]

        @pl.when(i == n_tiles - 1)
        def _():
            tri = lax.broadcasted_iota(jnp.int32, (CHUNK, CHUNK), 0) >= lax.broadcasted_iota(jnp.int32, (CHUNK, CHUNK), 1)
            lane = lax.broadcasted_iota(jnp.int32, (CHUNK, CHUNK), 1)
            dbs = jnp.zeros((CHUNK, CHUNK), F32)
            for g in range(GROUPS):
                dws_ref[g] = jnp.where(tri, dws_ref[g], 0.0)
                group_sum = jnp.sum(dbs_acc_ref[:, g * CHUNK:(g + 1) * CHUNK], axis=1, keepdims=True)
                dbs = jnp.where(lane == g, group_sum, dbs)
            dbs_ref[...] = dbs
            for k in range(CONV_B):
                dcw_ref[k:k + 1, :] = _rowsum(dwacc_ref[k])

    def rev(i):
        return n_tiles - 1 - i

    row = pl.BlockSpec((tm, w), lambda i: (rev(i), 0))
    full = lambda shape: pl.BlockSpec(shape, lambda i: (0,) * len(shape))
    in_specs = [pl.BlockSpec((tm, 4 * w), lambda i: (rev(i), 0)),
                row, row, row, pl.BlockSpec((tm, 2 * w), lambda i: (rev(i), 0)),
                _vec_spec(w), _vec_spec(w), full((GROUPS, CHUNK, CHUNK)), full((GROUPS, CHUNK, CHUNK)), full((CHUNK, w)),
                full((CONV_B, w)), _vec_spec(w), _vec_spec(w)]
    out_specs = [pl.BlockSpec((tm, 6 * w), lambda i: (rev(i), 0)), _vec_spec(w), _vec_spec(w), full((GROUPS, CHUNK, CHUNK)),
                 full((CHUNK, CHUNK)), full((CONV_B, w)), _vec_spec(w), _vec_spec(w), _vec_spec(w)]
    vec = jax.ShapeDtypeStruct((1, w), F32)
    out_shape = [jax.ShapeDtypeStruct((t, 6 * w), BF16), vec, vec, jax.ShapeDtypeStruct((GROUPS, CHUNK, CHUNK), F32),
                 jax.ShapeDtypeStruct((CHUNK, CHUNK), F32), jax.ShapeDtypeStruct((CONV_B, w), F32), vec, vec, vec]
    scratch = [pltpu.VMEM((HALO_B, w), F32), pltpu.VMEM((tm, w), F32), pltpu.VMEM((tm + HALO_B, w), F32),
               pltpu.VMEM((tm, w), F32), pltpu.VMEM((tm, w), F32), pltpu.VMEM((CHUNK, w), F32), pltpu.VMEM((tm, w), F32),
               pltpu.VMEM((SUB - 1, HALO_B + tm - SUB, w), F32), pltpu.VMEM((CONV_B, SUB, w), F32),
               pltpu.VMEM((CONV_B, SUB, w), F32)]
    return _row_call("branch_bwd", body, n_steps=n_tiles, in_specs=in_specs, out_specs=out_specs, out_shape=out_shape,
                     scratch=scratch, exch=exch,
                     args=(z, c_saved, dpa, dcs, dgates, ln_v_g, ln_v_b, ws_m, ws_mt, bias_full, conv_w, ln_b_g, ln_b_b))


def _conv3_window(prev8, x):
    win = jnp.concatenate([prev8, x], axis=0)
    n = x.shape[0]
    return [win[HALO_F - 2:HALO_F - 2 + n], win[HALO_F - 1:HALO_F - 1 + n], x]


def _ffn_mid_fwd(up0, conv_w, conv_b, seq):
    nb, t, f = up0.shape
    half = nb // 2
    tm = min(TM_FFN, seq)
    tiles_per_seq = seq // tm
    n_strips = tm // ROWS_F

    def body(up_ref, w_ref, b_ref, act_ref, upc_ref, hist_ref):
        i = pl.program_id(0)

        @pl.when(i % tiles_per_seq == 0)
        def _():
            hist_ref[...] = jnp.zeros_like(hist_ref)

        for j in range(half):
            jv = j + half
            wg = [w_ref[j, k:k + 1, :] for k in range(CONV_F)]
            wv = [w_ref[jv, k:k + 1, :] for k in range(CONV_F)]
            bg, bv = b_ref[j], b_ref[jv]

            def strip(c, carry):
                rows = pl.ds(pl.multiple_of(c * ROWS_F, ROWS_F), ROWS_F)
                xg = up_ref[j, rows, :].astype(F32)
                xv = up_ref[jv, rows, :].astype(F32)
                sg = _conv3_window(carry[0], xg)
                sv = _conv3_window(carry[1], xv)
                gate = bg + wg[0] * sg[0] + wg[1] * sg[1] + wg[2] * sg[2]
                val = bv + wv[0] * sv[0] + wv[1] * sv[1] + wv[2] * sv[2]
                act_ref[j, rows, :] = (_gelu(gate) * val).astype(BF16)
                upc_ref[j, rows, :] = gate.astype(BF16)
                upc_ref[jv, rows, :] = val.astype(BF16)
                return xg[ROWS_F - HALO_F:], xv[ROWS_F - HALO_F:]

            last = lax.fori_loop(0, n_strips, strip, (hist_ref[j], hist_ref[jv]))
            hist_ref[j] = last[0]
            hist_ref[jv] = last[1]

    return pl.pallas_call(
        body, grid=(t // tm,),
        in_specs=[pl.BlockSpec((nb, tm, f), lambda i: (0, i, 0)), pl.BlockSpec((nb, CONV_F, f), lambda i: (0, 0, 0)),
                  pl.BlockSpec((nb, 1, f), lambda i: (0, 0, 0))],
        out_specs=[pl.BlockSpec((half, tm, f), lambda i: (0, i, 0)), pl.BlockSpec((nb, tm, f), lambda i: (0, i, 0))],
        out_shape=[jax.ShapeDtypeStruct((half, t, f), BF16), jax.ShapeDtypeStruct((nb, t, f), BF16)],
        scratch_shapes=[pltpu.VMEM((nb, HALO_F, f), F32)],
        name="ffn_mid_fwd", compiler_params=_params(1))(up0, conv_w, conv_b)


def _ffn_mid_bwd(up0, upc, dact, conv_w, seq, exch=()):
    nb, t, f = up0.shape
    half = nb // 2
    tm = min(TM_FFN, seq)
    tiles_per_seq = seq // tm
    n_tiles = t // tm
    n_strips = tm // ROWS_F

    def body(up_ref, upc_ref, da_ref, w_ref, dup_ref, dw_ref, db_ref, carry_ref, dwacc_ref, dbacc_ref):
        i = pl.program_id(0)
        r = n_tiles - 1 - i

        @pl.when(i == 0)
        def _():
            dwacc_ref[...] = jnp.zeros_like(dwacc_ref)
            dbacc_ref[...] = jnp.zeros_like(dbacc_ref)

        @pl.when(r % tiles_per_seq == tiles_per_seq - 1)
        def _():
            carry_ref[...] = jnp.zeros_like(carry_ref)

        for j in range(half):
            jv = j + half
            wg = [w_ref[j, k:k + 1, :] for k in range(CONV_F)]
            wv = [w_ref[jv, k:k + 1, :] for k in range(CONV_F)]

            def strip(ci, carry):
                rows = pl.ds(pl.multiple_of((n_strips - 1 - ci) * ROWS_F, ROWS_F), ROWS_F)
                val = upc_ref[jv, rows, :].astype(F32)
                gg, dgg = _gelu_and_grad(upc_ref[j, rows, :].astype(F32))
                da = da_ref[j, rows, :].astype(F32)
                d_gate = da * val * dgg
                d_val = da * gg
                for blk, d, nxt, wk in ((j, d_gate, carry[0], wg), (jv, d_val, carry[1], wv)):
                    dbacc_ref[blk] += d
                    dwin = jnp.concatenate([d, nxt], axis=0)
                    shifted = [dwin[2:2 + ROWS_F], dwin[1:1 + ROWS_F], d]
                    x = up_ref[blk, rows, :].astype(F32)
                    for k in range(CONV_F):
                        dwacc_ref[blk, k] += shifted[k] * x
                    dx = wk[0] * shifted[0] + wk[1] * shifted[1] + wk[2] * shifted[2]
                    dup_ref[blk, rows, :] = dx.astype(BF16)
                return d_gate[0:HALO_F], d_val[0:HALO_F]

            carry = lax.fori_loop(0, n_strips, strip, (carry_ref[j], carry_ref[jv]))
            carry_ref[j] = carry[0]
            carry_ref[jv] = carry[1]

        @pl.when(i == n_tiles - 1)
        def _():
            for blk in range(nb):
                db_ref[blk] = _rowsum(dbacc_ref[blk])
                for k in range(CONV_F):
                    dw_ref[blk, k:k + 1, :] = _rowsum(dwacc_ref[blk, k])

    def rev(i):
        return n_tiles - 1 - i

    return _row_call(
        "ffn_mid_bwd", body, n_steps=n_tiles,
        in_specs=[pl.BlockSpec((nb, tm, f), lambda i: (0, rev(i), 0)), pl.BlockSpec((nb, tm, f), lambda i: (0, rev(i), 0)),
                  pl.BlockSpec((half, tm, f), lambda i: (0, rev(i), 0)),
                  pl.BlockSpec((nb, CONV_F, f), lambda i: (0, 0, 0))],
        out_specs=[pl.BlockSpec((nb, tm, f), lambda i: (0, rev(i), 0)), pl.BlockSpec((nb, CONV_F, f), lambda i: (0, 0, 0)),
                   pl.BlockSpec((nb, 1, f), lambda i: (0, 0, 0))],
        out_shape=[jax.ShapeDtypeStruct((nb, t, f), BF16), jax.ShapeDtypeStruct((nb, CONV_F, f), F32),
                   jax.ShapeDtypeStruct((nb, 1, f), F32)],
        scratch=[pltpu.VMEM((nb, HALO_F, f), F32), pltpu.VMEM((nb, CONV_F, ROWS_F, f), F32), pltpu.VMEM((nb, ROWS_F, f), F32)],
        args=(up0, upc, dact, conv_w), exch=exch)


def _all_gather(xs):
    n = len(xs)

    def body(*refs):
        x_refs, o_refs = refs[:n], refs[n:2 * n]
        send_sems, recv_sems, local_sems = refs[2 * n:]
        x, y, c = lax.axis_index("x"), lax.axis_index("y"), lax.axis_index("c")
        me, sibling = (x, y, c), (x, y, 1 - c)
        chips = [(1 - x, y), (x, 1 - y), (1 - x, 1 - y)]

        def slot(pos):
            return 4 * pos[0] + 2 * pos[1] + pos[2]

        def copy(a, k, block, to, src=None):
            dst = o_refs[a].at[slot(block)]
            return pltpu.make_async_remote_copy(
                src_ref=dst if src is None else src, dst_ref=dst, send_sem=send_sems.at[a * 7 + k],
                recv_sem=recv_sems.at[a * 7 + k], device_id=to, device_id_type=MESH)

        mine = [pltpu.make_async_copy(x_refs[a], o_refs[a].at[slot(me)], local_sems.at[a]) for a in range(n)]
        for cp in mine:
            cp.start()
        first = []
        for a in range(n):
            first.append(copy(a, 0, me, sibling, src=x_refs[a]))
            first += [copy(a, 1 + j, me, (*chip, c), src=x_refs[a]) for j, chip in enumerate(chips)]
        for cp in first:
            cp.start()
        passed = []
        for j, chip in enumerate(chips):
            for a in range(n):
                copy(a, 1 + j, (*chip, c), me).wait_recv()
                cp = copy(a, 4 + j, (*chip, c), sibling)
                cp.start()
                passed.append(cp)
        for a in range(n):
            copy(a, 0, sibling, me).wait_recv()
        for j, chip in enumerate(chips):
            for a in range(n):
                copy(a, 4 + j, (*chip, 1 - c), me).wait_recv()
        for cp in first + passed:
            cp.wait_send()
        for cp in mine:
            cp.wait()

    return pl.pallas_call(
        body, in_specs=[HBM_SPEC] * n, out_specs=[HBM_SPEC] * n,
        out_shape=[jax.ShapeDtypeStruct((N_DEV,) + v.shape, v.dtype) for v in xs],
        scratch_shapes=[pltpu.SemaphoreType.DMA((7 * n,)), pltpu.SemaphoreType.DMA((7 * n,)), pltpu.SemaphoreType.DMA((n,))],
        name="all_gather_weights")(*xs)


def _adamw_update(g, w, m, v):
    c1 = 1.0 - ADAM_B1 ** ADAM_STEP
    c2 = 1.0 - ADAM_B2 ** ADAM_STEP
    m_new = ADAM_B1 * m + (1.0 - ADAM_B1) * g
    v_new = ADAM_B2 * v + (1.0 - ADAM_B2) * (g * g)
    delta = -ADAM_LR * ((m_new / c1) / (jnp.sqrt(v_new / c2) + ADAM_EPS) + ADAM_WD * w)
    return m_new, v_new, delta


def _adamw(name, parts, w, m, v, rows_per_step):
    r, c = w.shape
    tr = r if r <= rows_per_step else (rows_per_step if r % rows_per_step == 0 else r // 2)

    def body(p_ref, w_ref, m_ref, v_ref, g_ref, d_ref, mo_ref, vo_ref):
        g = p_ref[0].astype(F32)
        for s in range(1, N_DEV):
            g = g + p_ref[s].astype(F32)
        m_new, v_new, delta = _adamw_update(g, w_ref[...], m_ref[...], v_ref[...])
        g_ref[...] = g
        mo_ref[...] = m_new
        vo_ref[...] = v_new
        d_ref[...] = delta

    row = pl.BlockSpec((tr, c), lambda i: (i, 0))
    out = jax.ShapeDtypeStruct((r, c), F32)
    return pl.pallas_call(body, grid=(r // tr,), in_specs=[pl.BlockSpec((N_DEV, tr, c), lambda i: (0, i, 0)), row, row, row],
                          out_specs=[row, row, row, row], out_shape=[out, out, out, out], name=name,
                          compiler_params=_params(1))(parts, w, m, v)


_VEC_NAMES = ("ln_v_g", "ln_v_b", "b_s", "conv_b_b", "ln_b_g", "ln_b_b", "g_ffn", "g_pg", "g_ple", "g_final")
_LOSS_ROW = len(_VEC_NAMES)
_REP_LAYOUT = dict({k: (i, 1, 1024) for i, k in enumerate(_VEC_NAMES)}, ffn_conv_b=(16, 8, 704), w_s=(24, 128, 1024))


def _pack_replicated_grads(d, loss_row):
    head = jnp.concatenate([d[k].reshape(1, 1024) for k in _VEC_NAMES] + [loss_row], axis=1).reshape(_LOSS_ROW + 1, 1024)
    return jnp.concatenate([jnp.pad(head, ((0, 16 - _LOSS_ROW - 1), (0, 0))),
                            jnp.pad(d["ffn_conv_b"].reshape(8, 704), ((0, 0), (0, 1024 - 704))),
                            d["w_s"].reshape(128, 1024)], axis=0)


def _rows8(vec):
    return jnp.pad(vec.reshape(1, 1024), ((0, 7), (0, 0)))


def _adamw_small(name, parts, items, loss_row=None):
    n = len(items)

    def body(p_ref, *refs):
        ins, outs = refs[:3 * n], refs[3 * n:]
        for a, (row, rows, cols) in enumerate(it[:3] for it in items):
            g = p_ref[0, row:row + rows, 0:cols]
            for s in range(1, N_DEV):
                g = g + p_ref[s, row:row + rows, 0:cols]
            m_new, v_new, delta = _adamw_update(g, ins[3 * a][...], ins[3 * a + 1][...], ins[3 * a + 2][...])
            for ref, val in zip(outs[4 * a:4 * a + 4], (g, delta, m_new, v_new)):
                ref[...] = val
        if loss_row is not None:
            total = p_ref[0, loss_row:loss_row + 1, 0:128]
            for s in range(1, N_DEV):
                total = total + p_ref[s, loss_row:loss_row + 1, 0:128]
            outs[-1][...] = total

    out_shape = [jax.ShapeDtypeStruct((rows, cols), F32) for _, rows, cols, *_ in items for _ in range(4)]
    if loss_row is not None:
        out_shape.append(jax.ShapeDtypeStruct((1, 128), F32))
    flat = pl.pallas_call(body, out_shape=out_shape, name=name,
                          compiler_params=pltpu.CompilerParams(vmem_limit_bytes=VMEM_LIMIT))(
        parts, *[arr for it in items for arr in it[3:]])
    res = [tuple(flat[4 * a:4 * a + 4]) for a in range(n)]
    return res + [flat[-1]] if loss_row is not None else res


def _pack_sharded_small(conv_b_w, ffn_conv_w):
    lead = conv_b_w.shape[:-2]
    pad0 = [(0, 0)] * len(lead)
    a = jnp.pad(conv_b_w, pad0 + [(0, 1), (0, 0)])
    b = jnp.pad(ffn_conv_w.reshape(lead + (CONV_F * 704,)), pad0 + [(0, 24 * 128 - CONV_F * 704)]).reshape(lead + (24, 128))
    return jnp.concatenate([a, b], axis=-2)


def _unpack_sharded_small(pk):
    lead = pk.shape[:-2]
    conv_b_w = pk[..., 0:CONV_B, :]
    ffn = pk[..., 32:56, :].reshape(lead + (24 * 128,))[..., :CONV_F * 704].reshape(lead + (CONV_F, 704))
    return conv_b_w, ffn


_WEIGHTS = ("g_mix", "w_in", "ln_v_g", "ln_v_b", "w_s", "b_s", "w_a_out", "conv_b_w", "conv_b_b", "ln_b_g", "ln_b_b",
            "w_b_out", "w_o", "g_ffn", "w_up", "ffn_conv_w", "ffn_conv_b", "w_down", "g_pg", "w_pg", "w_ple", "g_ple",
            "g_final")
_BIG = ("w_in", "w_a_out", "w_b_out", "w_o", "w_up", "w_down", "w_pg", "w_ple")


def kernel(x, p, g_mix, w_in, ln_v_g, ln_v_b, w_s, b_s, w_a_out, conv_b_w, conv_b_b, ln_b_g, ln_b_b, w_b_out, w_o, g_ffn, w_up, ffn_conv_w, ffn_conv_b, w_down, g_pg, w_pg, w_ple, g_ple, g_final, loss_target, m_g_mix, m_w_in, m_ln_v_g, m_ln_v_b, m_w_s, m_b_s, m_w_a_out, m_conv_b_w, m_conv_b_b, m_ln_b_g, m_ln_b_b, m_w_b_out, m_w_o, m_g_ffn, m_w_up, m_ffn_conv_w, m_ffn_conv_b, m_w_down, m_g_pg, m_w_pg, m_w_ple, m_g_ple, m_g_final, v_g_mix, v_w_in, v_ln_v_g, v_ln_v_b, v_w_s, v_b_s, v_w_a_out, v_conv_b_w, v_conv_b_b, v_ln_b_g, v_ln_b_b, v_w_b_out, v_w_o, v_g_ffn, v_w_up, v_ffn_conv_w, v_ffn_conv_b, v_w_down, v_g_pg, v_w_pg, v_w_ple, v_g_ple, v_g_final):
    local = dict(locals())
    wts = {k: local[k] for k in _WEIGHTS}
    mom = {k: local["m_" + k] for k in _WEIGHTS}
    var = {k: local["v_" + k] for k in _WEIGHTS}
    shapes = {k: wts[k].shape for k in _WEIGHTS}

    bsz, seq, d = x.shape
    t = bsz * seq
    x0 = x.reshape(t, d)
    p0 = p.reshape(t, p.shape[-1])
    target = loss_target.reshape(t, d)
    tm = min(TM_MM, t)
    tm_wide = min(TM_WIDE, t)
    tt = min(TT_MM, t)
    n_row = t // tm
    n_tok = t // tt

    def sq(a):
        return a.reshape(a.shape[1:])

    shard = {k: sq(wts[k]).astype(BF16) for k in _BIG}
    w_in3, small8 = _all_gather([shard["w_in"], _pack_sharded_small(sq(conv_b_w), sq(ffn_conv_w))])
    conv_b_w8, ffn_conv_w8 = _unpack_sharded_small(small8)
    conv_w_full = conv_b_w8.transpose(1, 0, 2).reshape(CONV_B, N_DEV * conv_b_w8.shape[-1])
    n_in = w_in3.shape[2]
    f_dev = shard["w_up"].shape[1]
    n_blk, f_blk = N_DEV // 2, 2 * f_dev

    def pair_blocks(a):
        return a.reshape(n_blk, 2, a.shape[1], f_dev).transpose(0, 2, 1, 3).reshape(n_blk, a.shape[1], f_blk)

    def unpair_blocks(a):
        return a.reshape(n_blk, a.shape[1], 2, f_dev).transpose(0, 2, 1, 3).reshape(N_DEV, a.shape[1], f_dev)

    ws_m = jnp.where(jnp.tril(jnp.ones((CHUNK, CHUNK), bool))[None], sq(w_s), 0.0).astype(BF16)
    ws_mt = jnp.swapaxes(ws_m, 1, 2)
    bias_full = jnp.broadcast_to(sq(b_s).T[:, :, None], (CHUNK, GROUPS, CHUNK)).reshape(CHUNK, GROUPS * CHUNK)
    ffn_b = ffn_conv_b.reshape(n_blk, 1, f_blk)
    ffn_w = pair_blocks(ffn_conv_w8)

    h1 = _rms_fwd("rms_mix", x0, g_mix)
    z, (wa3, wb3, wo3) = _matmul(
        "mm_in", h1, w_in3, dims=_NN, grid=(N_DEV, t // tm_wide, 1),
        a_spec=pl.BlockSpec((tm_wide, d), lambda j, i, k: (i, 0)),
        b_spec=pl.BlockSpec((None, d, n_in), lambda j, i, k: (j, 0, 0)),
        o_spec=pl.BlockSpec((tm_wide, n_in), lambda j, i, k: (i, j)), acc_shape=(tm_wide, n_in),
        out_shape=jax.ShapeDtypeStruct((t, N_DEV * n_in), BF16),
        exch=[("gather", shard[k]) for k in ("w_a_out", "w_b_out", "w_o")])
    w_a = wa3.reshape(-1, d)
    w_b = wb3.reshape(-1, d)
    w_om = wo3.reshape(-1, d)
    (pa, cs, c_saved), (w_up3, wd3, wpg3, wple3) = _branch_fwd(
        z, ln_v_g, ln_v_b, ws_m, bias_full, conv_w_full, conv_b_b, ln_b_g, ln_b_b, seq,
        exch=[("gather", shard[k]) for k in ("w_up", "w_down", "w_pg", "w_ple")])
    w_pgm = wpg3.reshape(-1, d)
    w_upb = pair_blocks(w_up3)
    w_db = wd3.reshape(n_blk // 2, f_blk, d)
    w_plem = wple3.transpose(1, 0, 2).reshape(wple3.shape[1], d)
    ya = _mm_rows("mm_a_out", pa, w_a, dims=_NN, tm=tm, out_dtype=BF16)
    yb, merged = _mm_rows("mm_b_out", cs, w_b, dims=_NN, tm=tm, epi=_epi_merge_fwd(z, ya, tm))
    x1, h2 = _mm_rows("mm_o", merged, w_om, dims=_NN, tm=tm, epi=_epi_residual_rms(x0, g_ffn, tm))
    up0 = _matmul("mm_up", h2, w_upb, dims=_NN, grid=(n_blk, t // tm_wide, 1),
                  a_spec=pl.BlockSpec((tm_wide, d), lambda j, i, k: (i, 0)),
                  b_spec=pl.BlockSpec((None, d, f_blk), lambda j, i, k: (j, 0, 0)),
                  o_spec=pl.BlockSpec((None, tm_wide, f_blk), lambda j, i, k: (j, i, 0)), acc_shape=(tm_wide, f_blk),
                  out_shape=jax.ShapeDtypeStruct((n_blk, t, f_blk), BF16))
    act, upc = _ffn_mid_fwd(up0, ffn_w, ffn_b, seq)
    x2, hq = _matmul("mm_down", act, w_db, dims=_NN, grid=(n_row, 1, 1), split=n_blk // 2,
                     a_spec=pl.BlockSpec((n_blk // 2, tm, f_blk), lambda i, j, k: (0, i, 0)),
                     b_spec=pl.BlockSpec((n_blk // 2, f_blk, d), lambda i, j, k: (0, 0, 0)),
                     acc_shape=(tm, d), epi=_epi_residual_rms(x1, g_pg, tm))
    r = _mm_rows("mm_ple", p0, w_plem, dims=_NN, tm=tm, out_dtype=F32)

    dx3, dq, dr, loss_v, dg_final, dg_ple = _mm_rows(
        "mm_pg", hq, w_pgm, dims=_NN, tm=tm // 2, epi=_epi_head(x2, r, target, g_ple, g_final.reshape(1, d), tm // 2))

    recv = {}
    gw_pg = _mm_wgrad("wg_pg", hq, dq, out_dtype=BF16, tt=tt).reshape(wpg3.shape)
    dw_ple = _mm_wgrad("wg_ple", p0, dr, out_dtype=BF16, tt=tt)
    gw_ple = dw_ple.reshape(dw_ple.shape[0], N_DEV, -1).transpose(1, 0, 2)
    dx2, dx2b, dg_pg = _mm_rows("mm_pg_t", dq, w_pgm, dims=_NT, tm=tm, epi=_epi_rms_bwd(x2, g_pg, dx3, tm, want_bf16=True))

    dact, (recv["w_pg"], recv["w_ple"]) = _matmul(
        "mm_down_t", dx2b, w_db, dims=_NT, grid=(n_blk // 2, n_row, 1),
        a_spec=pl.BlockSpec((tm, d), lambda j, i, k: (i, 0)),
        b_spec=pl.BlockSpec((None, f_blk, d), lambda j, i, k: (j, 0, 0)),
        o_spec=pl.BlockSpec((None, tm, f_blk), lambda j, i, k: (j, i, 0)), acc_shape=(tm, f_blk),
        out_shape=jax.ShapeDtypeStruct((n_blk // 2, t, f_blk), BF16),
        exch=[("scatter", gw_pg), ("scatter", gw_ple)])
    gw_down = _matmul("wg_down", act, dx2b, dims=_TN, grid=(n_blk // 2, 1, n_tok),
                      a_spec=pl.BlockSpec((None, tt, f_blk), lambda j, i, k: (j, k, 0)),
                      b_spec=pl.BlockSpec((tt, d), lambda j, i, k: (k, 0)),
                      o_spec=pl.BlockSpec((None, f_blk, d), lambda j, i, k: (j, 0, 0)), acc_shape=(f_blk, d),
                      out_shape=jax.ShapeDtypeStruct((n_blk // 2, f_blk, d), BF16)).reshape(wd3.shape)
    (d_up0, dffn_wb, dffn_b), (recv["w_down"],) = _ffn_mid_bwd(up0, upc, dact, ffn_w, seq, exch=[("scatter", gw_down)])
    dffn_w8 = unpair_blocks(dffn_wb)
    gw_up = _matmul("wg_up", h2, d_up0, dims=_TN, grid=(n_blk, 1, n_tok),
                    a_spec=pl.BlockSpec((tt, d), lambda j, i, k: (k, 0)),
                    b_spec=pl.BlockSpec((None, tt, f_blk), lambda j, i, k: (j, k, 0)),
                    o_spec=pl.BlockSpec((None, d, f_blk), lambda j, i, k: (j, 0, 0)), acc_shape=(d, f_blk),
                    out_shape=jax.ShapeDtypeStruct((n_blk, d, f_blk), BF16))
    gw_up = unpair_blocks(gw_up)
    (dx1, dx1b, dg_ffn), (recv["w_up"],) = _matmul(
        "mm_up_t", d_up0, w_upb, dims=_NT, grid=(n_row, 1, n_blk),
        a_spec=pl.BlockSpec((None, tm, f_blk), lambda i, j, k: (k, i, 0)),
        b_spec=pl.BlockSpec((None, d, f_blk), lambda i, j, k: (k, 0, 0)),
        acc_shape=(tm, d), epi=_epi_rms_bwd(x1, g_ffn, dx2, tm, want_bf16=True), exch=[("scatter", gw_up)])

    dya, dyb, dgates = _mm_rows("mm_o_t", dx1b, w_om, dims=_NT, tm=tm, epi=_epi_merge_bwd(z, ya, yb, tm))
    gw_o = _mm_wgrad("wg_o", merged, dx1b, out_dtype=BF16, tt=tt).reshape(wo3.shape)
    dpa = _mm_rows("mm_a_out_t", dya, w_a, dims=_NT, tm=tm, out_dtype=BF16)
    dcs = _mm_rows("mm_b_out_t", dyb, w_b, dims=_NT, tm=tm, out_dtype=BF16)
    gw_a = _mm_wgrad("wg_a_out", pa, dya, out_dtype=BF16, tt=tt).reshape(wa3.shape)
    gw_b = _mm_wgrad("wg_b_out", cs, dyb, out_dtype=BF16, tt=tt).reshape(wb3.shape)
    (dz, dlvg, dlvb, dws, dbs_full, dconv_w, dconv_b, dlbg, dlbb), (recv["w_o"], recv["w_a_out"], recv["w_b_out"]) = _branch_bwd(
        z, c_saved, dpa, dcs, dgates, ln_v_g, ln_v_b, ws_m, ws_mt, bias_full, conv_w_full, ln_b_g, ln_b_b, seq,
        exch=[("scatter", gw_o), ("scatter", gw_a), ("scatter", gw_b)])
    db_s = dbs_full[:, :GROUPS].T
    rep_partial = _pack_replicated_grads(
        dict(ln_v_g=dlvg, ln_v_b=dlvb, b_s=db_s, conv_b_b=dconv_b, ln_b_g=dlbg, ln_b_b=dlbb, g_ffn=dg_ffn,
             g_pg=dg_pg, g_ple=dg_ple, g_final=dg_final, ffn_conv_b=dffn_b, w_s=dws), loss_v)
    dconv_w8 = dconv_w.reshape(CONV_B, N_DEV, -1).transpose(1, 0, 2)
    small_partial = _pack_sharded_small(dconv_w8, dffn_w8)
    gw_in, (recv_small, recv_rep) = _matmul(
        "wg_in", h1, dz, dims=_TN, grid=(N_DEV, 1, n_tok),
        a_spec=pl.BlockSpec((tt, d), lambda j, i, k: (k, 0)),
        b_spec=pl.BlockSpec((tt, n_in), lambda j, i, k: (k, j)),
        o_spec=pl.BlockSpec((None, d, n_in), lambda j, i, k: (j, 0, 0)), acc_shape=(d, n_in),
        out_shape=jax.ShapeDtypeStruct((N_DEV, d, n_in), BF16),
        exch=[("scatter", small_partial), ("gather", rep_partial)])
    (grad_x, dg_mix), (recv["w_in"],) = _matmul(
        "mm_in_t", dz, w_in3, dims=_NT, grid=(n_row, 1, N_DEV // 2), split=2,
        a_spec=pl.BlockSpec((tm, 2 * n_in), lambda i, j, k: (i, k)),
        b_spec=pl.BlockSpec((2, d, n_in), lambda i, j, k: (k, 0, 0)),
        acc_shape=(tm, d), epi=_epi_rms_bwd(x0, g_mix, dx1, tm, want_bf16=False), exch=[("scatter", gw_in)])
    (recv_g_mix,) = _exchange("exchange_g_mix", [("gather", _rows8(dg_mix))])

    grads, deltas, new_m, new_v = {}, {}, {}, {}
    by_kind = (grads, deltas, new_m, new_v)

    def two_d(a):
        a = sq(a)
        return a.reshape(-1, a.shape[-1])

    for k in _BIG:
        parts = recv[k].reshape(N_DEV, -1, recv[k].shape[-1])
        outs = _adamw("adamw_" + k, parts, two_d(wts[k]), two_d(mom[k]), two_d(var[k]), ADAMW_ROWS)
        for tgt, o in zip(by_kind, outs):
            tgt[k] = o.reshape(shapes[k])

    small = [_pack_sharded_small(sq(s["conv_b_w"]), sq(s["ffn_conv_w"])) for s in (wts, mom, var)]
    for tgt, o in zip(by_kind, _adamw("adamw_conv", recv_small, small[0], small[1], small[2], 56)):
        cw, fw = _unpack_sharded_small(o)
        tgt["conv_b_w"] = cw.reshape(shapes["conv_b_w"])
        tgt["ffn_conv_w"] = fw.reshape(shapes["ffn_conv_w"])

    names = list(_REP_LAYOUT)
    items = [_REP_LAYOUT[k] + tuple(s[k].reshape(_REP_LAYOUT[k][1:]) for s in (wts, mom, var)) for k in names]
    *rep_outs, loss_sum = _adamw_small("adamw_replicated", recv_rep, items, loss_row=_LOSS_ROW)
    for k, outs in zip(names, rep_outs):
        for tgt, o in zip(by_kind, outs):
            tgt[k] = o.reshape(shapes[k])
    loss = loss_sum[0, 0]

    g_mix_item = (0, 1, 1024) + tuple(s["g_mix"].reshape(1, 1024) for s in (wts, mom, var))
    for tgt, o in zip(by_kind, _adamw_small("adamw_g_mix", recv_g_mix, [g_mix_item])[0]):
        tgt["g_mix"] = o.reshape(shapes["g_mix"])

    return (loss, grad_x.reshape(x.shape), *[grads[k] for k in _WEIGHTS], *[deltas[k] for k in _WEIGHTS],
            *[new_m[k] for k in _WEIGHTS], *[new_v[k] for k in _WEIGHTS])
```

```python
import math

import jax
import jax.numpy as jnp
from jax import lax
from jax.experimental import pallas as pl
from jax.experimental.pallas import tpu as pltpu

F32 = jnp.float32
BF16 = jnp.bfloat16

N_DEV = 8
EPS_RMS = 1e-6
EPS_LN = 1e-5
CHUNK = 128
GROUPS = 8
CONV_B = 31
CONV_F = 3
HALO_B = 32
HALO_F = 8
ROWS_F = 16
SUB = 8

ADAM_LR = 0.001
ADAM_B1 = 0.9
ADAM_B2 = 0.999
ADAM_EPS = 1e-08
ADAM_WD = 0.01
ADAM_STEP = 10

VMEM_LIMIT = 56 * 1024 * 1024
TM_MM = 1024
TM_WIDE = 2048
TT_MM = 2048
TM_EW = 256
TM_FFN = 512
ADAMW_ROWS = 512
EPI_STRIP = 256

_NN = (((1,), (0,)), ((), ()))
_NT = (((1,), (1,)), ((), ()))
_TN = (((0,), (0,)), ((), ()))
MESH = pl.DeviceIdType.MESH
HBM_SPEC = pl.BlockSpec(memory_space=pltpu.HBM)


def _params(n_axes):
    return pltpu.CompilerParams(dimension_semantics=("arbitrary",) * n_axes, vmem_limit_bytes=VMEM_LIMIT)


def _gelu(x):
    k = math.sqrt(2.0 / math.pi)
    return 0.5 * x * (1.0 + jnp.tanh(k * (x + 0.044715 * (x * x * x))))


def _gelu_and_grad(x):
    k = math.sqrt(2.0 / math.pi)
    x2 = x * x
    t = jnp.tanh(k * (x + 0.044715 * (x2 * x)))
    g = 0.5 * x * (1.0 + t)
    dg = 0.5 * (1.0 + t) + 0.5 * x * (1.0 - t * t) * (k * (1.0 + 3.0 * 0.044715 * x2))
    return g, dg


def _sigmoid(x):
    return 1.0 / (1.0 + jnp.exp(-x))


def _rowsum(x):
    return jnp.sum(x, axis=0, keepdims=True)


def _mean(x):
    return jnp.mean(x, axis=-1, keepdims=True)


def _exchange_io(exch):
    n = len(exch)
    out_shape = [jax.ShapeDtypeStruct(v.shape if kind == "scatter" else (N_DEV,) + v.shape, v.dtype) for kind, v in exch]
    scratch = [pltpu.SemaphoreType.DMA((7 * n,)), pltpu.SemaphoreType.DMA((7 * n,)), pltpu.SemaphoreType.DMA((n,))] if n else []
    return [HBM_SPEC] * n, [HBM_SPEC] * n, out_shape, scratch


def _exchange_step(kinds, x_refs, o_refs, send_sems, recv_sems, local_sems):
    n = len(kinds)
    x, y, c = lax.axis_index("x"), lax.axis_index("y"), lax.axis_index("c")
    me = 4 * x + 2 * y + c

    def src(a, to_slot):
        return x_refs[a].at[to_slot] if kinds[a] == "scatter" else x_refs[a]

    mine = [pltpu.make_async_copy(src(a, me), o_refs[a].at[me], local_sems.at[a]) for a in range(n)]
    sends, recvs = [], []
    for m in range(1, N_DEV):
        mx, my, mc = (m >> 2) & 1, (m >> 1) & 1, m & 1
        px, py, pc = (1 - x if mx else x), (1 - y if my else y), (1 - c if mc else c)
        peer = 4 * px + 2 * py + pc
        for a in range(n):
            k = a * 7 + m - 1
            sends.append(pltpu.make_async_remote_copy(
                src_ref=src(a, peer), dst_ref=o_refs[a].at[me], send_sem=send_sems.at[k], recv_sem=recv_sems.at[k],
                device_id=(px, py, pc), device_id_type=MESH))
            recvs.append(pltpu.make_async_remote_copy(
                src_ref=src(a, peer), dst_ref=o_refs[a].at[peer], send_sem=send_sems.at[k], recv_sem=recv_sems.at[k],
                device_id=(px, py, pc), device_id_type=MESH))

    def start():
        for cp in mine + sends:
            cp.start()

    def finish():
        for cp in recvs:
            cp.wait_recv()
        for cp in sends:
            cp.wait_send()
        for cp in mine:
            cp.wait()

    return start, finish


def _exchange(name, exch):
    n = len(exch)
    kinds = [k for k, _ in exch]
    in_specs, out_specs, out_shape, scratch = _exchange_io(exch)

    def body(*refs):
        start, finish = _exchange_step(kinds, refs[:n], refs[n:2 * n], *refs[2 * n:])
        start()
        finish()

    return pl.pallas_call(body, in_specs=in_specs, out_specs=out_specs, out_shape=out_shape, scratch_shapes=scratch,
                          name=name)(*[v for _, v in exch])


class _Epilogue:
    def __init__(self, fn, ins=(), in_specs=(), out_specs=(), out_shape=(), strip=None):
        self.fn, self.ins, self.in_specs = fn, list(ins), list(in_specs)
        self.out_specs, self.out_shape = list(out_specs), list(out_shape)
        self.strip = strip


def _matmul(name, a, b, *, dims, grid, a_spec, b_spec, acc_shape, o_spec=None, out_shape=None, epi=None, exch=(), split=1):
    nk = grid[2]
    plain = epi is None
    if plain:
        def store(acc, ins, outs, i):
            outs[0][...] = acc.astype(outs[0].dtype)
        epi = _Epilogue(store, out_specs=[o_spec], out_shape=[out_shape])
    n_in = 2 + len(epi.ins)
    n_out = len(epi.out_specs)
    n_ex = len(exch)
    kinds = [k for k, _ in exch]
    ex_in, ex_out, ex_shape, ex_scratch = _exchange_io(exch)

    def body(*refs):
        a_ref, b_ref = refs[:2]
        step0 = pl.program_id(0)
        epi_ins, rest = refs[2:n_in], refs[n_in:]
        x_refs, rest = rest[:n_ex], rest[n_ex:]
        outs, rest = rest[:n_out], rest[n_out:]
        o_refs, scr = rest[:n_ex], rest[n_ex:]
        if n_ex:
            pid = [pl.program_id(ax) for ax in range(3)]
            ex_start, ex_finish = _exchange_step(kinds, x_refs, o_refs, *scr[len(scr) - 3:])
            pl.when((pid[0] == 0) & (pid[1] == 0) & (pid[2] == 0))(ex_start)
        if split == 1:
            part = lax.dot_general(a_ref[...].astype(BF16), b_ref[...].astype(BF16), dims, preferred_element_type=F32)
        else:
            kk = b_ref.shape[-1]
            part = None
            for s in range(split):
                a_s = a_ref[s] if len(a_ref.shape) == 3 else a_ref[:, s * kk:(s + 1) * kk]
                p_s = lax.dot_general(a_s.astype(BF16), b_ref[s].astype(BF16), dims, preferred_element_type=F32)
                part = p_s if part is None else part + p_s

        def run_epilogue(rows_of_acc):
            rows = acc_shape[0]
            strip = rows if epi.strip is None else min(epi.strip, rows)
            for s in range(0, rows, strip):
                def view(ref):
                    return ref.at[pl.ds(s, strip)] if ref.shape[0] == rows else ref
                first = (step0 == 0) if s == 0 else False
                epi.fn(rows_of_acc(s, strip), [view(r) for r in epi_ins], [view(r) for r in outs], first)

        if nk == 1:
            run_epilogue(lambda s, n: part[s:s + n])
        else:
            acc_ref = scr[0]
            k = pl.program_id(2)

            @pl.when(k == 0)
            def _():
                acc_ref[...] = part

            @pl.when(k > 0)
            def _():
                acc_ref[...] += part

            @pl.when(k == nk - 1)
            def _():
                run_epilogue(lambda s, n: acc_ref[pl.ds(s, n), :])
        if n_ex:
            pl.when((pid[0] == grid[0] - 1) & (pid[1] == grid[1] - 1) & (pid[2] == grid[2] - 1))(ex_finish)

    scratch = ([pltpu.VMEM(acc_shape, F32)] if nk > 1 else []) + ex_scratch
    res = pl.pallas_call(body, grid=grid, in_specs=[a_spec, b_spec] + epi.in_specs + ex_in,
                         out_specs=epi.out_specs + ex_out, out_shape=epi.out_shape + ex_shape, scratch_shapes=scratch,
                         name=name, compiler_params=_params(3))(a, b, *epi.ins, *[v for _, v in exch])
    main = res[0] if plain else res[:n_out]
    return (main, res[n_out:]) if n_ex else main


def _mm_rows(name, a, w, *, dims, tm, out_dtype=None, epi=None):
    t, k = a.shape
    n = w.shape[1] if dims == _NN else w.shape[0]
    tm = min(tm, t)
    return _matmul(name, a, w, dims=dims, grid=(t // tm, 1, 1),
                   a_spec=pl.BlockSpec((tm, k), lambda i, j, kk: (i, 0)),
                   b_spec=pl.BlockSpec(w.shape, lambda i, j, kk: (0, 0)),
                   o_spec=pl.BlockSpec((tm, n), lambda i, j, kk: (i, 0)), acc_shape=(tm, n),
                   out_shape=jax.ShapeDtypeStruct((t, n), out_dtype) if epi is None else None, epi=epi)


def _mm_wgrad(name, a, b, *, out_dtype, tt):
    t, m = a.shape
    n = b.shape[1]
    tt = min(tt, t)
    return _matmul(name, a, b, dims=_TN, grid=(1, 1, t // tt),
                   a_spec=pl.BlockSpec((tt, m), lambda i, j, kk: (kk, 0)),
                   b_spec=pl.BlockSpec((tt, n), lambda i, j, kk: (kk, 0)),
                   o_spec=pl.BlockSpec((m, n), lambda i, j, kk: (0, 0)),
                   acc_shape=(m, n), out_shape=jax.ShapeDtypeStruct((m, n), out_dtype))


def _row3(tm, d, col=0):
    return pl.BlockSpec((tm, d), lambda i, j, k: (i, col))


def _vec3(d):
    return pl.BlockSpec((1, d), lambda i, j, k: (0, 0))


def _accumulate_over_rows(ref, part, first):
    if first is False:
        ref[...] += part
        return

    @pl.when(first)
    def _():
        ref[...] = part + jnp.zeros_like(ref)

    @pl.when(jnp.logical_not(first))
    def _():
        ref[...] += part


def _epi_residual_rms(res, g, tm):
    t, d = res.shape

    def fn(acc, ins, outs, i):
        res_ref, g_ref = ins
        xv = acc + res_ref[...]
        outs[0][...] = xv
        rstd = lax.rsqrt(_mean(xv * xv) + EPS_RMS)
        outs[1][...] = ((xv * rstd) * g_ref[...]).astype(BF16)

    return _Epilogue(fn, strip=EPI_STRIP, ins=[res, g], in_specs=[_row3(tm, d), _vec3(d)], out_specs=[_row3(tm, d), _row3(tm, d)],
                     out_shape=[jax.ShapeDtypeStruct((t, d), F32), jax.ShapeDtypeStruct((t, d), BF16)])


def _epi_rms_bwd(x, g, dres, tm, *, want_bf16):
    t, d = x.shape

    def fn(acc, ins, outs, i):
        x_ref, g_ref, dres_ref = ins
        xv = x_ref[...]
        rstd = lax.rsqrt(_mean(xv * xv) + EPS_RMS)
        nrm = xv * rstd
        dn = acc * g_ref[...]
        dx = dres_ref[...] + rstd * (dn - nrm * _mean(dn * nrm))
        outs[0][...] = dx
        if want_bf16:
            outs[1][...] = dx.astype(BF16)
        _accumulate_over_rows(outs[-1], _rowsum(acc * nrm), i)

    row = _row3(tm, d)
    n_dx = 2 if want_bf16 else 1
    return _Epilogue(fn, strip=EPI_STRIP, ins=[x, g, dres], in_specs=[row, _vec3(d), row], out_specs=[row] * n_dx + [_vec3(d)],
                     out_shape=[jax.ShapeDtypeStruct((t, d), F32)] + [jax.ShapeDtypeStruct((t, d), BF16)] * (n_dx - 1)
                     + [jax.ShapeDtypeStruct((1, d), F32)])


def _epi_merge_fwd(z, ya, tm):
    t, w = ya.shape

    def fn(acc, ins, outs, i):
        ga_ref, gb_ref, ya_ref = ins
        outs[0][...] = acc.astype(BF16)
        sa = _sigmoid(ga_ref[...].astype(F32))
        sb = _sigmoid(gb_ref[...].astype(F32))
        outs[1][...] = (sa * ya_ref[...].astype(F32) + sb * acc).astype(BF16)

    row = _row3(tm, w)
    return _Epilogue(fn, strip=EPI_STRIP, ins=[z, z, ya], in_specs=[_row3(tm, w, 4), _row3(tm, w, 5), row], out_specs=[row, row],
                     out_shape=[jax.ShapeDtypeStruct((t, w), BF16)] * 2)


def _epi_merge_bwd(z, ya, yb, tm):
    t, w = ya.shape

    def fn(acc, ins, outs, i):
        ga_ref, gb_ref, ya_ref, yb_ref = ins
        sa = _sigmoid(ga_ref[...].astype(F32))
        sb = _sigmoid(gb_ref[...].astype(F32))
        outs[0][...] = (acc * sa).astype(BF16)
        outs[1][...] = (acc * sb).astype(BF16)
        outs[2][:, 0:w] = (acc * ya_ref[...].astype(F32) * sa * (1.0 - sa)).astype(BF16)
        outs[2][:, w:2 * w] = (acc * yb_ref[...].astype(F32) * sb * (1.0 - sb)).astype(BF16)

    row = _row3(tm, w)
    return _Epilogue(fn, strip=EPI_STRIP, ins=[z, z, ya, yb], in_specs=[_row3(tm, w, 4), _row3(tm, w, 5), row, row],
                     out_specs=[row, row, _row3(tm, 2 * w)],
                     out_shape=[jax.ShapeDtypeStruct((t, w), BF16)] * 2 + [jax.ShapeDtypeStruct((t, 2 * w), BF16)])


def _epi_head(x2, r, target, g_ple, g_final, tm):
    t, d = x2.shape

    def fn(acc, ins, outs, i):
        x2_ref, r_ref, tg_ref, gple_ref, gfin_ref = ins
        dx3_ref, dq_ref, dr_ref, loss_ref, dgfin_ref, dgple_ref = outs
        pg = _sigmoid(acc)
        rv = r_ref[...]
        rstd_r = lax.rsqrt(_mean(rv * rv) + EPS_RMS)
        nr = rv * rstd_r
        pe = nr * gple_ref[...]
        x3 = x2_ref[...] + pe * pg
        rstd3 = lax.rsqrt(_mean(x3 * x3) + EPS_RMS)
        n3 = x3 * rstd3
        err = n3 * gfin_ref[...] - tg_ref[...]
        loss_part = jnp.sum(_rowsum(err * err), axis=1, keepdims=True) * (0.5 / d)
        dy = err * (1.0 / d)
        dn3 = dy * gfin_ref[...]
        dx3 = rstd3 * (dn3 - n3 * _mean(dn3 * n3))
        dx3_ref[...] = dx3
        dq_ref[...] = (dx3 * pe * pg * (1.0 - pg)).astype(BF16)
        dpe = dx3 * pg
        dnr = dpe * gple_ref[...]
        dr_ref[...] = (rstd_r * (dnr - nr * _mean(dnr * nr))).astype(BF16)
        _accumulate_over_rows(loss_ref, loss_part, i)
        _accumulate_over_rows(dgfin_ref, _rowsum(dy * n3), i)
        _accumulate_over_rows(dgple_ref, _rowsum(dpe * nr), i)

    row = _row3(tm, d)
    vec = jax.ShapeDtypeStruct((1, d), F32)
    return _Epilogue(fn, strip=EPI_STRIP, ins=[x2, r, target, g_ple, g_final], in_specs=[row, row, row, _vec3(d), _vec3(d)],
                     out_specs=[row, row, row, _vec3(d), _vec3(d), _vec3(d)],
                     out_shape=[jax.ShapeDtypeStruct((t, d), F32), jax.ShapeDtypeStruct((t, d), BF16),
                                jax.ShapeDtypeStruct((t, d), BF16), vec, vec, vec])


def _vec_spec(d):
    return pl.BlockSpec((1, d), lambda i: (0, 0))


def _row_call(name, body, *, n_steps, in_specs, out_specs, out_shape, scratch, args, exch=()):
    n_in, n_out, n_scr, n_ex = len(in_specs), len(out_specs), len(scratch), len(exch)
    kinds = [k for k, _ in exch]
    ex_in, ex_out, ex_shape, ex_scratch = _exchange_io(exch)

    def wrapped(*refs):
        ins, rest = refs[:n_in], refs[n_in:]
        x_refs, rest = rest[:n_ex], rest[n_ex:]
        outs, rest = rest[:n_out], rest[n_out:]
        o_refs, rest = rest[:n_ex], rest[n_ex:]
        scr, sems = rest[:n_scr], rest[n_scr:]
        if n_ex:
            ex_start, ex_finish = _exchange_step(kinds, x_refs, o_refs, *sems)
            pl.when(pl.program_id(0) == 0)(ex_start)
        body(*ins, *outs, *scr)
        if n_ex:
            pl.when(pl.program_id(0) == n_steps - 1)(ex_finish)

    res = pl.pallas_call(wrapped, grid=(n_steps,), in_specs=list(in_specs) + ex_in, out_specs=list(out_specs) + ex_out,
                         out_shape=list(out_shape) + ex_shape, scratch_shapes=list(scratch) + ex_scratch, name=name,
                         compiler_params=_params(1))(*args, *[v for _, v in exch])
    return res[:n_out], res[n_out:]


def _rms_fwd(name, x, g, gather=()):
    t, d = x.shape
    tm = min(TM_EW * 2, t)
    n_steps = t // tm
    n = len(gather)

    def body(x_ref, g_ref, *refs):
        h_ref = refs[n]
        if n:
            start, finish = _all_gather_steps(refs[:n], refs[n + 1:2 * n + 1], *refs[2 * n + 1:])
            pl.when(pl.program_id(0) == 0)(start)
        xv = x_ref[...]
        rstd = lax.rsqrt(_mean(xv * xv) + EPS_RMS)
        h_ref[...] = ((xv * rstd) * g_ref[...]).astype(BF16)
        if n:
            pl.when(pl.program_id(0) == n_steps - 1)(finish)

    row = pl.BlockSpec((tm, d), lambda i: (i, 0))
    sems = [pltpu.SemaphoreType.DMA((7 * n,)), pltpu.SemaphoreType.DMA((7 * n,)), pltpu.SemaphoreType.DMA((n,))] if n else []
    res = pl.pallas_call(
        body, grid=(n_steps,), in_specs=[row, _vec_spec(d)] + [HBM_SPEC] * n, out_specs=[row] + [HBM_SPEC] * n,
        out_shape=[jax.ShapeDtypeStruct((t, d), BF16)] + [jax.ShapeDtypeStruct((N_DEV,) + v.shape, v.dtype) for v in gather],
        scratch_shapes=sems, name=name, compiler_params=_params(1))(x, g, *gather)
    return res[0], res[1:]


def _fill_shifted(buf_ref, sh_ref):
    n = sh_ref.shape[1]
    for p in range(1, SUB):
        sh_ref[p - 1] = buf_ref[p:p + n, :]


def _branch_fwd(z, ln_v_g, ln_v_b, ws_m, bias_full, conv_w, conv_b, ln_b_g, ln_b_b, seq, exch=()):
    t = z.shape[0]
    w = 1024
    tm = min(TM_EW, seq)
    tiles_per_seq = seq // tm
    n_chunks = tm // CHUNK

    def body(z_ref, lvg_ref, lvb_ref, ws_ref, bias_ref, cw_ref, cb_ref, lbg_ref, lbb_ref,
             pa_ref, cs_ref, c_ref, hist_ref, buf_ref, mix_ref, sh_ref, wb_ref):
        i = pl.program_id(0)
        u = z_ref[:, 0:w].astype(F32)
        v = z_ref[:, w:2 * w].astype(F32)
        ug = _gelu(u)
        vg = _gelu(v)
        dv = vg - _mean(vg)
        vhat = dv * lax.rsqrt(_mean(dv * dv) + EPS_LN)
        vn = (vhat * lvg_ref[...] + lvb_ref[...]).astype(BF16)
        for ci in range(n_chunks):
            rows = slice(ci * CHUNK, (ci + 1) * CHUNK)
            for g in range(GROUPS):
                cols = slice(g * CHUNK, (g + 1) * CHUNK)
                mix_ref[rows, cols] = lax.dot_general(ws_ref[g], vn[rows, cols], _NN, preferred_element_type=F32)
            mix_ref[rows, :] += bias_ref[...]
        pa_ref[...] = (ug * mix_ref[...]).astype(BF16)

        a = z_ref[:, 2 * w:3 * w].astype(F32)
        gl = z_ref[:, 3 * w:4 * w].astype(F32)
        glu = a * _sigmoid(gl)

        @pl.when(i % tiles_per_seq == 0)
        def _():
            hist_ref[...] = jnp.zeros_like(hist_ref)

        buf_ref[0:HALO_B, :] = hist_ref[...]
        buf_ref[HALO_B:, :] = glu
        hist_ref[...] = glu[tm - HALO_B:, :]
        _fill_shifted(buf_ref, sh_ref)

        @pl.when(i == 0)
        def _():
            for k in range(CONV_B):
                wb_ref[k] = jnp.broadcast_to(cw_ref[k:k + 1, :], (SUB, w))

        groups = 4

        def strip(si, _):
            s = pl.multiple_of(si * (groups * SUB), groups * SUB)
            acc = [jnp.zeros((SUB, w), F32) + cb_ref[...] for _ in range(groups)]
            for k in range(CONV_B):
                whole, part = divmod(HALO_B - (CONV_B - 1) + k, SUB)
                wk = wb_ref[k]
                for g in range(groups):
                    at = pl.ds(s + SUB * (whole + g), SUB)
                    acc[g] = acc[g] + wk * (buf_ref[at, :] if part == 0 else sh_ref[part - 1, at, :])
            for g in range(0, groups, 2):
                at = pl.ds(s + SUB * g, 2 * SUB)
                c = jnp.concatenate(acc[g:g + 2], axis=0)
                c_ref[at, :] = c
                dc = c - _mean(c)
                chat = dc * lax.rsqrt(_mean(dc * dc) + EPS_LN)
                cn = chat * lbg_ref[...] + lbb_ref[...]
                cs_ref[at, :] = (cn * _sigmoid(cn)).astype(BF16)
            return 0

        lax.fori_loop(0, tm // (groups * SUB), strip, 0)

    row = pl.BlockSpec((tm, w), lambda i: (i, 0))
    in_specs = [pl.BlockSpec((tm, 4 * w), lambda i: (i, 0)), _vec_spec(w), _vec_spec(w),
                pl.BlockSpec((GROUPS, CHUNK, CHUNK), lambda i: (0, 0, 0)), pl.BlockSpec((CHUNK, w), lambda i: (0, 0)),
                pl.BlockSpec((CONV_B, w), lambda i: (0, 0)), _vec_spec(w), _vec_spec(w), _vec_spec(w)]
    return _row_call(
        "branch_fwd", body, n_steps=t // tm, in_specs=in_specs, out_specs=[row, row, row],
        out_shape=[jax.ShapeDtypeStruct((t, w), BF16), jax.ShapeDtypeStruct((t, w), BF16), jax.ShapeDtypeStruct((t, w), F32)],
        scratch=[pltpu.VMEM((HALO_B, w), F32), pltpu.VMEM((HALO_B + tm, w), F32), pltpu.VMEM((tm, w), F32),
                 pltpu.VMEM((SUB - 1, HALO_B + tm - SUB, w), F32), pltpu.VMEM((CONV_B, SUB, w), F32)],
        args=(z, ln_v_g, ln_v_b, ws_m, bias_full, conv_w, conv_b, ln_b_g, ln_b_b), exch=exch)


def _branch_bwd(z, c_saved, dpa, dcs, dgates, ln_v_g, ln_v_b, ws_m, ws_mt, bias_full, conv_w, ln_b_g, ln_b_b, seq, exch=()):
    t = z.shape[0]
    w = 1024
    tm = min(TM_EW, seq)
    tiles_per_seq = seq // tm
    n_tiles = t // tm
    n_chunks = tm // CHUNK

    def body(z_ref, c_ref, dpa_ref, dcs_ref, dgt_ref, lvg_ref, lvb_ref, ws_ref, wst_ref, bias_ref, cw_ref,
             lbg_ref, lbb_ref,
             dz_ref, dlvg_ref, dlvb_ref, dws_ref, dbs_ref, dcw_ref, dcb_ref, dlbg_ref, dlbb_ref,
             carry_ref, glu_ref, dbuf_ref, mix_ref, dvn_ref, dbs_acc_ref, dglu_ref, sh_ref, wb_ref, dwacc_ref):
        i = pl.program_id(0)
        r = n_tiles - 1 - i

        @pl.when(i == 0)
        def _():
            for ref in (dlvg_ref, dlvb_ref, dws_ref, dbs_acc_ref, dwacc_ref, dcb_ref, dlbg_ref, dlbb_ref):
                ref[...] = jnp.zeros_like(ref)

        u = z_ref[:, 0:w].astype(F32)
        v = z_ref[:, w:2 * w].astype(F32)
        ug, dug = _gelu_and_grad(u)
        vg, dvg = _gelu_and_grad(v)
        dv0 = vg - _mean(vg)
        rstd_v = lax.rsqrt(_mean(dv0 * dv0) + EPS_LN)
        vhat = dv0 * rstd_v
        vn = (vhat * lvg_ref[...] + lvb_ref[...]).astype(BF16)
        dpa = dpa_ref[...].astype(F32)
        dmix = dpa * ug
        dmix_b = dmix.astype(BF16)
        for ci in range(n_chunks):
            rows = slice(ci * CHUNK, (ci + 1) * CHUNK)
            for g in range(GROUPS):
                cols = slice(g * CHUNK, (g + 1) * CHUNK)
                mix_ref[rows, cols] = lax.dot_general(ws_ref[g], vn[rows, cols], _NN, preferred_element_type=F32)
                dvn_ref[rows, cols] = lax.dot_general(wst_ref[g], dmix_b[rows, cols], _NN, preferred_element_type=F32)
                dws_ref[g] += lax.dot_general(dmix_b[rows, cols], vn[rows, cols], _NT, preferred_element_type=F32)
            mix_ref[rows, :] += bias_ref[...]
            dbs_acc_ref[...] += dmix[rows, :]
        dz_ref[:, 0:w] = (dpa * mix_ref[...] * dug).astype(BF16)
        dvn = dvn_ref[...]
        dlvg_ref[...] += _rowsum(dvn * vhat)
        dlvb_ref[...] += _rowsum(dvn)
        dvh = dvn * lvg_ref[...]
        dvg_in = rstd_v * (dvh - _mean(dvh) - vhat * _mean(dvh * vhat))
        dz_ref[:, w:2 * w] = (dvg_in * dvg).astype(BF16)

        c = c_ref[...]
        dc0 = c - _mean(c)
        rstd_c = lax.rsqrt(_mean(dc0 * dc0) + EPS_LN)
        chat = dc0 * rstd_c
        cn = chat * lbg_ref[...] + lbb_ref[...]
        sg = _sigmoid(cn)
        dcn = dcs_ref[...].astype(F32) * (sg * (1.0 + cn * (1.0 - sg)))
        dlbg_ref[...] += _rowsum(dcn * chat)
        dlbb_ref[...] += _rowsum(dcn)
        dch = dcn * lbg_ref[...]
        dc = rstd_c * (dch - _mean(dch) - chat * _mean(dch * chat))
        dcb_ref[...] += _rowsum(dc)

        a = z_ref[:, 2 * w:3 * w].astype(F32)
        gl = z_ref[:, 3 * w:4 * w].astype(F32)
        sgl = _sigmoid(gl)
        glu_ref[...] = a * sgl

        @pl.when(r % tiles_per_seq == tiles_per_seq - 1)
        def _():
            carry_ref[...] = jnp.zeros_like(carry_ref)

        dbuf_ref[0:tm, :] = dc
        dbuf_ref[tm:, :] = carry_ref[...]
        carry_ref[...] = dc[0:HALO_B, :]
        _fill_shifted(dbuf_ref, sh_ref)

        @pl.when(i == 0)
        def _():
            for k in range(CONV_B):
                wb_ref[k] = jnp.broadcast_to(cw_ref[k:k + 1, :], (SUB, w))

        groups = 2

        def strip(si, _):
            s = pl.multiple_of(si * (groups * SUB), groups * SUB)
            glu_rows = [glu_ref[pl.ds(s + SUB * g, SUB), :] for g in range(groups)]
            acc = [jnp.zeros((SUB, w), F32) for _ in range(groups)]
            for k in range(CONV_B):
                whole, part = divmod(CONV_B - 1 - k, SUB)
                wk = wb_ref[k]
                dw_part = jnp.zeros((SUB, w), F32)
                for g in range(groups):
                    at = pl.ds(s + SUB * (whole + g), SUB)
                    d_rows = dbuf_ref[at, :] if part == 0 else sh_ref[part - 1, at, :]
                    acc[g] = acc[g] + wk * d_rows
                    dw_part = dw_part + d_rows * glu_rows[g]
                dwacc_ref[k] += dw_part
            dglu_ref[pl.ds(s, groups * SUB), :] = jnp.concatenate(acc, axis=0)
            return 0

        lax.fori_loop(0, tm // (groups * SUB), strip, 0)
        dglu = dglu_ref[...]
        dz_ref[:, 2 * w:3 * w] = (dglu * sgl).astype(BF16)
        dz_ref[:, 3 * w:4 * w] = (dglu * a * sgl * (1.0 - sgl)).astype(BF16)
        dz_ref[:, 4 * w:6 * w] = dgt_ref[...]

        @pl.when(i == n_tiles - 1)
        def _():
            tri = lax.broadcasted_iota(jnp.int32, (CHUNK, CHUNK), 0) >= lax.broadcasted_iota(jnp.int32, (CHUNK, CHUNK), 1)
            lane = lax.broadcasted_iota(jnp.int32, (CHUNK, CHUNK), 1)
            dbs = jnp.zeros((CHUNK, CHUNK), F32)
            for g in range(GROUPS):
                dws_ref[g] = jnp.where(tri, dws_ref[g], 0.0)
                group_sum = jnp.sum(dbs_acc_ref[:, g * CHUNK:(g + 1) * CHUNK], axis=1, keepdims=True)
                dbs = jnp.where(lane == g, group_sum, dbs)
            dbs_ref[...] = dbs
            for k in range(CONV_B):
                dcw_ref[k:k + 1, :] = _rowsum(dwacc_ref[k])

    def rev(i):
        return n_tiles - 1 - i

    row = pl.BlockSpec((tm, w), lambda i: (rev(i), 0))
    full = lambda shape: pl.BlockSpec(shape, lambda i: (0,) * len(shape))
    in_specs = [pl.BlockSpec((tm, 4 * w), lambda i: (rev(i), 0)),
                row, row, row, pl.BlockSpec((tm, 2 * w), lambda i: (rev(i), 0)),
                _vec_spec(w), _vec_spec(w), full((GROUPS, CHUNK, CHUNK)), full((GROUPS, CHUNK, CHUNK)), full((CHUNK, w)),
                full((CONV_B, w)), _vec_spec(w), _vec_spec(w)]
    out_specs = [pl.BlockSpec((tm, 6 * w), lambda i: (rev(i), 0)), _vec_spec(w), _vec_spec(w), full((GROUPS, CHUNK, CHUNK)),
                 full((CHUNK, CHUNK)), full((CONV_B, w)), _vec_spec(w), _vec_spec(w), _vec_spec(w)]
    vec = jax.ShapeDtypeStruct((1, w), F32)
    out_shape = [jax.ShapeDtypeStruct((t, 6 * w), BF16), vec, vec, jax.ShapeDtypeStruct((GROUPS, CHUNK, CHUNK), F32),
                 jax.ShapeDtypeStruct((CHUNK, CHUNK), F32), jax.ShapeDtypeStruct((CONV_B, w), F32), vec, vec, vec]
    scratch = [pltpu.VMEM((HALO_B, w), F32), pltpu.VMEM((tm, w), F32), pltpu.VMEM((tm + HALO_B, w), F32),
               pltpu.VMEM((tm, w), F32), pltpu.VMEM((tm, w), F32), pltpu.VMEM((CHUNK, w), F32), pltpu.VMEM((tm, w), F32),
               pltpu.VMEM((SUB - 1, HALO_B + tm - SUB, w), F32), pltpu.VMEM((CONV_B, SUB, w), F32),
               pltpu.VMEM((CONV_B, SUB, w), F32)]
    return _row_call("branch_bwd", body, n_steps=n_tiles, in_specs=in_specs, out_specs=out_specs, out_shape=out_shape,
                     scratch=scratch, exch=exch,
                     args=(z, c_saved, dpa, dcs, dgates, ln_v_g, ln_v_b, ws_m, ws_mt, bias_full, conv_w, ln_b_g, ln_b_b))


def _conv3_window(prev8, x):
    win = jnp.concatenate([prev8, x], axis=0)
    n = x.shape[0]
    return [win[HALO_F - 2:HALO_F - 2 + n], win[HALO_F - 1:HALO_F - 1 + n], x]


def _ffn_mid_fwd(up0, conv_w, conv_b, seq):
    nb, t, f = up0.shape
    half = nb // 2
    tm = min(TM_FFN, seq)
    tiles_per_seq = seq // tm
    n_strips = tm // ROWS_F

    def body(up_ref, w_ref, b_ref, act_ref, upc_ref, hist_ref):
        i = pl.program_id(0)

        @pl.when(i % tiles_per_seq == 0)
        def _():
            hist_ref[...] = jnp.zeros_like(hist_ref)

        for j in range(half):
            jv = j + half
            wg = [w_ref[j, k:k + 1, :] for k in range(CONV_F)]
            wv = [w_ref[jv, k:k + 1, :] for k in range(CONV_F)]
            bg, bv = b_ref[j], b_ref[jv]

            def strip(c, carry):
                rows = pl.ds(pl.multiple_of(c * ROWS_F, ROWS_F), ROWS_F)
                xg = up_ref[j, rows, :].astype(F32)
                xv = up_ref[jv, rows, :].astype(F32)
                sg = _conv3_window(carry[0], xg)
                sv = _conv3_window(carry[1], xv)
                gate = bg + wg[0] * sg[0] + wg[1] * sg[1] + wg[2] * sg[2]
                val = bv + wv[0] * sv[0] + wv[1] * sv[1] + wv[2] * sv[2]
                act_ref[j, rows, :] = (_gelu(gate) * val).astype(BF16)
                upc_ref[j, rows, :] = gate.astype(BF16)
                upc_ref[jv, rows, :] = val.astype(BF16)
                return xg[ROWS_F - HALO_F:], xv[ROWS_F - HALO_F:]

            last = lax.fori_loop(0, n_strips, strip, (hist_ref[j], hist_ref[jv]))
            hist_ref[j] = last[0]
            hist_ref[jv] = last[1]

    return pl.pallas_call(
        body, grid=(t // tm,),
        in_specs=[pl.BlockSpec((nb, tm, f), lambda i: (0, i, 0)), pl.BlockSpec((nb, CONV_F, f), lambda i: (0, 0, 0)),
                  pl.BlockSpec((nb, 1, f), lambda i: (0, 0, 0))],
        out_specs=[pl.BlockSpec((half, tm, f), lambda i: (0, i, 0)), pl.BlockSpec((nb, tm, f), lambda i: (0, i, 0))],
        out_shape=[jax.ShapeDtypeStruct((half, t, f), BF16), jax.ShapeDtypeStruct((nb, t, f), BF16)],
        scratch_shapes=[pltpu.VMEM((nb, HALO_F, f), F32)],
        name="ffn_mid_fwd", compiler_params=_params(1))(up0, conv_w, conv_b)


def _ffn_mid_bwd(up0, upc, dact, conv_w, seq, exch=()):
    nb, t, f = up0.shape
    half = nb // 2
    tm = min(TM_FFN, seq)
    tiles_per_seq = seq // tm
    n_tiles = t // tm
    n_strips = tm // ROWS_F

    def body(up_ref, upc_ref, da_ref, w_ref, dup_ref, dw_ref, db_ref, carry_ref, dwacc_ref, dbacc_ref):
        i = pl.program_id(0)
        r = n_tiles - 1 - i

        @pl.when(i == 0)
        def _():
            dwacc_ref[...] = jnp.zeros_like(dwacc_ref)
            dbacc_ref[...] = jnp.zeros_like(dbacc_ref)

        @pl.when(r % tiles_per_seq == tiles_per_seq - 1)
        def _():
            carry_ref[...] = jnp.zeros_like(carry_ref)

        for j in range(half):
            jv = j + half
            wg = [w_ref[j, k:k + 1, :] for k in range(CONV_F)]
            wv = [w_ref[jv, k:k + 1, :] for k in range(CONV_F)]

            def strip(ci, carry):
                rows = pl.ds(pl.multiple_of((n_strips - 1 - ci) * ROWS_F, ROWS_F), ROWS_F)
                val = upc_ref[jv, rows, :].astype(F32)
                gg, dgg = _gelu_and_grad(upc_ref[j, rows, :].astype(F32))
                da = da_ref[j, rows, :].astype(F32)
                d_gate = da * val * dgg
                d_val = da * gg
                for blk, d, nxt, wk in ((j, d_gate, carry[0], wg), (jv, d_val, carry[1], wv)):
                    dbacc_ref[blk] += d
                    dwin = jnp.concatenate([d, nxt], axis=0)
                    shifted = [dwin[2:2 + ROWS_F], dwin[1:1 + ROWS_F], d]
                    x = up_ref[blk, rows, :].astype(F32)
                    for k in range(CONV_F):
                        dwacc_ref[blk, k] += shifted[k] * x
                    dx = wk[0] * shifted[0] + wk[1] * shifted[1] + wk[2] * shifted[2]
                    dup_ref[blk, rows, :] = dx.astype(BF16)
                return d_gate[0:HALO_F], d_val[0:HALO_F]

            carry = lax.fori_loop(0, n_strips, strip, (carry_ref[j], carry_ref[jv]))
            carry_ref[j] = carry[0]
            carry_ref[jv] = carry[1]

        @pl.when(i == n_tiles - 1)
        def _():
            for blk in range(nb):
                db_ref[blk] = _rowsum(dbacc_ref[blk])
                for k in range(CONV_F):
                    dw_ref[blk, k:k + 1, :] = _rowsum(dwacc_ref[blk, k])

    def rev(i):
        return n_tiles - 1 - i

    return _row_call(
        "ffn_mid_bwd", body, n_steps=n_tiles,
        in_specs=[pl.BlockSpec((nb, tm, f), lambda i: (0, rev(i), 0)), pl.BlockSpec((nb, tm, f), lambda i: (0, rev(i), 0)),
                  pl.BlockSpec((half, tm, f), lambda i: (0, rev(i), 0)),
                  pl.BlockSpec((nb, CONV_F, f), lambda i: (0, 0, 0))],
        out_specs=[pl.BlockSpec((nb, tm, f), lambda i: (0, rev(i), 0)), pl.BlockSpec((nb, CONV_F, f), lambda i: (0, 0, 0)),
                   pl.BlockSpec((nb, 1, f), lambda i: (0, 0, 0))],
        out_shape=[jax.ShapeDtypeStruct((nb, t, f), BF16), jax.ShapeDtypeStruct((nb, CONV_F, f), F32),
                   jax.ShapeDtypeStruct((nb, 1, f), F32)],
        scratch=[pltpu.VMEM((nb, HALO_F, f), F32), pltpu.VMEM((nb, CONV_F, ROWS_F, f), F32), pltpu.VMEM((nb, ROWS_F, f), F32)],
        args=(up0, upc, dact, conv_w), exch=exch)


def _all_gather_steps(x_refs, o_refs, send_sems, recv_sems, local_sems):
    n = len(x_refs)
    x, y, c = lax.axis_index("x"), lax.axis_index("y"), lax.axis_index("c")
    me, sibling = (x, y, c), (x, y, 1 - c)
    chips = [(1 - x, y), (x, 1 - y), (1 - x, 1 - y)]

    def slot(pos):
        return 4 * pos[0] + 2 * pos[1] + pos[2]

    def copy(a, k, block, to, src=None):
        dst = o_refs[a].at[slot(block)]
        return pltpu.make_async_remote_copy(
            src_ref=dst if src is None else src, dst_ref=dst, send_sem=send_sems.at[a * 7 + k],
            recv_sem=recv_sems.at[a * 7 + k], device_id=to, device_id_type=MESH)

    mine = [pltpu.make_async_copy(x_refs[a], o_refs[a].at[slot(me)], local_sems.at[a]) for a in range(n)]
    first = []
    for a in range(n):
        first.append(copy(a, 0, me, sibling, src=x_refs[a]))
        first += [copy(a, 1 + j, me, (*chip, c), src=x_refs[a]) for j, chip in enumerate(chips)]

    def start():
        for cp in mine + first:
            cp.start()

    def finish():
        passed = []
        for j, chip in enumerate(chips):
            for a in range(n):
                copy(a, 1 + j, (*chip, c), me).wait_recv()
                cp = copy(a, 4 + j, (*chip, c), sibling)
                cp.start()
                passed.append(cp)
        for a in range(n):
            copy(a, 0, sibling, me).wait_recv()
        for j, chip in enumerate(chips):
            for a in range(n):
                copy(a, 4 + j, (*chip, 1 - c), me).wait_recv()
        for cp in first + passed:
            cp.wait_send()
        for cp in mine:
            cp.wait()

    return start, finish


def _adamw_update(g, w, m, v):
    c1 = 1.0 - ADAM_B1 ** ADAM_STEP
    c2 = 1.0 - ADAM_B2 ** ADAM_STEP
    m_new = ADAM_B1 * m + (1.0 - ADAM_B1) * g
    v_new = ADAM_B2 * v + (1.0 - ADAM_B2) * (g * g)
    delta = -ADAM_LR * ((m_new / c1) / (jnp.sqrt(v_new / c2) + ADAM_EPS) + ADAM_WD * w)
    return m_new, v_new, delta


def _adamw(name, parts, w, m, v, rows_per_step):
    r, c = w.shape
    tr = r if r <= rows_per_step else (rows_per_step if r % rows_per_step == 0 else r // 2)

    def body(p_ref, w_ref, m_ref, v_ref, g_ref, d_ref, mo_ref, vo_ref):
        g = p_ref[0].astype(F32)
        for s in range(1, N_DEV):
            g = g + p_ref[s].astype(F32)
        m_new, v_new, delta = _adamw_update(g, w_ref[...], m_ref[...], v_ref[...])
        g_ref[...] = g
        mo_ref[...] = m_new
        vo_ref[...] = v_new
        d_ref[...] = delta

    row = pl.BlockSpec((tr, c), lambda i: (i, 0))
    out = jax.ShapeDtypeStruct((r, c), F32)
    return pl.pallas_call(body, grid=(r // tr,), in_specs=[pl.BlockSpec((N_DEV, tr, c), lambda i: (0, i, 0)), row, row, row],
                          out_specs=[row, row, row, row], out_shape=[out, out, out, out], name=name,
                          compiler_params=_params(1))(parts, w, m, v)


_VEC_NAMES = ("ln_v_g", "ln_v_b", "b_s", "conv_b_b", "ln_b_g", "ln_b_b", "g_ffn", "g_pg", "g_ple", "g_final")
_LOSS_ROW = len(_VEC_NAMES)
_REP_LAYOUT = dict({k: (i, 1, 1024) for i, k in enumerate(_VEC_NAMES)}, ffn_conv_b=(16, 8, 704), w_s=(24, 128, 1024))


def _pack_replicated_grads(d, loss_row):
    head = jnp.concatenate([d[k].reshape(1, 1024) for k in _VEC_NAMES] + [loss_row], axis=1).reshape(_LOSS_ROW + 1, 1024)
    return jnp.concatenate([jnp.pad(head, ((0, 16 - _LOSS_ROW - 1), (0, 0))),
                            jnp.pad(d["ffn_conv_b"].reshape(8, 704), ((0, 0), (0, 1024 - 704))),
                            d["w_s"].reshape(128, 1024)], axis=0)


def _rows8(vec):
    return jnp.pad(vec.reshape(1, 1024), ((0, 7), (0, 0)))


def _adamw_small(name, parts, items, loss_row=None):
    n = len(items)

    def body(p_ref, *refs):
        ins, outs = refs[:3 * n], refs[3 * n:]
        for a, (row, rows, cols) in enumerate(it[:3] for it in items):
            g = p_ref[0, row:row + rows, 0:cols]
            for s in range(1, N_DEV):
                g = g + p_ref[s, row:row + rows, 0:cols]
            m_new, v_new, delta = _adamw_update(g, ins[3 * a][...], ins[3 * a + 1][...], ins[3 * a + 2][...])
            for ref, val in zip(outs[4 * a:4 * a + 4], (g, delta, m_new, v_new)):
                ref[...] = val
        if loss_row is not None:
            total = p_ref[0, loss_row:loss_row + 1, 0:128]
            for s in range(1, N_DEV):
                total = total + p_ref[s, loss_row:loss_row + 1, 0:128]
            outs[-1][...] = total

    out_shape = [jax.ShapeDtypeStruct((rows, cols), F32) for _, rows, cols, *_ in items for _ in range(4)]
    if loss_row is not None:
        out_shape.append(jax.ShapeDtypeStruct((1, 128), F32))
    flat = pl.pallas_call(body, out_shape=out_shape, name=name,
                          compiler_params=pltpu.CompilerParams(vmem_limit_bytes=VMEM_LIMIT))(
        parts, *[arr for it in items for arr in it[3:]])
    res = [tuple(flat[4 * a:4 * a + 4]) for a in range(n)]
    return res + [flat[-1]] if loss_row is not None else res


def _pack_sharded_small(conv_b_w, ffn_conv_w):
    lead = conv_b_w.shape[:-2]
    pad0 = [(0, 0)] * len(lead)
    a = jnp.pad(conv_b_w, pad0 + [(0, 1), (0, 0)])
    b = jnp.pad(ffn_conv_w.reshape(lead + (CONV_F * 704,)), pad0 + [(0, 24 * 128 - CONV_F * 704)]).reshape(lead + (24, 128))
    return jnp.concatenate([a, b], axis=-2)


def _unpack_sharded_small(pk):
    lead = pk.shape[:-2]
    conv_b_w = pk[..., 0:CONV_B, :]
    ffn = pk[..., 32:56, :].reshape(lead + (24 * 128,))[..., :CONV_F * 704].reshape(lead + (CONV_F, 704))
    return conv_b_w, ffn


_WEIGHTS = ("g_mix", "w_in", "ln_v_g", "ln_v_b", "w_s", "b_s", "w_a_out", "conv_b_w", "conv_b_b", "ln_b_g", "ln_b_b",
            "w_b_out", "w_o", "g_ffn", "w_up", "ffn_conv_w", "ffn_conv_b", "w_down", "g_pg", "w_pg", "w_ple", "g_ple",
            "g_final")
_BIG = ("w_in", "w_a_out", "w_b_out", "w_o", "w_up", "w_down", "w_pg", "w_ple")


def kernel(x, p, g_mix, w_in, ln_v_g, ln_v_b, w_s, b_s, w_a_out, conv_b_w, conv_b_b, ln_b_g, ln_b_b, w_b_out, w_o, g_ffn, w_up, ffn_conv_w, ffn_conv_b, w_down, g_pg, w_pg, w_ple, g_ple, g_final, loss_target, m_g_mix, m_w_in, m_ln_v_g, m_ln_v_b, m_w_s, m_b_s, m_w_a_out, m_conv_b_w, m_conv_b_b, m_ln_b_g, m_ln_b_b, m_w_b_out, m_w_o, m_g_ffn, m_w_up, m_ffn_conv_w, m_ffn_conv_b, m_w_down, m_g_pg, m_w_pg, m_w_ple, m_g_ple, m_g_final, v_g_mix, v_w_in, v_ln_v_g, v_ln_v_b, v_w_s, v_b_s, v_w_a_out, v_conv_b_w, v_conv_b_b, v_ln_b_g, v_ln_b_b, v_w_b_out, v_w_o, v_g_ffn, v_w_up, v_ffn_conv_w, v_ffn_conv_b, v_w_down, v_g_pg, v_w_pg, v_w_ple, v_g_ple, v_g_final):
    local = dict(locals())
    wts = {k: local[k] for k in _WEIGHTS}
    mom = {k: local["m_" + k] for k in _WEIGHTS}
    var = {k: local["v_" + k] for k in _WEIGHTS}
    shapes = {k: wts[k].shape for k in _WEIGHTS}

    bsz, seq, d = x.shape
    t = bsz * seq
    x0 = x.reshape(t, d)
    p0 = p.reshape(t, p.shape[-1])
    target = loss_target.reshape(t, d)
    tm = min(TM_MM, t)
    tm_wide = min(TM_WIDE, t)
    tt = min(TT_MM, t)
    n_row = t // tm
    n_tok = t // tt

    def sq(a):
        return a.reshape(a.shape[1:])

    shard = {k: sq(wts[k]).astype(BF16) for k in _BIG}
    h1, (w_in3, small8) = _rms_fwd("rms_mix_gather_w_in", x0, g_mix,
                                   gather=[shard["w_in"], _pack_sharded_small(sq(conv_b_w), sq(ffn_conv_w))])
    conv_b_w8, ffn_conv_w8 = _unpack_sharded_small(small8)
    conv_w_full = conv_b_w8.transpose(1, 0, 2).reshape(CONV_B, N_DEV * conv_b_w8.shape[-1])
    n_in = w_in3.shape[2]
    f_dev = shard["w_up"].shape[1]
    n_blk, f_blk = N_DEV // 2, 2 * f_dev

    def pair_blocks(a):
        return a.reshape(n_blk, 2, a.shape[1], f_dev).transpose(0, 2, 1, 3).reshape(n_blk, a.shape[1], f_blk)

    def unpair_blocks(a):
        return a.reshape(n_blk, a.shape[1], 2, f_dev).transpose(0, 2, 1, 3).reshape(N_DEV, a.shape[1], f_dev)

    ws_m = jnp.where(jnp.tril(jnp.ones((CHUNK, CHUNK), bool))[None], sq(w_s), 0.0).astype(BF16)
    ws_mt = jnp.swapaxes(ws_m, 1, 2)
    bias_full = jnp.broadcast_to(sq(b_s).T[:, :, None], (CHUNK, GROUPS, CHUNK)).reshape(CHUNK, GROUPS * CHUNK)
    ffn_b = ffn_conv_b.reshape(n_blk, 1, f_blk)
    ffn_w = pair_blocks(ffn_conv_w8)

    z, (wa3, wb3, wo3) = _matmul(
        "mm_in", h1, w_in3, dims=_NN, grid=(N_DEV, t // tm_wide, 1),
        a_spec=pl.BlockSpec((tm_wide, d), lambda j, i, k: (i, 0)),
        b_spec=pl.BlockSpec((None, d, n_in), lambda j, i, k: (j, 0, 0)),
        o_spec=pl.BlockSpec((tm_wide, n_in), lambda j, i, k: (i, j)), acc_shape=(tm_wide, n_in),
        out_shape=jax.ShapeDtypeStruct((t, N_DEV * n_in), BF16),
        exch=[("gather", shard[k]) for k in ("w_a_out", "w_b_out", "w_o")])
    w_a = wa3.reshape(-1, d)
    w_b = wb3.reshape(-1, d)
    w_om = wo3.reshape(-1, d)
    (pa, cs, c_saved), (w_up3, wd3, wpg3, wple3) = _branch_fwd(
        z, ln_v_g, ln_v_b, ws_m, bias_full, conv_w_full, conv_b_b, ln_b_g, ln_b_b, seq,
        exch=[("gather", shard[k]) for k in ("w_up", "w_down", "w_pg", "w_ple")])
    w_pgm = wpg3.reshape(-1, d)
    w_upb = pair_blocks(w_up3)
    w_db = wd3.reshape(n_blk // 2, f_blk, d)
    w_plem = wple3.transpose(1, 0, 2).reshape(wple3.shape[1], d)
    ya = _mm_rows("mm_a_out", pa, w_a, dims=_NN, tm=tm, out_dtype=BF16)
    yb, merged = _mm_rows("mm_b_out", cs, w_b, dims=_NN, tm=tm, epi=_epi_merge_fwd(z, ya, tm))
    x1, h2 = _mm_rows("mm_o", merged, w_om, dims=_NN, tm=tm, epi=_epi_residual_rms(x0, g_ffn, tm))
    up0 = _matmul("mm_up", h2, w_upb, dims=_NN, grid=(n_blk, t // tm_wide, 1),
                  a_spec=pl.BlockSpec((tm_wide, d), lambda j, i, k: (i, 0)),
                  b_spec=pl.BlockSpec((None, d, f_blk), lambda j, i, k: (j, 0, 0)),
                  o_spec=pl.BlockSpec((None, tm_wide, f_blk), lambda j, i, k: (j, i, 0)), acc_shape=(tm_wide, f_blk),
                  out_shape=jax.ShapeDtypeStruct((n_blk, t, f_blk), BF16))
    act, upc = _ffn_mid_fwd(up0, ffn_w, ffn_b, seq)
    x2, hq = _matmul("mm_down", act, w_db, dims=_NN, grid=(n_row, 1, 1), split=n_blk // 2,
                     a_spec=pl.BlockSpec((n_blk // 2, tm, f_blk), lambda i, j, k: (0, i, 0)),
                     b_spec=pl.BlockSpec((n_blk // 2, f_blk, d), lambda i, j, k: (0, 0, 0)),
                     acc_shape=(tm, d), epi=_epi_residual_rms(x1, g_pg, tm))
    r = _mm_rows("mm_ple", p0, w_plem, dims=_NN, tm=tm, out_dtype=F32)

    dx3, dq, dr, loss_v, dg_final, dg_ple = _mm_rows(
        "mm_pg", hq, w_pgm, dims=_NN, tm=tm // 2, epi=_epi_head(x2, r, target, g_ple, g_final.reshape(1, d), tm // 2))

    recv = {}
    gw_pg = _mm_wgrad("wg_pg", hq, dq, out_dtype=BF16, tt=tt).reshape(wpg3.shape)
    dw_ple = _mm_wgrad("wg_ple", p0, dr, out_dtype=BF16, tt=tt)
    gw_ple = dw_ple.reshape(dw_ple.shape[0], N_DEV, -1).transpose(1, 0, 2)
    dx2, dx2b, dg_pg = _mm_rows("mm_pg_t", dq, w_pgm, dims=_NT, tm=tm, epi=_epi_rms_bwd(x2, g_pg, dx3, tm, want_bf16=True))

    dact, (recv["w_pg"], recv["w_ple"]) = _matmul(
        "mm_down_t", dx2b, w_db, dims=_NT, grid=(n_blk // 2, n_row, 1),
        a_spec=pl.BlockSpec((tm, d), lambda j, i, k: (i, 0)),
        b_spec=pl.BlockSpec((None, f_blk, d), lambda j, i, k: (j, 0, 0)),
        o_spec=pl.BlockSpec((None, tm, f_blk), lambda j, i, k: (j, i, 0)), acc_shape=(tm, f_blk),
        out_shape=jax.ShapeDtypeStruct((n_blk // 2, t, f_blk), BF16),
        exch=[("scatter", gw_pg), ("scatter", gw_ple)])
    gw_down = _matmul("wg_down", act, dx2b, dims=_TN, grid=(n_blk // 2, 1, n_tok),
                      a_spec=pl.BlockSpec((None, tt, f_blk), lambda j, i, k: (j, k, 0)),
                      b_spec=pl.BlockSpec((tt, d), lambda j, i, k: (k, 0)),
                      o_spec=pl.BlockSpec((None, f_blk, d), lambda j, i, k: (j, 0, 0)), acc_shape=(f_blk, d),
                      out_shape=jax.ShapeDtypeStruct((n_blk // 2, f_blk, d), BF16)).reshape(wd3.shape)
    (d_up0, dffn_wb, dffn_b), (recv["w_down"],) = _ffn_mid_bwd(up0, upc, dact, ffn_w, seq, exch=[("scatter", gw_down)])
    dffn_w8 = unpair_blocks(dffn_wb)
    gw_up = _matmul("wg_up", h2, d_up0, dims=_TN, grid=(n_blk, 1, n_tok),
                    a_spec=pl.BlockSpec((tt, d), lambda j, i, k: (k, 0)),
                    b_spec=pl.BlockSpec((None, tt, f_blk), lambda j, i, k: (j, k, 0)),
                    o_spec=pl.BlockSpec((None, d, f_blk), lambda j, i, k: (j, 0, 0)), acc_shape=(d, f_blk),
                    out_shape=jax.ShapeDtypeStruct((n_blk, d, f_blk), BF16))
    gw_up = unpair_blocks(gw_up)
    (dx1, dx1b, dg_ffn), (recv["w_up"],) = _matmul(
        "mm_up_t", d_up0, w_upb, dims=_NT, grid=(n_row, 1, n_blk),
        a_spec=pl.BlockSpec((None, tm, f_blk), lambda i, j, k: (k, i, 0)),
        b_spec=pl.BlockSpec((None, d, f_blk), lambda i, j, k: (k, 0, 0)),
        acc_shape=(tm, d), epi=_epi_rms_bwd(x1, g_ffn, dx2, tm, want_bf16=True), exch=[("scatter", gw_up)])

    dya, dyb, dgates = _mm_rows("mm_o_t", dx1b, w_om, dims=_NT, tm=tm, epi=_epi_merge_bwd(z, ya, yb, tm))
    gw_o = _mm_wgrad("wg_o", merged, dx1b, out_dtype=BF16, tt=tt).reshape(wo3.shape)
    dpa = _mm_rows("mm_a_out_t", dya, w_a, dims=_NT, tm=tm, out_dtype=BF16)
    dcs = _mm_rows("mm_b_out_t", dyb, w_b, dims=_NT, tm=tm, out_dtype=BF16)
    gw_a = _mm_wgrad("wg_a_out", pa, dya, out_dtype=BF16, tt=tt).reshape(wa3.shape)
    gw_b = _mm_wgrad("wg_b_out", cs, dyb, out_dtype=BF16, tt=tt).reshape(wb3.shape)
    (dz, dlvg, dlvb, dws, dbs_full, dconv_w, dconv_b, dlbg, dlbb), (recv["w_o"], recv["w_a_out"], recv["w_b_out"]) = _branch_bwd(
        z, c_saved, dpa, dcs, dgates, ln_v_g, ln_v_b, ws_m, ws_mt, bias_full, conv_w_full, ln_b_g, ln_b_b, seq,
        exch=[("scatter", gw_o), ("scatter", gw_a), ("scatter", gw_b)])
    db_s = dbs_full[:, :GROUPS].T
    rep_partial = _pack_replicated_grads(
        dict(ln_v_g=dlvg, ln_v_b=dlvb, b_s=db_s, conv_b_b=dconv_b, ln_b_g=dlbg, ln_b_b=dlbb, g_ffn=dg_ffn,
             g_pg=dg_pg, g_ple=dg_ple, g_final=dg_final, ffn_conv_b=dffn_b, w_s=dws), loss_v)
    dconv_w8 = dconv_w.reshape(CONV_B, N_DEV, -1).transpose(1, 0, 2)
    small_partial = _pack_sharded_small(dconv_w8, dffn_w8)
    gw_in, (recv_small, recv_rep) = _matmul(
        "wg_in", h1, dz, dims=_TN, grid=(N_DEV, 1, n_tok),
        a_spec=pl.BlockSpec((tt, d), lambda j, i, k: (k, 0)),
        b_spec=pl.BlockSpec((tt, n_in), lambda j, i, k: (k, j)),
        o_spec=pl.BlockSpec((None, d, n_in), lambda j, i, k: (j, 0, 0)), acc_shape=(d, n_in),
        out_shape=jax.ShapeDtypeStruct((N_DEV, d, n_in), BF16),
        exch=[("scatter", small_partial), ("gather", rep_partial)])
    (grad_x, dg_mix), (recv["w_in"],) = _matmul(
        "mm_in_t", dz, w_in3, dims=_NT, grid=(n_row, 1, N_DEV // 2), split=2,
        a_spec=pl.BlockSpec((tm, 2 * n_in), lambda i, j, k: (i, k)),
        b_spec=pl.BlockSpec((2, d, n_in), lambda i, j, k: (k, 0, 0)),
        acc_shape=(tm, d), epi=_epi_rms_bwd(x0, g_mix, dx1, tm, want_bf16=False), exch=[("scatter", gw_in)])
    (recv_g_mix,) = _exchange("exchange_g_mix", [("gather", _rows8(dg_mix))])

    grads, deltas, new_m, new_v = {}, {}, {}, {}
    by_kind = (grads, deltas, new_m, new_v)

    def two_d(a):
        a = sq(a)
        return a.reshape(-1, a.shape[-1])

    for k in _BIG:
        parts = recv[k].reshape(N_DEV, -1, recv[k].shape[-1])
        outs = _adamw("adamw_" + k, parts, two_d(wts[k]), two_d(mom[k]), two_d(var[k]), ADAMW_ROWS)
        for tgt, o in zip(by_kind, outs):
            tgt[k] = o.reshape(shapes[k])

    small = [_pack_sharded_small(sq(s["conv_b_w"]), sq(s["ffn_conv_w"])) for s in (wts, mom, var)]
    for tgt, o in zip(by_kind, _adamw("adamw_conv", recv_small, small[0], small[1], small[2], 56)):
        cw, fw = _unpack_sharded_small(o)
        tgt["conv_b_w"] = cw.reshape(shapes["conv_b_w"])
        tgt["ffn_conv_w"] = fw.reshape(shapes["ffn_conv_w"])

    names = list(_REP_LAYOUT)
    items = [_REP_LAYOUT[k] + tuple(s[k].reshape(_REP_LAYOUT[k][1:]) for s in (wts, mom, var)) for k in names]
    *rep_outs, loss_sum = _adamw_small("adamw_replicated", recv_rep, items, loss_row=_LOSS_ROW)
    for k, outs in zip(names, rep_outs):
        for tgt, o in zip(by_kind, outs):
            tgt[k] = o.reshape(shapes[k])
    loss = loss_sum[0, 0]

    g_mix_item = (0, 1, 1024) + tuple(s["g_mix"].reshape(1, 1024) for s in (wts, mom, var))
    for tgt, o in zip(by_kind, _adamw_small("adamw_g_mix", recv_g_mix, [g_mix_item])[0]):
        tgt["g_mix"] = o.reshape(shapes["g_mix"])

    return (loss, grad_x.reshape(x.shape), *[grads[k] for k in _WEIGHTS], *[deltas[k] for k in _WEIGHTS],
            *[new_m[k] for k in _WEIGHTS], *[new_v[k] for k in _WEIGHTS])
```

```python
import math

import jax
import jax.numpy as jnp
from jax import lax
from jax.experimental import pallas as pl
from jax.experimental.pallas import tpu as pltpu

F32 = jnp.float32
BF16 = jnp.bfloat16

N_DEV = 8
EPS_RMS = 1e-6
EPS_LN = 1e-5
CHUNK = 128
GROUPS = 8
CONV_B = 31
CONV_F = 3
HALO_B = 32
HALO_F = 8
ROWS_F = 16
SUB = 8

ADAM_LR = 0.001
ADAM_B1 = 0.9
ADAM_B2 = 0.999
ADAM_EPS = 1e-08
ADAM_WD = 0.01
ADAM_STEP = 10

VMEM_LIMIT = 56 * 1024 * 1024
TM_MM = 1024
TM_WIDE = 2048
TT_MM = 2048
TM_EW = 256
TM_FFN = 512
ADAMW_ROWS = 512
EPI_STRIP = 256

_NN = (((1,), (0,)), ((), ()))
_NT = (((1,), (1,)), ((), ()))
_TN = (((0,), (0,)), ((), ()))
MESH = pl.DeviceIdType.MESH
HBM_SPEC = pl.BlockSpec(memory_space=pltpu.HBM)


def _params(n_axes):
    return pltpu.CompilerParams(dimension_semantics=("arbitrary",) * n_axes, vmem_limit_bytes=VMEM_LIMIT)


def _gelu(x):
    k = math.sqrt(2.0 / math.pi)
    return 0.5 * x * (1.0 + jnp.tanh(k * (x + 0.044715 * (x * x * x))))


def _gelu_and_grad(x):
    k = math.sqrt(2.0 / math.pi)
    x2 = x * x
    t = jnp.tanh(k * (x + 0.044715 * (x2 * x)))
    g = 0.5 * x * (1.0 + t)
    dg = 0.5 * (1.0 + t) + 0.5 * x * (1.0 - t * t) * (k * (1.0 + 3.0 * 0.044715 * x2))
    return g, dg


def _sigmoid(x):
    return 1.0 / (1.0 + jnp.exp(-x))


def _rowsum(x):
    return jnp.sum(x, axis=0, keepdims=True)


def _mean(x):
    return jnp.mean(x, axis=-1, keepdims=True)


def _exchange_io(exch):
    n = len(exch)
    out_shape = [jax.ShapeDtypeStruct(v.shape if kind == "scatter" else (N_DEV,) + v.shape, v.dtype) for kind, v in exch]
    scratch = [pltpu.SemaphoreType.DMA((7 * n,)), pltpu.SemaphoreType.DMA((7 * n,)), pltpu.SemaphoreType.DMA((n,))] if n else []
    return [HBM_SPEC] * n, [HBM_SPEC] * n, out_shape, scratch


def _exchange_step(kinds, x_refs, o_refs, send_sems, recv_sems, local_sems):
    n = len(kinds)
    x, y, c = lax.axis_index("x"), lax.axis_index("y"), lax.axis_index("c")
    me = 4 * x + 2 * y + c

    def src(a, to_slot):
        return x_refs[a].at[to_slot] if kinds[a] == "scatter" else x_refs[a]

    mine = [pltpu.make_async_copy(src(a, me), o_refs[a].at[me], local_sems.at[a]) for a in range(n)]
    sends, recvs = [], []
    for m in range(1, N_DEV):
        mx, my, mc = (m >> 2) & 1, (m >> 1) & 1, m & 1
        px, py, pc = (1 - x if mx else x), (1 - y if my else y), (1 - c if mc else c)
        peer = 4 * px + 2 * py + pc
        for a in range(n):
            k = a * 7 + m - 1
            sends.append(pltpu.make_async_remote_copy(
                src_ref=src(a, peer), dst_ref=o_refs[a].at[me], send_sem=send_sems.at[k], recv_sem=recv_sems.at[k],
                device_id=(px, py, pc), device_id_type=MESH))
            recvs.append(pltpu.make_async_remote_copy(
                src_ref=src(a, peer), dst_ref=o_refs[a].at[peer], send_sem=send_sems.at[k], recv_sem=recv_sems.at[k],
                device_id=(px, py, pc), device_id_type=MESH))

    def start():
        for cp in mine + sends:
            cp.start()

    def finish():
        for cp in recvs:
            cp.wait_recv()
        for cp in sends:
            cp.wait_send()
        for cp in mine:
            cp.wait()

    return start, finish


def _exchange(name, exch):
    n = len(exch)
    kinds = [k for k, _ in exch]
    in_specs, out_specs, out_shape, scratch = _exchange_io(exch)

    def body(*refs):
        start, finish = _exchange_step(kinds, refs[:n], refs[n:2 * n], *refs[2 * n:])
        start()
        finish()

    return pl.pallas_call(body, in_specs=in_specs, out_specs=out_specs, out_shape=out_shape, scratch_shapes=scratch,
                          name=name)(*[v for _, v in exch])


class _Epilogue:
    def __init__(self, fn, ins=(), in_specs=(), out_specs=(), out_shape=(), strip=None):
        self.fn, self.ins, self.in_specs = fn, list(ins), list(in_specs)
        self.out_specs, self.out_shape = list(out_specs), list(out_shape)
        self.strip = strip


def _matmul(name, a, b, *, dims, grid, a_spec, b_spec, acc_shape, o_spec=None, out_shape=None, epi=None, exch=(), split=1):
    nk = grid[2]
    plain = epi is None
    if plain:
        def store(acc, ins, outs, i):
            outs[0][...] = acc.astype(outs[0].dtype)
        epi = _Epilogue(store, out_specs=[o_spec], out_shape=[out_shape])
    n_in = 2 + len(epi.ins)
    n_out = len(epi.out_specs)
    n_ex = len(exch)
    kinds = [k for k, _ in exch]
    ex_in, ex_out, ex_shape, ex_scratch = _exchange_io(exch)

    def body(*refs):
        a_ref, b_ref = refs[:2]
        step0 = pl.program_id(0)
        epi_ins, rest = refs[2:n_in], refs[n_in:]
        x_refs, rest = rest[:n_ex], rest[n_ex:]
        outs, rest = rest[:n_out], rest[n_out:]
        o_refs, scr = rest[:n_ex], rest[n_ex:]
        if n_ex:
            pid = [pl.program_id(ax) for ax in range(3)]
            ex_start, ex_finish = _exchange_step(kinds, x_refs, o_refs, *scr[len(scr) - 3:])
            pl.when((pid[0] == 0) & (pid[1] == 0) & (pid[2] == 0))(ex_start)
        if split == 1:
            part = lax.dot_general(a_ref[...].astype(BF16), b_ref[...].astype(BF16), dims, preferred_element_type=F32)
        else:
            kk = b_ref.shape[-1]
            part = None
            for s in range(split):
                a_s = a_ref[s] if len(a_ref.shape) == 3 else a_ref[:, s * kk:(s + 1) * kk]
                p_s = lax.dot_general(a_s.astype(BF16), b_ref[s].astype(BF16), dims, preferred_element_type=F32)
                part = p_s if part is None else part + p_s

        def run_epilogue(rows_of_acc):
            rows = acc_shape[0]
            strip = rows if epi.strip is None else min(epi.strip, rows)
            for s in range(0, rows, strip):
                def view(ref):
                    return ref.at[pl.ds(s, strip)] if ref.shape[0] == rows else ref
                first = (step0 == 0) if s == 0 else False
                epi.fn(rows_of_acc(s, strip), [view(r) for r in epi_ins], [view(r) for r in outs], first)

        if nk == 1:
            run_epilogue(lambda s, n: part[s:s + n])
        else:
            acc_ref = scr[0]
            k = pl.program_id(2)

            @pl.when(k == 0)
            def _():
                acc_ref[...] = part

            @pl.when(k > 0)
            def _():
                acc_ref[...] += part

            @pl.when(k == nk - 1)
            def _():
                run_epilogue(lambda s, n: acc_ref[pl.ds(s, n), :])
        if n_ex:
            pl.when((pid[0] == grid[0] - 1) & (pid[1] == grid[1] - 1) & (pid[2] == grid[2] - 1))(ex_finish)

    scratch = ([pltpu.VMEM(acc_shape, F32)] if nk > 1 else []) + ex_scratch
    res = pl.pallas_call(body, grid=grid, in_specs=[a_spec, b_spec] + epi.in_specs + ex_in,
                         out_specs=epi.out_specs + ex_out, out_shape=epi.out_shape + ex_shape, scratch_shapes=scratch,
                         name=name, compiler_params=_params(3))(a, b, *epi.ins, *[v for _, v in exch])
    main = res[0] if plain else res[:n_out]
    return (main, res[n_out:]) if n_ex else main


def _mm_rows(name, a, w, *, dims, tm, out_dtype=None, epi=None):
    t, k = a.shape
    n = w.shape[1] if dims == _NN else w.shape[0]
    tm = min(tm, t)
    return _matmul(name, a, w, dims=dims, grid=(t // tm, 1, 1),
                   a_spec=pl.BlockSpec((tm, k), lambda i, j, kk: (i, 0)),
                   b_spec=pl.BlockSpec(w.shape, lambda i, j, kk: (0, 0)),
                   o_spec=pl.BlockSpec((tm, n), lambda i, j, kk: (i, 0)), acc_shape=(tm, n),
                   out_shape=jax.ShapeDtypeStruct((t, n), out_dtype) if epi is None else None, epi=epi)


def _mm_wgrad(name, a, b, *, out_dtype, tt):
    t, m = a.shape
    n = b.shape[1]
    tt = min(tt, t)
    return _matmul(name, a, b, dims=_TN, grid=(1, 1, t // tt),
                   a_spec=pl.BlockSpec((tt, m), lambda i, j, kk: (kk, 0)),
                   b_spec=pl.BlockSpec((tt, n), lambda i, j, kk: (kk, 0)),
                   o_spec=pl.BlockSpec((m, n), lambda i, j, kk: (0, 0)),
                   acc_shape=(m, n), out_shape=jax.ShapeDtypeStruct((m, n), out_dtype))


def _row3(tm, d, col=0):
    return pl.BlockSpec((tm, d), lambda i, j, k: (i, col))


def _vec3(d):
    return pl.BlockSpec((1, d), lambda i, j, k: (0, 0))


def _accumulate_over_rows(ref, part, first):
    if first is False:
        ref[...] += part
        return

    @pl.when(first)
    def _():
        ref[...] = part + jnp.zeros_like(ref)

    @pl.when(jnp.logical_not(first))
    def _():
        ref[...] += part


def _epi_residual_rms(res, g, tm):
    t, d = res.shape

    def fn(acc, ins, outs, i):
        res_ref, g_ref = ins
        xv = acc + res_ref[...]
        outs[0][...] = xv
        rstd = lax.rsqrt(_mean(xv * xv) + EPS_RMS)
        outs[1][...] = ((xv * rstd) * g_ref[...]).astype(BF16)

    return _Epilogue(fn, strip=EPI_STRIP, ins=[res, g], in_specs=[_row3(tm, d), _vec3(d)], out_specs=[_row3(tm, d), _row3(tm, d)],
                     out_shape=[jax.ShapeDtypeStruct((t, d), F32), jax.ShapeDtypeStruct((t, d), BF16)])


def _epi_rms_bwd(x, g, dres, tm, *, want_bf16):
    t, d = x.shape

    def fn(acc, ins, outs, i):
        x_ref, g_ref, dres_ref = ins
        xv = x_ref[...]
        rstd = lax.rsqrt(_mean(xv * xv) + EPS_RMS)
        nrm = xv * rstd
        dn = acc * g_ref[...]
        dx = dres_ref[...] + rstd * (dn - nrm * _mean(dn * nrm))
        outs[0][...] = dx
        if want_bf16:
            outs[1][...] = dx.astype(BF16)
        _accumulate_over_rows(outs[-1], _rowsum(acc * nrm), i)

    row = _row3(tm, d)
    n_dx = 2 if want_bf16 else 1
    return _Epilogue(fn, strip=EPI_STRIP, ins=[x, g, dres], in_specs=[row, _vec3(d), row], out_specs=[row] * n_dx + [_vec3(d)],
                     out_shape=[jax.ShapeDtypeStruct((t, d), F32)] + [jax.ShapeDtypeStruct((t, d), BF16)] * (n_dx - 1)
                     + [jax.ShapeDtypeStruct((1, d), F32)])


def _epi_merge_fwd(z, ya, tm):
    t, w = ya.shape

    def fn(acc, ins, outs, i):
        ga_ref, gb_ref, ya_ref = ins
        outs[0][...] = acc.astype(BF16)
        sa = _sigmoid(ga_ref[...].astype(F32))
        sb = _sigmoid(gb_ref[...].astype(F32))
        outs[1][...] = (sa * ya_ref[...].astype(F32) + sb * acc).astype(BF16)

    row = _row3(tm, w)
    return _Epilogue(fn, strip=EPI_STRIP, ins=[z, z, ya], in_specs=[_row3(tm, w, 4), _row3(tm, w, 5), row], out_specs=[row, row],
                     out_shape=[jax.ShapeDtypeStruct((t, w), BF16)] * 2)


def _epi_merge_bwd(z, ya, yb, tm):
    t, w = ya.shape

    def fn(acc, ins, outs, i):
        ga_ref, gb_ref, ya_ref, yb_ref = ins
        sa = _sigmoid(ga_ref[...].astype(F32))
        sb = _sigmoid(gb_ref[...].astype(F32))
        outs[0][...] = (acc * sa).astype(BF16)
        outs[1][...] = (acc * sb).astype(BF16)
        outs[2][:, 0:w] = (acc * ya_ref[...].astype(F32) * sa * (1.0 - sa)).astype(BF16)
        outs[2][:, w:2 * w] = (acc * yb_ref[...].astype(F32) * sb * (1.0 - sb)).astype(BF16)

    row = _row3(tm, w)
    return _Epilogue(fn, strip=EPI_STRIP, ins=[z, z, ya, yb], in_specs=[_row3(tm, w, 4), _row3(tm, w, 5), row, row],
                     out_specs=[row, row, _row3(tm, 2 * w)],
                     out_shape=[jax.ShapeDtypeStruct((t, w), BF16)] * 2 + [jax.ShapeDtypeStruct((t, 2 * w), BF16)])


def _epi_head(x2, p, w_ple, target, g_ple, g_final, tm):
    t, d = x2.shape

    def fn(acc, ins, outs, i):
        x2_ref, p_ref, wple_ref, tg_ref, gple_ref, gfin_ref = ins
        dx3_ref, dq_ref, dr_ref, loss_ref, dgfin_ref, dgple_ref = outs
        pg = _sigmoid(acc)
        rv = lax.dot_general(p_ref[...].astype(BF16), wple_ref[...], _NN, preferred_element_type=F32)
        rstd_r = lax.rsqrt(_mean(rv * rv) + EPS_RMS)
        nr = rv * rstd_r
        pe = nr * gple_ref[...]
        x3 = x2_ref[...] + pe * pg
        rstd3 = lax.rsqrt(_mean(x3 * x3) + EPS_RMS)
        n3 = x3 * rstd3
        err = n3 * gfin_ref[...] - tg_ref[...]
        loss_part = jnp.sum(_rowsum(err * err), axis=1, keepdims=True) * (0.5 / d)
        dy = err * (1.0 / d)
        dn3 = dy * gfin_ref[...]
        dx3 = rstd3 * (dn3 - n3 * _mean(dn3 * n3))
        dx3_ref[...] = dx3
        dq_ref[...] = (dx3 * pe * pg * (1.0 - pg)).astype(BF16)
        dpe = dx3 * pg
        dnr = dpe * gple_ref[...]
        dr_ref[...] = (rstd_r * (dnr - nr * _mean(dnr * nr))).astype(BF16)
        _accumulate_over_rows(loss_ref, loss_part, i)
        _accumulate_over_rows(dgfin_ref, _rowsum(dy * n3), i)
        _accumulate_over_rows(dgple_ref, _rowsum(dpe * nr), i)

    row = _row3(tm, d)
    vec = jax.ShapeDtypeStruct((1, d), F32)
    return _Epilogue(fn, strip=EPI_STRIP, ins=[x2, p, w_ple, target, g_ple, g_final],
                     in_specs=[row, _row3(tm, p.shape[1]), pl.BlockSpec(w_ple.shape, lambda i, j, k: (0, 0)), row, _vec3(d), _vec3(d)],
                     out_specs=[row, row, row, _vec3(d), _vec3(d), _vec3(d)],
                     out_shape=[jax.ShapeDtypeStruct((t, d), F32), jax.ShapeDtypeStruct((t, d), BF16),
                                jax.ShapeDtypeStruct((t, d), BF16), vec, vec, vec])


def _vec_spec(d):
    return pl.BlockSpec((1, d), lambda i: (0, 0))


def _row_call(name, body, *, n_steps, in_specs, out_specs, out_shape, scratch, args, exch=()):
    n_in, n_out, n_scr, n_ex = len(in_specs), len(out_specs), len(scratch), len(exch)
    kinds = [k for k, _ in exch]
    ex_in, ex_out, ex_shape, ex_scratch = _exchange_io(exch)

    def wrapped(*refs):
        ins, rest = refs[:n_in], refs[n_in:]
        x_refs, rest = rest[:n_ex], rest[n_ex:]
        outs, rest = rest[:n_out], rest[n_out:]
        o_refs, rest = rest[:n_ex], rest[n_ex:]
        scr, sems = rest[:n_scr], rest[n_scr:]
        if n_ex:
            ex_start, ex_finish = _exchange_step(kinds, x_refs, o_refs, *sems)
            pl.when(pl.program_id(0) == 0)(ex_start)
        body(*ins, *outs, *scr)
        if n_ex:
            pl.when(pl.program_id(0) == n_steps - 1)(ex_finish)

    res = pl.pallas_call(wrapped, grid=(n_steps,), in_specs=list(in_specs) + ex_in, out_specs=list(out_specs) + ex_out,
                         out_shape=list(out_shape) + ex_shape, scratch_shapes=list(scratch) + ex_scratch, name=name,
                         compiler_params=_params(1))(*args, *[v for _, v in exch])
    return res[:n_out], res[n_out:]


def _rms_fwd(name, x, g, gather=()):
    t, d = x.shape
    tm = min(TM_EW * 2, t)
    n_steps = t // tm
    n = len(gather)

    def body(x_ref, g_ref, *refs):
        h_ref = refs[n]
        if n:
            start, finish = _all_gather_steps(refs[:n], refs[n + 1:2 * n + 1], *refs[2 * n + 1:])
            pl.when(pl.program_id(0) == 0)(start)
        xv = x_ref[...]
        rstd = lax.rsqrt(_mean(xv * xv) + EPS_RMS)
        h_ref[...] = ((xv * rstd) * g_ref[...]).astype(BF16)
        if n:
            pl.when(pl.program_id(0) == n_steps - 1)(finish)

    row = pl.BlockSpec((tm, d), lambda i: (i, 0))
    sems = [pltpu.SemaphoreType.DMA((7 * n,)), pltpu.SemaphoreType.DMA((7 * n,)), pltpu.SemaphoreType.DMA((n,))] if n else []
    res = pl.pallas_call(
        body, grid=(n_steps,), in_specs=[row, _vec_spec(d)] + [HBM_SPEC] * n, out_specs=[row] + [HBM_SPEC] * n,
        out_shape=[jax.ShapeDtypeStruct((t, d), BF16)] + [jax.ShapeDtypeStruct((N_DEV,) + v.shape, v.dtype) for v in gather],
        scratch_shapes=sems, name=name, compiler_params=_params(1))(x, g, *gather)
    return res[0], res[1:]


def _fill_shifted(buf_ref, sh_ref):
    n = sh_ref.shape[1]
    for p in range(1, SUB):
        sh_ref[p - 1] = buf_ref[p:p + n, :]


def _branch_fwd(z, ln_v_g, ln_v_b, ws_m, bias_full, conv_w, conv_b, ln_b_g, ln_b_b, seq, exch=()):
    t = z.shape[0]
    w = 1024
    tm = min(TM_EW, seq)
    tiles_per_seq = seq // tm
    n_chunks = tm // CHUNK

    def body(z_ref, lvg_ref, lvb_ref, ws_ref, bias_ref, cw_ref, cb_ref, lbg_ref, lbb_ref,
             pa_ref, cs_ref, c_ref, hist_ref, buf_ref, mix_ref, sh_ref, wb_ref):
        i = pl.program_id(0)
        u = z_ref[:, 0:w].astype(F32)
        v = z_ref[:, w:2 * w].astype(F32)
        ug = _gelu(u)
        vg = _gelu(v)
        dv = vg - _mean(vg)
        vhat = dv * lax.rsqrt(_mean(dv * dv) + EPS_LN)
        vn = (vhat * lvg_ref[...] + lvb_ref[...]).astype(BF16)
        for ci in range(n_chunks):
            rows = slice(ci * CHUNK, (ci + 1) * CHUNK)
            for g in range(GROUPS):
                cols = slice(g * CHUNK, (g + 1) * CHUNK)
                mix_ref[rows, cols] = lax.dot_general(ws_ref[g], vn[rows, cols], _NN, preferred_element_type=F32)
            mix_ref[rows, :] += bias_ref[...]
        pa_ref[...] = (ug * mix_ref[...]).astype(BF16)

        a = z_ref[:, 2 * w:3 * w].astype(F32)
        gl = z_ref[:, 3 * w:4 * w].astype(F32)
        glu = a * _sigmoid(gl)

        @pl.when(i % tiles_per_seq == 0)
        def _():
            hist_ref[...] = jnp.zeros_like(hist_ref)

        buf_ref[0:HALO_B, :] = hist_ref[...]
        buf_ref[HALO_B:, :] = glu
        hist_ref[...] = glu[tm - HALO_B:, :]
        _fill_shifted(buf_ref, sh_ref)

        @pl.when(i == 0)
        def _():
            for k in range(CONV_B):
                wb_ref[k] = jnp.broadcast_to(cw_ref[k:k + 1, :], (SUB, w))

        groups = 4

        def strip(si, _):
            s = pl.multiple_of(si * (groups * SUB), groups * SUB)
            acc = [jnp.zeros((SUB, w), F32) + cb_ref[...] for _ in range(groups)]
            for k in range(CONV_B):
                whole, part = divmod(HALO_B - (CONV_B - 1) + k, SUB)
                wk = wb_ref[k]
                for g in range(groups):
                    at = pl.ds(s + SUB * (whole + g), SUB)
                    acc[g] = acc[g] + wk * (buf_ref[at, :] if part == 0 else sh_ref[part - 1, at, :])
            for g in range(0, groups, 2):
                at = pl.ds(s + SUB * g, 2 * SUB)
                c = jnp.concatenate(acc[g:g + 2], axis=0)
                c_ref[at, :] = c
                dc = c - _mean(c)
                chat = dc * lax.rsqrt(_mean(dc * dc) + EPS_LN)
                cn = chat * lbg_ref[...] + lbb_ref[...]
                cs_ref[at, :] = (cn * _sigmoid(cn)).astype(BF16)
            return 0

        lax.fori_loop(0, tm // (groups * SUB), strip, 0)

    row = pl.BlockSpec((tm, w), lambda i: (i, 0))
    in_specs = [pl.BlockSpec((tm, 4 * w), lambda i: (i, 0)), _vec_spec(w), _vec_spec(w),
                pl.BlockSpec((GROUPS, CHUNK, CHUNK), lambda i: (0, 0, 0)), pl.BlockSpec((CHUNK, w), lambda i: (0, 0)),
                pl.BlockSpec((CONV_B, w), lambda i: (0, 0)), _vec_spec(w), _vec_spec(w), _vec_spec(w)]
    return _row_call(
        "branch_fwd", body, n_steps=t // tm, in_specs=in_specs, out_specs=[row, row, row],
        out_shape=[jax.ShapeDtypeStruct((t, w), BF16), jax.ShapeDtypeStruct((t, w), BF16), jax.ShapeDtypeStruct((t, w), F32)],
        scratch=[pltpu.VMEM((HALO_B, w), F32), pltpu.VMEM((HALO_B + tm, w), F32), pltpu.VMEM((tm, w), F32),
                 pltpu.VMEM((SUB - 1, HALO_B + tm - SUB, w), F32), pltpu.VMEM((CONV_B, SUB, w), F32)],
        args=(z, ln_v_g, ln_v_b, ws_m, bias_full, conv_w, conv_b, ln_b_g, ln_b_b), exch=exch)


def _branch_bwd(z, c_saved, dpa, dcs, dgates, ln_v_g, ln_v_b, ws_m, ws_mt, bias_full, conv_w, ln_b_g, ln_b_b, seq, exch=()):
    t = z.shape[0]
    w = 1024
    tm = min(TM_EW, seq)
    tiles_per_seq = seq // tm
    n_tiles = t // tm
    n_chunks = tm // CHUNK

    def body(z_ref, c_ref, dpa_ref, dcs_ref, dgt_ref, lvg_ref, lvb_ref, ws_ref, wst_ref, bias_ref, cw_ref,
             lbg_ref, lbb_ref,
             dz_ref, dlvg_ref, dlvb_ref, dws_ref, dbs_ref, dcw_ref, dcb_ref, dlbg_ref, dlbb_ref,
             carry_ref, glu_ref, dbuf_ref, mix_ref, dvn_ref, dbs_acc_ref, dglu_ref, sh_ref, wb_ref, dwacc_ref):
        i = pl.program_id(0)
        r = n_tiles - 1 - i

        @pl.when(i == 0)
        def _():
            for ref in (dlvg_ref, dlvb_ref, dws_ref, dbs_acc_ref, dwacc_ref, dcb_ref, dlbg_ref, dlbb_ref):
                ref[...] = jnp.zeros_like(ref)

        u = z_ref[:, 0:w].astype(F32)
        v = z_ref[:, w:2 * w].astype(F32)
        ug, dug = _gelu_and_grad(u)
        vg, dvg = _gelu_and_grad(v)
        dv0 = vg - _mean(vg)
        rstd_v = lax.rsqrt(_mean(dv0 * dv0) + EPS_LN)
        vhat = dv0 * rstd_v
        vn = (vhat * lvg_ref[...] + lvb_ref[...]).astype(BF16)
        dpa = dpa_ref[...].astype(F32)
        dmix = dpa * ug
        dmix_b = dmix.astype(BF16)
        for ci in range(n_chunks):
            rows = slice(ci * CHUNK, (ci + 1) * CHUNK)
            for g in range(GROUPS):
                cols = slice(g * CHUNK, (g + 1) * CHUNK)
                mix_ref[rows, cols] = lax.dot_general(ws_ref[g], vn[rows, cols], _NN, preferred_element_type=F32)
                dvn_ref[rows, cols] = lax.dot_general(wst_ref[g], dmix_b[rows, cols], _NN, preferred_element_type=F32)
                dws_ref[g] += lax.dot_general(dmix_b[rows, cols], vn[rows, cols], _NT, preferred_element_type=F32)
            mix_ref[rows, :] += bias_ref[...]
            dbs_acc_ref[...] += dmix[rows, :]
        dz_ref[:, 0:w] = (dpa * mix_ref[...] * dug).astype(BF16)
        dvn = dvn_ref[...]
        dlvg_ref[...] += _rowsum(dvn * vhat)
        dlvb_ref[...] += _rowsum(dvn)
        dvh = dvn * lvg_ref[...]
        dvg_in = rstd_v * (dvh - _mean(dvh) - vhat * _mean(dvh * vhat))
        dz_ref[:, w:2 * w] = (dvg_in * dvg).astype(BF16)

        c = c_ref[...]
        dc0 = c - _mean(c)
        rstd_c = lax.rsqrt(_mean(dc0 * dc0) + EPS_LN)
        chat = dc0 * rstd_c
        cn = chat * lbg_ref[...] + lbb_ref[...]
        sg = _sigmoid(cn)
        dcn = dcs_ref[...].astype(F32) * (sg * (1.0 + cn * (1.0 - sg)))
        dlbg_ref[...] += _rowsum(dcn * chat)
        dlbb_ref[...] += _rowsum(dcn)
        dch = dcn * lbg_ref[...]
        dc = rstd_c * (dch - _mean(dch) - chat * _mean(dch * chat))
        dcb_ref[...] += _rowsum(dc)

        a = z_ref[:, 2 * w:3 * w].astype(F32)
        gl = z_ref[:, 3 * w:4 * w].astype(F32)
        sgl = _sigmoid(gl)
        glu_ref[...] = a * sgl

        @pl.when(r % tiles_per_seq == tiles_per_seq - 1)
        def _():
            carry_ref[...] = jnp.zeros_like(carry_ref)

        dbuf_ref[0:tm, :] = dc
        dbuf_ref[tm:, :] = carry_ref[...]
        carry_ref[...] = dc[0:HALO_B, :]
        _fill_shifted(dbuf_ref, sh_ref)

        @pl.when(i == 0)
        def _():
            for k in range(CONV_B):
                wb_ref[k] = jnp.broadcast_to(cw_ref[k:k + 1, :], (SUB, w))

        groups = 2

        def strip(si, _):
            s = pl.multiple_of(si * (groups * SUB), groups * SUB)
            glu_rows = [glu_ref[pl.ds(s + SUB * g, SUB), :] for g in range(groups)]
            acc = [jnp.zeros((SUB, w), F32) for _ in range(groups)]
            for k in range(CONV_B):
                whole, part = divmod(CONV_B - 1 - k, SUB)
                wk = wb_ref[k]
                dw_part = jnp.zeros((SUB, w), F32)
                for g in range(groups):
                    at = pl.ds(s + SUB * (whole + g), SUB)
                    d_rows = dbuf_ref[at, :] if part == 0 else sh_ref[part - 1, at, :]
                    acc[g] = acc[g] + wk * d_rows
                    dw_part = dw_part + d_rows * glu_rows[g]
                dwacc_ref[k] += dw_part
            dglu_ref[pl.ds(s, groups * SUB), :] = jnp.concatenate(acc, axis=0)
            return 0

        lax.fori_loop(0, tm // (groups * SUB), strip, 0)
        dglu = dglu_ref[...]
        dz_ref[:, 2 * w:3 * w] = (dglu * sgl).astype(BF16)
        dz_ref[:, 3 * w:4 * w] = (dglu * a * sgl * (1.0 - sgl)).astype(BF16)
        dz_ref[:, 4 * w:6 * w] = dgt_ref[...]

        @pl.when(i == n_tiles - 1)
        def _():
            tri = lax.broadcasted_iota(jnp.int32, (CHUNK, CHUNK), 0) >= lax.broadcasted_iota(jnp.int32, (CHUNK, CHUNK), 1)
            lane = lax.broadcasted_iota(jnp.int32, (CHUNK, CHUNK), 1)
            dbs = jnp.zeros((CHUNK, CHUNK), F32)
            for g in range(GROUPS):
                dws_ref[g] = jnp.where(tri, dws_ref[g], 0.0)
                group_sum = jnp.sum(dbs_acc_ref[:, g * CHUNK:(g + 1) * CHUNK], axis=1, keepdims=True)
                dbs = jnp.where(lane == g, group_sum, dbs)
            dbs_ref[...] = dbs
            for k in range(CONV_B):
                dcw_ref[k:k + 1, :] = _rowsum(dwacc_ref[k])

    def rev(i):
        return n_tiles - 1 - i

    row = pl.BlockSpec((tm, w), lambda i: (rev(i), 0))
    full = lambda shape: pl.BlockSpec(shape, lambda i: (0,) * len(shape))
    in_specs = [pl.BlockSpec((tm, 4 * w), lambda i: (rev(i), 0)),
                row, row, row, pl.BlockSpec((tm, 2 * w), lambda i: (rev(i), 0)),
                _vec_spec(w), _vec_spec(w), full((GROUPS, CHUNK, CHUNK)), full((GROUPS, CHUNK, CHUNK)), full((CHUNK, w)),
                full((CONV_B, w)), _vec_spec(w), _vec_spec(w)]
    out_specs = [pl.BlockSpec((tm, 6 * w), lambda i: (rev(i), 0)), _vec_spec(w), _vec_spec(w), full((GROUPS, CHUNK, CHUNK)),
                 full((CHUNK, CHUNK)), full((CONV_B, w)), _vec_spec(w), _vec_spec(w), _vec_spec(w)]
    vec = jax.ShapeDtypeStruct((1, w), F32)
    out_shape = [jax.ShapeDtypeStruct((t, 6 * w), BF16), vec, vec, jax.ShapeDtypeStruct((GROUPS, CHUNK, CHUNK), F32),
                 jax.ShapeDtypeStruct((CHUNK, CHUNK), F32), jax.ShapeDtypeStruct((CONV_B, w), F32), vec, vec, vec]
    scratch = [pltpu.VMEM((HALO_B, w), F32), pltpu.VMEM((tm, w), F32), pltpu.VMEM((tm + HALO_B, w), F32),
               pltpu.VMEM((tm, w), F32), pltpu.VMEM((tm, w), F32), pltpu.VMEM((CHUNK, w), F32), pltpu.VMEM((tm, w), F32),
               pltpu.VMEM((SUB - 1, HALO_B + tm - SUB, w), F32), pltpu.VMEM((CONV_B, SUB, w), F32),
               pltpu.VMEM((CONV_B, SUB, w), F32)]
    return _row_call("branch_bwd", body, n_steps=n_tiles, in_specs=in_specs, out_specs=out_specs, out_shape=out_shape,
                     scratch=scratch, exch=exch,
                     args=(z, c_saved, dpa, dcs, dgates, ln_v_g, ln_v_b, ws_m, ws_mt, bias_full, conv_w, ln_b_g, ln_b_b))


def _conv3_window(prev8, x):
    win = jnp.concatenate([prev8, x], axis=0)
    n = x.shape[0]
    return [win[HALO_F - 2:HALO_F - 2 + n], win[HALO_F - 1:HALO_F - 1 + n], x]


def _ffn_mid_fwd(up0, conv_w, conv_b, seq):
    nb, t, f = up0.shape
    half = nb // 2
    tm = min(TM_FFN, seq)
    tiles_per_seq = seq // tm
    n_strips = tm // ROWS_F

    def body(up_ref, w_ref, b_ref, act_ref, upc_ref, hist_ref):
        i = pl.program_id(0)

        @pl.when(i % tiles_per_seq == 0)
        def _():
            hist_ref[...] = jnp.zeros_like(hist_ref)

        for j in range(half):
            jv = j + half
            wg = [w_ref[j, k:k + 1, :] for k in range(CONV_F)]
            wv = [w_ref[jv, k:k + 1, :] for k in range(CONV_F)]
            bg, bv = b_ref[j], b_ref[jv]

            def strip(c, carry):
                rows = pl.ds(pl.multiple_of(c * ROWS_F, ROWS_F), ROWS_F)
                xg = up_ref[j, rows, :].astype(F32)
                xv = up_ref[jv, rows, :].astype(F32)
                sg = _conv3_window(carry[0], xg)
                sv = _conv3_window(carry[1], xv)
                gate = bg + wg[0] * sg[0] + wg[1] * sg[1] + wg[2] * sg[2]
                val = bv + wv[0] * sv[0] + wv[1] * sv[1] + wv[2] * sv[2]
                act_ref[j, rows, :] = (_gelu(gate) * val).astype(BF16)
                upc_ref[j, rows, :] = gate.astype(BF16)
                upc_ref[jv, rows, :] = val.astype(BF16)
                return xg[ROWS_F - HALO_F:], xv[ROWS_F - HALO_F:]

            last = lax.fori_loop(0, n_strips, strip, (hist_ref[j], hist_ref[jv]))
            hist_ref[j] = last[0]
            hist_ref[jv] = last[1]

    return pl.pallas_call(
        body, grid=(t // tm,),
        in_specs=[pl.BlockSpec((nb, tm, f), lambda i: (0, i, 0)), pl.BlockSpec((nb, CONV_F, f), lambda i: (0, 0, 0)),
                  pl.BlockSpec((nb, 1, f), lambda i: (0, 0, 0))],
        out_specs=[pl.BlockSpec((half, tm, f), lambda i: (0, i, 0)), pl.BlockSpec((nb, tm, f), lambda i: (0, i, 0))],
        out_shape=[jax.ShapeDtypeStruct((half, t, f), BF16), jax.ShapeDtypeStruct((nb, t, f), BF16)],
        scratch_shapes=[pltpu.VMEM((nb, HALO_F, f), F32)],
        name="ffn_mid_fwd", compiler_params=_params(1))(up0, conv_w, conv_b)


def _ffn_mid_bwd(up0, upc, dact, conv_w, seq, exch=()):
    nb, t, f = up0.shape
    half = nb // 2
    tm = min(TM_FFN, seq)
    tiles_per_seq = seq // tm
    n_tiles = t // tm
    n_strips = tm // ROWS_F

    def body(up_ref, upc_ref, da_ref, w_ref, dup_ref, dw_ref, db_ref, carry_ref, dwacc_ref, dbacc_ref):
        i = pl.program_id(0)
        r = n_tiles - 1 - i

        @pl.when(i == 0)
        def _():
            dwacc_ref[...] = jnp.zeros_like(dwacc_ref)
            dbacc_ref[...] = jnp.zeros_like(dbacc_ref)

        @pl.when(r % tiles_per_seq == tiles_per_seq - 1)
        def _():
            carry_ref[...] = jnp.zeros_like(carry_ref)

        for j in range(half):
            jv = j + half
            wg = [w_ref[j, k:k + 1, :] for k in range(CONV_F)]
            wv = [w_ref[jv, k:k + 1, :] for k in range(CONV_F)]

            def strip(ci, carry):
                rows = pl.ds(pl.multiple_of((n_strips - 1 - ci) * ROWS_F, ROWS_F), ROWS_F)
                val = upc_ref[jv, rows, :].astype(F32)
                gg, dgg = _gelu_and_grad(upc_ref[j, rows, :].astype(F32))
                da = da_ref[j, rows, :].astype(F32)
                d_gate = da * val * dgg
                d_val = da * gg
                for blk, d, nxt, wk in ((j, d_gate, carry[0], wg), (jv, d_val, carry[1], wv)):
                    dbacc_ref[blk] += d
                    dwin = jnp.concatenate([d, nxt], axis=0)
                    shifted = [dwin[2:2 + ROWS_F], dwin[1:1 + ROWS_F], d]
                    x = up_ref[blk, rows, :].astype(F32)
                    for k in range(CONV_F):
                        dwacc_ref[blk, k] += shifted[k] * x
                    dx = wk[0] * shifted[0] + wk[1] * shifted[1] + wk[2] * shifted[2]
                    dup_ref[blk, rows, :] = dx.astype(BF16)
                return d_gate[0:HALO_F], d_val[0:HALO_F]

            carry = lax.fori_loop(0, n_strips, strip, (carry_ref[j], carry_ref[jv]))
            carry_ref[j] = carry[0]
            carry_ref[jv] = carry[1]

        @pl.when(i == n_tiles - 1)
        def _():
            for blk in range(nb):
                db_ref[blk] = _rowsum(dbacc_ref[blk])
                for k in range(CONV_F):
                    dw_ref[blk, k:k + 1, :] = _rowsum(dwacc_ref[blk, k])

    def rev(i):
        return n_tiles - 1 - i

    return _row_call(
        "ffn_mid_bwd", body, n_steps=n_tiles,
        in_specs=[pl.BlockSpec((nb, tm, f), lambda i: (0, rev(i), 0)), pl.BlockSpec((nb, tm, f), lambda i: (0, rev(i), 0)),
                  pl.BlockSpec((half, tm, f), lambda i: (0, rev(i), 0)),
                  pl.BlockSpec((nb, CONV_F, f), lambda i: (0, 0, 0))],
        out_specs=[pl.BlockSpec((nb, tm, f), lambda i: (0, rev(i), 0)), pl.BlockSpec((nb, CONV_F, f), lambda i: (0, 0, 0)),
                   pl.BlockSpec((nb, 1, f), lambda i: (0, 0, 0))],
        out_shape=[jax.ShapeDtypeStruct((nb, t, f), BF16), jax.ShapeDtypeStruct((nb, CONV_F, f), F32),
                   jax.ShapeDtypeStruct((nb, 1, f), F32)],
        scratch=[pltpu.VMEM((nb, HALO_F, f), F32), pltpu.VMEM((nb, CONV_F, ROWS_F, f), F32), pltpu.VMEM((nb, ROWS_F, f), F32)],
        args=(up0, upc, dact, conv_w), exch=exch)


def _all_gather_steps(x_refs, o_refs, send_sems, recv_sems, local_sems):
    n = len(x_refs)
    x, y, c = lax.axis_index("x"), lax.axis_index("y"), lax.axis_index("c")
    me, sibling = (x, y, c), (x, y, 1 - c)
    chips = [(1 - x, y), (x, 1 - y), (1 - x, 1 - y)]

    def slot(pos):
        return 4 * pos[0] + 2 * pos[1] + pos[2]

    def copy(a, k, block, to, src=None):
        dst = o_refs[a].at[slot(block)]
        return pltpu.make_async_remote_copy(
            src_ref=dst if src is None else src, dst_ref=dst, send_sem=send_sems.at[a * 7 + k],
            recv_sem=recv_sems.at[a * 7 + k], device_id=to, device_id_type=MESH)

    mine = [pltpu.make_async_copy(x_refs[a], o_refs[a].at[slot(me)], local_sems.at[a]) for a in range(n)]
    first = []
    for a in range(n):
        first.append(copy(a, 0, me, sibling, src=x_refs[a]))
        first += [copy(a, 1 + j, me, (*chip, c), src=x_refs[a]) for j, chip in enumerate(chips)]

    def start():
        for cp in mine + first:
            cp.start()

    def finish():
        passed = []
        for j, chip in enumerate(chips):
            for a in range(n):
                copy(a, 1 + j, (*chip, c), me).wait_recv()
                cp = copy(a, 4 + j, (*chip, c), sibling)
                cp.start()
                passed.append(cp)
        for a in range(n):
            copy(a, 0, sibling, me).wait_recv()
        for j, chip in enumerate(chips):
            for a in range(n):
                copy(a, 4 + j, (*chip, 1 - c), me).wait_recv()
        for cp in first + passed:
            cp.wait_send()
        for cp in mine:
            cp.wait()

    return start, finish


def _adamw_update(g, w, m, v):
    c1 = 1.0 - ADAM_B1 ** ADAM_STEP
    c2 = 1.0 - ADAM_B2 ** ADAM_STEP
    m_new = ADAM_B1 * m + (1.0 - ADAM_B1) * g
    v_new = ADAM_B2 * v + (1.0 - ADAM_B2) * (g * g)
    delta = -ADAM_LR * ((m_new / c1) / (jnp.sqrt(v_new / c2) + ADAM_EPS) + ADAM_WD * w)
    return m_new, v_new, delta


def _adamw(name, parts, w, m, v, rows_per_step):
    r, c = w.shape
    tr = r if r <= rows_per_step else (rows_per_step if r % rows_per_step == 0 else r // 2)

    def body(p_ref, w_ref, m_ref, v_ref, g_ref, d_ref, mo_ref, vo_ref):
        g = p_ref[0].astype(F32)
        for s in range(1, N_DEV):
            g = g + p_ref[s].astype(F32)
        m_new, v_new, delta = _adamw_update(g, w_ref[...], m_ref[...], v_ref[...])
        g_ref[...] = g
        mo_ref[...] = m_new
        vo_ref[...] = v_new
        d_ref[...] = delta

    row = pl.BlockSpec((tr, c), lambda i: (i, 0))
    out = jax.ShapeDtypeStruct((r, c), F32)
    return pl.pallas_call(body, grid=(r // tr,), in_specs=[pl.BlockSpec((N_DEV, tr, c), lambda i: (0, i, 0)), row, row, row],
                          out_specs=[row, row, row, row], out_shape=[out, out, out, out], name=name,
                          compiler_params=_params(1))(parts, w, m, v)


_VEC_NAMES = ("ln_v_g", "ln_v_b", "b_s", "conv_b_b", "ln_b_g", "ln_b_b", "g_ffn", "g_pg", "g_ple", "g_final")
_LOSS_ROW = len(_VEC_NAMES)
_REP_LAYOUT = dict({k: (i, 1, 1024) for i, k in enumerate(_VEC_NAMES)}, ffn_conv_b=(16, 8, 704), w_s=(24, 128, 1024))


def _pack_replicated_grads(d, loss_row):
    head = jnp.concatenate([d[k].reshape(1, 1024) for k in _VEC_NAMES] + [loss_row], axis=1).reshape(_LOSS_ROW + 1, 1024)
    return jnp.concatenate([jnp.pad(head, ((0, 16 - _LOSS_ROW - 1), (0, 0))),
                            jnp.pad(d["ffn_conv_b"].reshape(8, 704), ((0, 0), (0, 1024 - 704))),
                            d["w_s"].reshape(128, 1024)], axis=0)


def _rows8(vec):
    return jnp.pad(vec.reshape(1, 1024), ((0, 7), (0, 0)))


def _adamw_small(name, parts, items, loss_row=None):
    n = len(items)

    def body(p_ref, *refs):
        ins, outs = refs[:3 * n], refs[3 * n:]
        for a, (row, rows, cols) in enumerate(it[:3] for it in items):
            g = p_ref[0, row:row + rows, 0:cols]
            for s in range(1, N_DEV):
                g = g + p_ref[s, row:row + rows, 0:cols]
            m_new, v_new, delta = _adamw_update(g, ins[3 * a][...], ins[3 * a + 1][...], ins[3 * a + 2][...])
            for ref, val in zip(outs[4 * a:4 * a + 4], (g, delta, m_new, v_new)):
                ref[...] = val
        if loss_row is not None:
            total = p_ref[0, loss_row:loss_row + 1, 0:128]
            for s in range(1, N_DEV):
                total = total + p_ref[s, loss_row:loss_row + 1, 0:128]
            outs[-1][...] = total

    out_shape = [jax.ShapeDtypeStruct((rows, cols), F32) for _, rows, cols, *_ in items for _ in range(4)]
    if loss_row is not None:
        out_shape.append(jax.ShapeDtypeStruct((1, 128), F32))
    flat = pl.pallas_call(body, out_shape=out_shape, name=name,
                          compiler_params=pltpu.CompilerParams(vmem_limit_bytes=VMEM_LIMIT))(
        parts, *[arr for it in items for arr in it[3:]])
    res = [tuple(flat[4 * a:4 * a + 4]) for a in range(n)]
    return res + [flat[-1]] if loss_row is not None else res


def _pack_sharded_small(conv_b_w, ffn_conv_w):
    lead = conv_b_w.shape[:-2]
    pad0 = [(0, 0)] * len(lead)
    a = jnp.pad(conv_b_w, pad0 + [(0, 1), (0, 0)])
    b = jnp.pad(ffn_conv_w.reshape(lead + (CONV_F * 704,)), pad0 + [(0, 24 * 128 - CONV_F * 704)]).reshape(lead + (24, 128))
    return jnp.concatenate([a, b], axis=-2)


def _unpack_sharded_small(pk):
    lead = pk.shape[:-2]
    conv_b_w = pk[..., 0:CONV_B, :]
    ffn = pk[..., 32:56, :].reshape(lead + (24 * 128,))[..., :CONV_F * 704].reshape(lead + (CONV_F, 704))
    return conv_b_w, ffn


_WEIGHTS = ("g_mix", "w_in", "ln_v_g", "ln_v_b", "w_s", "b_s", "w_a_out", "conv_b_w", "conv_b_b", "ln_b_g", "ln_b_b",
            "w_b_out", "w_o", "g_ffn", "w_up", "ffn_conv_w", "ffn_conv_b", "w_down", "g_pg", "w_pg", "w_ple", "g_ple",
            "g_final")
_BIG = ("w_in", "w_a_out", "w_b_out", "w_o", "w_up", "w_down", "w_pg", "w_ple")


def kernel(x, p, g_mix, w_in, ln_v_g, ln_v_b, w_s, b_s, w_a_out, conv_b_w, conv_b_b, ln_b_g, ln_b_b, w_b_out, w_o, g_ffn, w_up, ffn_conv_w, ffn_conv_b, w_down, g_pg, w_pg, w_ple, g_ple, g_final, loss_target, m_g_mix, m_w_in, m_ln_v_g, m_ln_v_b, m_w_s, m_b_s, m_w_a_out, m_conv_b_w, m_conv_b_b, m_ln_b_g, m_ln_b_b, m_w_b_out, m_w_o, m_g_ffn, m_w_up, m_ffn_conv_w, m_ffn_conv_b, m_w_down, m_g_pg, m_w_pg, m_w_ple, m_g_ple, m_g_final, v_g_mix, v_w_in, v_ln_v_g, v_ln_v_b, v_w_s, v_b_s, v_w_a_out, v_conv_b_w, v_conv_b_b, v_ln_b_g, v_ln_b_b, v_w_b_out, v_w_o, v_g_ffn, v_w_up, v_ffn_conv_w, v_ffn_conv_b, v_w_down, v_g_pg, v_w_pg, v_w_ple, v_g_ple, v_g_final):
    local = dict(locals())
    wts = {k: local[k] for k in _WEIGHTS}
    mom = {k: local["m_" + k] for k in _WEIGHTS}
    var = {k: local["v_" + k] for k in _WEIGHTS}
    shapes = {k: wts[k].shape for k in _WEIGHTS}

    bsz, seq, d = x.shape
    t = bsz * seq
    x0 = x.reshape(t, d)
    p0 = p.reshape(t, p.shape[-1])
    target = loss_target.reshape(t, d)
    tm = min(TM_MM, t)
    tm_wide = min(TM_WIDE, t)
    tt = min(TT_MM, t)
    n_row = t // tm
    n_tok = t // tt

    def sq(a):
        return a.reshape(a.shape[1:])

    shard = {k: sq(wts[k]).astype(BF16) for k in _BIG}
    h1, (w_in3, small8) = _rms_fwd("rms_mix_gather_w_in", x0, g_mix,
                                   gather=[shard["w_in"], _pack_sharded_small(sq(conv_b_w), sq(ffn_conv_w))])
    conv_b_w8, ffn_conv_w8 = _unpack_sharded_small(small8)
    conv_w_full = conv_b_w8.transpose(1, 0, 2).reshape(CONV_B, N_DEV * conv_b_w8.shape[-1])
    n_in = w_in3.shape[2]
    f_dev = shard["w_up"].shape[1]
    n_blk, f_blk = N_DEV // 2, 2 * f_dev

    def pair_blocks(a):
        return a.reshape(n_blk, 2, a.shape[1], f_dev).transpose(0, 2, 1, 3).reshape(n_blk, a.shape[1], f_blk)

    def unpair_blocks(a):
        return a.reshape(n_blk, a.shape[1], 2, f_dev).transpose(0, 2, 1, 3).reshape(N_DEV, a.shape[1], f_dev)

    ws_m = jnp.where(jnp.tril(jnp.ones((CHUNK, CHUNK), bool))[None], sq(w_s), 0.0).astype(BF16)
    ws_mt = jnp.swapaxes(ws_m, 1, 2)
    bias_full = jnp.broadcast_to(sq(b_s).T[:, :, None], (CHUNK, GROUPS, CHUNK)).reshape(CHUNK, GROUPS * CHUNK)
    ffn_b = ffn_conv_b.reshape(n_blk, 1, f_blk)
    ffn_w = pair_blocks(ffn_conv_w8)

    z, (wa3, wb3, wo3) = _matmul(
        "mm_in", h1, w_in3, dims=_NN, grid=(N_DEV, t // tm_wide, 1),
        a_spec=pl.BlockSpec((tm_wide, d), lambda j, i, k: (i, 0)),
        b_spec=pl.BlockSpec((None, d, n_in), lambda j, i, k: (j, 0, 0)),
        o_spec=pl.BlockSpec((tm_wide, n_in), lambda j, i, k: (i, j)), acc_shape=(tm_wide, n_in),
        out_shape=jax.ShapeDtypeStruct((t, N_DEV * n_in), BF16),
        exch=[("gather", shard[k]) for k in ("w_a_out", "w_b_out", "w_o")])
    w_a = wa3.reshape(-1, d)
    w_b = wb3.reshape(-1, d)
    w_om = wo3.reshape(-1, d)
    (pa, cs, c_saved), (w_up3, wd3, wpg3, wple3) = _branch_fwd(
        z, ln_v_g, ln_v_b, ws_m, bias_full, conv_w_full, conv_b_b, ln_b_g, ln_b_b, seq,
        exch=[("gather", shard[k]) for k in ("w_up", "w_down", "w_pg", "w_ple")])
    w_pgm = wpg3.reshape(-1, d)
    w_upb = pair_blocks(w_up3)
    w_db = wd3.reshape(n_blk // 2, f_blk, d)
    w_plem = wple3.transpose(1, 0, 2).reshape(wple3.shape[1], d)
    ya = _mm_rows("mm_a_out", pa, w_a, dims=_NN, tm=tm, out_dtype=BF16)
    yb, merged = _mm_rows("mm_b_out", cs, w_b, dims=_NN, tm=tm, epi=_epi_merge_fwd(z, ya, tm))
    x1, h2 = _mm_rows("mm_o", merged, w_om, dims=_NN, tm=tm, epi=_epi_residual_rms(x0, g_ffn, tm))
    up0 = _matmul("mm_up", h2, w_upb, dims=_NN, grid=(n_blk, t // tm_wide, 1),
                  a_spec=pl.BlockSpec((tm_wide, d), lambda j, i, k: (i, 0)),
                  b_spec=pl.BlockSpec((None, d, f_blk), lambda j, i, k: (j, 0, 0)),
                  o_spec=pl.BlockSpec((None, tm_wide, f_blk), lambda j, i, k: (j, i, 0)), acc_shape=(tm_wide, f_blk),
                  out_shape=jax.ShapeDtypeStruct((n_blk, t, f_blk), BF16))
    act, upc = _ffn_mid_fwd(up0, ffn_w, ffn_b, seq)
    x2, hq = _matmul("mm_down", act, w_db, dims=_NN, grid=(n_row, 1, 1), split=n_blk // 2,
                     a_spec=pl.BlockSpec((n_blk // 2, tm, f_blk), lambda i, j, k: (0, i, 0)),
                     b_spec=pl.BlockSpec((n_blk // 2, f_blk, d), lambda i, j, k: (0, 0, 0)),
                     acc_shape=(tm, d), epi=_epi_residual_rms(x1, g_pg, tm))

    dx3, dq, dr, loss_v, dg_final, dg_ple = _mm_rows(
        "mm_pg", hq, w_pgm, dims=_NN, tm=tm // 2, epi=_epi_head(x2, p0, w_plem, target, g_ple, g_final.reshape(1, d), tm // 2))

    recv = {}
    gw_pg = _mm_wgrad("wg_pg", hq, dq, out_dtype=BF16, tt=tt).reshape(wpg3.shape)
    dw_ple = _mm_wgrad("wg_ple", p0, dr, out_dtype=BF16, tt=tt)
    gw_ple = dw_ple.reshape(dw_ple.shape[0], N_DEV, -1).transpose(1, 0, 2)
    dx2, dx2b, dg_pg = _mm_rows("mm_pg_t", dq, w_pgm, dims=_NT, tm=tm, epi=_epi_rms_bwd(x2, g_pg, dx3, tm, want_bf16=True))

    dact, (recv["w_pg"], recv["w_ple"]) = _matmul(
        "mm_down_t", dx2b, w_db, dims=_NT, grid=(n_blk // 2, n_row, 1),
        a_spec=pl.BlockSpec((tm, d), lambda j, i, k: (i, 0)),
        b_spec=pl.BlockSpec((None, f_blk, d), lambda j, i, k: (j, 0, 0)),
        o_spec=pl.BlockSpec((None, tm, f_blk), lambda j, i, k: (j, i, 0)), acc_shape=(tm, f_blk),
        out_shape=jax.ShapeDtypeStruct((n_blk // 2, t, f_blk), BF16),
        exch=[("scatter", gw_pg), ("scatter", gw_ple)])
    gw_down = _matmul("wg_down", act, dx2b, dims=_TN, grid=(n_blk // 2, 1, n_tok),
                      a_spec=pl.BlockSpec((None, tt, f_blk), lambda j, i, k: (j, k, 0)),
                      b_spec=pl.BlockSpec((tt, d), lambda j, i, k: (k, 0)),
                      o_spec=pl.BlockSpec((None, f_blk, d), lambda j, i, k: (j, 0, 0)), acc_shape=(f_blk, d),
                      out_shape=jax.ShapeDtypeStruct((n_blk // 2, f_blk, d), BF16)).reshape(wd3.shape)
    (d_up0, dffn_wb, dffn_b), (recv["w_down"],) = _ffn_mid_bwd(up0, upc, dact, ffn_w, seq, exch=[("scatter", gw_down)])
    dffn_w8 = unpair_blocks(dffn_wb)
    gw_up = _matmul("wg_up", h2, d_up0, dims=_TN, grid=(n_blk, 1, n_tok),
                    a_spec=pl.BlockSpec((tt, d), lambda j, i, k: (k, 0)),
                    b_spec=pl.BlockSpec((None, tt, f_blk), lambda j, i, k: (j, k, 0)),
                    o_spec=pl.BlockSpec((None, d, f_blk), lambda j, i, k: (j, 0, 0)), acc_shape=(d, f_blk),
                    out_shape=jax.ShapeDtypeStruct((n_blk, d, f_blk), BF16))
    gw_up = unpair_blocks(gw_up)
    (dx1, dx1b, dg_ffn), (recv["w_up"],) = _matmul(
        "mm_up_t", d_up0, w_upb, dims=_NT, grid=(n_row, 1, n_blk),
        a_spec=pl.BlockSpec((None, tm, f_blk), lambda i, j, k: (k, i, 0)),
        b_spec=pl.BlockSpec((None, d, f_blk), lambda i, j, k: (k, 0, 0)),
        acc_shape=(tm, d), epi=_epi_rms_bwd(x1, g_ffn, dx2, tm, want_bf16=True), exch=[("scatter", gw_up)])

    dya, dyb, dgates = _mm_rows("mm_o_t", dx1b, w_om, dims=_NT, tm=tm, epi=_epi_merge_bwd(z, ya, yb, tm))
    gw_o = _mm_wgrad("wg_o", merged, dx1b, out_dtype=BF16, tt=tt).reshape(wo3.shape)
    dpa = _mm_rows("mm_a_out_t", dya, w_a, dims=_NT, tm=tm, out_dtype=BF16)
    dcs = _mm_rows("mm_b_out_t", dyb, w_b, dims=_NT, tm=tm, out_dtype=BF16)
    gw_a = _mm_wgrad("wg_a_out", pa, dya, out_dtype=BF16, tt=tt).reshape(wa3.shape)
    gw_b = _mm_wgrad("wg_b_out", cs, dyb, out_dtype=BF16, tt=tt).reshape(wb3.shape)
    (dz, dlvg, dlvb, dws, dbs_full, dconv_w, dconv_b, dlbg, dlbb), (recv["w_o"], recv["w_a_out"], recv["w_b_out"]) = _branch_bwd(
        z, c_saved, dpa, dcs, dgates, ln_v_g, ln_v_b, ws_m, ws_mt, bias_full, conv_w_full, ln_b_g, ln_b_b, seq,
        exch=[("scatter", gw_o), ("scatter", gw_a), ("scatter", gw_b)])
    db_s = dbs_full[:, :GROUPS].T
    rep_partial = _pack_replicated_grads(
        dict(ln_v_g=dlvg, ln_v_b=dlvb, b_s=db_s, conv_b_b=dconv_b, ln_b_g=dlbg, ln_b_b=dlbb, g_ffn=dg_ffn,
             g_pg=dg_pg, g_ple=dg_ple, g_final=dg_final, ffn_conv_b=dffn_b, w_s=dws), loss_v)
    dconv_w8 = dconv_w.reshape(CONV_B, N_DEV, -1).transpose(1, 0, 2)
    small_partial = _pack_sharded_small(dconv_w8, dffn_w8)
    gw_in, (recv_small, recv_rep) = _matmul(
        "wg_in", h1, dz, dims=_TN, grid=(N_DEV, 1, n_tok),
        a_spec=pl.BlockSpec((tt, d), lambda j, i, k: (k, 0)),
        b_spec=pl.BlockSpec((tt, n_in), lambda j, i, k: (k, j)),
        o_spec=pl.BlockSpec((None, d, n_in), lambda j, i, k: (j, 0, 0)), acc_shape=(d, n_in),
        out_shape=jax.ShapeDtypeStruct((N_DEV, d, n_in), BF16),
        exch=[("scatter", small_partial), ("gather", rep_partial)])
    (grad_x, dg_mix), (recv["w_in"],) = _matmul(
        "mm_in_t", dz, w_in3, dims=_NT, grid=(n_row, 1, N_DEV // 2), split=2,
        a_spec=pl.BlockSpec((tm, 2 * n_in), lambda i, j, k: (i, k)),
        b_spec=pl.BlockSpec((2, d, n_in), lambda i, j, k: (k, 0, 0)),
        acc_shape=(tm, d), epi=_epi_rms_bwd(x0, g_mix, dx1, tm, want_bf16=False), exch=[("scatter", gw_in)])
    (recv_g_mix,) = _exchange("exchange_g_mix", [("gather", _rows8(dg_mix))])

    grads, deltas, new_m, new_v = {}, {}, {}, {}
    by_kind = (grads, deltas, new_m, new_v)

    def two_d(a):
        a = sq(a)
        return a.reshape(-1, a.shape[-1])

    for k in _BIG:
        parts = recv[k].reshape(N_DEV, -1, recv[k].shape[-1])
        outs = _adamw("adamw_" + k, parts, two_d(wts[k]), two_d(mom[k]), two_d(var[k]), ADAMW_ROWS)
        for tgt, o in zip(by_kind, outs):
            tgt[k] = o.reshape(shapes[k])

    small = [_pack_sharded_small(sq(s["conv_b_w"]), sq(s["ffn_conv_w"])) for s in (wts, mom, var)]
    for tgt, o in zip(by_kind, _adamw("adamw_conv", recv_small, small[0], small[1], small[2], 56)):
        cw, fw = _unpack_sharded_small(o)
        tgt["conv_b_w"] = cw.reshape(shapes["conv_b_w"])
        tgt["ffn_conv_w"] = fw.reshape(shapes["ffn_conv_w"])

    names = list(_REP_LAYOUT)
    items = [_REP_LAYOUT[k] + tuple(s[k].reshape(_REP_LAYOUT[k][1:]) for s in (wts, mom, var)) for k in names]
    *rep_outs, loss_sum = _adamw_small("adamw_replicated", recv_rep, items, loss_row=_LOSS_ROW)
    for k, outs in zip(names, rep_outs):
        for tgt, o in zip(by_kind, outs):
            tgt[k] = o.reshape(shapes[k])
    loss = loss_sum[0, 0]

    g_mix_item = (0, 1, 1024) + tuple(s["g_mix"].reshape(1, 1024) for s in (wts, mom, var))
    for tgt, o in zip(by_kind, _adamw_small("adamw_g_mix", recv_g_mix, [g_mix_item])[0]):
        tgt["g_mix"] = o.reshape(shapes["g_mix"])

    return (loss, grad_x.reshape(x.shape), *[grads[k] for k in _WEIGHTS], *[deltas[k] for k in _WEIGHTS],
            *[new_m[k] for k in _WEIGHTS], *[new_v[k] for k in _WEIGHTS])
```

```python
import math

import jax
import jax.numpy as jnp
from jax import lax
from jax.experimental import pallas as pl
from jax.experimental.pallas import tpu as pltpu

F32 = jnp.float32
BF16 = jnp.bfloat16

N_DEV = 8
EPS_RMS = 1e-6
EPS_LN = 1e-5
CHUNK = 128
GROUPS = 8
CONV_B = 31
CONV_F = 3
HALO_B = 32
HALO_F = 8
ROWS_F = 16
SUB = 8

ADAM_LR = 0.001
ADAM_B1 = 0.9
ADAM_B2 = 0.999
ADAM_EPS = 1e-08
ADAM_WD = 0.01
ADAM_STEP = 10

VMEM_LIMIT = 56 * 1024 * 1024
TM_MM = 1024
TM_WIDE = 2048
TT_MM = 2048
TM_EW = 256
TM_FFN = 512
ADAMW_ROWS = 512
EPI_STRIP = 256

_NN = (((1,), (0,)), ((), ()))
_NT = (((1,), (1,)), ((), ()))
_TN = (((0,), (0,)), ((), ()))
MESH = pl.DeviceIdType.MESH
HBM_SPEC = pl.BlockSpec(memory_space=pltpu.HBM)


def _params(n_axes):
    return pltpu.CompilerParams(dimension_semantics=("arbitrary",) * n_axes, vmem_limit_bytes=VMEM_LIMIT)


def _gelu(x):
    k = math.sqrt(2.0 / math.pi)
    return 0.5 * x * (1.0 + jnp.tanh(k * (x + 0.044715 * (x * x * x))))


def _gelu_and_grad(x):
    k = math.sqrt(2.0 / math.pi)
    x2 = x * x
    t = jnp.tanh(k * (x + 0.044715 * (x2 * x)))
    g = 0.5 * x * (1.0 + t)
    dg = 0.5 * (1.0 + t) + 0.5 * x * (1.0 - t * t) * (k * (1.0 + 3.0 * 0.044715 * x2))
    return g, dg


def _sigmoid(x):
    return 1.0 / (1.0 + jnp.exp(-x))


def _rowsum(x):
    return jnp.sum(x, axis=0, keepdims=True)


def _mean(x):
    return jnp.mean(x, axis=-1, keepdims=True)


def _exchange_io(exch):
    n = len(exch)
    out_shape = [jax.ShapeDtypeStruct(v.shape if kind == "scatter" else (N_DEV,) + v.shape, v.dtype) for kind, v in exch]
    scratch = [pltpu.SemaphoreType.DMA((7 * n,)), pltpu.SemaphoreType.DMA((7 * n,)), pltpu.SemaphoreType.DMA((n,))] if n else []
    return [HBM_SPEC] * n, [HBM_SPEC] * n, out_shape, scratch


def _exchange_step(kinds, x_refs, o_refs, send_sems, recv_sems, local_sems):
    n = len(kinds)
    x, y, c = lax.axis_index("x"), lax.axis_index("y"), lax.axis_index("c")
    me = 4 * x + 2 * y + c

    def src(a, to_slot):
        return x_refs[a].at[to_slot] if kinds[a] == "scatter" else x_refs[a]

    mine = [pltpu.make_async_copy(src(a, me), o_refs[a].at[me], local_sems.at[a]) for a in range(n)]
    sends, recvs = [], []
    for m in range(1, N_DEV):
        mx, my, mc = (m >> 2) & 1, (m >> 1) & 1, m & 1
        px, py, pc = (1 - x if mx else x), (1 - y if my else y), (1 - c if mc else c)
        peer = 4 * px + 2 * py + pc
        for a in range(n):
            k = a * 7 + m - 1
            sends.append(pltpu.make_async_remote_copy(
                src_ref=src(a, peer), dst_ref=o_refs[a].at[me], send_sem=send_sems.at[k], recv_sem=recv_sems.at[k],
                device_id=(px, py, pc), device_id_type=MESH))
            recvs.append(pltpu.make_async_remote_copy(
                src_ref=src(a, peer), dst_ref=o_refs[a].at[peer], send_sem=send_sems.at[k], recv_sem=recv_sems.at[k],
                device_id=(px, py, pc), device_id_type=MESH))

    def start():
        for cp in mine + sends:
            cp.start()

    def finish():
        for cp in recvs:
            cp.wait_recv()
        for cp in sends:
            cp.wait_send()
        for cp in mine:
            cp.wait()

    return start, finish


def _exchange(name, exch):
    n = len(exch)
    kinds = [k for k, _ in exch]
    in_specs, out_specs, out_shape, scratch = _exchange_io(exch)

    def body(*refs):
        start, finish = _exchange_step(kinds, refs[:n], refs[n:2 * n], *refs[2 * n:])
        start()
        finish()

    return pl.pallas_call(body, in_specs=in_specs, out_specs=out_specs, out_shape=out_shape, scratch_shapes=scratch,
                          name=name)(*[v for _, v in exch])


class _Epilogue:
    def __init__(self, fn, ins=(), in_specs=(), out_specs=(), out_shape=(), strip=None):
        self.fn, self.ins, self.in_specs = fn, list(ins), list(in_specs)
        self.out_specs, self.out_shape = list(out_specs), list(out_shape)
        self.strip = strip


def _matmul(name, a, b, *, dims, grid, a_spec, b_spec, acc_shape, o_spec=None, out_shape=None, epi=None, exch=(), split=1):
    nk = grid[2]
    plain = epi is None
    if plain:
        def store(acc, ins, outs, i):
            outs[0][...] = acc.astype(outs[0].dtype)
        epi = _Epilogue(store, out_specs=[o_spec], out_shape=[out_shape])
    n_in = 2 + len(epi.ins)
    n_out = len(epi.out_specs)
    n_ex = len(exch)
    kinds = [k for k, _ in exch]
    ex_in, ex_out, ex_shape, ex_scratch = _exchange_io(exch)

    def body(*refs):
        a_ref, b_ref = refs[:2]
        step0 = pl.program_id(0)
        epi_ins, rest = refs[2:n_in], refs[n_in:]
        x_refs, rest = rest[:n_ex], rest[n_ex:]
        outs, rest = rest[:n_out], rest[n_out:]
        o_refs, scr = rest[:n_ex], rest[n_ex:]
        if n_ex:
            pid = [pl.program_id(ax) for ax in range(3)]
            ex_start, ex_finish = _exchange_step(kinds, x_refs, o_refs, *scr[len(scr) - 3:])
            pl.when((pid[0] == 0) & (pid[1] == 0) & (pid[2] == 0))(ex_start)
        if split == 1:
            part = lax.dot_general(a_ref[...].astype(BF16), b_ref[...].astype(BF16), dims, preferred_element_type=F32)
        else:
            kk = b_ref.shape[-1]
            part = None
            for s in range(split):
                a_s = a_ref[s] if len(a_ref.shape) == 3 else a_ref[:, s * kk:(s + 1) * kk]
                p_s = lax.dot_general(a_s.astype(BF16), b_ref[s].astype(BF16), dims, preferred_element_type=F32)
                part = p_s if part is None else part + p_s

        def run_epilogue(rows_of_acc):
            rows = acc_shape[0]
            strip = rows if epi.strip is None else min(epi.strip, rows)
            for s in range(0, rows, strip):
                def view(ref):
                    return ref.at[pl.ds(s, strip)] if ref.shape[0] == rows else ref
                first = (step0 == 0) if s == 0 else False
                epi.fn(rows_of_acc(s, strip), [view(r) for r in epi_ins], [view(r) for r in outs], first)

        if nk == 1:
            run_epilogue(lambda s, n: part[s:s + n])
        else:
            acc_ref = scr[0]
            k = pl.program_id(2)

            @pl.when(k == 0)
            def _():
                acc_ref[...] = part

            @pl.when(k > 0)
            def _():
                acc_ref[...] += part

            @pl.when(k == nk - 1)
            def _():
                run_epilogue(lambda s, n: acc_ref[pl.ds(s, n), :])
        if n_ex:
            pl.when((pid[0] == grid[0] - 1) & (pid[1] == grid[1] - 1) & (pid[2] == grid[2] - 1))(ex_finish)

    scratch = ([pltpu.VMEM(acc_shape, F32)] if nk > 1 else []) + ex_scratch
    res = pl.pallas_call(body, grid=grid, in_specs=[a_spec, b_spec] + epi.in_specs + ex_in,
                         out_specs=epi.out_specs + ex_out, out_shape=epi.out_shape + ex_shape, scratch_shapes=scratch,
                         name=name, compiler_params=_params(3))(a, b, *epi.ins, *[v for _, v in exch])
    main = res[0] if plain else res[:n_out]
    return (main, res[n_out:]) if n_ex else main


def _mm_rows(name, a, w, *, dims, tm, out_dtype=None, epi=None):
    t, k = a.shape
    n = w.shape[1] if dims == _NN else w.shape[0]
    tm = min(tm, t)
    return _matmul(name, a, w, dims=dims, grid=(t // tm, 1, 1),
                   a_spec=pl.BlockSpec((tm, k), lambda i, j, kk: (i, 0)),
                   b_spec=pl.BlockSpec(w.shape, lambda i, j, kk: (0, 0)),
                   o_spec=pl.BlockSpec((tm, n), lambda i, j, kk: (i, 0)), acc_shape=(tm, n),
                   out_shape=jax.ShapeDtypeStruct((t, n), out_dtype) if epi is None else None, epi=epi)


def _mm_wgrad(name, a, b, *, out_dtype, tt):
    t, m = a.shape
    n = b.shape[1]
    tt = min(tt, t)
    return _matmul(name, a, b, dims=_TN, grid=(1, 1, t // tt),
                   a_spec=pl.BlockSpec((tt, m), lambda i, j, kk: (kk, 0)),
                   b_spec=pl.BlockSpec((tt, n), lambda i, j, kk: (kk, 0)),
                   o_spec=pl.BlockSpec((m, n), lambda i, j, kk: (0, 0)),
                   acc_shape=(m, n), out_shape=jax.ShapeDtypeStruct((m, n), out_dtype))


def _row3(tm, d, col=0):
    return pl.BlockSpec((tm, d), lambda i, j, k: (i, col))


def _vec3(d):
    return pl.BlockSpec((1, d), lambda i, j, k: (0, 0))


def _accumulate_over_rows(ref, part, first):
    if first is False:
        ref[...] += part
        return

    @pl.when(first)
    def _():
        ref[...] = part + jnp.zeros_like(ref)

    @pl.when(jnp.logical_not(first))
    def _():
        ref[...] += part


def _epi_residual_rms(res, g, tm):
    t, d = res.shape

    def fn(acc, ins, outs, i):
        res_ref, g_ref = ins
        xv = acc + res_ref[...]
        outs[0][...] = xv
        rstd = lax.rsqrt(_mean(xv * xv) + EPS_RMS)
        outs[1][...] = ((xv * rstd) * g_ref[...]).astype(BF16)

    return _Epilogue(fn, strip=EPI_STRIP, ins=[res, g], in_specs=[_row3(tm, d), _vec3(d)], out_specs=[_row3(tm, d), _row3(tm, d)],
                     out_shape=[jax.ShapeDtypeStruct((t, d), F32), jax.ShapeDtypeStruct((t, d), BF16)])


def _epi_rms_bwd(x, g, dres, tm, *, want_bf16):
    t, d = x.shape

    def fn(acc, ins, outs, i):
        x_ref, g_ref, dres_ref = ins
        xv = x_ref[...]
        rstd = lax.rsqrt(_mean(xv * xv) + EPS_RMS)
        nrm = xv * rstd
        dn = acc * g_ref[...]
        dx = dres_ref[...] + rstd * (dn - nrm * _mean(dn * nrm))
        outs[0][...] = dx
        if want_bf16:
            outs[1][...] = dx.astype(BF16)
        _accumulate_over_rows(outs[-1], _rowsum(acc * nrm), i)

    row = _row3(tm, d)
    n_dx = 2 if want_bf16 else 1
    return _Epilogue(fn, strip=EPI_STRIP, ins=[x, g, dres], in_specs=[row, _vec3(d), row], out_specs=[row] * n_dx + [_vec3(d)],
                     out_shape=[jax.ShapeDtypeStruct((t, d), F32)] + [jax.ShapeDtypeStruct((t, d), BF16)] * (n_dx - 1)
                     + [jax.ShapeDtypeStruct((1, d), F32)])


def _epi_merge_fwd(z, ya, tm):
    t, w = ya.shape

    def fn(acc, ins, outs, i):
        ga_ref, gb_ref, ya_ref = ins
        outs[0][...] = acc.astype(BF16)
        sa = _sigmoid(ga_ref[...].astype(F32))
        sb = _sigmoid(gb_ref[...].astype(F32))
        outs[1][...] = (sa * ya_ref[...].astype(F32) + sb * acc).astype(BF16)

    row = _row3(tm, w)
    return _Epilogue(fn, strip=EPI_STRIP, ins=[z, z, ya], in_specs=[_row3(tm, w, 4), _row3(tm, w, 5), row], out_specs=[row, row],
                     out_shape=[jax.ShapeDtypeStruct((t, w), BF16)] * 2)


def _epi_merge_bwd(z, ya, yb, tm):
    t, w = ya.shape

    def fn(acc, ins, outs, i):
        ga_ref, gb_ref, ya_ref, yb_ref = ins
        sa = _sigmoid(ga_ref[...].astype(F32))
        sb = _sigmoid(gb_ref[...].astype(F32))
        outs[0][...] = (acc * sa).astype(BF16)
        outs[1][...] = (acc * sb).astype(BF16)
        outs[2][:, 0:w] = (acc * ya_ref[...].astype(F32) * sa * (1.0 - sa)).astype(BF16)
        outs[2][:, w:2 * w] = (acc * yb_ref[...].astype(F32) * sb * (1.0 - sb)).astype(BF16)

    row = _row3(tm, w)
    return _Epilogue(fn, strip=EPI_STRIP, ins=[z, z, ya, yb], in_specs=[_row3(tm, w, 4), _row3(tm, w, 5), row, row],
                     out_specs=[row, row, _row3(tm, 2 * w)],
                     out_shape=[jax.ShapeDtypeStruct((t, w), BF16)] * 2 + [jax.ShapeDtypeStruct((t, 2 * w), BF16)])


def _epi_head(x2, p, w_ple, target, g_ple, g_final, tm):
    t, d = x2.shape

    def fn(acc, ins, outs, i):
        x2_ref, p_ref, wple_ref, tg_ref, gple_ref, gfin_ref = ins
        dx3_ref, dq_ref, dr_ref, loss_ref, dgfin_ref, dgple_ref = outs
        pg = _sigmoid(acc)
        rv = lax.dot_general(p_ref[...].astype(BF16), wple_ref[...], _NN, preferred_element_type=F32)
        rstd_r = lax.rsqrt(_mean(rv * rv) + EPS_RMS)
        nr = rv * rstd_r
        pe = nr * gple_ref[...]
        x3 = x2_ref[...] + pe * pg
        rstd3 = lax.rsqrt(_mean(x3 * x3) + EPS_RMS)
        n3 = x3 * rstd3
        err = n3 * gfin_ref[...] - tg_ref[...]
        loss_part = jnp.sum(_rowsum(err * err), axis=1, keepdims=True) * (0.5 / d)
        dy = err * (1.0 / d)
        dn3 = dy * gfin_ref[...]
        dx3 = rstd3 * (dn3 - n3 * _mean(dn3 * n3))
        dx3_ref[...] = dx3
        dq_ref[...] = (dx3 * pe * pg * (1.0 - pg)).astype(BF16)
        dpe = dx3 * pg
        dnr = dpe * gple_ref[...]
        dr_ref[...] = (rstd_r * (dnr - nr * _mean(dnr * nr))).astype(BF16)
        _accumulate_over_rows(loss_ref, loss_part, i)
        _accumulate_over_rows(dgfin_ref, _rowsum(dy * n3), i)
        _accumulate_over_rows(dgple_ref, _rowsum(dpe * nr), i)

    row = _row3(tm, d)
    vec = jax.ShapeDtypeStruct((1, d), F32)
    return _Epilogue(fn, strip=EPI_STRIP, ins=[x2, p, w_ple, target, g_ple, g_final],
                     in_specs=[row, _row3(tm, p.shape[1]), pl.BlockSpec(w_ple.shape, lambda i, j, k: (0, 0)), row, _vec3(d), _vec3(d)],
                     out_specs=[row, row, row, _vec3(d), _vec3(d), _vec3(d)],
                     out_shape=[jax.ShapeDtypeStruct((t, d), F32), jax.ShapeDtypeStruct((t, d), BF16),
                                jax.ShapeDtypeStruct((t, d), BF16), vec, vec, vec])


def _vec_spec(d):
    return pl.BlockSpec((1, d), lambda i: (0, 0))


def _row_call(name, body, *, n_steps, in_specs, out_specs, out_shape, scratch, args, exch=()):
    n_in, n_out, n_scr, n_ex = len(in_specs), len(out_specs), len(scratch), len(exch)
    kinds = [k for k, _ in exch]
    ex_in, ex_out, ex_shape, ex_scratch = _exchange_io(exch)

    def wrapped(*refs):
        ins, rest = refs[:n_in], refs[n_in:]
        x_refs, rest = rest[:n_ex], rest[n_ex:]
        outs, rest = rest[:n_out], rest[n_out:]
        o_refs, rest = rest[:n_ex], rest[n_ex:]
        scr, sems = rest[:n_scr], rest[n_scr:]
        if n_ex:
            ex_start, ex_finish = _exchange_step(kinds, x_refs, o_refs, *sems)
            pl.when(pl.program_id(0) == 0)(ex_start)
        body(*ins, *outs, *scr)
        if n_ex:
            pl.when(pl.program_id(0) == n_steps - 1)(ex_finish)

    res = pl.pallas_call(wrapped, grid=(n_steps,), in_specs=list(in_specs) + ex_in, out_specs=list(out_specs) + ex_out,
                         out_shape=list(out_shape) + ex_shape, scratch_shapes=list(scratch) + ex_scratch, name=name,
                         compiler_params=_params(1))(*args, *[v for _, v in exch])
    return res[:n_out], res[n_out:]


def _rms_fwd(name, x, g, gather=()):
    t, d = x.shape
    tm = min(TM_EW * 2, t)
    n_steps = t // tm
    n = len(gather)

    def body(x_ref, g_ref, *refs):
        h_ref = refs[n]
        if n:
            start, finish = _all_gather_steps(refs[:n], refs[n + 1:2 * n + 1], *refs[2 * n + 1:])
            pl.when(pl.program_id(0) == 0)(start)
        xv = x_ref[...]
        rstd = lax.rsqrt(_mean(xv * xv) + EPS_RMS)
        h_ref[...] = ((xv * rstd) * g_ref[...]).astype(BF16)
        if n:
            pl.when(pl.program_id(0) == n_steps - 1)(finish)

    row = pl.BlockSpec((tm, d), lambda i: (i, 0))
    sems = [pltpu.SemaphoreType.DMA((7 * n,)), pltpu.SemaphoreType.DMA((7 * n,)), pltpu.SemaphoreType.DMA((n,))] if n else []
    res = pl.pallas_call(
        body, grid=(n_steps,), in_specs=[row, _vec_spec(d)] + [HBM_SPEC] * n, out_specs=[row] + [HBM_SPEC] * n,
        out_shape=[jax.ShapeDtypeStruct((t, d), BF16)] + [jax.ShapeDtypeStruct((N_DEV,) + v.shape, v.dtype) for v in gather],
        scratch_shapes=sems, name=name, compiler_params=_params(1))(x, g, *gather)
    return res[0], res[1:]


def _fill_shifted(buf_ref, sh_ref):
    n = sh_ref.shape[1]
    for p in range(1, SUB):
        sh_ref[p - 1] = buf_ref[p:p + n, :]


def _branch_fwd(z, ln_v_g, ln_v_b, ws_m, bias_full, conv_w, conv_b, ln_b_g, ln_b_b, seq, exch=()):
    t = z.shape[0]
    w = 1024
    tm = min(TM_EW, seq)
    tiles_per_seq = seq // tm
    n_chunks = tm // CHUNK

    def body(z_ref, lvg_ref, lvb_ref, ws_ref, bias_ref, cw_ref, cb_ref, lbg_ref, lbb_ref,
             pa_ref, cs_ref, c_ref, hist_ref, buf_ref, mix_ref, sh_ref, wb_ref):
        i = pl.program_id(0)
        u = z_ref[:, 0:w].astype(F32)
        v = z_ref[:, w:2 * w].astype(F32)
        ug = _gelu(u)
        vg = _gelu(v)
        dv = vg - _mean(vg)
        vhat = dv * lax.rsqrt(_mean(dv * dv) + EPS_LN)
        vn = (vhat * lvg_ref[...] + lvb_ref[...]).astype(BF16)
        for ci in range(n_chunks):
            rows = slice(ci * CHUNK, (ci + 1) * CHUNK)
            for g in range(GROUPS):
                cols = slice(g * CHUNK, (g + 1) * CHUNK)
                mix_ref[rows, cols] = lax.dot_general(ws_ref[g], vn[rows, cols], _NN, preferred_element_type=F32)
            mix_ref[rows, :] += bias_ref[...]
        pa_ref[...] = (ug * mix_ref[...]).astype(BF16)

        a = z_ref[:, 2 * w:3 * w].astype(F32)
        gl = z_ref[:, 3 * w:4 * w].astype(F32)
        glu = a * _sigmoid(gl)

        @pl.when(i % tiles_per_seq == 0)
        def _():
            hist_ref[...] = jnp.zeros_like(hist_ref)

        buf_ref[0:HALO_B, :] = hist_ref[...]
        buf_ref[HALO_B:, :] = glu
        hist_ref[...] = glu[tm - HALO_B:, :]
        _fill_shifted(buf_ref, sh_ref)

        @pl.when(i == 0)
        def _():
            for k in range(CONV_B):
                wb_ref[k] = jnp.broadcast_to(cw_ref[k:k + 1, :], (SUB, w))

        groups = 4

        def strip(si, _):
            s = pl.multiple_of(si * (groups * SUB), groups * SUB)
            acc = [jnp.zeros((SUB, w), F32) + cb_ref[...] for _ in range(groups)]
            for k in range(CONV_B):
                whole, part = divmod(HALO_B - (CONV_B - 1) + k, SUB)
                wk = wb_ref[k]
                for g in range(groups):
                    at = pl.ds(s + SUB * (whole + g), SUB)
                    acc[g] = acc[g] + wk * (buf_ref[at, :] if part == 0 else sh_ref[part - 1, at, :])
            for g in range(0, groups, 2):
                at = pl.ds(s + SUB * g, 2 * SUB)
                c = jnp.concatenate(acc[g:g + 2], axis=0)
                c_ref[at, :] = c
                dc = c - _mean(c)
                chat = dc * lax.rsqrt(_mean(dc * dc) + EPS_LN)
                cn = chat * lbg_ref[...] + lbb_ref[...]
                cs_ref[at, :] = (cn * _sigmoid(cn)).astype(BF16)
            return 0

        lax.fori_loop(0, tm // (groups * SUB), strip, 0)

    row = pl.BlockSpec((tm, w), lambda i: (i, 0))
    in_specs = [pl.BlockSpec((tm, 4 * w), lambda i: (i, 0)), _vec_spec(w), _vec_spec(w),
                pl.BlockSpec((GROUPS, CHUNK, CHUNK), lambda i: (0, 0, 0)), pl.BlockSpec((CHUNK, w), lambda i: (0, 0)),
                pl.BlockSpec((CONV_B, w), lambda i: (0, 0)), _vec_spec(w), _vec_spec(w), _vec_spec(w)]
    return _row_call(
        "branch_fwd", body, n_steps=t // tm, in_specs=in_specs, out_specs=[row, row, row],
        out_shape=[jax.ShapeDtypeStruct((t, w), BF16), jax.ShapeDtypeStruct((t, w), BF16), jax.ShapeDtypeStruct((t, w), F32)],
        scratch=[pltpu.VMEM((HALO_B, w), F32), pltpu.VMEM((HALO_B + tm, w), F32), pltpu.VMEM((tm, w), F32),
                 pltpu.VMEM((SUB - 1, HALO_B + tm - SUB, w), F32), pltpu.VMEM((CONV_B, SUB, w), F32)],
        args=(z, ln_v_g, ln_v_b, ws_m, bias_full, conv_w, conv_b, ln_b_g, ln_b_b), exch=exch)


def _branch_bwd(z, c_saved, dpa, dcs, dgates, ln_v_g, ln_v_b, ws_m, ws_mt, bias_full, conv_w, ln_b_g, ln_b_b, seq, exch=()):
    t = z.shape[0]
    w = 1024
    tm = min(TM_EW, seq)
    tiles_per_seq = seq // tm
    n_tiles = t // tm
    n_chunks = tm // CHUNK

    def body(z_ref, c_ref, dpa_ref, dcs_ref, dgt_ref, lvg_ref, lvb_ref, ws_ref, wst_ref, bias_ref, cw_ref,
             lbg_ref, lbb_ref,
             dz_ref, dlvg_ref, dlvb_ref, dws_ref, dbs_ref, dcw_ref, dcb_ref, dlbg_ref, dlbb_ref,
             carry_ref, glu_ref, dbuf_ref, mix_ref, dvn_ref, dbs_acc_ref, dglu_ref, sh_ref, wb_ref, dwacc_ref):
        i = pl.program_id(0)
        r = n_tiles - 1 - i

        @pl.when(i == 0)
        def _():
            for ref in (dlvg_ref, dlvb_ref, dws_ref, dbs_acc_ref, dwacc_ref, dcb_ref, dlbg_ref, dlbb_ref):
                ref[...] = jnp.zeros_like(ref)

        u = z_ref[:, 0:w].astype(F32)
        v = z_ref[:, w:2 * w].astype(F32)
        ug, dug = _gelu_and_grad(u)
        vg, dvg = _gelu_and_grad(v)
        dv0 = vg - _mean(vg)
        rstd_v = lax.rsqrt(_mean(dv0 * dv0) + EPS_LN)
        vhat = dv0 * rstd_v
        vn = (vhat * lvg_ref[...] + lvb_ref[...]).astype(BF16)
        dpa = dpa_ref[...].astype(F32)
        dmix = dpa * ug
        dmix_b = dmix.astype(BF16)
        for ci in range(n_chunks):
            rows = slice(ci * CHUNK, (ci + 1) * CHUNK)
            for g in range(GROUPS):
                cols = slice(g * CHUNK, (g + 1) * CHUNK)
                mix_ref[rows, cols] = lax.dot_general(ws_ref[g], vn[rows, cols], _NN, preferred_element_type=F32)
                dvn_ref[rows, cols] = lax.dot_general(wst_ref[g], dmix_b[rows, cols], _NN, preferred_element_type=F32)
                dws_ref[g] += lax.dot_general(dmix_b[rows, cols], vn[rows, cols], _NT, preferred_element_type=F32)
            mix_ref[rows, :] += bias_ref[...]
            dbs_acc_ref[...] += dmix[rows, :]
        dz_ref[:, 0:w] = (dpa * mix_ref[...] * dug).astype(BF16)
        dvn = dvn_ref[...]
        dlvg_ref[...] += _rowsum(dvn * vhat)
        dlvb_ref[...] += _rowsum(dvn)
        dvh = dvn * lvg_ref[...]
        dvg_in = rstd_v * (dvh - _mean(dvh) - vhat * _mean(dvh * vhat))
        dz_ref[:, w:2 * w] = (dvg_in * dvg).astype(BF16)

        c = c_ref[...]
        dc0 = c - _mean(c)
        rstd_c = lax.rsqrt(_mean(dc0 * dc0) + EPS_LN)
        chat = dc0 * rstd_c
        cn = chat * lbg_ref[...] + lbb_ref[...]
        sg = _sigmoid(cn)
        dcn = dcs_ref[...].astype(F32) * (sg * (1.0 + cn * (1.0 - sg)))
        dlbg_ref[...] += _rowsum(dcn * chat)
        dlbb_ref[...] += _rowsum(dcn)
        dch = dcn * lbg_ref[...]
        dc = rstd_c * (dch - _mean(dch) - chat * _mean(dch * chat))
        dcb_ref[...] += _rowsum(dc)

        a = z_ref[:, 2 * w:3 * w].astype(F32)
        gl = z_ref[:, 3 * w:4 * w].astype(F32)
        sgl = _sigmoid(gl)
        glu_ref[...] = a * sgl

        @pl.when(r % tiles_per_seq == tiles_per_seq - 1)
        def _():
            carry_ref[...] = jnp.zeros_like(carry_ref)

        dbuf_ref[0:tm, :] = dc
        dbuf_ref[tm:, :] = carry_ref[...]
        carry_ref[...] = dc[0:HALO_B, :]
        _fill_shifted(dbuf_ref, sh_ref)

        @pl.when(i == 0)
        def _():
            for k in range(CONV_B):
                wb_ref[k] = jnp.broadcast_to(cw_ref[k:k + 1, :], (SUB, w))

        groups = 2

        def strip(si, _):
            s = pl.multiple_of(si * (groups * SUB), groups * SUB)
            glu_rows = [glu_ref[pl.ds(s + SUB * g, SUB), :] for g in range(groups)]
            acc = [jnp.zeros((SUB, w), F32) for _ in range(groups)]
            for k in range(CONV_B):
                whole, part = divmod(CONV_B - 1 - k, SUB)
                wk = wb_ref[k]
                dw_part = jnp.zeros((SUB, w), F32)
                for g in range(groups):
                    at = pl.ds(s + SUB * (whole + g), SUB)
                    d_rows = dbuf_ref[at, :] if part == 0 else sh_ref[part - 1, at, :]
                    acc[g] = acc[g] + wk * d_rows
                    dw_part = dw_part + d_rows * glu_rows[g]
                dwacc_ref[k] += dw_part
            dglu_ref[pl.ds(s, groups * SUB), :] = jnp.concatenate(acc, axis=0)
            return 0

        lax.fori_loop(0, tm // (groups * SUB), strip, 0)
        dglu = dglu_ref[...]
        dz_ref[:, 2 * w:3 * w] = (dglu * sgl).astype(BF16)
        dz_ref[:, 3 * w:4 * w] = (dglu * a * sgl * (1.0 - sgl)).astype(BF16)
        dz_ref[:, 4 * w:6 * w] = dgt_ref[...]

        @pl.when(i == n_tiles - 1)
        def _():
            tri = lax.broadcasted_iota(jnp.int32, (CHUNK, CHUNK), 0) >= lax.broadcasted_iota(jnp.int32, (CHUNK, CHUNK), 1)
            lane = lax.broadcasted_iota(jnp.int32, (CHUNK, CHUNK), 1)
            dbs = jnp.zeros((CHUNK, CHUNK), F32)
            for g in range(GROUPS):
                dws_ref[g] = jnp.where(tri, dws_ref[g], 0.0)
                group_sum = jnp.sum(dbs_acc_ref[:, g * CHUNK:(g + 1) * CHUNK], axis=1, keepdims=True)
                dbs = jnp.where(lane == g, group_sum, dbs)
            dbs_ref[...] = dbs
            for k in range(CONV_B):
                dcw_ref[k:k + 1, :] = _rowsum(dwacc_ref[k])

    def rev(i):
        return n_tiles - 1 - i

    row = pl.BlockSpec((tm, w), lambda i: (rev(i), 0))
    full = lambda shape: pl.BlockSpec(shape, lambda i: (0,) * len(shape))
    in_specs = [pl.BlockSpec((tm, 4 * w), lambda i: (rev(i), 0)),
                row, row, row, pl.BlockSpec((tm, 2 * w), lambda i: (rev(i), 0)),
                _vec_spec(w), _vec_spec(w), full((GROUPS, CHUNK, CHUNK)), full((GROUPS, CHUNK, CHUNK)), full((CHUNK, w)),
                full((CONV_B, w)), _vec_spec(w), _vec_spec(w)]
    out_specs = [pl.BlockSpec((tm, 6 * w), lambda i: (rev(i), 0)), _vec_spec(w), _vec_spec(w), full((GROUPS, CHUNK, CHUNK)),
                 full((CHUNK, CHUNK)), full((CONV_B, w)), _vec_spec(w), _vec_spec(w), _vec_spec(w)]
    vec = jax.ShapeDtypeStruct((1, w), F32)
    out_shape = [jax.ShapeDtypeStruct((t, 6 * w), BF16), vec, vec, jax.ShapeDtypeStruct((GROUPS, CHUNK, CHUNK), F32),
                 jax.ShapeDtypeStruct((CHUNK, CHUNK), F32), jax.ShapeDtypeStruct((CONV_B, w), F32), vec, vec, vec]
    scratch = [pltpu.VMEM((HALO_B, w), F32), pltpu.VMEM((tm, w), F32), pltpu.VMEM((tm + HALO_B, w), F32),
               pltpu.VMEM((tm, w), F32), pltpu.VMEM((tm, w), F32), pltpu.VMEM((CHUNK, w), F32), pltpu.VMEM((tm, w), F32),
               pltpu.VMEM((SUB - 1, HALO_B + tm - SUB, w), F32), pltpu.VMEM((CONV_B, SUB, w), F32),
               pltpu.VMEM((CONV_B, SUB, w), F32)]
    return _row_call("branch_bwd", body, n_steps=n_tiles, in_specs=in_specs, out_specs=out_specs, out_shape=out_shape,
                     scratch=scratch, exch=exch,
                     args=(z, c_saved, dpa, dcs, dgates, ln_v_g, ln_v_b, ws_m, ws_mt, bias_full, conv_w, ln_b_g, ln_b_b))


def _conv3_window(prev8, x):
    win = jnp.concatenate([prev8, x], axis=0)
    n = x.shape[0]
    return [win[HALO_F - 2:HALO_F - 2 + n], win[HALO_F - 1:HALO_F - 1 + n], x]


def _ffn_mid_fwd(up0, conv_w, conv_b, seq):
    nb, t, f = up0.shape
    half = nb // 2
    tm = min(TM_FFN, seq)
    tiles_per_seq = seq // tm
    n_strips = tm // ROWS_F

    def body(up_ref, w_ref, b_ref, act_ref, upc_ref, hist_ref):
        i = pl.program_id(0)

        @pl.when(i % tiles_per_seq == 0)
        def _():
            hist_ref[...] = jnp.zeros_like(hist_ref)

        for j in range(half):
            jv = j + half
            wg = [w_ref[j, k:k + 1, :] for k in range(CONV_F)]
            wv = [w_ref[jv, k:k + 1, :] for k in range(CONV_F)]
            bg, bv = b_ref[j], b_ref[jv]

            def strip(c, carry):
                rows = pl.ds(pl.multiple_of(c * ROWS_F, ROWS_F), ROWS_F)
                xg = up_ref[j, rows, :].astype(F32)
                xv = up_ref[jv, rows, :].astype(F32)
                sg = _conv3_window(carry[0], xg)
                sv = _conv3_window(carry[1], xv)
                gate = bg + wg[0] * sg[0] + wg[1] * sg[1] + wg[2] * sg[2]
                val = bv + wv[0] * sv[0] + wv[1] * sv[1] + wv[2] * sv[2]
                act_ref[j, rows, :] = (_gelu(gate) * val).astype(BF16)
                upc_ref[j, rows, :] = gate.astype(BF16)
                upc_ref[jv, rows, :] = val.astype(BF16)
                return xg[ROWS_F - HALO_F:], xv[ROWS_F - HALO_F:]

            last = lax.fori_loop(0, n_strips, strip, (hist_ref[j], hist_ref[jv]))
            hist_ref[j] = last[0]
            hist_ref[jv] = last[1]

    return pl.pallas_call(
        body, grid=(t // tm,),
        in_specs=[pl.BlockSpec((nb, tm, f), lambda i: (0, i, 0)), pl.BlockSpec((nb, CONV_F, f), lambda i: (0, 0, 0)),
                  pl.BlockSpec((nb, 1, f), lambda i: (0, 0, 0))],
        out_specs=[pl.BlockSpec((half, tm, f), lambda i: (0, i, 0)), pl.BlockSpec((nb, tm, f), lambda i: (0, i, 0))],
        out_shape=[jax.ShapeDtypeStruct((half, t, f), BF16), jax.ShapeDtypeStruct((nb, t, f), BF16)],
        scratch_shapes=[pltpu.VMEM((nb, HALO_F, f), F32)],
        name="ffn_mid_fwd", compiler_params=_params(1))(up0, conv_w, conv_b)


def _ffn_mid_bwd(up0, upc, dact, conv_w, seq, exch=()):
    nb, t, f = up0.shape
    half = nb // 2
    tm = min(TM_FFN, seq)
    tiles_per_seq = seq // tm
    n_tiles = t // tm
    n_strips = tm // ROWS_F

    def body(up_ref, upc_ref, da_ref, w_ref, dup_ref, dw_ref, db_ref, carry_ref, dwacc_ref, dbacc_ref):
        i = pl.program_id(0)
        r = n_tiles - 1 - i

        @pl.when(i == 0)
        def _():
            dwacc_ref[...] = jnp.zeros_like(dwacc_ref)
            dbacc_ref[...] = jnp.zeros_like(dbacc_ref)

        @pl.when(r % tiles_per_seq == tiles_per_seq - 1)
        def _():
            carry_ref[...] = jnp.zeros_like(carry_ref)

        for j in range(half):
            jv = j + half
            wg = [w_ref[j, k:k + 1, :] for k in range(CONV_F)]
            wv = [w_ref[jv, k:k + 1, :] for k in range(CONV_F)]

            def strip(ci, carry):
                rows = pl.ds(pl.multiple_of((n_strips - 1 - ci) * ROWS_F, ROWS_F), ROWS_F)
                val = upc_ref[jv, rows, :].astype(F32)
                gg, dgg = _gelu_and_grad(upc_ref[j, rows, :].astype(F32))
                da = da_ref[j, rows, :].astype(F32)
                d_gate = da * val * dgg
                d_val = da * gg
                for blk, d, nxt, wk in ((j, d_gate, carry[0], wg), (jv, d_val, carry[1], wv)):
                    dbacc_ref[blk] += d
                    dwin = jnp.concatenate([d, nxt], axis=0)
                    shifted = [dwin[2:2 + ROWS_F], dwin[1:1 + ROWS_F], d]
                    x = up_ref[blk, rows, :].astype(F32)
                    for k in range(CONV_F):
                        dwacc_ref[blk, k] += shifted[k] * x
                    dx = wk[0] * shifted[0] + wk[1] * shifted[1] + wk[2] * shifted[2]
                    dup_ref[blk, rows, :] = dx.astype(BF16)
                return d_gate[0:HALO_F], d_val[0:HALO_F]

            carry = lax.fori_loop(0, n_strips, strip, (carry_ref[j], carry_ref[jv]))
            carry_ref[j] = carry[0]
            carry_ref[jv] = carry[1]

        @pl.when(i == n_tiles - 1)
        def _():
            for blk in range(nb):
                db_ref[blk] = _rowsum(dbacc_ref[blk])
                for k in range(CONV_F):
                    dw_ref[blk, k:k + 1, :] = _rowsum(dwacc_ref[blk, k])

    def rev(i):
        return n_tiles - 1 - i

    return _row_call(
        "ffn_mid_bwd", body, n_steps=n_tiles,
        in_specs=[pl.BlockSpec((nb, tm, f), lambda i: (0, rev(i), 0)), pl.BlockSpec((nb, tm, f), lambda i: (0, rev(i), 0)),
                  pl.BlockSpec((half, tm, f), lambda i: (0, rev(i), 0)),
                  pl.BlockSpec((nb, CONV_F, f), lambda i: (0, 0, 0))],
        out_specs=[pl.BlockSpec((nb, tm, f), lambda i: (0, rev(i), 0)), pl.BlockSpec((nb, CONV_F, f), lambda i: (0, 0, 0)),
                   pl.BlockSpec((nb, 1, f), lambda i: (0, 0, 0))],
        out_shape=[jax.ShapeDtypeStruct((nb, t, f), BF16), jax.ShapeDtypeStruct((nb, CONV_F, f), F32),
                   jax.ShapeDtypeStruct((nb, 1, f), F32)],
        scratch=[pltpu.VMEM((nb, HALO_F, f), F32), pltpu.VMEM((nb, CONV_F, ROWS_F, f), F32), pltpu.VMEM((nb, ROWS_F, f), F32)],
        args=(up0, upc, dact, conv_w), exch=exch)


def _all_gather_steps(x_refs, o_refs, send_sems, recv_sems, local_sems):
    n = len(x_refs)
    x, y, c = lax.axis_index("x"), lax.axis_index("y"), lax.axis_index("c")
    me, sibling = (x, y, c), (x, y, 1 - c)
    chips = [(1 - x, y), (x, 1 - y), (1 - x, 1 - y)]

    def slot(pos):
        return 4 * pos[0] + 2 * pos[1] + pos[2]

    def copy(a, k, block, to, src=None):
        dst = o_refs[a].at[slot(block)]
        return pltpu.make_async_remote_copy(
            src_ref=dst if src is None else src, dst_ref=dst, send_sem=send_sems.at[a * 7 + k],
            recv_sem=recv_sems.at[a * 7 + k], device_id=to, device_id_type=MESH)

    mine = [pltpu.make_async_copy(x_refs[a], o_refs[a].at[slot(me)], local_sems.at[a]) for a in range(n)]
    first = []
    for a in range(n):
        first.append(copy(a, 0, me, sibling, src=x_refs[a]))
        first += [copy(a, 1 + j, me, (*chip, c), src=x_refs[a]) for j, chip in enumerate(chips)]

    def start():
        for cp in mine + first:
            cp.start()

    def finish():
        passed = []
        for j, chip in enumerate(chips):
            for a in range(n):
                copy(a, 1 + j, (*chip, c), me).wait_recv()
                cp = copy(a, 4 + j, (*chip, c), sibling)
                cp.start()
                passed.append(cp)
        for a in range(n):
            copy(a, 0, sibling, me).wait_recv()
        for j, chip in enumerate(chips):
            for a in range(n):
                copy(a, 4 + j, (*chip, 1 - c), me).wait_recv()
        for cp in first + passed:
            cp.wait_send()
        for cp in mine:
            cp.wait()

    return start, finish


def _adamw_update(g, w, m, v):
    c1 = 1.0 - ADAM_B1 ** ADAM_STEP
    c2 = 1.0 - ADAM_B2 ** ADAM_STEP
    m_new = ADAM_B1 * m + (1.0 - ADAM_B1) * g
    v_new = ADAM_B2 * v + (1.0 - ADAM_B2) * (g * g)
    delta = -ADAM_LR * ((m_new / c1) / (jnp.sqrt(v_new / c2) + ADAM_EPS) + ADAM_WD * w)
    return m_new, v_new, delta


def _adamw(name, parts, w, m, v, rows_per_step):
    r, c = w.shape
    tr = r if r <= rows_per_step else (rows_per_step if r % rows_per_step == 0 else r // 2)

    def body(p_ref, w_ref, m_ref, v_ref, g_ref, d_ref, mo_ref, vo_ref):
        g = p_ref[0].astype(F32)
        for s in range(1, N_DEV):
            g = g + p_ref[s].astype(F32)
        m_new, v_new, delta = _adamw_update(g, w_ref[...], m_ref[...], v_ref[...])
        g_ref[...] = g
        mo_ref[...] = m_new
        vo_ref[...] = v_new
        d_ref[...] = delta

    row = pl.BlockSpec((tr, c), lambda i: (i, 0))
    out = jax.ShapeDtypeStruct((r, c), F32)
    return pl.pallas_call(body, grid=(r // tr,), in_specs=[pl.BlockSpec((N_DEV, tr, c), lambda i: (0, i, 0)), row, row, row],
                          out_specs=[row, row, row, row], out_shape=[out, out, out, out], name=name,
                          compiler_params=_params(1))(parts, w, m, v)


_VEC_NAMES = ("ln_v_g", "ln_v_b", "b_s", "conv_b_b", "ln_b_g", "ln_b_b", "g_ffn", "g_pg", "g_ple", "g_final")
_LOSS_ROW = len(_VEC_NAMES)
_REP_LAYOUT = dict({k: (i, 1, 1024) for i, k in enumerate(_VEC_NAMES)}, ffn_conv_b=(16, 8, 704), w_s=(24, 128, 1024))


def _pack_replicated_grads(d, loss_row):
    head = jnp.concatenate([d[k].reshape(1, 1024) for k in _VEC_NAMES] + [loss_row], axis=1).reshape(_LOSS_ROW + 1, 1024)
    return jnp.concatenate([jnp.pad(head, ((0, 16 - _LOSS_ROW - 1), (0, 0))),
                            jnp.pad(d["ffn_conv_b"].reshape(8, 704), ((0, 0), (0, 1024 - 704))),
                            d["w_s"].reshape(128, 1024)], axis=0)


def _rows8(vec):
    return jnp.pad(vec.reshape(1, 1024), ((0, 7), (0, 0)))


def _adamw_small(name, parts, items, loss_row=None):
    n = len(items)

    def body(p_ref, *refs):
        ins, outs = refs[:3 * n], refs[3 * n:]
        for a, (row, rows, cols) in enumerate(it[:3] for it in items):
            g = p_ref[0, row:row + rows, 0:cols]
            for s in range(1, N_DEV):
                g = g + p_ref[s, row:row + rows, 0:cols]
            m_new, v_new, delta = _adamw_update(g, ins[3 * a][...], ins[3 * a + 1][...], ins[3 * a + 2][...])
            for ref, val in zip(outs[4 * a:4 * a + 4], (g, delta, m_new, v_new)):
                ref[...] = val
        if loss_row is not None:
            total = p_ref[0, loss_row:loss_row + 1, 0:128]
            for s in range(1, N_DEV):
                total = total + p_ref[s, loss_row:loss_row + 1, 0:128]
            outs[-1][...] = total

    out_shape = [jax.ShapeDtypeStruct((rows, cols), F32) for _, rows, cols, *_ in items for _ in range(4)]
    if loss_row is not None:
        out_shape.append(jax.ShapeDtypeStruct((1, 128), F32))
    flat = pl.pallas_call(body, out_shape=out_shape, name=name,
                          compiler_params=pltpu.CompilerParams(vmem_limit_bytes=VMEM_LIMIT))(
        parts, *[arr for it in items for arr in it[3:]])
    res = [tuple(flat[4 * a:4 * a + 4]) for a in range(n)]
    return res + [flat[-1]] if loss_row is not None else res


def _pack_sharded_small(conv_b_w, ffn_conv_w):
    lead = conv_b_w.shape[:-2]
    pad0 = [(0, 0)] * len(lead)
    a = jnp.pad(conv_b_w, pad0 + [(0, 1), (0, 0)])
    b = jnp.pad(ffn_conv_w.reshape(lead + (CONV_F * 704,)), pad0 + [(0, 24 * 128 - CONV_F * 704)]).reshape(lead + (24, 128))
    return jnp.concatenate([a, b], axis=-2)


def _unpack_sharded_small(pk):
    lead = pk.shape[:-2]
    conv_b_w = pk[..., 0:CONV_B, :]
    ffn = pk[..., 32:56, :].reshape(lead + (24 * 128,))[..., :CONV_F * 704].reshape(lead + (CONV_F, 704))
    return conv_b_w, ffn


_WEIGHTS = ("g_mix", "w_in", "ln_v_g", "ln_v_b", "w_s", "b_s", "w_a_out", "conv_b_w", "conv_b_b", "ln_b_g", "ln_b_b",
            "w_b_out", "w_o", "g_ffn", "w_up", "ffn_conv_w", "ffn_conv_b", "w_down", "g_pg", "w_pg", "w_ple", "g_ple",
            "g_final")
_BIG = ("w_in", "w_a_out", "w_b_out", "w_o", "w_up", "w_down", "w_pg", "w_ple")


def kernel(x, p, g_mix, w_in, ln_v_g, ln_v_b, w_s, b_s, w_a_out, conv_b_w, conv_b_b, ln_b_g, ln_b_b, w_b_out, w_o, g_ffn, w_up, ffn_conv_w, ffn_conv_b, w_down, g_pg, w_pg, w_ple, g_ple, g_final, loss_target, m_g_mix, m_w_in, m_ln_v_g, m_ln_v_b, m_w_s, m_b_s, m_w_a_out, m_conv_b_w, m_conv_b_b, m_ln_b_g, m_ln_b_b, m_w_b_out, m_w_o, m_g_ffn, m_w_up, m_ffn_conv_w, m_ffn_conv_b, m_w_down, m_g_pg, m_w_pg, m_w_ple, m_g_ple, m_g_final, v_g_mix, v_w_in, v_ln_v_g, v_ln_v_b, v_w_s, v_b_s, v_w_a_out, v_conv_b_w, v_conv_b_b, v_ln_b_g, v_ln_b_b, v_w_b_out, v_w_o, v_g_ffn, v_w_up, v_ffn_conv_w, v_ffn_conv_b, v_w_down, v_g_pg, v_w_pg, v_w_ple, v_g_ple, v_g_final):
    local = dict(locals())
    wts = {k: local[k] for k in _WEIGHTS}
    mom = {k: local["m_" + k] for k in _WEIGHTS}
    var = {k: local["v_" + k] for k in _WEIGHTS}
    shapes = {k: wts[k].shape for k in _WEIGHTS}

    bsz, seq, d = x.shape
    t = bsz * seq
    x0 = x.reshape(t, d)
    p0 = p.reshape(t, p.shape[-1])
    target = loss_target.reshape(t, d)
    tm = min(TM_MM, t)
    tm_wide = min(TM_WIDE, t)
    tm_in = min(2 * TM_WIDE, t)
    tt = min(TT_MM, t)
    n_row = t // tm
    n_tok = t // tt

    def sq(a):
        return a.reshape(a.shape[1:])

    shard = {k: sq(wts[k]).astype(BF16) for k in _BIG}
    h1, (w_in3, small8) = _rms_fwd("rms_mix_gather_w_in", x0, g_mix,
                                   gather=[shard["w_in"], _pack_sharded_small(sq(conv_b_w), sq(ffn_conv_w))])
    conv_b_w8, ffn_conv_w8 = _unpack_sharded_small(small8)
    conv_w_full = conv_b_w8.transpose(1, 0, 2).reshape(CONV_B, N_DEV * conv_b_w8.shape[-1])
    n_in = w_in3.shape[2]
    f_dev = shard["w_up"].shape[1]
    n_blk, f_blk = N_DEV // 2, 2 * f_dev

    def pair_blocks(a):
        return a.reshape(n_blk, 2, a.shape[1], f_dev).transpose(0, 2, 1, 3).reshape(n_blk, a.shape[1], f_blk)

    def unpair_blocks(a):
        return a.reshape(n_blk, a.shape[1], 2, f_dev).transpose(0, 2, 1, 3).reshape(N_DEV, a.shape[1], f_dev)

    ws_m = jnp.where(jnp.tril(jnp.ones((CHUNK, CHUNK), bool))[None], sq(w_s), 0.0).astype(BF16)
    ws_mt = jnp.swapaxes(ws_m, 1, 2)
    bias_full = jnp.broadcast_to(sq(b_s).T[:, :, None], (CHUNK, GROUPS, CHUNK)).reshape(CHUNK, GROUPS * CHUNK)
    ffn_b = ffn_conv_b.reshape(n_blk, 1, f_blk)
    ffn_w = pair_blocks(ffn_conv_w8)

    z, (wa3, wb3, wo3) = _matmul(
        "mm_in", h1, w_in3, dims=_NN, grid=(N_DEV, t // tm_in, 1),
        a_spec=pl.BlockSpec((tm_in, d), lambda j, i, k: (i, 0)),
        b_spec=pl.BlockSpec((None, d, n_in), lambda j, i, k: (j, 0, 0)),
        o_spec=pl.BlockSpec((tm_in, n_in), lambda j, i, k: (i, j)), acc_shape=(tm_in, n_in),
        out_shape=jax.ShapeDtypeStruct((t, N_DEV * n_in), BF16),
        exch=[("gather", shard[k]) for k in ("w_a_out", "w_b_out", "w_o")])
    w_a = wa3.reshape(-1, d)
    w_b = wb3.reshape(-1, d)
    w_om = wo3.reshape(-1, d)
    (pa, cs, c_saved), (w_up3, wd3, wpg3, wple3) = _branch_fwd(
        z, ln_v_g, ln_v_b, ws_m, bias_full, conv_w_full, conv_b_b, ln_b_g, ln_b_b, seq,
        exch=[("gather", shard[k]) for k in ("w_up", "w_down", "w_pg", "w_ple")])
    w_pgm = wpg3.reshape(-1, d)
    w_upb = pair_blocks(w_up3)
    w_db = wd3.reshape(n_blk // 2, f_blk, d)
    w_plem = wple3.transpose(1, 0, 2).reshape(wple3.shape[1], d)
    ya = _mm_rows("mm_a_out", pa, w_a, dims=_NN, tm=tm, out_dtype=BF16)
    yb, merged = _mm_rows("mm_b_out", cs, w_b, dims=_NN, tm=tm, epi=_epi_merge_fwd(z, ya, tm))
    x1, h2 = _mm_rows("mm_o", merged, w_om, dims=_NN, tm=tm, epi=_epi_residual_rms(x0, g_ffn, tm))
    up0 = _matmul("mm_up", h2, w_upb, dims=_NN, grid=(n_blk, t // tm_wide, 1),
                  a_spec=pl.BlockSpec((tm_wide, d), lambda j, i, k: (i, 0)),
                  b_spec=pl.BlockSpec((None, d, f_blk), lambda j, i, k: (j, 0, 0)),
                  o_spec=pl.BlockSpec((None, tm_wide, f_blk), lambda j, i, k: (j, i, 0)), acc_shape=(tm_wide, f_blk),
                  out_shape=jax.ShapeDtypeStruct((n_blk, t, f_blk), BF16))
    act, upc = _ffn_mid_fwd(up0, ffn_w, ffn_b, seq)
    x2, hq = _matmul("mm_down", act, w_db, dims=_NN, grid=(n_row, 1, 1), split=n_blk // 2,
                     a_spec=pl.BlockSpec((n_blk // 2, tm, f_blk), lambda i, j, k: (0, i, 0)),
                     b_spec=pl.BlockSpec((n_blk // 2, f_blk, d), lambda i, j, k: (0, 0, 0)),
                     acc_shape=(tm, d), epi=_epi_residual_rms(x1, g_pg, tm))

    dx3, dq, dr, loss_v, dg_final, dg_ple = _mm_rows(
        "mm_pg", hq, w_pgm, dims=_NN, tm=tm, epi=_epi_head(x2, p0, w_plem, target, g_ple, g_final.reshape(1, d), tm))

    recv = {}
    gw_pg = _mm_wgrad("wg_pg", hq, dq, out_dtype=BF16, tt=tt).reshape(wpg3.shape)
    dw_ple = _mm_wgrad("wg_ple", p0, dr, out_dtype=BF16, tt=tt)
    gw_ple = dw_ple.reshape(dw_ple.shape[0], N_DEV, -1).transpose(1, 0, 2)
    dx2, dx2b, dg_pg = _mm_rows("mm_pg_t", dq, w_pgm, dims=_NT, tm=tm, epi=_epi_rms_bwd(x2, g_pg, dx3, tm, want_bf16=True))

    dact, (recv["w_pg"], recv["w_ple"]) = _matmul(
        "mm_down_t", dx2b, w_db, dims=_NT, grid=(n_blk // 2, n_row, 1),
        a_spec=pl.BlockSpec((tm, d), lambda j, i, k: (i, 0)),
        b_spec=pl.BlockSpec((None, f_blk, d), lambda j, i, k: (j, 0, 0)),
        o_spec=pl.BlockSpec((None, tm, f_blk), lambda j, i, k: (j, i, 0)), acc_shape=(tm, f_blk),
        out_shape=jax.ShapeDtypeStruct((n_blk // 2, t, f_blk), BF16),
        exch=[("scatter", gw_pg), ("scatter", gw_ple)])
    gw_down = _matmul("wg_down", act, dx2b, dims=_TN, grid=(n_blk // 2, 1, n_tok),
                      a_spec=pl.BlockSpec((None, tt, f_blk), lambda j, i, k: (j, k, 0)),
                      b_spec=pl.BlockSpec((tt, d), lambda j, i, k: (k, 0)),
                      o_spec=pl.BlockSpec((None, f_blk, d), lambda j, i, k: (j, 0, 0)), acc_shape=(f_blk, d),
                      out_shape=jax.ShapeDtypeStruct((n_blk // 2, f_blk, d), BF16)).reshape(wd3.shape)
    (d_up0, dffn_wb, dffn_b), (recv["w_down"],) = _ffn_mid_bwd(up0, upc, dact, ffn_w, seq, exch=[("scatter", gw_down)])
    dffn_w8 = unpair_blocks(dffn_wb)
    gw_up = _matmul("wg_up", h2, d_up0, dims=_TN, grid=(n_blk, 1, n_tok),
                    a_spec=pl.BlockSpec((tt, d), lambda j, i, k: (k, 0)),
                    b_spec=pl.BlockSpec((None, tt, f_blk), lambda j, i, k: (j, k, 0)),
                    o_spec=pl.BlockSpec((None, d, f_blk), lambda j, i, k: (j, 0, 0)), acc_shape=(d, f_blk),
                    out_shape=jax.ShapeDtypeStruct((n_blk, d, f_blk), BF16))
    gw_up = unpair_blocks(gw_up)
    (dx1, dx1b, dg_ffn), (recv["w_up"],) = _matmul(
        "mm_up_t", d_up0, w_upb, dims=_NT, grid=(n_row, 1, n_blk),
        a_spec=pl.BlockSpec((None, tm, f_blk), lambda i, j, k: (k, i, 0)),
        b_spec=pl.BlockSpec((None, d, f_blk), lambda i, j, k: (k, 0, 0)),
        acc_shape=(tm, d), epi=_epi_rms_bwd(x1, g_ffn, dx2, tm, want_bf16=True), exch=[("scatter", gw_up)])

    dya, dyb, dgates = _mm_rows("mm_o_t", dx1b, w_om, dims=_NT, tm=tm, epi=_epi_merge_bwd(z, ya, yb, tm))
    gw_o = _mm_wgrad("wg_o", merged, dx1b, out_dtype=BF16, tt=tt).reshape(wo3.shape)
    dpa = _mm_rows("mm_a_out_t", dya, w_a, dims=_NT, tm=tm, out_dtype=BF16)
    dcs = _mm_rows("mm_b_out_t", dyb, w_b, dims=_NT, tm=tm, out_dtype=BF16)
    gw_a = _mm_wgrad("wg_a_out", pa, dya, out_dtype=BF16, tt=tt).reshape(wa3.shape)
    gw_b = _mm_wgrad("wg_b_out", cs, dyb, out_dtype=BF16, tt=tt).reshape(wb3.shape)
    (dz, dlvg, dlvb, dws, dbs_full, dconv_w, dconv_b, dlbg, dlbb), (recv["w_o"], recv["w_a_out"], recv["w_b_out"]) = _branch_bwd(
        z, c_saved, dpa, dcs, dgates, ln_v_g, ln_v_b, ws_m, ws_mt, bias_full, conv_w_full, ln_b_g, ln_b_b, seq,
        exch=[("scatter", gw_o), ("scatter", gw_a), ("scatter", gw_b)])
    db_s = dbs_full[:, :GROUPS].T
    rep_partial = _pack_replicated_grads(
        dict(ln_v_g=dlvg, ln_v_b=dlvb, b_s=db_s, conv_b_b=dconv_b, ln_b_g=dlbg, ln_b_b=dlbb, g_ffn=dg_ffn,
             g_pg=dg_pg, g_ple=dg_ple, g_final=dg_final, ffn_conv_b=dffn_b, w_s=dws), loss_v)
    dconv_w8 = dconv_w.reshape(CONV_B, N_DEV, -1).transpose(1, 0, 2)
    small_partial = _pack_sharded_small(dconv_w8, dffn_w8)
    gw_in, (recv_small, recv_rep) = _matmul(
        "wg_in", h1, dz, dims=_TN, grid=(N_DEV, 1, n_tok),
        a_spec=pl.BlockSpec((tt, d), lambda j, i, k: (k, 0)),
        b_spec=pl.BlockSpec((tt, n_in), lambda j, i, k: (k, j)),
        o_spec=pl.BlockSpec((None, d, n_in), lambda j, i, k: (j, 0, 0)), acc_shape=(d, n_in),
        out_shape=jax.ShapeDtypeStruct((N_DEV, d, n_in), BF16),
        exch=[("scatter", small_partial), ("gather", rep_partial)])
    (grad_x, dg_mix), (recv["w_in"],) = _matmul(
        "mm_in_t", dz, w_in3, dims=_NT, grid=(n_row, 1, N_DEV // 2), split=2,
        a_spec=pl.BlockSpec((tm, 2 * n_in), lambda i, j, k: (i, k)),
        b_spec=pl.BlockSpec((2, d, n_in), lambda i, j, k: (k, 0, 0)),
        acc_shape=(tm, d), epi=_epi_rms_bwd(x0, g_mix, dx1, tm, want_bf16=False), exch=[("scatter", gw_in)])
    (recv_g_mix,) = _exchange("exchange_g_mix", [("gather", _rows8(dg_mix))])

    grads, deltas, new_m, new_v = {}, {}, {}, {}
    by_kind = (grads, deltas, new_m, new_v)

    def two_d(a):
        a = sq(a)
        return a.reshape(-1, a.shape[-1])

    for k in _BIG:
        parts = recv[k].reshape(N_DEV, -1, recv[k].shape[-1])
        outs = _adamw("adamw_" + k, parts, two_d(wts[k]), two_d(mom[k]), two_d(var[k]), ADAMW_ROWS)
        for tgt, o in zip(by_kind, outs):
            tgt[k] = o.reshape(shapes[k])

    small = [_pack_sharded_small(sq(s["conv_b_w"]), sq(s["ffn_conv_w"])) for s in (wts, mom, var)]
    for tgt, o in zip(by_kind, _adamw("adamw_conv", recv_small, small[0], small[1], small[2], 56)):
        cw, fw = _unpack_sharded_small(o)
        tgt["conv_b_w"] = cw.reshape(shapes["conv_b_w"])
        tgt["ffn_conv_w"] = fw.reshape(shapes["ffn_conv_w"])

    names = list(_REP_LAYOUT)
    items = [_REP_LAYOUT[k] + tuple(s[k].reshape(_REP_LAYOUT[k][1:]) for s in (wts, mom, var)) for k in names]
    *rep_outs, loss_sum = _adamw_small("adamw_replicated", recv_rep, items, loss_row=_LOSS_ROW)
    for k, outs in zip(names, rep_outs):
        for tgt, o in zip(by_kind, outs):
            tgt[k] = o.reshape(shapes[k])
    loss = loss_sum[0, 0]

    g_mix_item = (0, 1, 1024) + tuple(s["g_mix"].reshape(1, 1024) for s in (wts, mom, var))
    for tgt, o in zip(by_kind, _adamw_small("adamw_g_mix", recv_g_mix, [g_mix_item])[0]):
        tgt["g_mix"] = o.reshape(shapes["g_mix"])

    return (loss, grad_x.reshape(x.shape), *[grads[k] for k in _WEIGHTS], *[deltas[k] for k in _WEIGHTS],
            *[new_m[k] for k in _WEIGHTS], *[new_v[k] for k in _WEIGHTS])
```

```python
import math

import jax
import jax.numpy as jnp
from jax import lax
from jax.experimental import pallas as pl
from jax.experimental.pallas import tpu as pltpu

F32 = jnp.float32
BF16 = jnp.bfloat16

N_DEV = 8
EPS_RMS = 1e-6
EPS_LN = 1e-5
CHUNK = 128
GROUPS = 8
CONV_B = 31
CONV_F = 3
HALO_B = 32
HALO_F = 8
ROWS_F = 16
SUB = 8

ADAM_LR = 0.001
ADAM_B1 = 0.9
ADAM_B2 = 0.999
ADAM_EPS = 1e-08
ADAM_WD = 0.01
ADAM_STEP = 10

VMEM_LIMIT = 56 * 1024 * 1024
TM_MM = 1024
TM_WIDE = 2048
TT_MM = 2048
TM_EW = 256
TM_FFN = 512
ADAMW_ROWS = 512
EPI_STRIP = 256

_NN = (((1,), (0,)), ((), ()))
_NT = (((1,), (1,)), ((), ()))
_TN = (((0,), (0,)), ((), ()))
MESH = pl.DeviceIdType.MESH
HBM_SPEC = pl.BlockSpec(memory_space=pltpu.HBM)


def _params(n_axes):
    return pltpu.CompilerParams(dimension_semantics=("arbitrary",) * n_axes, vmem_limit_bytes=VMEM_LIMIT)


def _gelu(x):
    k = math.sqrt(2.0 / math.pi)
    return 0.5 * x * (1.0 + jnp.tanh(k * (x + 0.044715 * (x * x * x))))


def _gelu_and_grad(x):
    k = math.sqrt(2.0 / math.pi)
    x2 = x * x
    t = jnp.tanh(k * (x + 0.044715 * (x2 * x)))
    g = 0.5 * x * (1.0 + t)
    dg = 0.5 * (1.0 + t) + 0.5 * x * (1.0 - t * t) * (k * (1.0 + 3.0 * 0.044715 * x2))
    return g, dg


def _sigmoid(x):
    return 1.0 / (1.0 + jnp.exp(-x))


def _rowsum(x):
    return jnp.sum(x, axis=0, keepdims=True)


def _mean(x):
    return jnp.mean(x, axis=-1, keepdims=True)


def _exchange_io(exch):
    n = len(exch)
    out_shape = [jax.ShapeDtypeStruct(v.shape if kind == "scatter" else (N_DEV,) + v.shape, v.dtype) for kind, v in exch]
    scratch = [pltpu.SemaphoreType.DMA((7 * n,)), pltpu.SemaphoreType.DMA((7 * n,)), pltpu.SemaphoreType.DMA((n,))] if n else []
    return [HBM_SPEC] * n, [HBM_SPEC] * n, out_shape, scratch


def _exchange_step(kinds, x_refs, o_refs, send_sems, recv_sems, local_sems):
    n = len(kinds)
    x, y, c = lax.axis_index("x"), lax.axis_index("y"), lax.axis_index("c")
    me = 4 * x + 2 * y + c

    def src(a, to_slot):
        return x_refs[a].at[to_slot] if kinds[a] == "scatter" else x_refs[a]

    mine = [pltpu.make_async_copy(src(a, me), o_refs[a].at[me], local_sems.at[a]) for a in range(n)]
    sends, recvs = [], []
    for m in range(1, N_DEV):
        mx, my, mc = (m >> 2) & 1, (m >> 1) & 1, m & 1
        px, py, pc = (1 - x if mx else x), (1 - y if my else y), (1 - c if mc else c)
        peer = 4 * px + 2 * py + pc
        for a in range(n):
            k = a * 7 + m - 1
            sends.append(pltpu.make_async_remote_copy(
                src_ref=src(a, peer), dst_ref=o_refs[a].at[me], send_sem=send_sems.at[k], recv_sem=recv_sems.at[k],
                device_id=(px, py, pc), device_id_type=MESH))
            recvs.append(pltpu.make_async_remote_copy(
                src_ref=src(a, peer), dst_ref=o_refs[a].at[peer], send_sem=send_sems.at[k], recv_sem=recv_sems.at[k],
                device_id=(px, py, pc), device_id_type=MESH))

    def start():
        for cp in mine + sends:
            cp.start()

    def finish():
        for cp in recvs:
            cp.wait_recv()
        for cp in sends:
            cp.wait_send()
        for cp in mine:
            cp.wait()

    return start, finish


def _exchange(name, exch):
    n = len(exch)
    kinds = [k for k, _ in exch]
    in_specs, out_specs, out_shape, scratch = _exchange_io(exch)

    def body(*refs):
        start, finish = _exchange_step(kinds, refs[:n], refs[n:2 * n], *refs[2 * n:])
        start()
        finish()

    return pl.pallas_call(body, in_specs=in_specs, out_specs=out_specs, out_shape=out_shape, scratch_shapes=scratch,
                          name=name)(*[v for _, v in exch])


class _Epilogue:
    def __init__(self, fn, ins=(), in_specs=(), out_specs=(), out_shape=(), strip=None):
        self.fn, self.ins, self.in_specs = fn, list(ins), list(in_specs)
        self.out_specs, self.out_shape = list(out_specs), list(out_shape)
        self.strip = strip


def _matmul(name, a, b, *, dims, grid, a_spec, b_spec, acc_shape, o_spec=None, out_shape=None, epi=None, exch=(), split=1):
    nk = grid[2]
    plain = epi is None
    if plain:
        def store(acc, ins, outs, i):
            outs[0][...] = acc.astype(outs[0].dtype)
        epi = _Epilogue(store, out_specs=[o_spec], out_shape=[out_shape])
    n_in = 2 + len(epi.ins)
    n_out = len(epi.out_specs)
    n_ex = len(exch)
    kinds = [k for k, _ in exch]
    ex_in, ex_out, ex_shape, ex_scratch = _exchange_io(exch)

    def body(*refs):
        a_ref, b_ref = refs[:2]
        step0 = pl.program_id(0)
        epi_ins, rest = refs[2:n_in], refs[n_in:]
        x_refs, rest = rest[:n_ex], rest[n_ex:]
        outs, rest = rest[:n_out], rest[n_out:]
        o_refs, scr = rest[:n_ex], rest[n_ex:]
        if n_ex:
            pid = [pl.program_id(ax) for ax in range(3)]
            ex_start, ex_finish = _exchange_step(kinds, x_refs, o_refs, *scr[len(scr) - 3:])
            pl.when((pid[0] == 0) & (pid[1] == 0) & (pid[2] == 0))(ex_start)
        if split == 1:
            part = lax.dot_general(a_ref[...].astype(BF16), b_ref[...].astype(BF16), dims, preferred_element_type=F32)
        else:
            kk = b_ref.shape[-1]
            part = None
            for s in range(split):
                a_s = a_ref[s] if len(a_ref.shape) == 3 else a_ref[:, s * kk:(s + 1) * kk]
                p_s = lax.dot_general(a_s.astype(BF16), b_ref[s].astype(BF16), dims, preferred_element_type=F32)
                part = p_s if part is None else part + p_s

        def run_epilogue(rows_of_acc):
            rows = acc_shape[0]
            strip = rows if epi.strip is None else min(epi.strip, rows)
            for s in range(0, rows, strip):
                def view(ref):
                    return ref.at[pl.ds(s, strip)] if ref.shape[0] == rows else ref
                first = (step0 == 0) if s == 0 else False
                epi.fn(rows_of_acc(s, strip), [view(r) for r in epi_ins], [view(r) for r in outs], first)

        if nk == 1:
            run_epilogue(lambda s, n: part[s:s + n])
        else:
            acc_ref = scr[0]
            k = pl.program_id(2)

            @pl.when(k == 0)
            def _():
                acc_ref[...] = part

            @pl.when(k > 0)
            def _():
                acc_ref[...] += part

            @pl.when(k == nk - 1)
            def _():
                run_epilogue(lambda s, n: acc_ref[pl.ds(s, n), :])
        if n_ex:
            pl.when((pid[0] == grid[0] - 1) & (pid[1] == grid[1] - 1) & (pid[2] == grid[2] - 1))(ex_finish)

    scratch = ([pltpu.VMEM(acc_shape, F32)] if nk > 1 else []) + ex_scratch
    res = pl.pallas_call(body, grid=grid, in_specs=[a_spec, b_spec] + epi.in_specs + ex_in,
                         out_specs=epi.out_specs + ex_out, out_shape=epi.out_shape + ex_shape, scratch_shapes=scratch,
                         name=name, compiler_params=_params(3))(a, b, *epi.ins, *[v for _, v in exch])
    main = res[0] if plain else res[:n_out]
    return (main, res[n_out:]) if n_ex else main


def _mm_rows(name, a, w, *, dims, tm, out_dtype=None, epi=None):
    t, k = a.shape
    n = w.shape[1] if dims == _NN else w.shape[0]
    tm = min(tm, t)
    return _matmul(name, a, w, dims=dims, grid=(t // tm, 1, 1),
                   a_spec=pl.BlockSpec((tm, k), lambda i, j, kk: (i, 0)),
                   b_spec=pl.BlockSpec(w.shape, lambda i, j, kk: (0, 0)),
                   o_spec=pl.BlockSpec((tm, n), lambda i, j, kk: (i, 0)), acc_shape=(tm, n),
                   out_shape=jax.ShapeDtypeStruct((t, n), out_dtype) if epi is None else None, epi=epi)


def _mm_wgrad(name, a, b, *, out_dtype, tt):
    t, m = a.shape
    n = b.shape[1]
    tt = min(tt, t)
    return _matmul(name, a, b, dims=_TN, grid=(1, 1, t // tt),
                   a_spec=pl.BlockSpec((tt, m), lambda i, j, kk: (kk, 0)),
                   b_spec=pl.BlockSpec((tt, n), lambda i, j, kk: (kk, 0)),
                   o_spec=pl.BlockSpec((m, n), lambda i, j, kk: (0, 0)),
                   acc_shape=(m, n), out_shape=jax.ShapeDtypeStruct((m, n), out_dtype))


def _row3(tm, d, col=0):
    return pl.BlockSpec((tm, d), lambda i, j, k: (i, col))


def _vec3(d):
    return pl.BlockSpec((1, d), lambda i, j, k: (0, 0))


def _accumulate_over_rows(ref, part, first):
    if first is False:
        ref[...] += part
        return

    @pl.when(first)
    def _():
        ref[...] = part + jnp.zeros_like(ref)

    @pl.when(jnp.logical_not(first))
    def _():
        ref[...] += part


def _epi_residual_rms(res, g, tm):
    t, d = res.shape

    def fn(acc, ins, outs, i):
        res_ref, g_ref = ins
        xv = acc + res_ref[...]
        outs[0][...] = xv
        rstd = lax.rsqrt(_mean(xv * xv) + EPS_RMS)
        outs[1][...] = ((xv * rstd) * g_ref[...]).astype(BF16)

    return _Epilogue(fn, strip=EPI_STRIP, ins=[res, g], in_specs=[_row3(tm, d), _vec3(d)], out_specs=[_row3(tm, d), _row3(tm, d)],
                     out_shape=[jax.ShapeDtypeStruct((t, d), F32), jax.ShapeDtypeStruct((t, d), BF16)])


def _epi_rms_bwd(x, g, dres, tm, *, want_bf16):
    t, d = x.shape

    def fn(acc, ins, outs, i):
        x_ref, g_ref, dres_ref = ins
        xv = x_ref[...]
        rstd = lax.rsqrt(_mean(xv * xv) + EPS_RMS)
        nrm = xv * rstd
        dn = acc * g_ref[...]
        dx = dres_ref[...] + rstd * (dn - nrm * _mean(dn * nrm))
        outs[0][...] = dx
        if want_bf16:
            outs[1][...] = dx.astype(BF16)
        _accumulate_over_rows(outs[-1], _rowsum(acc * nrm), i)

    row = _row3(tm, d)
    n_dx = 2 if want_bf16 else 1
    return _Epilogue(fn, strip=EPI_STRIP, ins=[x, g, dres], in_specs=[row, _vec3(d), row], out_specs=[row] * n_dx + [_vec3(d)],
                     out_shape=[jax.ShapeDtypeStruct((t, d), F32)] + [jax.ShapeDtypeStruct((t, d), BF16)] * (n_dx - 1)
                     + [jax.ShapeDtypeStruct((1, d), F32)])


def _epi_merge_fwd(z, ya, tm):
    t, w = ya.shape

    def fn(acc, ins, outs, i):
        ga_ref, gb_ref, ya_ref = ins
        outs[0][...] = acc.astype(BF16)
        sa = _sigmoid(ga_ref[...].astype(F32))
        sb = _sigmoid(gb_ref[...].astype(F32))
        outs[1][...] = (sa * ya_ref[...].astype(F32) + sb * acc).astype(BF16)

    row = _row3(tm, w)
    return _Epilogue(fn, strip=EPI_STRIP, ins=[z, z, ya], in_specs=[_row3(tm, w, 4), _row3(tm, w, 5), row], out_specs=[row, row],
                     out_shape=[jax.ShapeDtypeStruct((t, w), BF16)] * 2)


def _epi_merge_bwd(z, ya, yb, tm):
    t, w = ya.shape

    def fn(acc, ins, outs, i):
        ga_ref, gb_ref, ya_ref, yb_ref = ins
        sa = _sigmoid(ga_ref[...].astype(F32))
        sb = _sigmoid(gb_ref[...].astype(F32))
        outs[0][...] = (acc * sa).astype(BF16)
        outs[1][...] = (acc * sb).astype(BF16)
        outs[2][:, 0:w] = (acc * ya_ref[...].astype(F32) * sa * (1.0 - sa)).astype(BF16)
        outs[2][:, w:2 * w] = (acc * yb_ref[...].astype(F32) * sb * (1.0 - sb)).astype(BF16)

    row = _row3(tm, w)
    return _Epilogue(fn, strip=EPI_STRIP, ins=[z, z, ya, yb], in_specs=[_row3(tm, w, 4), _row3(tm, w, 5), row, row],
                     out_specs=[row, row, _row3(tm, 2 * w)],
                     out_shape=[jax.ShapeDtypeStruct((t, w), BF16)] * 2 + [jax.ShapeDtypeStruct((t, 2 * w), BF16)])


def _epi_head(x2, p, w_ple, target, g_ple, g_final, tm):
    t, d = x2.shape

    def fn(acc, ins, outs, i):
        x2_ref, p_ref, wple_ref, tg_ref, gple_ref, gfin_ref = ins
        dx3_ref, dq_ref, dr_ref, loss_ref, dgfin_ref, dgple_ref = outs
        pg = _sigmoid(acc)
        rv = lax.dot_general(p_ref[...].astype(BF16), wple_ref[...], _NN, preferred_element_type=F32)
        rstd_r = lax.rsqrt(_mean(rv * rv) + EPS_RMS)
        nr = rv * rstd_r
        pe = nr * gple_ref[...]
        x3 = x2_ref[...] + pe * pg
        rstd3 = lax.rsqrt(_mean(x3 * x3) + EPS_RMS)
        n3 = x3 * rstd3
        err = n3 * gfin_ref[...] - tg_ref[...]
        loss_part = jnp.sum(_rowsum(err * err), axis=1, keepdims=True) * (0.5 / d)
        dy = err * (1.0 / d)
        dn3 = dy * gfin_ref[...]
        dx3 = rstd3 * (dn3 - n3 * _mean(dn3 * n3))
        dx3_ref[...] = dx3
        dq_ref[...] = (dx3 * pe * pg * (1.0 - pg)).astype(BF16)
        dpe = dx3 * pg
        dnr = dpe * gple_ref[...]
        dr_ref[...] = (rstd_r * (dnr - nr * _mean(dnr * nr))).astype(BF16)
        _accumulate_over_rows(loss_ref, loss_part, i)
        _accumulate_over_rows(dgfin_ref, _rowsum(dy * n3), i)
        _accumulate_over_rows(dgple_ref, _rowsum(dpe * nr), i)

    row = _row3(tm, d)
    vec = jax.ShapeDtypeStruct((1, d), F32)
    return _Epilogue(fn, strip=EPI_STRIP, ins=[x2, p, w_ple, target, g_ple, g_final],
                     in_specs=[row, _row3(tm, p.shape[1]), pl.BlockSpec(w_ple.shape, lambda i, j, k: (0, 0)), row, _vec3(d), _vec3(d)],
                     out_specs=[row, row, row, _vec3(d), _vec3(d), _vec3(d)],
                     out_shape=[jax.ShapeDtypeStruct((t, d), F32), jax.ShapeDtypeStruct((t, d), BF16),
                                jax.ShapeDtypeStruct((t, d), BF16), vec, vec, vec])


def _vec_spec(d):
    return pl.BlockSpec((1, d), lambda i: (0, 0))


def _row_call(name, body, *, n_steps, in_specs, out_specs, out_shape, scratch, args, exch=()):
    n_in, n_out, n_scr, n_ex = len(in_specs), len(out_specs), len(scratch), len(exch)
    kinds = [k for k, _ in exch]
    ex_in, ex_out, ex_shape, ex_scratch = _exchange_io(exch)

    def wrapped(*refs):
        ins, rest = refs[:n_in], refs[n_in:]
        x_refs, rest = rest[:n_ex], rest[n_ex:]
        outs, rest = rest[:n_out], rest[n_out:]
        o_refs, rest = rest[:n_ex], rest[n_ex:]
        scr, sems = rest[:n_scr], rest[n_scr:]
        if n_ex:
            ex_start, ex_finish = _exchange_step(kinds, x_refs, o_refs, *sems)
            pl.when(pl.program_id(0) == 0)(ex_start)
        body(*ins, *outs, *scr)
        if n_ex:
            pl.when(pl.program_id(0) == n_steps - 1)(ex_finish)

    res = pl.pallas_call(wrapped, grid=(n_steps,), in_specs=list(in_specs) + ex_in, out_specs=list(out_specs) + ex_out,
                         out_shape=list(out_shape) + ex_shape, scratch_shapes=list(scratch) + ex_scratch, name=name,
                         compiler_params=_params(1))(*args, *[v for _, v in exch])
    return res[:n_out], res[n_out:]


def _rms_fwd(name, x, g, gather=()):
    t, d = x.shape
    tm = min(TM_EW * 2, t)
    n_steps = t // tm
    n = len(gather)

    def body(x_ref, g_ref, *refs):
        h_ref = refs[n]
        if n:
            start, finish = _all_gather_steps(refs[:n], refs[n + 1:2 * n + 1], *refs[2 * n + 1:])
            pl.when(pl.program_id(0) == 0)(start)
        xv = x_ref[...]
        rstd = lax.rsqrt(_mean(xv * xv) + EPS_RMS)
        h_ref[...] = ((xv * rstd) * g_ref[...]).astype(BF16)
        if n:
            pl.when(pl.program_id(0) == n_steps - 1)(finish)

    row = pl.BlockSpec((tm, d), lambda i: (i, 0))
    sems = [pltpu.SemaphoreType.DMA((7 * n,)), pltpu.SemaphoreType.DMA((7 * n,)), pltpu.SemaphoreType.DMA((n,))] if n else []
    res = pl.pallas_call(
        body, grid=(n_steps,), in_specs=[row, _vec_spec(d)] + [HBM_SPEC] * n, out_specs=[row] + [HBM_SPEC] * n,
        out_shape=[jax.ShapeDtypeStruct((t, d), BF16)] + [jax.ShapeDtypeStruct((N_DEV,) + v.shape, v.dtype) for v in gather],
        scratch_shapes=sems, name=name, compiler_params=_params(1))(x, g, *gather)
    return res[0], res[1:]


def _fill_shifted(buf_ref, sh_ref):
    n = sh_ref.shape[1]
    for p in range(1, SUB):
        sh_ref[p - 1] = buf_ref[p:p + n, :]


def _branch_fwd(z, ln_v_g, ln_v_b, ws_m, bias_full, conv_w, conv_b, ln_b_g, ln_b_b, seq, exch=()):
    t = z.shape[0]
    w = 1024
    tm = min(2 * TM_EW, seq)
    tiles_per_seq = seq // tm
    n_chunks = tm // CHUNK

    def body(z_ref, lvg_ref, lvb_ref, ws_ref, bias_ref, cw_ref, cb_ref, lbg_ref, lbb_ref,
             pa_ref, cs_ref, c_ref, hist_ref, buf_ref, mix_ref, sh_ref, wb_ref):
        i = pl.program_id(0)
        u = z_ref[:, 0:w].astype(F32)
        v = z_ref[:, w:2 * w].astype(F32)
        ug = _gelu(u)
        vg = _gelu(v)
        dv = vg - _mean(vg)
        vhat = dv * lax.rsqrt(_mean(dv * dv) + EPS_LN)
        vn = (vhat * lvg_ref[...] + lvb_ref[...]).astype(BF16)
        for ci in range(n_chunks):
            rows = slice(ci * CHUNK, (ci + 1) * CHUNK)
            for g in range(GROUPS):
                cols = slice(g * CHUNK, (g + 1) * CHUNK)
                mix_ref[rows, cols] = lax.dot_general(ws_ref[g], vn[rows, cols], _NN, preferred_element_type=F32)
            mix_ref[rows, :] += bias_ref[...]
        pa_ref[...] = (ug * mix_ref[...]).astype(BF16)

        a = z_ref[:, 2 * w:3 * w].astype(F32)
        gl = z_ref[:, 3 * w:4 * w].astype(F32)
        glu = a * _sigmoid(gl)

        @pl.when(i % tiles_per_seq == 0)
        def _():
            hist_ref[...] = jnp.zeros_like(hist_ref)

        buf_ref[0:HALO_B, :] = hist_ref[...]
        buf_ref[HALO_B:, :] = glu
        hist_ref[...] = glu[tm - HALO_B:, :]
        _fill_shifted(buf_ref, sh_ref)

        @pl.when(i == 0)
        def _():
            for k in range(CONV_B):
                wb_ref[k] = jnp.broadcast_to(cw_ref[k:k + 1, :], (SUB, w))

        groups = 4

        def strip(si, _):
            s = pl.multiple_of(si * (groups * SUB), groups * SUB)
            acc = [jnp.zeros((SUB, w), F32) + cb_ref[...] for _ in range(groups)]
            for k in range(CONV_B):
                whole, part = divmod(HALO_B - (CONV_B - 1) + k, SUB)
                wk = wb_ref[k]
                for g in range(groups):
                    at = pl.ds(s + SUB * (whole + g), SUB)
                    acc[g] = acc[g] + wk * (buf_ref[at, :] if part == 0 else sh_ref[part - 1, at, :])
            for g in range(0, groups, 2):
                at = pl.ds(s + SUB * g, 2 * SUB)
                c = jnp.concatenate(acc[g:g + 2], axis=0)
                c_ref[at, :] = c
                dc = c - _mean(c)
                chat = dc * lax.rsqrt(_mean(dc * dc) + EPS_LN)
                cn = chat * lbg_ref[...] + lbb_ref[...]
                cs_ref[at, :] = (cn * _sigmoid(cn)).astype(BF16)
            return 0

        lax.fori_loop(0, tm // (groups * SUB), strip, 0)

    row = pl.BlockSpec((tm, w), lambda i: (i, 0))
    in_specs = [pl.BlockSpec((tm, 4 * w), lambda i: (i, 0)), _vec_spec(w), _vec_spec(w),
                pl.BlockSpec((GROUPS, CHUNK, CHUNK), lambda i: (0, 0, 0)), pl.BlockSpec((CHUNK, w), lambda i: (0, 0)),
                pl.BlockSpec((CONV_B, w), lambda i: (0, 0)), _vec_spec(w), _vec_spec(w), _vec_spec(w)]
    return _row_call(
        "branch_fwd", body, n_steps=t // tm, in_specs=in_specs, out_specs=[row, row, row],
        out_shape=[jax.ShapeDtypeStruct((t, w), BF16), jax.ShapeDtypeStruct((t, w), BF16), jax.ShapeDtypeStruct((t, w), F32)],
        scratch=[pltpu.VMEM((HALO_B, w), F32), pltpu.VMEM((HALO_B + tm, w), F32), pltpu.VMEM((tm, w), F32),
                 pltpu.VMEM((SUB - 1, HALO_B + tm - SUB, w), F32), pltpu.VMEM((CONV_B, SUB, w), F32)],
        args=(z, ln_v_g, ln_v_b, ws_m, bias_full, conv_w, conv_b, ln_b_g, ln_b_b), exch=exch)


def _branch_bwd(z, c_saved, dpa, dcs, dgates, ln_v_g, ln_v_b, ws_m, ws_mt, bias_full, conv_w, ln_b_g, ln_b_b, seq, exch=()):
    t = z.shape[0]
    w = 1024
    tm = min(TM_EW, seq)
    tiles_per_seq = seq // tm
    n_tiles = t // tm
    n_chunks = tm // CHUNK

    def body(z_ref, c_ref, dpa_ref, dcs_ref, dgt_ref, lvg_ref, lvb_ref, ws_ref, wst_ref, bias_ref, cw_ref,
             lbg_ref, lbb_ref,
             dz_ref, dlvg_ref, dlvb_ref, dws_ref, dbs_ref, dcw_ref, dcb_ref, dlbg_ref, dlbb_ref,
             carry_ref, glu_ref, dbuf_ref, mix_ref, dvn_ref, dbs_acc_ref, dglu_ref, sh_ref, wb_ref, dwacc_ref):
        i = pl.program_id(0)
        r = n_tiles - 1 - i

        @pl.when(i == 0)
        def _():
            for ref in (dlvg_ref, dlvb_ref, dws_ref, dbs_acc_ref, dwacc_ref, dcb_ref, dlbg_ref, dlbb_ref):
                ref[...] = jnp.zeros_like(ref)

        u = z_ref[:, 0:w].astype(F32)
        v = z_ref[:, w:2 * w].astype(F32)
        ug, dug = _gelu_and_grad(u)
        vg, dvg = _gelu_and_grad(v)
        dv0 = vg - _mean(vg)
        rstd_v = lax.rsqrt(_mean(dv0 * dv0) + EPS_LN)
        vhat = dv0 * rstd_v
        vn = (vhat * lvg_ref[...] + lvb_ref[...]).astype(BF16)
        dpa = dpa_ref[...].astype(F32)
        dmix = dpa * ug
        dmix_b = dmix.astype(BF16)
        for ci in range(n_chunks):
            rows = slice(ci * CHUNK, (ci + 1) * CHUNK)
            for g in range(GROUPS):
                cols = slice(g * CHUNK, (g + 1) * CHUNK)
                mix_ref[rows, cols] = lax.dot_general(ws_ref[g], vn[rows, cols], _NN, preferred_element_type=F32)
                dvn_ref[rows, cols] = lax.dot_general(wst_ref[g], dmix_b[rows, cols], _NN, preferred_element_type=F32)
                dws_ref[g] += lax.dot_general(dmix_b[rows, cols], vn[rows, cols], _NT, preferred_element_type=F32)
            mix_ref[rows, :] += bias_ref[...]
            dbs_acc_ref[...] += dmix[rows, :]
        dz_ref[:, 0:w] = (dpa * mix_ref[...] * dug).astype(BF16)
        dvn = dvn_ref[...]
        dlvg_ref[...] += _rowsum(dvn * vhat)
        dlvb_ref[...] += _rowsum(dvn)
        dvh = dvn * lvg_ref[...]
        dvg_in = rstd_v * (dvh - _mean(dvh) - vhat * _mean(dvh * vhat))
        dz_ref[:, w:2 * w] = (dvg_in * dvg).astype(BF16)

        c = c_ref[...]
        dc0 = c - _mean(c)
        rstd_c = lax.rsqrt(_mean(dc0 * dc0) + EPS_LN)
        chat = dc0 * rstd_c
        cn = chat * lbg_ref[...] + lbb_ref[...]
        sg = _sigmoid(cn)
        dcn = dcs_ref[...].astype(F32) * (sg * (1.0 + cn * (1.0 - sg)))
        dlbg_ref[...] += _rowsum(dcn * chat)
        dlbb_ref[...] += _rowsum(dcn)
        dch = dcn * lbg_ref[...]
        dc = rstd_c * (dch - _mean(dch) - chat * _mean(dch * chat))
        dcb_ref[...] += _rowsum(dc)

        a = z_ref[:, 2 * w:3 * w].astype(F32)
        gl = z_ref[:, 3 * w:4 * w].astype(F32)
        sgl = _sigmoid(gl)
        glu_ref[...] = a * sgl

        @pl.when(r % tiles_per_seq == tiles_per_seq - 1)
        def _():
            carry_ref[...] = jnp.zeros_like(carry_ref)

        dbuf_ref[0:tm, :] = dc
        dbuf_ref[tm:, :] = carry_ref[...]
        carry_ref[...] = dc[0:HALO_B, :]
        _fill_shifted(dbuf_ref, sh_ref)

        @pl.when(i == 0)
        def _():
            for k in range(CONV_B):
                wb_ref[k] = jnp.broadcast_to(cw_ref[k:k + 1, :], (SUB, w))

        groups = 2

        def strip(si, _):
            s = pl.multiple_of(si * (groups * SUB), groups * SUB)
            glu_rows = [glu_ref[pl.ds(s + SUB * g, SUB), :] for g in range(groups)]
            acc = [jnp.zeros((SUB, w), F32) for _ in range(groups)]
            for k in range(CONV_B):
                whole, part = divmod(CONV_B - 1 - k, SUB)
                wk = wb_ref[k]
                dw_part = jnp.zeros((SUB, w), F32)
                for g in range(groups):
                    at = pl.ds(s + SUB * (whole + g), SUB)
                    d_rows = dbuf_ref[at, :] if part == 0 else sh_ref[part - 1, at, :]
                    acc[g] = acc[g] + wk * d_rows
                    dw_part = dw_part + d_rows * glu_rows[g]
                dwacc_ref[k] += dw_part
            dglu_ref[pl.ds(s, groups * SUB), :] = jnp.concatenate(acc, axis=0)
            return 0

        lax.fori_loop(0, tm // (groups * SUB), strip, 0)
        dglu = dglu_ref[...]
        dz_ref[:, 2 * w:3 * w] = (dglu * sgl).astype(BF16)
        dz_ref[:, 3 * w:4 * w] = (dglu * a * sgl * (1.0 - sgl)).astype(BF16)
        dz_ref[:, 4 * w:6 * w] = dgt_ref[...]

        @pl.when(i == n_tiles - 1)
        def _():
            tri = lax.broadcasted_iota(jnp.int32, (CHUNK, CHUNK), 0) >= lax.broadcasted_iota(jnp.int32, (CHUNK, CHUNK), 1)
            lane = lax.broadcasted_iota(jnp.int32, (CHUNK, CHUNK), 1)
            dbs = jnp.zeros((CHUNK, CHUNK), F32)
            for g in range(GROUPS):
                dws_ref[g] = jnp.where(tri, dws_ref[g], 0.0)
                group_sum = jnp.sum(dbs_acc_ref[:, g * CHUNK:(g + 1) * CHUNK], axis=1, keepdims=True)
                dbs = jnp.where(lane == g, group_sum, dbs)
            dbs_ref[...] = dbs
            for k in range(CONV_B):
                dcw_ref[k:k + 1, :] = _rowsum(dwacc_ref[k])

    def rev(i):
        return n_tiles - 1 - i

    row = pl.BlockSpec((tm, w), lambda i: (rev(i), 0))
    full = lambda shape: pl.BlockSpec(shape, lambda i: (0,) * len(shape))
    in_specs = [pl.BlockSpec((tm, 4 * w), lambda i: (rev(i), 0)),
                row, row, row, pl.BlockSpec((tm, 2 * w), lambda i: (rev(i), 0)),
                _vec_spec(w), _vec_spec(w), full((GROUPS, CHUNK, CHUNK)), full((GROUPS, CHUNK, CHUNK)), full((CHUNK, w)),
                full((CONV_B, w)), _vec_spec(w), _vec_spec(w)]
    out_specs = [pl.BlockSpec((tm, 6 * w), lambda i: (rev(i), 0)), _vec_spec(w), _vec_spec(w), full((GROUPS, CHUNK, CHUNK)),
                 full((CHUNK, CHUNK)), full((CONV_B, w)), _vec_spec(w), _vec_spec(w), _vec_spec(w)]
    vec = jax.ShapeDtypeStruct((1, w), F32)
    out_shape = [jax.ShapeDtypeStruct((t, 6 * w), BF16), vec, vec, jax.ShapeDtypeStruct((GROUPS, CHUNK, CHUNK), F32),
                 jax.ShapeDtypeStruct((CHUNK, CHUNK), F32), jax.ShapeDtypeStruct((CONV_B, w), F32), vec, vec, vec]
    scratch = [pltpu.VMEM((HALO_B, w), F32), pltpu.VMEM((tm, w), F32), pltpu.VMEM((tm + HALO_B, w), F32),
               pltpu.VMEM((tm, w), F32), pltpu.VMEM((tm, w), F32), pltpu.VMEM((CHUNK, w), F32), pltpu.VMEM((tm, w), F32),
               pltpu.VMEM((SUB - 1, HALO_B + tm - SUB, w), F32), pltpu.VMEM((CONV_B, SUB, w), F32),
               pltpu.VMEM((CONV_B, SUB, w), F32)]
    return _row_call("branch_bwd", body, n_steps=n_tiles, in_specs=in_specs, out_specs=out_specs, out_shape=out_shape,
                     scratch=scratch, exch=exch,
                     args=(z, c_saved, dpa, dcs, dgates, ln_v_g, ln_v_b, ws_m, ws_mt, bias_full, conv_w, ln_b_g, ln_b_b))


def _conv3_window(prev8, x):
    win = jnp.concatenate([prev8, x], axis=0)
    n = x.shape[0]
    return [win[HALO_F - 2:HALO_F - 2 + n], win[HALO_F - 1:HALO_F - 1 + n], x]


def _ffn_mid_fwd(up0, conv_w, conv_b, seq):
    nb, t, f = up0.shape
    half = nb // 2
    tm = min(TM_FFN, seq)
    tiles_per_seq = seq // tm
    n_strips = tm // ROWS_F

    def body(up_ref, w_ref, b_ref, act_ref, upc_ref, hist_ref):
        i = pl.program_id(0)

        @pl.when(i % tiles_per_seq == 0)
        def _():
            hist_ref[...] = jnp.zeros_like(hist_ref)

        for j in range(half):
            jv = j + half
            wg = [w_ref[j, k:k + 1, :] for k in range(CONV_F)]
            wv = [w_ref[jv, k:k + 1, :] for k in range(CONV_F)]
            bg, bv = b_ref[j], b_ref[jv]

            def strip(c, carry):
                rows = pl.ds(pl.multiple_of(c * ROWS_F, ROWS_F), ROWS_F)
                xg = up_ref[j, rows, :].astype(F32)
                xv = up_ref[jv, rows, :].astype(F32)
                sg = _conv3_window(carry[0], xg)
                sv = _conv3_window(carry[1], xv)
                gate = bg + wg[0] * sg[0] + wg[1] * sg[1] + wg[2] * sg[2]
                val = bv + wv[0] * sv[0] + wv[1] * sv[1] + wv[2] * sv[2]
                act_ref[j, rows, :] = (_gelu(gate) * val).astype(BF16)
                upc_ref[j, rows, :] = gate.astype(BF16)
                upc_ref[jv, rows, :] = val.astype(BF16)
                return xg[ROWS_F - HALO_F:], xv[ROWS_F - HALO_F:]

            last = lax.fori_loop(0, n_strips, strip, (hist_ref[j], hist_ref[jv]))
            hist_ref[j] = last[0]
            hist_ref[jv] = last[1]

    return pl.pallas_call(
        body, grid=(t // tm,),
        in_specs=[pl.BlockSpec((nb, tm, f), lambda i: (0, i, 0)), pl.BlockSpec((nb, CONV_F, f), lambda i: (0, 0, 0)),
                  pl.BlockSpec((nb, 1, f), lambda i: (0, 0, 0))],
        out_specs=[pl.BlockSpec((half, tm, f), lambda i: (0, i, 0)), pl.BlockSpec((nb, tm, f), lambda i: (0, i, 0))],
        out_shape=[jax.ShapeDtypeStruct((half, t, f), BF16), jax.ShapeDtypeStruct((nb, t, f), BF16)],
        scratch_shapes=[pltpu.VMEM((nb, HALO_F, f), F32)],
        name="ffn_mid_fwd", compiler_params=_params(1))(up0, conv_w, conv_b)


def _ffn_mid_bwd(up0, upc, dact, conv_w, seq, exch=()):
    nb, t, f = up0.shape
    half = nb // 2
    tm = min(TM_FFN, seq)
    tiles_per_seq = seq // tm
    n_tiles = t // tm
    n_strips = tm // ROWS_F

    def body(up_ref, upc_ref, da_ref, w_ref, dup_ref, dw_ref, db_ref, carry_ref, dwacc_ref, dbacc_ref):
        i = pl.program_id(0)
        r = n_tiles - 1 - i

        @pl.when(i == 0)
        def _():
            dwacc_ref[...] = jnp.zeros_like(dwacc_ref)
            dbacc_ref[...] = jnp.zeros_like(dbacc_ref)

        @pl.when(r % tiles_per_seq == tiles_per_seq - 1)
        def _():
            carry_ref[...] = jnp.zeros_like(carry_ref)

        for j in range(half):
            jv = j + half
            wg = [w_ref[j, k:k + 1, :] for k in range(CONV_F)]
            wv = [w_ref[jv, k:k + 1, :] for k in range(CONV_F)]

            def strip(ci, carry):
                rows = pl.ds(pl.multiple_of((n_strips - 1 - ci) * ROWS_F, ROWS_F), ROWS_F)
                val = upc_ref[jv, rows, :].astype(F32)
                gg, dgg = _gelu_and_grad(upc_ref[j, rows, :].astype(F32))
                da = da_ref[j, rows, :].astype(F32)
                d_gate = da * val * dgg
                d_val = da * gg
                for blk, d, nxt, wk in ((j, d_gate, carry[0], wg), (jv, d_val, carry[1], wv)):
                    dbacc_ref[blk] += d
                    dwin = jnp.concatenate([d, nxt], axis=0)
                    shifted = [dwin[2:2 + ROWS_F], dwin[1:1 + ROWS_F], d]
                    x = up_ref[blk, rows, :].astype(F32)
                    for k in range(CONV_F):
                        dwacc_ref[blk, k] += shifted[k] * x
                    dx = wk[0] * shifted[0] + wk[1] * shifted[1] + wk[2] * shifted[2]
                    dup_ref[blk, rows, :] = dx.astype(BF16)
                return d_gate[0:HALO_F], d_val[0:HALO_F]

            carry = lax.fori_loop(0, n_strips, strip, (carry_ref[j], carry_ref[jv]))
            carry_ref[j] = carry[0]
            carry_ref[jv] = carry[1]

        @pl.when(i == n_tiles - 1)
        def _():
            for blk in range(nb):
                db_ref[blk] = _rowsum(dbacc_ref[blk])
                for k in range(CONV_F):
                    dw_ref[blk, k:k + 1, :] = _rowsum(dwacc_ref[blk, k])

    def rev(i):
        return n_tiles - 1 - i

    return _row_call(
        "ffn_mid_bwd", body, n_steps=n_tiles,
        in_specs=[pl.BlockSpec((nb, tm, f), lambda i: (0, rev(i), 0)), pl.BlockSpec((nb, tm, f), lambda i: (0, rev(i), 0)),
                  pl.BlockSpec((half, tm, f), lambda i: (0, rev(i), 0)),
                  pl.BlockSpec((nb, CONV_F, f), lambda i: (0, 0, 0))],
        out_specs=[pl.BlockSpec((nb, tm, f), lambda i: (0, rev(i), 0)), pl.BlockSpec((nb, CONV_F, f), lambda i: (0, 0, 0)),
                   pl.BlockSpec((nb, 1, f), lambda i: (0, 0, 0))],
        out_shape=[jax.ShapeDtypeStruct((nb, t, f), BF16), jax.ShapeDtypeStruct((nb, CONV_F, f), F32),
                   jax.ShapeDtypeStruct((nb, 1, f), F32)],
        scratch=[pltpu.VMEM((nb, HALO_F, f), F32), pltpu.VMEM((nb, CONV_F, ROWS_F, f), F32), pltpu.VMEM((nb, ROWS_F, f), F32)],
        args=(up0, upc, dact, conv_w), exch=exch)


def _all_gather_steps(x_refs, o_refs, send_sems, recv_sems, local_sems):
    n = len(x_refs)
    x, y, c = lax.axis_index("x"), lax.axis_index("y"), lax.axis_index("c")
    me, sibling = (x, y, c), (x, y, 1 - c)
    chips = [(1 - x, y), (x, 1 - y), (1 - x, 1 - y)]

    def slot(pos):
        return 4 * pos[0] + 2 * pos[1] + pos[2]

    def copy(a, k, block, to, src=None):
        dst = o_refs[a].at[slot(block)]
        return pltpu.make_async_remote_copy(
            src_ref=dst if src is None else src, dst_ref=dst, send_sem=send_sems.at[a * 7 + k],
            recv_sem=recv_sems.at[a * 7 + k], device_id=to, device_id_type=MESH)

    mine = [pltpu.make_async_copy(x_refs[a], o_refs[a].at[slot(me)], local_sems.at[a]) for a in range(n)]
    first = []
    for a in range(n):
        first.append(copy(a, 0, me, sibling, src=x_refs[a]))
        first += [copy(a, 1 + j, me, (*chip, c), src=x_refs[a]) for j, chip in enumerate(chips)]

    def start():
        for cp in mine + first:
            cp.start()

    def finish():
        passed = []
        for j, chip in enumerate(chips):
            for a in range(n):
                copy(a, 1 + j, (*chip, c), me).wait_recv()
                cp = copy(a, 4 + j, (*chip, c), sibling)
                cp.start()
                passed.append(cp)
        for a in range(n):
            copy(a, 0, sibling, me).wait_recv()
        for j, chip in enumerate(chips):
            for a in range(n):
                copy(a, 4 + j, (*chip, 1 - c), me).wait_recv()
        for cp in first + passed:
            cp.wait_send()
        for cp in mine:
            cp.wait()

    return start, finish


def _adamw_update(g, w, m, v):
    c1 = 1.0 - ADAM_B1 ** ADAM_STEP
    c2 = 1.0 - ADAM_B2 ** ADAM_STEP
    m_new = ADAM_B1 * m + (1.0 - ADAM_B1) * g
    v_new = ADAM_B2 * v + (1.0 - ADAM_B2) * (g * g)
    delta = -ADAM_LR * ((m_new / c1) / (jnp.sqrt(v_new / c2) + ADAM_EPS) + ADAM_WD * w)
    return m_new, v_new, delta


def _adamw(name, parts, w, m, v, rows_per_step):
    r, c = w.shape
    tr = r if r <= rows_per_step else (rows_per_step if r % rows_per_step == 0 else r // 2)

    def body(p_ref, w_ref, m_ref, v_ref, g_ref, d_ref, mo_ref, vo_ref):
        g = p_ref[0].astype(F32)
        for s in range(1, N_DEV):
            g = g + p_ref[s].astype(F32)
        m_new, v_new, delta = _adamw_update(g, w_ref[...], m_ref[...], v_ref[...])
        g_ref[...] = g
        mo_ref[...] = m_new
        vo_ref[...] = v_new
        d_ref[...] = delta

    row = pl.BlockSpec((tr, c), lambda i: (i, 0))
    out = jax.ShapeDtypeStruct((r, c), F32)
    return pl.pallas_call(body, grid=(r // tr,), in_specs=[pl.BlockSpec((N_DEV, tr, c), lambda i: (0, i, 0)), row, row, row],
                          out_specs=[row, row, row, row], out_shape=[out, out, out, out], name=name,
                          compiler_params=_params(1))(parts, w, m, v)


_VEC_NAMES = ("ln_v_g", "ln_v_b", "b_s", "conv_b_b", "ln_b_g", "ln_b_b", "g_ffn", "g_pg", "g_ple", "g_final")
_LOSS_ROW = len(_VEC_NAMES)
_REP_LAYOUT = dict({k: (i, 1, 1024) for i, k in enumerate(_VEC_NAMES)}, ffn_conv_b=(16, 8, 704), w_s=(24, 128, 1024))


def _pack_replicated_grads(d, loss_row):
    head = jnp.concatenate([d[k].reshape(1, 1024) for k in _VEC_NAMES] + [loss_row], axis=1).reshape(_LOSS_ROW + 1, 1024)
    return jnp.concatenate([jnp.pad(head, ((0, 16 - _LOSS_ROW - 1), (0, 0))),
                            jnp.pad(d["ffn_conv_b"].reshape(8, 704), ((0, 0), (0, 1024 - 704))),
                            d["w_s"].reshape(128, 1024)], axis=0)


def _rows8(vec):
    return jnp.pad(vec.reshape(1, 1024), ((0, 7), (0, 0)))


def _adamw_small(name, parts, items, loss_row=None):
    n = len(items)

    def body(p_ref, *refs):
        ins, outs = refs[:3 * n], refs[3 * n:]
        for a, (row, rows, cols) in enumerate(it[:3] for it in items):
            g = p_ref[0, row:row + rows, 0:cols]
            for s in range(1, N_DEV):
                g = g + p_ref[s, row:row + rows, 0:cols]
            m_new, v_new, delta = _adamw_update(g, ins[3 * a][...], ins[3 * a + 1][...], ins[3 * a + 2][...])
            for ref, val in zip(outs[4 * a:4 * a + 4], (g, delta, m_new, v_new)):
                ref[...] = val
        if loss_row is not None:
            total = p_ref[0, loss_row:loss_row + 1, 0:128]
            for s in range(1, N_DEV):
                total = total + p_ref[s, loss_row:loss_row + 1, 0:128]
            outs[-1][...] = total

    out_shape = [jax.ShapeDtypeStruct((rows, cols), F32) for _, rows, cols, *_ in items for _ in range(4)]
    if loss_row is not None:
        out_shape.append(jax.ShapeDtypeStruct((1, 128), F32))
    flat = pl.pallas_call(body, out_shape=out_shape, name=name,
                          compiler_params=pltpu.CompilerParams(vmem_limit_bytes=VMEM_LIMIT))(
        parts, *[arr for it in items for arr in it[3:]])
    res = [tuple(flat[4 * a:4 * a + 4]) for a in range(n)]
    return res + [flat[-1]] if loss_row is not None else res


def _pack_sharded_small(conv_b_w, ffn_conv_w):
    lead = conv_b_w.shape[:-2]
    pad0 = [(0, 0)] * len(lead)
    a = jnp.pad(conv_b_w, pad0 + [(0, 1), (0, 0)])
    b = jnp.pad(ffn_conv_w.reshape(lead + (CONV_F * 704,)), pad0 + [(0, 24 * 128 - CONV_F * 704)]).reshape(lead + (24, 128))
    return jnp.concatenate([a, b], axis=-2)


def _unpack_sharded_small(pk):
    lead = pk.shape[:-2]
    conv_b_w = pk[..., 0:CONV_B, :]
    ffn = pk[..., 32:56, :].reshape(lead + (24 * 128,))[..., :CONV_F * 704].reshape(lead + (CONV_F, 704))
    return conv_b_w, ffn


_WEIGHTS = ("g_mix", "w_in", "ln_v_g", "ln_v_b", "w_s", "b_s", "w_a_out", "conv_b_w", "conv_b_b", "ln_b_g", "ln_b_b",
            "w_b_out", "w_o", "g_ffn", "w_up", "ffn_conv_w", "ffn_conv_b", "w_down", "g_pg", "w_pg", "w_ple", "g_ple",
            "g_final")
_BIG = ("w_in", "w_a_out", "w_b_out", "w_o", "w_up", "w_down", "w_pg", "w_ple")


def kernel(x, p, g_mix, w_in, ln_v_g, ln_v_b, w_s, b_s, w_a_out, conv_b_w, conv_b_b, ln_b_g, ln_b_b, w_b_out, w_o, g_ffn, w_up, ffn_conv_w, ffn_conv_b, w_down, g_pg, w_pg, w_ple, g_ple, g_final, loss_target, m_g_mix, m_w_in, m_ln_v_g, m_ln_v_b, m_w_s, m_b_s, m_w_a_out, m_conv_b_w, m_conv_b_b, m_ln_b_g, m_ln_b_b, m_w_b_out, m_w_o, m_g_ffn, m_w_up, m_ffn_conv_w, m_ffn_conv_b, m_w_down, m_g_pg, m_w_pg, m_w_ple, m_g_ple, m_g_final, v_g_mix, v_w_in, v_ln_v_g, v_ln_v_b, v_w_s, v_b_s, v_w_a_out, v_conv_b_w, v_conv_b_b, v_ln_b_g, v_ln_b_b, v_w_b_out, v_w_o, v_g_ffn, v_w_up, v_ffn_conv_w, v_ffn_conv_b, v_w_down, v_g_pg, v_w_pg, v_w_ple, v_g_ple, v_g_final):
    local = dict(locals())
    wts = {k: local[k] for k in _WEIGHTS}
    mom = {k: local["m_" + k] for k in _WEIGHTS}
    var = {k: local["v_" + k] for k in _WEIGHTS}
    shapes = {k: wts[k].shape for k in _WEIGHTS}

    bsz, seq, d = x.shape
    t = bsz * seq
    x0 = x.reshape(t, d)
    p0 = p.reshape(t, p.shape[-1])
    target = loss_target.reshape(t, d)
    tm = min(TM_MM, t)
    tm_wide = min(TM_WIDE, t)
    tm_in = min(2 * TM_WIDE, t)
    tt = min(TT_MM, t)
    n_row = t // tm
    n_tok = t // tt

    def sq(a):
        return a.reshape(a.shape[1:])

    shard = {k: sq(wts[k]).astype(BF16) for k in _BIG}
    h1, (w_in3, small8) = _rms_fwd("rms_mix_gather_w_in", x0, g_mix,
                                   gather=[shard["w_in"], _pack_sharded_small(sq(conv_b_w), sq(ffn_conv_w))])
    conv_b_w8, ffn_conv_w8 = _unpack_sharded_small(small8)
    conv_w_full = conv_b_w8.transpose(1, 0, 2).reshape(CONV_B, N_DEV * conv_b_w8.shape[-1])
    n_in = w_in3.shape[2]
    f_dev = shard["w_up"].shape[1]
    n_blk, f_blk = N_DEV // 2, 2 * f_dev

    def pair_blocks(a):
        return a.reshape(n_blk, 2, a.shape[1], f_dev).transpose(0, 2, 1, 3).reshape(n_blk, a.shape[1], f_blk)

    def unpair_blocks(a):
        return a.reshape(n_blk, a.shape[1], 2, f_dev).transpose(0, 2, 1, 3).reshape(N_DEV, a.shape[1], f_dev)

    ws_m = jnp.where(jnp.tril(jnp.ones((CHUNK, CHUNK), bool))[None], sq(w_s), 0.0).astype(BF16)
    ws_mt = jnp.swapaxes(ws_m, 1, 2)
    bias_full = jnp.broadcast_to(sq(b_s).T[:, :, None], (CHUNK, GROUPS, CHUNK)).reshape(CHUNK, GROUPS * CHUNK)
    ffn_b = ffn_conv_b.reshape(n_blk, 1, f_blk)
    ffn_w = pair_blocks(ffn_conv_w8)

    z, (wa3, wb3, wo3) = _matmul(
        "mm_in", h1, w_in3, dims=_NN, grid=(N_DEV, t // tm_in, 1),
        a_spec=pl.BlockSpec((tm_in, d), lambda j, i, k: (i, 0)),
        b_spec=pl.BlockSpec((None, d, n_in), lambda j, i, k: (j, 0, 0)),
        o_spec=pl.BlockSpec((tm_in, n_in), lambda j, i, k: (i, j)), acc_shape=(tm_in, n_in),
        out_shape=jax.ShapeDtypeStruct((t, N_DEV * n_in), BF16),
        exch=[("gather", shard[k]) for k in ("w_a_out", "w_b_out", "w_o")])
    w_a = wa3.reshape(-1, d)
    w_b = wb3.reshape(-1, d)
    w_om = wo3.reshape(-1, d)
    (pa, cs, c_saved), (w_up3, wd3, wpg3, wple3) = _branch_fwd(
        z, ln_v_g, ln_v_b, ws_m, bias_full, conv_w_full, conv_b_b, ln_b_g, ln_b_b, seq,
        exch=[("gather", shard[k]) for k in ("w_up", "w_down", "w_pg", "w_ple")])
    w_pgm = wpg3.reshape(-1, d)
    w_upb = pair_blocks(w_up3)
    w_db = wd3.reshape(n_blk // 2, f_blk, d)
    w_plem = wple3.transpose(1, 0, 2).reshape(wple3.shape[1], d)
    ya = _mm_rows("mm_a_out", pa, w_a, dims=_NN, tm=tm, out_dtype=BF16)
    yb, merged = _mm_rows("mm_b_out", cs, w_b, dims=_NN, tm=tm, epi=_epi_merge_fwd(z, ya, tm))
    x1, h2 = _mm_rows("mm_o", merged, w_om, dims=_NN, tm=tm, epi=_epi_residual_rms(x0, g_ffn, tm))
    up0 = _matmul("mm_up", h2, w_upb, dims=_NN, grid=(n_blk, t // tm_wide, 1),
                  a_spec=pl.BlockSpec((tm_wide, d), lambda j, i, k: (i, 0)),
                  b_spec=pl.BlockSpec((None, d, f_blk), lambda j, i, k: (j, 0, 0)),
                  o_spec=pl.BlockSpec((None, tm_wide, f_blk), lambda j, i, k: (j, i, 0)), acc_shape=(tm_wide, f_blk),
                  out_shape=jax.ShapeDtypeStruct((n_blk, t, f_blk), BF16))
    act, upc = _ffn_mid_fwd(up0, ffn_w, ffn_b, seq)
    x2, hq = _matmul("mm_down", act, w_db, dims=_NN, grid=(n_row, 1, 1), split=n_blk // 2,
                     a_spec=pl.BlockSpec((n_blk // 2, tm, f_blk), lambda i, j, k: (0, i, 0)),
                     b_spec=pl.BlockSpec((n_blk // 2, f_blk, d), lambda i, j, k: (0, 0, 0)),
                     acc_shape=(tm, d), epi=_epi_residual_rms(x1, g_pg, tm))

    dx3, dq, dr, loss_v, dg_final, dg_ple = _mm_rows(
        "mm_pg", hq, w_pgm, dims=_NN, tm=tm, epi=_epi_head(x2, p0, w_plem, target, g_ple, g_final.reshape(1, d), tm))

    recv = {}
    gw_pg = _mm_wgrad("wg_pg", hq, dq, out_dtype=BF16, tt=tt).reshape(wpg3.shape)
    dw_ple = _mm_wgrad("wg_ple", p0, dr, out_dtype=BF16, tt=tt)
    gw_ple = dw_ple.reshape(dw_ple.shape[0], N_DEV, -1).transpose(1, 0, 2)
    dx2, dx2b, dg_pg = _mm_rows("mm_pg_t", dq, w_pgm, dims=_NT, tm=tm, epi=_epi_rms_bwd(x2, g_pg, dx3, tm, want_bf16=True))

    dact, (recv["w_pg"], recv["w_ple"]) = _matmul(
        "mm_down_t", dx2b, w_db, dims=_NT, grid=(n_blk // 2, n_row, 1),
        a_spec=pl.BlockSpec((tm, d), lambda j, i, k: (i, 0)),
        b_spec=pl.BlockSpec((None, f_blk, d), lambda j, i, k: (j, 0, 0)),
        o_spec=pl.BlockSpec((None, tm, f_blk), lambda j, i, k: (j, i, 0)), acc_shape=(tm, f_blk),
        out_shape=jax.ShapeDtypeStruct((n_blk // 2, t, f_blk), BF16),
        exch=[("scatter", gw_pg), ("scatter", gw_ple)])
    gw_down = _matmul("wg_down", act, dx2b, dims=_TN, grid=(n_blk // 2, 1, n_tok),
                      a_spec=pl.BlockSpec((None, tt, f_blk), lambda j, i, k: (j, k, 0)),
                      b_spec=pl.BlockSpec((tt, d), lambda j, i, k: (k, 0)),
                      o_spec=pl.BlockSpec((None, f_blk, d), lambda j, i, k: (j, 0, 0)), acc_shape=(f_blk, d),
                      out_shape=jax.ShapeDtypeStruct((n_blk // 2, f_blk, d), BF16)).reshape(wd3.shape)
    (d_up0, dffn_wb, dffn_b), (recv["w_down"],) = _ffn_mid_bwd(up0, upc, dact, ffn_w, seq, exch=[("scatter", gw_down)])
    dffn_w8 = unpair_blocks(dffn_wb)
    gw_up = _matmul("wg_up", h2, d_up0, dims=_TN, grid=(n_blk, 1, n_tok),
                    a_spec=pl.BlockSpec((tt, d), lambda j, i, k: (k, 0)),
                    b_spec=pl.BlockSpec((None, tt, f_blk), lambda j, i, k: (j, k, 0)),
                    o_spec=pl.BlockSpec((None, d, f_blk), lambda j, i, k: (j, 0, 0)), acc_shape=(d, f_blk),
                    out_shape=jax.ShapeDtypeStruct((n_blk, d, f_blk), BF16))
    gw_up = unpair_blocks(gw_up)
    (dx1, dx1b, dg_ffn), (recv["w_up"],) = _matmul(
        "mm_up_t", d_up0, w_upb, dims=_NT, grid=(n_row, 1, n_blk),
        a_spec=pl.BlockSpec((None, tm, f_blk), lambda i, j, k: (k, i, 0)),
        b_spec=pl.BlockSpec((None, d, f_blk), lambda i, j, k: (k, 0, 0)),
        acc_shape=(tm, d), epi=_epi_rms_bwd(x1, g_ffn, dx2, tm, want_bf16=True), exch=[("scatter", gw_up)])

    dya, dyb, dgates = _mm_rows("mm_o_t", dx1b, w_om, dims=_NT, tm=tm, epi=_epi_merge_bwd(z, ya, yb, tm))
    gw_o = _mm_wgrad("wg_o", merged, dx1b, out_dtype=BF16, tt=tt).reshape(wo3.shape)
    dpa = _mm_rows("mm_a_out_t", dya, w_a, dims=_NT, tm=tm, out_dtype=BF16)
    dcs = _mm_rows("mm_b_out_t", dyb, w_b, dims=_NT, tm=tm, out_dtype=BF16)
    gw_a = _mm_wgrad("wg_a_out", pa, dya, out_dtype=BF16, tt=tt).reshape(wa3.shape)
    gw_b = _mm_wgrad("wg_b_out", cs, dyb, out_dtype=BF16, tt=tt).reshape(wb3.shape)
    (dz, dlvg, dlvb, dws, dbs_full, dconv_w, dconv_b, dlbg, dlbb), (recv["w_o"], recv["w_a_out"], recv["w_b_out"]) = _branch_bwd(
        z, c_saved, dpa, dcs, dgates, ln_v_g, ln_v_b, ws_m, ws_mt, bias_full, conv_w_full, ln_b_g, ln_b_b, seq,
        exch=[("scatter", gw_o), ("scatter", gw_a), ("scatter", gw_b)])
    db_s = dbs_full[:, :GROUPS].T
    rep_partial = _pack_replicated_grads(
        dict(ln_v_g=dlvg, ln_v_b=dlvb, b_s=db_s, conv_b_b=dconv_b, ln_b_g=dlbg, ln_b_b=dlbb, g_ffn=dg_ffn,
             g_pg=dg_pg, g_ple=dg_ple, g_final=dg_final, ffn_conv_b=dffn_b, w_s=dws), loss_v)
    dconv_w8 = dconv_w.reshape(CONV_B, N_DEV, -1).transpose(1, 0, 2)
    small_partial = _pack_sharded_small(dconv_w8, dffn_w8)
    gw_in, (recv_small, recv_rep) = _matmul(
        "wg_in", h1, dz, dims=_TN, grid=(N_DEV, 1, n_tok),
        a_spec=pl.BlockSpec((tt, d), lambda j, i, k: (k, 0)),
        b_spec=pl.BlockSpec((tt, n_in), lambda j, i, k: (k, j)),
        o_spec=pl.BlockSpec((None, d, n_in), lambda j, i, k: (j, 0, 0)), acc_shape=(d, n_in),
        out_shape=jax.ShapeDtypeStruct((N_DEV, d, n_in), BF16),
        exch=[("scatter", small_partial), ("gather", rep_partial)])
    (grad_x, dg_mix), (recv["w_in"],) = _matmul(
        "mm_in_t", dz, w_in3, dims=_NT, grid=(n_row, 1, N_DEV // 2), split=2,
        a_spec=pl.BlockSpec((tm, 2 * n_in), lambda i, j, k: (i, k)),
        b_spec=pl.BlockSpec((2, d, n_in), lambda i, j, k: (k, 0, 0)),
        acc_shape=(tm, d), epi=_epi_rms_bwd(x0, g_mix, dx1, tm, want_bf16=False), exch=[("scatter", gw_in)])
    (recv_g_mix,) = _exchange("exchange_g_mix", [("gather", _rows8(dg_mix))])

    grads, deltas, new_m, new_v = {}, {}, {}, {}
    by_kind = (grads, deltas, new_m, new_v)

    def two_d(a):
        a = sq(a)
        return a.reshape(-1, a.shape[-1])

    for k in _BIG:
        parts = recv[k].reshape(N_DEV, -1, recv[k].shape[-1])
        outs = _adamw("adamw_" + k, parts, two_d(wts[k]), two_d(mom[k]), two_d(var[k]), ADAMW_ROWS)
        for tgt, o in zip(by_kind, outs):
            tgt[k] = o.reshape(shapes[k])

    small = [_pack_sharded_small(sq(s["conv_b_w"]), sq(s["ffn_conv_w"])) for s in (wts, mom, var)]
    for tgt, o in zip(by_kind, _adamw("adamw_conv", recv_small, small[0], small[1], small[2], 56)):
        cw, fw = _unpack_sharded_small(o)
        tgt["conv_b_w"] = cw.reshape(shapes["conv_b_w"])
        tgt["ffn_conv_w"] = fw.reshape(shapes["ffn_conv_w"])

    names = list(_REP_LAYOUT)
    items = [_REP_LAYOUT[k] + tuple(s[k].reshape(_REP_LAYOUT[k][1:]) for s in (wts, mom, var)) for k in names]
    *rep_outs, loss_sum = _adamw_small("adamw_replicated", recv_rep, items, loss_row=_LOSS_ROW)
    for k, outs in zip(names, rep_outs):
        for tgt, o in zip(by_kind, outs):
            tgt[k] = o.reshape(shapes[k])
    loss = loss_sum[0, 0]

    g_mix_item = (0, 1, 1024) + tuple(s["g_mix"].reshape(1, 1024) for s in (wts, mom, var))
    for tgt, o in zip(by_kind, _adamw_small("adamw_g_mix", recv_g_mix, [g_mix_item])[0]):
        tgt["g_mix"] = o.reshape(shapes["g_mix"])

    return (loss, grad_x.reshape(x.shape), *[grads[k] for k in _WEIGHTS], *[deltas[k] for k in _WEIGHTS],
            *[new_m[k] for k in _WEIGHTS], *[new_v[k] for k in _WEIGHTS])
```

```python
import math

import jax
import jax.numpy as jnp
from jax import lax
from jax.experimental import pallas as pl
from jax.experimental.pallas import tpu as pltpu

F32 = jnp.float32
BF16 = jnp.bfloat16

N_DEV = 8
EPS_RMS = 1e-6
EPS_LN = 1e-5
CHUNK = 128
GROUPS = 8
CONV_B = 31
CONV_F = 3
HALO_B = 32
HALO_F = 8
ROWS_F = 16
SUB = 8

ADAM_LR = 0.001
ADAM_B1 = 0.9
ADAM_B2 = 0.999
ADAM_EPS = 1e-08
ADAM_WD = 0.01
ADAM_STEP = 10

VMEM_LIMIT = 56 * 1024 * 1024
TM_MM = 1024
TM_WIDE = 2048
TT_MM = 2048
TM_EW = 256
TM_FFN = 512
ADAMW_ROWS = 512
EPI_STRIP = 256

_NN = (((1,), (0,)), ((), ()))
_NT = (((1,), (1,)), ((), ()))
_TN = (((0,), (0,)), ((), ()))
MESH = pl.DeviceIdType.MESH
HBM_SPEC = pl.BlockSpec(memory_space=pltpu.HBM)


def _params(n_axes):
    return pltpu.CompilerParams(dimension_semantics=("arbitrary",) * n_axes, vmem_limit_bytes=VMEM_LIMIT)


def _gelu(x):
    k = math.sqrt(2.0 / math.pi)
    return 0.5 * x * (1.0 + jnp.tanh(k * (x + 0.044715 * (x * x * x))))


def _gelu_and_grad(x):
    k = math.sqrt(2.0 / math.pi)
    x2 = x * x
    t = jnp.tanh(k * (x + 0.044715 * (x2 * x)))
    g = 0.5 * x * (1.0 + t)
    dg = 0.5 * (1.0 + t) + 0.5 * x * (1.0 - t * t) * (k * (1.0 + 3.0 * 0.044715 * x2))
    return g, dg


def _sigmoid(x):
    return 1.0 / (1.0 + jnp.exp(-x))


def _rowsum(x):
    return jnp.sum(x, axis=0, keepdims=True)


def _mean(x):
    return jnp.mean(x, axis=-1, keepdims=True)


def _exchange_io(exch):
    n = len(exch)
    out_shape = [jax.ShapeDtypeStruct(v.shape if kind == "scatter" else (N_DEV,) + v.shape, v.dtype) for kind, v in exch]
    scratch = [pltpu.SemaphoreType.DMA((7 * n,)), pltpu.SemaphoreType.DMA((7 * n,)), pltpu.SemaphoreType.DMA((n,))] if n else []
    return [HBM_SPEC] * n, [HBM_SPEC] * n, out_shape, scratch


def _exchange_step(kinds, x_refs, o_refs, send_sems, recv_sems, local_sems):
    n = len(kinds)
    x, y, c = lax.axis_index("x"), lax.axis_index("y"), lax.axis_index("c")
    me = 4 * x + 2 * y + c

    def src(a, to_slot):
        return x_refs[a].at[to_slot] if kinds[a] == "scatter" else x_refs[a]

    mine = [pltpu.make_async_copy(src(a, me), o_refs[a].at[me], local_sems.at[a]) for a in range(n)]
    sends, recvs = [], []
    for m in range(1, N_DEV):
        mx, my, mc = (m >> 2) & 1, (m >> 1) & 1, m & 1
        px, py, pc = (1 - x if mx else x), (1 - y if my else y), (1 - c if mc else c)
        peer = 4 * px + 2 * py + pc
        for a in range(n):
            k = a * 7 + m - 1
            sends.append(pltpu.make_async_remote_copy(
                src_ref=src(a, peer), dst_ref=o_refs[a].at[me], send_sem=send_sems.at[k], recv_sem=recv_sems.at[k],
                device_id=(px, py, pc), device_id_type=MESH))
            recvs.append(pltpu.make_async_remote_copy(
                src_ref=src(a, peer), dst_ref=o_refs[a].at[peer], send_sem=send_sems.at[k], recv_sem=recv_sems.at[k],
                device_id=(px, py, pc), device_id_type=MESH))

    def start():
        for cp in mine + sends:
            cp.start()

    def finish():
        for cp in recvs:
            cp.wait_recv()
        for cp in sends:
            cp.wait_send()
        for cp in mine:
            cp.wait()

    return start, finish


def _exchange(name, exch):
    n = len(exch)
    kinds = [k for k, _ in exch]
    in_specs, out_specs, out_shape, scratch = _exchange_io(exch)

    def body(*refs):
        start, finish = _exchange_step(kinds, refs[:n], refs[n:2 * n], *refs[2 * n:])
        start()
        finish()

    return pl.pallas_call(body, in_specs=in_specs, out_specs=out_specs, out_shape=out_shape, scratch_shapes=scratch,
                          name=name)(*[v for _, v in exch])


class _Epilogue:
    def __init__(self, fn, ins=(), in_specs=(), out_specs=(), out_shape=(), strip=None):
        self.fn, self.ins, self.in_specs = fn, list(ins), list(in_specs)
        self.out_specs, self.out_shape = list(out_specs), list(out_shape)
        self.strip = strip


def _matmul(name, a, b, *, dims, grid, a_spec, b_spec, acc_shape, o_spec=None, out_shape=None, epi=None, exch=(), split=1):
    nk = grid[2]
    plain = epi is None
    if plain:
        def store(acc, ins, outs, i):
            outs[0][...] = acc.astype(outs[0].dtype)
        epi = _Epilogue(store, out_specs=[o_spec], out_shape=[out_shape])
    n_in = 2 + len(epi.ins)
    n_out = len(epi.out_specs)
    n_ex = len(exch)
    kinds = [k for k, _ in exch]
    ex_in, ex_out, ex_shape, ex_scratch = _exchange_io(exch)

    def body(*refs):
        a_ref, b_ref = refs[:2]
        step0 = pl.program_id(0)
        epi_ins, rest = refs[2:n_in], refs[n_in:]
        x_refs, rest = rest[:n_ex], rest[n_ex:]
        outs, rest = rest[:n_out], rest[n_out:]
        o_refs, scr = rest[:n_ex], rest[n_ex:]
        if n_ex:
            pid = [pl.program_id(ax) for ax in range(3)]
            ex_start, ex_finish = _exchange_step(kinds, x_refs, o_refs, *scr[len(scr) - 3:])
            pl.when((pid[0] == 0) & (pid[1] == 0) & (pid[2] == 0))(ex_start)
        if split == 1:
            part = lax.dot_general(a_ref[...].astype(BF16), b_ref[...].astype(BF16), dims, preferred_element_type=F32)
        else:
            kk = b_ref.shape[-1]
            part = None
            for s in range(split):
                a_s = a_ref[s] if len(a_ref.shape) == 3 else a_ref[:, s * kk:(s + 1) * kk]
                p_s = lax.dot_general(a_s.astype(BF16), b_ref[s].astype(BF16), dims, preferred_element_type=F32)
                part = p_s if part is None else part + p_s

        def run_epilogue(rows_of_acc):
            rows = acc_shape[0]
            strip = rows if epi.strip is None else min(epi.strip, rows)
            for s in range(0, rows, strip):
                def view(ref):
                    return ref.at[pl.ds(s, strip)] if ref.shape[0] == rows else ref
                first = (step0 == 0) if s == 0 else False
                epi.fn(rows_of_acc(s, strip), [view(r) for r in epi_ins], [view(r) for r in outs], first)

        if nk == 1:
            run_epilogue(lambda s, n: part[s:s + n])
        else:
            acc_ref = scr[0]
            k = pl.program_id(2)

            @pl.when(k == 0)
            def _():
                acc_ref[...] = part

            @pl.when(k > 0)
            def _():
                acc_ref[...] += part

            @pl.when(k == nk - 1)
            def _():
                run_epilogue(lambda s, n: acc_ref[pl.ds(s, n), :])
        if n_ex:
            pl.when((pid[0] == grid[0] - 1) & (pid[1] == grid[1] - 1) & (pid[2] == grid[2] - 1))(ex_finish)

    scratch = ([pltpu.VMEM(acc_shape, F32)] if nk > 1 else []) + ex_scratch
    res = pl.pallas_call(body, grid=grid, in_specs=[a_spec, b_spec] + epi.in_specs + ex_in,
                         out_specs=epi.out_specs + ex_out, out_shape=epi.out_shape + ex_shape, scratch_shapes=scratch,
                         name=name, compiler_params=_params(3))(a, b, *epi.ins, *[v for _, v in exch])
    main = res[0] if plain else res[:n_out]
    return (main, res[n_out:]) if n_ex else main


def _mm_rows(name, a, w, *, dims, tm, out_dtype=None, epi=None):
    t, k = a.shape
    n = w.shape[1] if dims == _NN else w.shape[0]
    tm = min(tm, t)
    return _matmul(name, a, w, dims=dims, grid=(t // tm, 1, 1),
                   a_spec=pl.BlockSpec((tm, k), lambda i, j, kk: (i, 0)),
                   b_spec=pl.BlockSpec(w.shape, lambda i, j, kk: (0, 0)),
                   o_spec=pl.BlockSpec((tm, n), lambda i, j, kk: (i, 0)), acc_shape=(tm, n),
                   out_shape=jax.ShapeDtypeStruct((t, n), out_dtype) if epi is None else None, epi=epi)


def _mm_wgrad(name, a, b, *, out_dtype, tt):
    t, m = a.shape
    n = b.shape[1]
    tt = min(tt, t)
    return _matmul(name, a, b, dims=_TN, grid=(1, 1, t // tt),
                   a_spec=pl.BlockSpec((tt, m), lambda i, j, kk: (kk, 0)),
                   b_spec=pl.BlockSpec((tt, n), lambda i, j, kk: (kk, 0)),
                   o_spec=pl.BlockSpec((m, n), lambda i, j, kk: (0, 0)),
                   acc_shape=(m, n), out_shape=jax.ShapeDtypeStruct((m, n), out_dtype))


def _row3(tm, d, col=0):
    return pl.BlockSpec((tm, d), lambda i, j, k: (i, col))


def _vec3(d):
    return pl.BlockSpec((1, d), lambda i, j, k: (0, 0))


def _accumulate_over_rows(ref, part, first):
    if first is False:
        ref[...] += part
        return

    @pl.when(first)
    def _():
        ref[...] = part + jnp.zeros_like(ref)

    @pl.when(jnp.logical_not(first))
    def _():
        ref[...] += part


def _epi_residual_rms(res, g, tm):
    t, d = res.shape

    def fn(acc, ins, outs, i):
        res_ref, g_ref = ins
        xv = acc + res_ref[...]
        outs[0][...] = xv
        rstd = lax.rsqrt(_mean(xv * xv) + EPS_RMS)
        outs[1][...] = ((xv * rstd) * g_ref[...]).astype(BF16)

    return _Epilogue(fn, strip=EPI_STRIP, ins=[res, g], in_specs=[_row3(tm, d), _vec3(d)], out_specs=[_row3(tm, d), _row3(tm, d)],
                     out_shape=[jax.ShapeDtypeStruct((t, d), F32), jax.ShapeDtypeStruct((t, d), BF16)])


def _epi_rms_bwd(x, g, dres, tm, *, want_bf16):
    t, d = x.shape

    def fn(acc, ins, outs, i):
        x_ref, g_ref, dres_ref = ins
        xv = x_ref[...]
        rstd = lax.rsqrt(_mean(xv * xv) + EPS_RMS)
        nrm = xv * rstd
        dn = acc * g_ref[...]
        dx = dres_ref[...] + rstd * (dn - nrm * _mean(dn * nrm))
        outs[0][...] = dx
        if want_bf16:
            outs[1][...] = dx.astype(BF16)
        _accumulate_over_rows(outs[-1], _rowsum(acc * nrm), i)

    row = _row3(tm, d)
    n_dx = 2 if want_bf16 else 1
    return _Epilogue(fn, strip=EPI_STRIP, ins=[x, g, dres], in_specs=[row, _vec3(d), row], out_specs=[row] * n_dx + [_vec3(d)],
                     out_shape=[jax.ShapeDtypeStruct((t, d), F32)] + [jax.ShapeDtypeStruct((t, d), BF16)] * (n_dx - 1)
                     + [jax.ShapeDtypeStruct((1, d), F32)])


def _epi_merge_fwd(z, ya, tm):
    t, w = ya.shape

    def fn(acc, ins, outs, i):
        ga_ref, gb_ref, ya_ref = ins
        outs[0][...] = acc.astype(BF16)
        sa = _sigmoid(ga_ref[...].astype(F32))
        sb = _sigmoid(gb_ref[...].astype(F32))
        outs[1][...] = (sa * ya_ref[...].astype(F32) + sb * acc).astype(BF16)

    row = _row3(tm, w)
    return _Epilogue(fn, strip=EPI_STRIP, ins=[z, z, ya], in_specs=[_row3(tm, w, 4), _row3(tm, w, 5), row], out_specs=[row, row],
                     out_shape=[jax.ShapeDtypeStruct((t, w), BF16)] * 2)


def _epi_merge_bwd(z, ya, yb, tm):
    t, w = ya.shape

    def fn(acc, ins, outs, i):
        ga_ref, gb_ref, ya_ref, yb_ref = ins
        sa = _sigmoid(ga_ref[...].astype(F32))
        sb = _sigmoid(gb_ref[...].astype(F32))
        outs[0][...] = (acc * sa).astype(BF16)
        outs[1][...] = (acc * sb).astype(BF16)
        outs[2][:, 0:w] = (acc * ya_ref[...].astype(F32) * sa * (1.0 - sa)).astype(BF16)
        outs[2][:, w:2 * w] = (acc * yb_ref[...].astype(F32) * sb * (1.0 - sb)).astype(BF16)

    row = _row3(tm, w)
    return _Epilogue(fn, strip=EPI_STRIP, ins=[z, z, ya, yb], in_specs=[_row3(tm, w, 4), _row3(tm, w, 5), row, row],
                     out_specs=[row, row, _row3(tm, 2 * w)],
                     out_shape=[jax.ShapeDtypeStruct((t, w), BF16)] * 2 + [jax.ShapeDtypeStruct((t, 2 * w), BF16)])


def _epi_head(x2, p, w_ple, target, g_ple, g_final, tm):
    t, d = x2.shape

    def fn(acc, ins, outs, i):
        x2_ref, p_ref, wple_ref, tg_ref, gple_ref, gfin_ref = ins
        dx3_ref, dq_ref, dr_ref, loss_ref, dgfin_ref, dgple_ref = outs
        pg = _sigmoid(acc)
        rv = lax.dot_general(p_ref[...].astype(BF16), wple_ref[...], _NN, preferred_element_type=F32)
        rstd_r = lax.rsqrt(_mean(rv * rv) + EPS_RMS)
        nr = rv * rstd_r
        pe = nr * gple_ref[...]
        x3 = x2_ref[...] + pe * pg
        rstd3 = lax.rsqrt(_mean(x3 * x3) + EPS_RMS)
        n3 = x3 * rstd3
        err = n3 * gfin_ref[...] - tg_ref[...]
        loss_part = jnp.sum(_rowsum(err * err), axis=1, keepdims=True) * (0.5 / d)
        dy = err * (1.0 / d)
        dn3 = dy * gfin_ref[...]
        dx3 = rstd3 * (dn3 - n3 * _mean(dn3 * n3))
        dx3_ref[...] = dx3
        dq_ref[...] = (dx3 * pe * pg * (1.0 - pg)).astype(BF16)
        dpe = dx3 * pg
        dnr = dpe * gple_ref[...]
        dr_ref[...] = (rstd_r * (dnr - nr * _mean(dnr * nr))).astype(BF16)
        _accumulate_over_rows(loss_ref, loss_part, i)
        _accumulate_over_rows(dgfin_ref, _rowsum(dy * n3), i)
        _accumulate_over_rows(dgple_ref, _rowsum(dpe * nr), i)

    row = _row3(tm, d)
    vec = jax.ShapeDtypeStruct((1, d), F32)
    return _Epilogue(fn, strip=EPI_STRIP, ins=[x2, p, w_ple, target, g_ple, g_final],
                     in_specs=[row, _row3(tm, p.shape[1]), pl.BlockSpec(w_ple.shape, lambda i, j, k: (0, 0)), row, _vec3(d), _vec3(d)],
                     out_specs=[row, row, row, _vec3(d), _vec3(d), _vec3(d)],
                     out_shape=[jax.ShapeDtypeStruct((t, d), F32), jax.ShapeDtypeStruct((t, d), BF16),
                                jax.ShapeDtypeStruct((t, d), BF16), vec, vec, vec])


def _vec_spec(d):
    return pl.BlockSpec((1, d), lambda i: (0, 0))


def _row_call(name, body, *, n_steps, in_specs, out_specs, out_shape, scratch, args, exch=()):
    n_in, n_out, n_scr, n_ex = len(in_specs), len(out_specs), len(scratch), len(exch)
    kinds = [k for k, _ in exch]
    ex_in, ex_out, ex_shape, ex_scratch = _exchange_io(exch)

    def wrapped(*refs):
        ins, rest = refs[:n_in], refs[n_in:]
        x_refs, rest = rest[:n_ex], rest[n_ex:]
        outs, rest = rest[:n_out], rest[n_out:]
        o_refs, rest = rest[:n_ex], rest[n_ex:]
        scr, sems = rest[:n_scr], rest[n_scr:]
        if n_ex:
            ex_start, ex_finish = _exchange_step(kinds, x_refs, o_refs, *sems)
            pl.when(pl.program_id(0) == 0)(ex_start)
        body(*ins, *outs, *scr)
        if n_ex:
            pl.when(pl.program_id(0) == n_steps - 1)(ex_finish)

    res = pl.pallas_call(wrapped, grid=(n_steps,), in_specs=list(in_specs) + ex_in, out_specs=list(out_specs) + ex_out,
                         out_shape=list(out_shape) + ex_shape, scratch_shapes=list(scratch) + ex_scratch, name=name,
                         compiler_params=_params(1))(*args, *[v for _, v in exch])
    return res[:n_out], res[n_out:]


def _rms_fwd(name, x, g, gather=()):
    t, d = x.shape
    tm = min(TM_EW * 2, t)
    n_steps = t // tm
    n = len(gather)

    def body(x_ref, g_ref, *refs):
        h_ref = refs[n]
        if n:
            start, finish = _all_gather_steps(refs[:n], refs[n + 1:2 * n + 1], *refs[2 * n + 1:])
            pl.when(pl.program_id(0) == 0)(start)
        xv = x_ref[...]
        rstd = lax.rsqrt(_mean(xv * xv) + EPS_RMS)
        h_ref[...] = ((xv * rstd) * g_ref[...]).astype(BF16)
        if n:
            pl.when(pl.program_id(0) == n_steps - 1)(finish)

    row = pl.BlockSpec((tm, d), lambda i: (i, 0))
    sems = [pltpu.SemaphoreType.DMA((7 * n,)), pltpu.SemaphoreType.DMA((7 * n,)), pltpu.SemaphoreType.DMA((n,))] if n else []
    res = pl.pallas_call(
        body, grid=(n_steps,), in_specs=[row, _vec_spec(d)] + [HBM_SPEC] * n, out_specs=[row] + [HBM_SPEC] * n,
        out_shape=[jax.ShapeDtypeStruct((t, d), BF16)] + [jax.ShapeDtypeStruct((N_DEV,) + v.shape, v.dtype) for v in gather],
        scratch_shapes=sems, name=name, compiler_params=_params(1))(x, g, *gather)
    return res[0], res[1:]


def _fill_shifted(buf_ref, sh_ref):
    n = sh_ref.shape[1]
    for p in range(1, SUB):
        sh_ref[p - 1] = buf_ref[p:p + n, :]


def _branch_fwd(z, ln_v_g, ln_v_b, ws_m, bias_full, conv_w, conv_b, ln_b_g, ln_b_b, seq, exch=()):
    t = z.shape[0]
    w = 1024
    tm = min(TM_EW, seq)
    tiles_per_seq = seq // tm
    n_chunks = tm // CHUNK

    def body(z_ref, lvg_ref, lvb_ref, ws_ref, bias_ref, cw_ref, cb_ref, lbg_ref, lbb_ref,
             pa_ref, cs_ref, c_ref, hist_ref, buf_ref, mix_ref, sh_ref, wb_ref):
        i = pl.program_id(0)
        u = z_ref[:, 0:w].astype(F32)
        v = z_ref[:, w:2 * w].astype(F32)
        ug = _gelu(u)
        vg = _gelu(v)
        dv = vg - _mean(vg)
        vhat = dv * lax.rsqrt(_mean(dv * dv) + EPS_LN)
        vn = (vhat * lvg_ref[...] + lvb_ref[...]).astype(BF16)
        for ci in range(n_chunks):
            rows = slice(ci * CHUNK, (ci + 1) * CHUNK)
            for g in range(GROUPS):
                cols = slice(g * CHUNK, (g + 1) * CHUNK)
                mix_ref[rows, cols] = lax.dot_general(ws_ref[g], vn[rows, cols], _NN, preferred_element_type=F32)
            mix_ref[rows, :] += bias_ref[...]
        pa_ref[...] = (ug * mix_ref[...]).astype(BF16)

        a = z_ref[:, 2 * w:3 * w].astype(F32)
        gl = z_ref[:, 3 * w:4 * w].astype(F32)
        glu = a * _sigmoid(gl)

        @pl.when(i % tiles_per_seq == 0)
        def _():
            hist_ref[...] = jnp.zeros_like(hist_ref)

        buf_ref[0:HALO_B, :] = hist_ref[...]
        buf_ref[HALO_B:, :] = glu
        hist_ref[...] = glu[tm - HALO_B:, :]
        _fill_shifted(buf_ref, sh_ref)

        @pl.when(i == 0)
        def _():
            for k in range(CONV_B):
                wb_ref[k] = jnp.broadcast_to(cw_ref[k:k + 1, :], (SUB, w))

        groups = 4

        def strip(si, _):
            s = pl.multiple_of(si * (groups * SUB), groups * SUB)
            acc = [jnp.zeros((SUB, w), F32) + cb_ref[...] for _ in range(groups)]
            for k in range(CONV_B):
                whole, part = divmod(HALO_B - (CONV_B - 1) + k, SUB)
                wk = wb_ref[k]
                for g in range(groups):
                    at = pl.ds(s + SUB * (whole + g), SUB)
                    acc[g] = acc[g] + wk * (buf_ref[at, :] if part == 0 else sh_ref[part - 1, at, :])
            for g in range(0, groups, 2):
                at = pl.ds(s + SUB * g, 2 * SUB)
                c = jnp.concatenate(acc[g:g + 2], axis=0)
                c_ref[at, :] = c
                dc = c - _mean(c)
                chat = dc * lax.rsqrt(_mean(dc * dc) + EPS_LN)
                cn = chat * lbg_ref[...] + lbb_ref[...]
                cs_ref[at, :] = (cn * _sigmoid(cn)).astype(BF16)
            return 0

        lax.fori_loop(0, tm // (groups * SUB), strip, 0)

    row = pl.BlockSpec((tm, w), lambda i: (i, 0))
    in_specs = [pl.BlockSpec((tm, 4 * w), lambda i: (i, 0)), _vec_spec(w), _vec_spec(w),
                pl.BlockSpec((GROUPS, CHUNK, CHUNK), lambda i: (0, 0, 0)), pl.BlockSpec((CHUNK, w), lambda i: (0, 0)),
                pl.BlockSpec((CONV_B, w), lambda i: (0, 0)), _vec_spec(w), _vec_spec(w), _vec_spec(w)]
    return _row_call(
        "branch_fwd", body, n_steps=t // tm, in_specs=in_specs, out_specs=[row, row, row],
        out_shape=[jax.ShapeDtypeStruct((t, w), BF16), jax.ShapeDtypeStruct((t, w), BF16), jax.ShapeDtypeStruct((t, w), F32)],
        scratch=[pltpu.VMEM((HALO_B, w), F32), pltpu.VMEM((HALO_B + tm, w), F32), pltpu.VMEM((tm, w), F32),
                 pltpu.VMEM((SUB - 1, HALO_B + tm - SUB, w), F32), pltpu.VMEM((CONV_B, SUB, w), F32)],
        args=(z, ln_v_g, ln_v_b, ws_m, bias_full, conv_w, conv_b, ln_b_g, ln_b_b), exch=exch)


def _branch_bwd(z, c_saved, dpa, dcs, dgates, ln_v_g, ln_v_b, ws_m, ws_mt, bias_full, conv_w, ln_b_g, ln_b_b, seq, exch=()):
    t = z.shape[0]
    w = 1024
    tm = min(TM_EW, seq)
    tiles_per_seq = seq // tm
    n_tiles = t // tm
    n_chunks = tm // CHUNK

    def body(z_ref, c_ref, dpa_ref, dcs_ref, dgt_ref, lvg_ref, lvb_ref, ws_ref, wst_ref, bias_ref, cw_ref,
             lbg_ref, lbb_ref,
             dz_ref, dlvg_ref, dlvb_ref, dws_ref, dbs_ref, dcw_ref, dcb_ref, dlbg_ref, dlbb_ref,
             carry_ref, glu_ref, dbuf_ref, mix_ref, dvn_ref, dbs_acc_ref, dglu_ref, sh_ref, wb_ref, dwacc_ref):
        i = pl.program_id(0)
        r = n_tiles - 1 - i

        @pl.when(i == 0)
        def _():
            for ref in (dlvg_ref, dlvb_ref, dws_ref, dbs_acc_ref, dwacc_ref, dcb_ref, dlbg_ref, dlbb_ref):
                ref[...] = jnp.zeros_like(ref)

        for ci in range(n_chunks):
            rows = slice(ci * CHUNK, (ci + 1) * CHUNK)
            ug, dug = _gelu_and_grad(z_ref[rows, 0:w].astype(F32))
            vg, dvg = _gelu_and_grad(z_ref[rows, w:2 * w].astype(F32))
            dv0 = vg - _mean(vg)
            rstd_v = lax.rsqrt(_mean(dv0 * dv0) + EPS_LN)
            vhat = dv0 * rstd_v
            vn = (vhat * lvg_ref[...] + lvb_ref[...]).astype(BF16)
            dpa = dpa_ref[rows, :].astype(F32)
            dmix = dpa * ug
            dmix_b = dmix.astype(BF16)
            for g in range(GROUPS):
                cols = slice(g * CHUNK, (g + 1) * CHUNK)
                mix_ref[rows, cols] = lax.dot_general(ws_ref[g], vn[:, cols], _NN, preferred_element_type=F32)
                dvn_ref[rows, cols] = lax.dot_general(wst_ref[g], dmix_b[:, cols], _NN, preferred_element_type=F32)
                dws_ref[g] += lax.dot_general(dmix_b[:, cols], vn[:, cols], _NT, preferred_element_type=F32)
            dbs_acc_ref[...] += dmix
            dz_ref[rows, 0:w] = (dpa * (mix_ref[rows, :] + bias_ref[...]) * dug).astype(BF16)
            dvn = dvn_ref[rows, :]
            dlvg_ref[...] += _rowsum(dvn * vhat)
            dlvb_ref[...] += _rowsum(dvn)
            dvh = dvn * lvg_ref[...]
            dvg_in = rstd_v * (dvh - _mean(dvh) - vhat * _mean(dvh * vhat))
            dz_ref[rows, w:2 * w] = (dvg_in * dvg).astype(BF16)

        c = c_ref[...]
        dc0 = c - _mean(c)
        rstd_c = lax.rsqrt(_mean(dc0 * dc0) + EPS_LN)
        chat = dc0 * rstd_c
        cn = chat * lbg_ref[...] + lbb_ref[...]
        sg = _sigmoid(cn)
        dcn = dcs_ref[...].astype(F32) * (sg * (1.0 + cn * (1.0 - sg)))
        dlbg_ref[...] += _rowsum(dcn * chat)
        dlbb_ref[...] += _rowsum(dcn)
        dch = dcn * lbg_ref[...]
        dc = rstd_c * (dch - _mean(dch) - chat * _mean(dch * chat))
        dcb_ref[...] += _rowsum(dc)

        a = z_ref[:, 2 * w:3 * w].astype(F32)
        gl = z_ref[:, 3 * w:4 * w].astype(F32)
        sgl = _sigmoid(gl)
        glu_ref[...] = a * sgl

        @pl.when(r % tiles_per_seq == tiles_per_seq - 1)
        def _():
            carry_ref[...] = jnp.zeros_like(carry_ref)

        dbuf_ref[0:tm, :] = dc
        dbuf_ref[tm:, :] = carry_ref[...]
        carry_ref[...] = dc[0:HALO_B, :]
        _fill_shifted(dbuf_ref, sh_ref)

        @pl.when(i == 0)
        def _():
            for k in range(CONV_B):
                wb_ref[k] = jnp.broadcast_to(cw_ref[k:k + 1, :], (SUB, w))

        groups = 2

        def strip(si, _):
            s = pl.multiple_of(si * (groups * SUB), groups * SUB)
            glu_rows = [glu_ref[pl.ds(s + SUB * g, SUB), :] for g in range(groups)]
            acc = [jnp.zeros((SUB, w), F32) for _ in range(groups)]
            for k in range(CONV_B):
                whole, part = divmod(CONV_B - 1 - k, SUB)
                wk = wb_ref[k]
                dw_part = jnp.zeros((SUB, w), F32)
                for g in range(groups):
                    at = pl.ds(s + SUB * (whole + g), SUB)
                    d_rows = dbuf_ref[at, :] if part == 0 else sh_ref[part - 1, at, :]
                    acc[g] = acc[g] + wk * d_rows
                    dw_part = dw_part + d_rows * glu_rows[g]
                dwacc_ref[k] += dw_part
            dglu_ref[pl.ds(s, groups * SUB), :] = jnp.concatenate(acc, axis=0)
            return 0

        lax.fori_loop(0, tm // (groups * SUB), strip, 0)
        dglu = dglu_ref[...]
        dz_ref[:, 2 * w:3 * w] = (dglu * sgl).astype(BF16)
        dz_ref[:, 3 * w:4 * w] = (dglu * a * sgl * (1.0 - sgl)).astype(BF16)
        dz_ref[:, 4 * w:6 * w] = dgt_ref[...]

        @pl.when(i == n_tiles - 1)
        def _():
            tri = lax.broadcasted_iota(jnp.int32, (CHUNK, CHUNK), 0) >= lax.broadcasted_iota(jnp.int32, (CHUNK, CHUNK), 1)
            lane = lax.broadcasted_iota(jnp.int32, (CHUNK, CHUNK), 1)
            dbs = jnp.zeros((CHUNK, CHUNK), F32)
            for g in range(GROUPS):
                dws_ref[g] = jnp.where(tri, dws_ref[g], 0.0)
                group_sum = jnp.sum(dbs_acc_ref[:, g * CHUNK:(g + 1) * CHUNK], axis=1, keepdims=True)
                dbs = jnp.where(lane == g, group_sum, dbs)
            dbs_ref[...] = dbs
            for k in range(CONV_B):
                dcw_ref[k:k + 1, :] = _rowsum(dwacc_ref[k])

    def rev(i):
        return n_tiles - 1 - i

    row = pl.BlockSpec((tm, w), lambda i: (rev(i), 0))
    full = lambda shape: pl.BlockSpec(shape, lambda i: (0,) * len(shape))
    in_specs = [pl.BlockSpec((tm, 4 * w), lambda i: (rev(i), 0)),
                row, row, row, pl.BlockSpec((tm, 2 * w), lambda i: (rev(i), 0)),
                _vec_spec(w), _vec_spec(w), full((GROUPS, CHUNK, CHUNK)), full((GROUPS, CHUNK, CHUNK)), full((CHUNK, w)),
                full((CONV_B, w)), _vec_spec(w), _vec_spec(w)]
    out_specs = [pl.BlockSpec((tm, 6 * w), lambda i: (rev(i), 0)), _vec_spec(w), _vec_spec(w), full((GROUPS, CHUNK, CHUNK)),
                 full((CHUNK, CHUNK)), full((CONV_B, w)), _vec_spec(w), _vec_spec(w), _vec_spec(w)]
    vec = jax.ShapeDtypeStruct((1, w), F32)
    out_shape = [jax.ShapeDtypeStruct((t, 6 * w), BF16), vec, vec, jax.ShapeDtypeStruct((GROUPS, CHUNK, CHUNK), F32),
                 jax.ShapeDtypeStruct((CHUNK, CHUNK), F32), jax.ShapeDtypeStruct((CONV_B, w), F32), vec, vec, vec]
    scratch = [pltpu.VMEM((HALO_B, w), F32), pltpu.VMEM((tm, w), F32), pltpu.VMEM((tm + HALO_B, w), F32),
               pltpu.VMEM((tm, w), F32), pltpu.VMEM((tm, w), F32), pltpu.VMEM((CHUNK, w), F32), pltpu.VMEM((tm, w), F32),
               pltpu.VMEM((SUB - 1, HALO_B + tm - SUB, w), F32), pltpu.VMEM((CONV_B, SUB, w), F32),
               pltpu.VMEM((CONV_B, SUB, w), F32)]
    return _row_call("branch_bwd", body, n_steps=n_tiles, in_specs=in_specs, out_specs=out_specs, out_shape=out_shape,
                     scratch=scratch, exch=exch,
                     args=(z, c_saved, dpa, dcs, dgates, ln_v_g, ln_v_b, ws_m, ws_mt, bias_full, conv_w, ln_b_g, ln_b_b))


def _conv3_window(prev8, x):
    win = jnp.concatenate([prev8, x], axis=0)
    n = x.shape[0]
    return [win[HALO_F - 2:HALO_F - 2 + n], win[HALO_F - 1:HALO_F - 1 + n], x]


def _ffn_mid_fwd(up0, conv_w, conv_b, seq):
    nb, t, f = up0.shape
    half = nb // 2
    tm = min(TM_FFN, seq)
    tiles_per_seq = seq // tm
    n_strips = tm // ROWS_F

    def body(up_ref, w_ref, b_ref, act_ref, upc_ref, hist_ref):
        i = pl.program_id(0)

        @pl.when(i % tiles_per_seq == 0)
        def _():
            hist_ref[...] = jnp.zeros_like(hist_ref)

        for j in range(half):
            jv = j + half
            wg = [w_ref[j, k:k + 1, :] for k in range(CONV_F)]
            wv = [w_ref[jv, k:k + 1, :] for k in range(CONV_F)]
            bg, bv = b_ref[j], b_ref[jv]

            def strip(c, carry):
                rows = pl.ds(pl.multiple_of(c * ROWS_F, ROWS_F), ROWS_F)
                xg = up_ref[j, rows, :].astype(F32)
                xv = up_ref[jv, rows, :].astype(F32)
                sg = _conv3_window(carry[0], xg)
                sv = _conv3_window(carry[1], xv)
                gate = bg + wg[0] * sg[0] + wg[1] * sg[1] + wg[2] * sg[2]
                val = bv + wv[0] * sv[0] + wv[1] * sv[1] + wv[2] * sv[2]
                act_ref[j, rows, :] = (_gelu(gate) * val).astype(BF16)
                upc_ref[j, rows, :] = gate.astype(BF16)
                upc_ref[jv, rows, :] = val.astype(BF16)
                return xg[ROWS_F - HALO_F:], xv[ROWS_F - HALO_F:]

            last = lax.fori_loop(0, n_strips, strip, (hist_ref[j], hist_ref[jv]))
            hist_ref[j] = last[0]
            hist_ref[jv] = last[1]

    return pl.pallas_call(
        body, grid=(t // tm,),
        in_specs=[pl.BlockSpec((nb, tm, f), lambda i: (0, i, 0)), pl.BlockSpec((nb, CONV_F, f), lambda i: (0, 0, 0)),
                  pl.BlockSpec((nb, 1, f), lambda i: (0, 0, 0))],
        out_specs=[pl.BlockSpec((half, tm, f), lambda i: (0, i, 0)), pl.BlockSpec((nb, tm, f), lambda i: (0, i, 0))],
        out_shape=[jax.ShapeDtypeStruct((half, t, f), BF16), jax.ShapeDtypeStruct((nb, t, f), BF16)],
        scratch_shapes=[pltpu.VMEM((nb, HALO_F, f), F32)],
        name="ffn_mid_fwd", compiler_params=_params(1))(up0, conv_w, conv_b)


def _ffn_mid_bwd(up0, upc, dact, conv_w, seq, exch=()):
    nb, t, f = up0.shape
    half = nb // 2
    tm = min(TM_FFN, seq)
    tiles_per_seq = seq // tm
    n_tiles = t // tm
    n_strips = tm // ROWS_F

    def body(up_ref, upc_ref, da_ref, w_ref, dup_ref, dw_ref, db_ref, carry_ref, dwacc_ref, dbacc_ref):
        i = pl.program_id(0)
        r = n_tiles - 1 - i

        @pl.when(i == 0)
        def _():
            dwacc_ref[...] = jnp.zeros_like(dwacc_ref)
            dbacc_ref[...] = jnp.zeros_like(dbacc_ref)

        @pl.when(r % tiles_per_seq == tiles_per_seq - 1)
        def _():
            carry_ref[...] = jnp.zeros_like(carry_ref)

        for j in range(half):
            jv = j + half
            wg = [w_ref[j, k:k + 1, :] for k in range(CONV_F)]
            wv = [w_ref[jv, k:k + 1, :] for k in range(CONV_F)]

            def strip(ci, carry):
                rows = pl.ds(pl.multiple_of((n_strips - 1 - ci) * ROWS_F, ROWS_F), ROWS_F)
                val = upc_ref[jv, rows, :].astype(F32)
                gg, dgg = _gelu_and_grad(upc_ref[j, rows, :].astype(F32))
                da = da_ref[j, rows, :].astype(F32)
                d_gate = da * val * dgg
                d_val = da * gg
                for blk, d, nxt, wk in ((j, d_gate, carry[0], wg), (jv, d_val, carry[1], wv)):
                    dbacc_ref[blk] += d
                    dwin = jnp.concatenate([d, nxt], axis=0)
                    shifted = [dwin[2:2 + ROWS_F], dwin[1:1 + ROWS_F], d]
                    x = up_ref[blk, rows, :].astype(F32)
                    for k in range(CONV_F):
                        dwacc_ref[blk, k] += shifted[k] * x
                    dx = wk[0] * shifted[0] + wk[1] * shifted[1] + wk[2] * shifted[2]
                    dup_ref[blk, rows, :] = dx.astype(BF16)
                return d_gate[0:HALO_F], d_val[0:HALO_F]

            carry = lax.fori_loop(0, n_strips, strip, (carry_ref[j], carry_ref[jv]))
            carry_ref[j] = carry[0]
            carry_ref[jv] = carry[1]

        @pl.when(i == n_tiles - 1)
        def _():
            for blk in range(nb):
                db_ref[blk] = _rowsum(dbacc_ref[blk])
                for k in range(CONV_F):
                    dw_ref[blk, k:k + 1, :] = _rowsum(dwacc_ref[blk, k])

    def rev(i):
        return n_tiles - 1 - i

    return _row_call(
        "ffn_mid_bwd", body, n_steps=n_tiles,
        in_specs=[pl.BlockSpec((nb, tm, f), lambda i: (0, rev(i), 0)), pl.BlockSpec((nb, tm, f), lambda i: (0, rev(i), 0)),
                  pl.BlockSpec((half, tm, f), lambda i: (0, rev(i), 0)),
                  pl.BlockSpec((nb, CONV_F, f), lambda i: (0, 0, 0))],
        out_specs=[pl.BlockSpec((nb, tm, f), lambda i: (0, rev(i), 0)), pl.BlockSpec((nb, CONV_F, f), lambda i: (0, 0, 0)),
                   pl.BlockSpec((nb, 1, f), lambda i: (0, 0, 0))],
        out_shape=[jax.ShapeDtypeStruct((nb, t, f), BF16), jax.ShapeDtypeStruct((nb, CONV_F, f), F32),
                   jax.ShapeDtypeStruct((nb, 1, f), F32)],
        scratch=[pltpu.VMEM((nb, HALO_F, f), F32), pltpu.VMEM((nb, CONV_F, ROWS_F, f), F32), pltpu.VMEM((nb, ROWS_F, f), F32)],
        args=(up0, upc, dact, conv_w), exch=exch)


def _all_gather_steps(x_refs, o_refs, send_sems, recv_sems, local_sems):
    n = len(x_refs)
    x, y, c = lax.axis_index("x"), lax.axis_index("y"), lax.axis_index("c")
    me, sibling = (x, y, c), (x, y, 1 - c)
    chips = [(1 - x, y), (x, 1 - y), (1 - x, 1 - y)]

    def slot(pos):
        return 4 * pos[0] + 2 * pos[1] + pos[2]

    def copy(a, k, block, to, src=None):
        dst = o_refs[a].at[slot(block)]
        return pltpu.make_async_remote_copy(
            src_ref=dst if src is None else src, dst_ref=dst, send_sem=send_sems.at[a * 7 + k],
            recv_sem=recv_sems.at[a * 7 + k], device_id=to, device_id_type=MESH)

    mine = [pltpu.make_async_copy(x_refs[a], o_refs[a].at[slot(me)], local_sems.at[a]) for a in range(n)]
    first = []
    for a in range(n):
        first.append(copy(a, 0, me, sibling, src=x_refs[a]))
        first += [copy(a, 1 + j, me, (*chip, c), src=x_refs[a]) for j, chip in enumerate(chips)]

    def start():
        for cp in mine + first:
            cp.start()

    def finish():
        passed = []
        for j, chip in enumerate(chips):
            for a in range(n):
                copy(a, 1 + j, (*chip, c), me).wait_recv()
                cp = copy(a, 4 + j, (*chip, c), sibling)
                cp.start()
                passed.append(cp)
        for a in range(n):
            copy(a, 0, sibling, me).wait_recv()
        for j, chip in enumerate(chips):
            for a in range(n):
                copy(a, 4 + j, (*chip, 1 - c), me).wait_recv()
        for cp in first + passed:
            cp.wait_send()
        for cp in mine:
            cp.wait()

    return start, finish


def _adamw_update(g, w, m, v):
    c1 = 1.0 - ADAM_B1 ** ADAM_STEP
    c2 = 1.0 - ADAM_B2 ** ADAM_STEP
    m_new = ADAM_B1 * m + (1.0 - ADAM_B1) * g
    v_new = ADAM_B2 * v + (1.0 - ADAM_B2) * (g * g)
    delta = -ADAM_LR * ((m_new / c1) / (jnp.sqrt(v_new / c2) + ADAM_EPS) + ADAM_WD * w)
    return m_new, v_new, delta


def _adamw(name, parts, w, m, v, rows_per_step):
    r, c = w.shape
    tr = r if r <= rows_per_step else (rows_per_step if r % rows_per_step == 0 else r // 2)

    def body(p_ref, w_ref, m_ref, v_ref, g_ref, d_ref, mo_ref, vo_ref):
        g = p_ref[0].astype(F32)
        for s in range(1, N_DEV):
            g = g + p_ref[s].astype(F32)
        m_new, v_new, delta = _adamw_update(g, w_ref[...], m_ref[...], v_ref[...])
        g_ref[...] = g
        mo_ref[...] = m_new
        vo_ref[...] = v_new
        d_ref[...] = delta

    row = pl.BlockSpec((tr, c), lambda i: (i, 0))
    out = jax.ShapeDtypeStruct((r, c), F32)
    return pl.pallas_call(body, grid=(r // tr,), in_specs=[pl.BlockSpec((N_DEV, tr, c), lambda i: (0, i, 0)), row, row, row],
                          out_specs=[row, row, row, row], out_shape=[out, out, out, out], name=name,
                          compiler_params=_params(1))(parts, w, m, v)


_VEC_NAMES = ("ln_v_g", "ln_v_b", "b_s", "conv_b_b", "ln_b_g", "ln_b_b", "g_ffn", "g_pg", "g_ple", "g_final")
_LOSS_ROW = len(_VEC_NAMES)
_REP_LAYOUT = dict({k: (i, 1, 1024) for i, k in enumerate(_VEC_NAMES)}, ffn_conv_b=(16, 8, 704), w_s=(24, 128, 1024))


def _pack_replicated_grads(d, loss_row):
    head = jnp.concatenate([d[k].reshape(1, 1024) for k in _VEC_NAMES] + [loss_row], axis=1).reshape(_LOSS_ROW + 1, 1024)
    return jnp.concatenate([jnp.pad(head, ((0, 16 - _LOSS_ROW - 1), (0, 0))),
                            jnp.pad(d["ffn_conv_b"].reshape(8, 704), ((0, 0), (0, 1024 - 704))),
                            d["w_s"].reshape(128, 1024)], axis=0)


def _rows8(vec):
    return jnp.pad(vec.reshape(1, 1024), ((0, 7), (0, 0)))


def _adamw_small(name, parts, items, loss_row=None):
    n = len(items)

    def body(p_ref, *refs):
        ins, outs = refs[:3 * n], refs[3 * n:]
        for a, (row, rows, cols) in enumerate(it[:3] for it in items):
            g = p_ref[0, row:row + rows, 0:cols]
            for s in range(1, N_DEV):
                g = g + p_ref[s, row:row + rows, 0:cols]
            m_new, v_new, delta = _adamw_update(g, ins[3 * a][...], ins[3 * a + 1][...], ins[3 * a + 2][...])
            for ref, val in zip(outs[4 * a:4 * a + 4], (g, delta, m_new, v_new)):
                ref[...] = val
        if loss_row is not None:
            total = p_ref[0, loss_row:loss_row + 1, 0:128]
            for s in range(1, N_DEV):
                total = total + p_ref[s, loss_row:loss_row + 1, 0:128]
            outs[-1][...] = total

    out_shape = [jax.ShapeDtypeStruct((rows, cols), F32) for _, rows, cols, *_ in items for _ in range(4)]
    if loss_row is not None:
        out_shape.append(jax.ShapeDtypeStruct((1, 128), F32))
    flat = pl.pallas_call(body, out_shape=out_shape, name=name,
                          compiler_params=pltpu.CompilerParams(vmem_limit_bytes=VMEM_LIMIT))(
        parts, *[arr for it in items for arr in it[3:]])
    res = [tuple(flat[4 * a:4 * a + 4]) for a in range(n)]
    return res + [flat[-1]] if loss_row is not None else res


def _pack_sharded_small(conv_b_w, ffn_conv_w):
    lead = conv_b_w.shape[:-2]
    pad0 = [(0, 0)] * len(lead)
    a = jnp.pad(conv_b_w, pad0 + [(0, 1), (0, 0)])
    b = jnp.pad(ffn_conv_w.reshape(lead + (CONV_F * 704,)), pad0 + [(0, 24 * 128 - CONV_F * 704)]).reshape(lead + (24, 128))
    return jnp.concatenate([a, b], axis=-2)


def _unpack_sharded_small(pk):
    lead = pk.shape[:-2]
    conv_b_w = pk[..., 0:CONV_B, :]
    ffn = pk[..., 32:56, :].reshape(lead + (24 * 128,))[..., :CONV_F * 704].reshape(lead + (CONV_F, 704))
    return conv_b_w, ffn


_WEIGHTS = ("g_mix", "w_in", "ln_v_g", "ln_v_b", "w_s", "b_s", "w_a_out", "conv_b_w", "conv_b_b", "ln_b_g", "ln_b_b",
            "w_b_out", "w_o", "g_ffn", "w_up", "ffn_conv_w", "ffn_conv_b", "w_down", "g_pg", "w_pg", "w_ple", "g_ple",
            "g_final")
_BIG = ("w_in", "w_a_out", "w_b_out", "w_o", "w_up", "w_down", "w_pg", "w_ple")


def kernel(x, p, g_mix, w_in, ln_v_g, ln_v_b, w_s, b_s, w_a_out, conv_b_w, conv_b_b, ln_b_g, ln_b_b, w_b_out, w_o, g_ffn, w_up, ffn_conv_w, ffn_conv_b, w_down, g_pg, w_pg, w_ple, g_ple, g_final, loss_target, m_g_mix, m_w_in, m_ln_v_g, m_ln_v_b, m_w_s, m_b_s, m_w_a_out, m_conv_b_w, m_conv_b_b, m_ln_b_g, m_ln_b_b, m_w_b_out, m_w_o, m_g_ffn, m_w_up, m_ffn_conv_w, m_ffn_conv_b, m_w_down, m_g_pg, m_w_pg, m_w_ple, m_g_ple, m_g_final, v_g_mix, v_w_in, v_ln_v_g, v_ln_v_b, v_w_s, v_b_s, v_w_a_out, v_conv_b_w, v_conv_b_b, v_ln_b_g, v_ln_b_b, v_w_b_out, v_w_o, v_g_ffn, v_w_up, v_ffn_conv_w, v_ffn_conv_b, v_w_down, v_g_pg, v_w_pg, v_w_ple, v_g_ple, v_g_final):
    local = dict(locals())
    wts = {k: local[k] for k in _WEIGHTS}
    mom = {k: local["m_" + k] for k in _WEIGHTS}
    var = {k: local["v_" + k] for k in _WEIGHTS}
    shapes = {k: wts[k].shape for k in _WEIGHTS}

    bsz, seq, d = x.shape
    t = bsz * seq
    x0 = x.reshape(t, d)
    p0 = p.reshape(t, p.shape[-1])
    target = loss_target.reshape(t, d)
    tm = min(TM_MM, t)
    tm_wide = min(TM_WIDE, t)
    tm_in = min(2 * TM_WIDE, t)
    tt = min(TT_MM, t)
    n_row = t // tm
    n_tok = t // tt

    def sq(a):
        return a.reshape(a.shape[1:])

    shard = {k: sq(wts[k]).astype(BF16) for k in _BIG}
    h1, (w_in3, small8) = _rms_fwd("rms_mix_gather_w_in", x0, g_mix,
                                   gather=[shard["w_in"], _pack_sharded_small(sq(conv_b_w), sq(ffn_conv_w))])
    conv_b_w8, ffn_conv_w8 = _unpack_sharded_small(small8)
    conv_w_full = conv_b_w8.transpose(1, 0, 2).reshape(CONV_B, N_DEV * conv_b_w8.shape[-1])
    n_in = w_in3.shape[2]
    f_dev = shard["w_up"].shape[1]
    n_blk, f_blk = N_DEV // 2, 2 * f_dev

    def pair_blocks(a):
        return a.reshape(n_blk, 2, a.shape[1], f_dev).transpose(0, 2, 1, 3).reshape(n_blk, a.shape[1], f_blk)

    def unpair_blocks(a):
        return a.reshape(n_blk, a.shape[1], 2, f_dev).transpose(0, 2, 1, 3).reshape(N_DEV, a.shape[1], f_dev)

    ws_m = jnp.where(jnp.tril(jnp.ones((CHUNK, CHUNK), bool))[None], sq(w_s), 0.0).astype(BF16)
    ws_mt = jnp.swapaxes(ws_m, 1, 2)
    bias_full = jnp.broadcast_to(sq(b_s).T[:, :, None], (CHUNK, GROUPS, CHUNK)).reshape(CHUNK, GROUPS * CHUNK)
    ffn_b = ffn_conv_b.reshape(n_blk, 1, f_blk)
    ffn_w = pair_blocks(ffn_conv_w8)

    z, (wa3, wb3, wo3) = _matmul(
        "mm_in", h1, w_in3, dims=_NN, grid=(N_DEV, t // tm_in, 1),
        a_spec=pl.BlockSpec((tm_in, d), lambda j, i, k: (i, 0)),
        b_spec=pl.BlockSpec((None, d, n_in), lambda j, i, k: (j, 0, 0)),
        o_spec=pl.BlockSpec((tm_in, n_in), lambda j, i, k: (i, j)), acc_shape=(tm_in, n_in),
        out_shape=jax.ShapeDtypeStruct((t, N_DEV * n_in), BF16),
        exch=[("gather", shard[k]) for k in ("w_a_out", "w_b_out", "w_o")])
    w_a = wa3.reshape(-1, d)
    w_b = wb3.reshape(-1, d)
    w_om = wo3.reshape(-1, d)
    (pa, cs, c_saved), (w_up3, wd3, wpg3, wple3) = _branch_fwd(
        z, ln_v_g, ln_v_b, ws_m, bias_full, conv_w_full, conv_b_b, ln_b_g, ln_b_b, seq,
        exch=[("gather", shard[k]) for k in ("w_up", "w_down", "w_pg", "w_ple")])
    w_pgm = wpg3.reshape(-1, d)
    w_upb = pair_blocks(w_up3)
    w_db = wd3.reshape(n_blk // 2, f_blk, d)
    w_plem = wple3.transpose(1, 0, 2).reshape(wple3.shape[1], d)
    ya = _mm_rows("mm_a_out", pa, w_a, dims=_NN, tm=tm, out_dtype=BF16)
    yb, merged = _mm_rows("mm_b_out", cs, w_b, dims=_NN, tm=tm, epi=_epi_merge_fwd(z, ya, tm))
    x1, h2 = _mm_rows("mm_o", merged, w_om, dims=_NN, tm=tm, epi=_epi_residual_rms(x0, g_ffn, tm))
    up0 = _matmul("mm_up", h2, w_upb, dims=_NN, grid=(n_blk, t // tm_wide, 1),
                  a_spec=pl.BlockSpec((tm_wide, d), lambda j, i, k: (i, 0)),
                  b_spec=pl.BlockSpec((None, d, f_blk), lambda j, i, k: (j, 0, 0)),
                  o_spec=pl.BlockSpec((None, tm_wide, f_blk), lambda j, i, k: (j, i, 0)), acc_shape=(tm_wide, f_blk),
                  out_shape=jax.ShapeDtypeStruct((n_blk, t, f_blk), BF16))
    act, upc = _ffn_mid_fwd(up0, ffn_w, ffn_b, seq)
    x2, hq = _matmul("mm_down", act, w_db, dims=_NN, grid=(n_row, 1, 1), split=n_blk // 2,
                     a_spec=pl.BlockSpec((n_blk // 2, tm, f_blk), lambda i, j, k: (0, i, 0)),
                     b_spec=pl.BlockSpec((n_blk // 2, f_blk, d), lambda i, j, k: (0, 0, 0)),
                     acc_shape=(tm, d), epi=_epi_residual_rms(x1, g_pg, tm))

    dx3, dq, dr, loss_v, dg_final, dg_ple = _mm_rows(
        "mm_pg", hq, w_pgm, dims=_NN, tm=tm, epi=_epi_head(x2, p0, w_plem, target, g_ple, g_final.reshape(1, d), tm))

    recv = {}
    gw_pg = _mm_wgrad("wg_pg", hq, dq, out_dtype=BF16, tt=tt).reshape(wpg3.shape)
    dw_ple = _mm_wgrad("wg_ple", p0, dr, out_dtype=BF16, tt=tt)
    gw_ple = dw_ple.reshape(dw_ple.shape[0], N_DEV, -1).transpose(1, 0, 2)
    dx2, dx2b, dg_pg = _mm_rows("mm_pg_t", dq, w_pgm, dims=_NT, tm=tm, epi=_epi_rms_bwd(x2, g_pg, dx3, tm, want_bf16=True))

    dact, (recv["w_pg"], recv["w_ple"]) = _matmul(
        "mm_down_t", dx2b, w_db, dims=_NT, grid=(n_blk // 2, n_row, 1),
        a_spec=pl.BlockSpec((tm, d), lambda j, i, k: (i, 0)),
        b_spec=pl.BlockSpec((None, f_blk, d), lambda j, i, k: (j, 0, 0)),
        o_spec=pl.BlockSpec((None, tm, f_blk), lambda j, i, k: (j, i, 0)), acc_shape=(tm, f_blk),
        out_shape=jax.ShapeDtypeStruct((n_blk // 2, t, f_blk), BF16),
        exch=[("scatter", gw_pg), ("scatter", gw_ple)])
    gw_down = _matmul("wg_down", act, dx2b, dims=_TN, grid=(n_blk // 2, 1, n_tok),
                      a_spec=pl.BlockSpec((None, tt, f_blk), lambda j, i, k: (j, k, 0)),
                      b_spec=pl.BlockSpec((tt, d), lambda j, i, k: (k, 0)),
                      o_spec=pl.BlockSpec((None, f_blk, d), lambda j, i, k: (j, 0, 0)), acc_shape=(f_blk, d),
                      out_shape=jax.ShapeDtypeStruct((n_blk // 2, f_blk, d), BF16)).reshape(wd3.shape)
    (d_up0, dffn_wb, dffn_b), (recv["w_down"],) = _ffn_mid_bwd(up0, upc, dact, ffn_w, seq, exch=[("scatter", gw_down)])
    dffn_w8 = unpair_blocks(dffn_wb)
    gw_up = _matmul("wg_up", h2, d_up0, dims=_TN, grid=(n_blk, 1, n_tok),
                    a_spec=pl.BlockSpec((tt, d), lambda j, i, k: (k, 0)),
                    b_spec=pl.BlockSpec((None, tt, f_blk), lambda j, i, k: (j, k, 0)),
                    o_spec=pl.BlockSpec((None, d, f_blk), lambda j, i, k: (j, 0, 0)), acc_shape=(d, f_blk),
                    out_shape=jax.ShapeDtypeStruct((n_blk, d, f_blk), BF16))
    gw_up = unpair_blocks(gw_up)
    (dx1, dx1b, dg_ffn), (recv["w_up"],) = _matmul(
        "mm_up_t", d_up0, w_upb, dims=_NT, grid=(n_row, 1, n_blk),
        a_spec=pl.BlockSpec((None, tm, f_blk), lambda i, j, k: (k, i, 0)),
        b_spec=pl.BlockSpec((None, d, f_blk), lambda i, j, k: (k, 0, 0)),
        acc_shape=(tm, d), epi=_epi_rms_bwd(x1, g_ffn, dx2, tm, want_bf16=True), exch=[("scatter", gw_up)])

    dya, dyb, dgates = _mm_rows("mm_o_t", dx1b, w_om, dims=_NT, tm=tm, epi=_epi_merge_bwd(z, ya, yb, tm))
    gw_o = _mm_wgrad("wg_o", merged, dx1b, out_dtype=BF16, tt=tt).reshape(wo3.shape)
    dpa = _mm_rows("mm_a_out_t", dya, w_a, dims=_NT, tm=tm, out_dtype=BF16)
    dcs = _mm_rows("mm_b_out_t", dyb, w_b, dims=_NT, tm=tm, out_dtype=BF16)
    gw_a = _mm_wgrad("wg_a_out", pa, dya, out_dtype=BF16, tt=tt).reshape(wa3.shape)
    gw_b = _mm_wgrad("wg_b_out", cs, dyb, out_dtype=BF16, tt=tt).reshape(wb3.shape)
    (dz, dlvg, dlvb, dws, dbs_full, dconv_w, dconv_b, dlbg, dlbb), (recv["w_o"], recv["w_a_out"], recv["w_b_out"]) = _branch_bwd(
        z, c_saved, dpa, dcs, dgates, ln_v_g, ln_v_b, ws_m, ws_mt, bias_full, conv_w_full, ln_b_g, ln_b_b, seq,
        exch=[("scatter", gw_o), ("scatter", gw_a), ("scatter", gw_b)])
    db_s = dbs_full[:, :GROUPS].T
    rep_partial = _pack_replicated_grads(
        dict(ln_v_g=dlvg, ln_v_b=dlvb, b_s=db_s, conv_b_b=dconv_b, ln_b_g=dlbg, ln_b_b=dlbb, g_ffn=dg_ffn,
             g_pg=dg_pg, g_ple=dg_ple, g_final=dg_final, ffn_conv_b=dffn_b, w_s=dws), loss_v)
    dconv_w8 = dconv_w.reshape(CONV_B, N_DEV, -1).transpose(1, 0, 2)
    small_partial = _pack_sharded_small(dconv_w8, dffn_w8)
    gw_in, (recv_small, recv_rep) = _matmul(
        "wg_in", h1, dz, dims=_TN, grid=(N_DEV, 1, n_tok),
        a_spec=pl.BlockSpec((tt, d), lambda j, i, k: (k, 0)),
        b_spec=pl.BlockSpec((tt, n_in), lambda j, i, k: (k, j)),
        o_spec=pl.BlockSpec((None, d, n_in), lambda j, i, k: (j, 0, 0)), acc_shape=(d, n_in),
        out_shape=jax.ShapeDtypeStruct((N_DEV, d, n_in), BF16),
        exch=[("scatter", small_partial), ("gather", rep_partial)])
    (grad_x, dg_mix), (recv["w_in"],) = _matmul(
        "mm_in_t", dz, w_in3, dims=_NT, grid=(n_row, 1, N_DEV // 2), split=2,
        a_spec=pl.BlockSpec((tm, 2 * n_in), lambda i, j, k: (i, k)),
        b_spec=pl.BlockSpec((2, d, n_in), lambda i, j, k: (k, 0, 0)),
        acc_shape=(tm, d), epi=_epi_rms_bwd(x0, g_mix, dx1, tm, want_bf16=False), exch=[("scatter", gw_in)])
    (recv_g_mix,) = _exchange("exchange_g_mix", [("gather", _rows8(dg_mix))])

    grads, deltas, new_m, new_v = {}, {}, {}, {}
    by_kind = (grads, deltas, new_m, new_v)

    def two_d(a):
        a = sq(a)
        return a.reshape(-1, a.shape[-1])

    for k in _BIG:
        parts = recv[k].reshape(N_DEV, -1, recv[k].shape[-1])
        outs = _adamw("adamw_" + k, parts, two_d(wts[k]), two_d(mom[k]), two_d(var[k]), ADAMW_ROWS)
        for tgt, o in zip(by_kind, outs):
            tgt[k] = o.reshape(shapes[k])

    small = [_pack_sharded_small(sq(s["conv_b_w"]), sq(s["ffn_conv_w"])) for s in (wts, mom, var)]
    for tgt, o in zip(by_kind, _adamw("adamw_conv", recv_small, small[0], small[1], small[2], 56)):
        cw, fw = _unpack_sharded_small(o)
        tgt["conv_b_w"] = cw.reshape(shapes["conv_b_w"])
        tgt["ffn_conv_w"] = fw.reshape(shapes["ffn_conv_w"])

    names = list(_REP_LAYOUT)
    items = [_REP_LAYOUT[k] + tuple(s[k].reshape(_REP_LAYOUT[k][1:]) for s in (wts, mom, var)) for k in names]
    *rep_outs, loss_sum = _adamw_small("adamw_replicated", recv_rep, items, loss_row=_LOSS_ROW)
    for k, outs in zip(names, rep_outs):
        for tgt, o in zip(by_kind, outs):
            tgt[k] = o.reshape(shapes[k])
    loss = loss_sum[0, 0]

    g_mix_item = (0, 1, 1024) + tuple(s["g_mix"].reshape(1, 1024) for s in (wts, mom, var))
    for tgt, o in zip(by_kind, _adamw_small("adamw_g_mix", recv_g_mix, [g_mix_item])[0]):
        tgt["g_mix"] = o.reshape(shapes["g_mix"])

    return (loss, grad_x.reshape(x.shape), *[grads[k] for k in _WEIGHTS], *[deltas[k] for k in _WEIGHTS],
            *[new_m[k] for k in _WEIGHTS], *[new_v[k] for k in _WEIGHTS])
```
